```python
import math
import jax, jax.numpy as jnp
from jax import lax
import numpy as np

D_MODEL = 4096
BATCH = 4
SEQ = 2048
DEPTH = 1
DEC_BATCH = 128
DEC_SEQ = 1
PAST_LEN = 16384
PAGE_SIZE = 128

SSM_WIDTH = D_MODEL // 2
SSM_GROUP = 16
SSM_GROUPS = SSM_WIDTH // SSM_GROUP
SSM_STATE = 64
GMLP_WIDTH = D_MODEL // 2
GMLP_HEAD = 128
GMLP_HEADS = GMLP_WIDTH // GMLP_HEAD
CHUNK = 128
D_FF = 11008
IN_WIDTH = SSM_WIDTH + 2 * GMLP_WIDTH + 2 * D_MODEL
EPS = 1e-6

kernel_name = "hybrid_s5_gmlp_macaron_decoder_step"


def rmsnorm(x, g):
    xf = x.astype(jnp.float32)
    r = lax.rsqrt(jnp.mean(xf * xf, axis=-1, keepdims=True) + EPS)
    return (xf * r * g.astype(jnp.float32)).astype(x.dtype)


def swiglu(x, w_gate, w_up, w_down):
    return (jax.nn.silu(x @ w_gate) * (x @ w_up)) @ w_down


def s5_discretise(lam_re, lam_im, log_dt, b_re, b_im):
    lam_re = lam_re.astype(jnp.float32); lam_im = lam_im.astype(jnp.float32)
    dt = jnp.exp(log_dt.astype(jnp.float32))[:, None]
    mag = jnp.exp(dt * lam_re)
    abar_re = mag * jnp.cos(dt * lam_im)
    abar_im = mag * jnp.sin(dt * lam_im)
    nr, ni = abar_re - 1.0, abar_im
    den = lam_re * lam_re + lam_im * lam_im
    coef_re = (nr * lam_re + ni * lam_im) / den
    coef_im = (ni * lam_re - nr * lam_im) / den
    b_re = b_re.astype(jnp.float32); b_im = b_im.astype(jnp.float32)
    bbar_re = coef_re[..., None] * b_re - coef_im[..., None] * b_im
    bbar_im = coef_re[..., None] * b_im + coef_im[..., None] * b_re
    return abar_re, abar_im, bbar_re, bbar_im


def _cscan_combine(e1, e2):
    a1r, a1i, b1r, b1i = e1
    a2r, a2i, b2r, b2i = e2
    return (a1r * a2r - a1i * a2i,
            a1r * a2i + a1i * a2r,
            a2r * b1r - a2i * b1i + b2r,
            a2r * b1i + a2i * b1r + b2i)


def s5_branch(u, x0_re, x0_im, lam_re, lam_im, log_dt, b_re, b_im, c_re, c_im, d_skip, w_glu):
    bsz, L, _ = u.shape
    abar_re, abar_im, bbar_re, bbar_im = s5_discretise(lam_re, lam_im, log_dt, b_re, b_im)
    ug = u.astype(jnp.float32).reshape(bsz, L, SSM_GROUPS, SSM_GROUP)
    bu_re = jnp.einsum('blgh,gph->blgp', ug, bbar_re)
    bu_im = jnp.einsum('blgh,gph->blgp', ug, bbar_im)
    x0_re = x0_re.astype(jnp.float32); x0_im = x0_im.astype(jnp.float32)
    bu_re = bu_re.at[:, 0].add(abar_re * x0_re - abar_im * x0_im)
    bu_im = bu_im.at[:, 0].add(abar_re * x0_im + abar_im * x0_re)
    a_re = jnp.broadcast_to(abar_re, bu_re.shape)
    a_im = jnp.broadcast_to(abar_im, bu_im.shape)
    _, _, s_re, s_im = lax.associative_scan(_cscan_combine, (a_re, a_im, bu_re, bu_im), axis=1)
    y = (jnp.einsum('blgp,ghp->blgh', s_re, c_re.astype(jnp.float32))
         - jnp.einsum('blgp,ghp->blgh', s_im, c_im.astype(jnp.float32))
         + d_skip.astype(jnp.float32) * ug)
    y = jax.nn.gelu(y).reshape(bsz, L, SSM_WIDTH).astype(u.dtype)
    z = y @ w_glu
    out = z[..., :D_MODEL] * jax.nn.sigmoid(z[..., D_MODEL:])
    return out, s_re[:, -1], s_im[:, -1]


def gmlp_branch(u, v, norm_v, w_s, b_s, w_gout):
    bsz, L, _ = v.shape
    v = rmsnorm(v, norm_v)
    Lp = -(-L // CHUNK) * CHUNK
    nc = Lp // CHUNK
    vp = jnp.pad(v, ((0, 0), (0, Lp - L), (0, 0))).reshape(bsz, nc, CHUNK, GMLP_HEADS, GMLP_HEAD)
    mask = jnp.tril(jnp.ones((CHUNK, CHUNK), dtype=w_s.dtype))
    ws = w_s * mask
    mixed = jnp.einsum('gts,bnsge->bntge', ws, vp) + b_s.T[:, :, None]
    mixed = mixed.reshape(bsz, Lp, GMLP_WIDTH)[:, :L]
    return (u * mixed) @ w_gout, v


def layer(x, x0_re, x0_im, prm):
    (n1, f1g, f1u, f1d, nmix, w_in, lam_re, lam_im, log_dt, b_re, b_im, c_re, c_im, d_skip, w_glu,
     gnv, w_s, b_s, w_gout, w_out, n2, f2g, f2u, f2d) = prm
    x = x + 0.5 * swiglu(rmsnorm(x, n1), f1g, f1u, f1d)
    h = rmsnorm(x, nmix)
    z = h @ w_in
    o1 = SSM_WIDTH
    o2 = o1 + GMLP_WIDTH
    o3 = o2 + GMLP_WIDTH
    o4 = o3 + D_MODEL
    ssm_in = z[..., :o1]
    gu = jax.nn.gelu(z[..., o1:o2])
    gv = jax.nn.gelu(z[..., o2:o3])
    g_ssm = jax.nn.sigmoid(z[..., o3:o4])
    g_gmlp = jax.nn.sigmoid(z[..., o4:])
    ya, s_re, s_im = s5_branch(ssm_in, x0_re, x0_im, lam_re, lam_im, log_dt, b_re, b_im,
                               c_re, c_im, d_skip, w_glu)
    yb, v_rows = gmlp_branch(gu, gv, gnv, w_s, b_s, w_gout)
    merged = g_ssm * ya + g_gmlp * yb
    x = x + merged @ w_out
    x = x + 0.5 * swiglu(rmsnorm(x, n2), f2g, f2u, f2d)
    return x, s_re, s_im, v_rows


def setup_inputs(seed: int = 0) -> dict:
    key = jax.random.key(seed)
    ks = iter(jax.random.split(key, 40))
    f32 = jnp.float32

    def nrm(shape, scale):
        return jax.random.normal(next(ks), shape, f32) * scale

    def gain(shape):
        return 1.0 + nrm(shape, 0.02)

    n_idx = jnp.arange(SSM_STATE, dtype=f32)
    lam_re = -0.5 * jnp.exp(nrm((DEPTH, SSM_GROUPS, SSM_STATE), 0.05))
    lam_im = math.pi * n_idx + nrm((DEPTH, SSM_GROUPS, SSM_STATE), 0.01)
    log_dt = jax.random.uniform(next(ks), (DEPTH, SSM_GROUPS), f32, math.log(1e-3), math.log(1e-1))
    return {
        "x_prompt": nrm((BATCH, SEQ, D_MODEL), 1.0),
        "x_sample": nrm((DEC_BATCH, DEC_SEQ, D_MODEL), 1.0),
        "state_ssm_re": nrm((DEPTH, DEC_BATCH, SSM_GROUPS, SSM_STATE), 0.3),
        "state_ssm_im": nrm((DEPTH, DEC_BATCH, SSM_GROUPS, SSM_STATE), 0.3),
        "norm_ffn1": gain((DEPTH, D_MODEL)),
        "ffn1_gate": nrm((DEPTH, D_MODEL, D_FF), D_MODEL ** -0.5),
        "ffn1_up": nrm((DEPTH, D_MODEL, D_FF), D_MODEL ** -0.5),
        "ffn1_down": nrm((DEPTH, D_FF, D_MODEL), D_FF ** -0.5),
        "norm_mix": gain((DEPTH, D_MODEL)),
        "w_in": nrm((DEPTH, D_MODEL, IN_WIDTH), D_MODEL ** -0.5),
        "ssm_lambda_re": lam_re,
        "ssm_lambda_im": lam_im,
        "ssm_log_dt": log_dt,
        "ssm_b_re": nrm((DEPTH, SSM_GROUPS, SSM_STATE, SSM_GROUP), (2 * SSM_GROUP) ** -0.5),
        "ssm_b_im": nrm((DEPTH, SSM_GROUPS, SSM_STATE, SSM_GROUP), (2 * SSM_GROUP) ** -0.5),
        "ssm_c_re": nrm((DEPTH, SSM_GROUPS, SSM_GROUP, SSM_STATE), (2 * SSM_STATE) ** -0.5),
        "ssm_c_im": nrm((DEPTH, SSM_GROUPS, SSM_GROUP, SSM_STATE), (2 * SSM_STATE) ** -0.5),
        "ssm_d": gain((DEPTH, SSM_GROUPS, SSM_GROUP)),
        "ssm_w_glu": nrm((DEPTH, SSM_WIDTH, 2 * D_MODEL), SSM_WIDTH ** -0.5),
        "gmlp_norm_v": gain((DEPTH, GMLP_WIDTH)),
        "gmlp_w_s": nrm((DEPTH, GMLP_HEADS, CHUNK, CHUNK), CHUNK ** -0.5),
        "gmlp_b_s": 1.0 + nrm((DEPTH, GMLP_HEADS, CHUNK), 0.02),
        "gmlp_w_out": nrm((DEPTH, GMLP_WIDTH, D_MODEL), GMLP_WIDTH ** -0.5),
        "w_out": nrm((DEPTH, D_MODEL, D_MODEL), D_MODEL ** -0.5),
        "norm_ffn2": gain((DEPTH, D_MODEL)),
        "ffn2_gate": nrm((DEPTH, D_MODEL, D_FF), D_MODEL ** -0.5),
        "ffn2_up": nrm((DEPTH, D_MODEL, D_FF), D_MODEL ** -0.5),
        "ffn2_down": nrm((DEPTH, D_FF, D_MODEL), D_FF ** -0.5),
        "norm_final": gain((D_MODEL,)),
    }


def reference(x_prompt, x_sample, state_ssm_re, state_ssm_im,
              norm_ffn1, ffn1_gate, ffn1_up, ffn1_down, norm_mix, w_in,
              ssm_lambda_re, ssm_lambda_im, ssm_log_dt, ssm_b_re, ssm_b_im, ssm_c_re, ssm_c_im,
              ssm_d, ssm_w_glu, gmlp_norm_v, gmlp_w_s, gmlp_b_s, gmlp_w_out, w_out,
              norm_ffn2, ffn2_gate, ffn2_up, ffn2_down, norm_final):
    yp, ys = x_prompt, x_sample
    zero_state = jnp.zeros((x_prompt.shape[0], SSM_GROUPS, SSM_STATE), jnp.float32)
    p_re, p_im, s_re, s_im, v_s = [], [], [], [], []
    for l in range(DEPTH):
        prm = (norm_ffn1[l], ffn1_gate[l], ffn1_up[l], ffn1_down[l], norm_mix[l], w_in[l],
               ssm_lambda_re[l], ssm_lambda_im[l], ssm_log_dt[l], ssm_b_re[l], ssm_b_im[l],
               ssm_c_re[l], ssm_c_im[l], ssm_d[l], ssm_w_glu[l],
               gmlp_norm_v[l], gmlp_w_s[l], gmlp_b_s[l], gmlp_w_out[l], w_out[l],
               norm_ffn2[l], ffn2_gate[l], ffn2_up[l], ffn2_down[l])
        yp, pr, pi, _ = layer(yp, zero_state, zero_state, prm)
        ys, sr, si, vr = layer(ys, state_ssm_re[l], state_ssm_im[l], prm)
        p_re.append(pr); p_im.append(pi); s_re.append(sr); s_im.append(si); v_s.append(vr)
    y_prompt = rmsnorm(yp, norm_final)
    y_sample = rmsnorm(ys, norm_final)
    return (y_prompt, y_sample, jnp.stack(p_re), jnp.stack(p_im), jnp.stack(s_re), jnp.stack(s_im), jnp.stack(v_s))
```

```python
import functools
import math

import jax
import jax.numpy as jnp
from jax import lax
from jax.experimental import pallas as pl
from jax.experimental.pallas import tpu as pltpu

F32 = jnp.float32
BF16 = jnp.bfloat16

EPS = 1e-6
LANES = 128
SSM_GROUP = 16
SSM_STATE = 64
GROUPS_PER_BLOCK = 16
UB = GROUPS_PER_BLOCK * SSM_GROUP
SB = GROUPS_PER_BLOCK * SSM_STATE
SLABS = SB // LANES
SCAN_T = 128
SCAN_PITCH = SCAN_T + 8
SCAN_INTERLEAVE = 4
CHUNK = 128
GMLP_HEAD = 128
VMEM_LIMIT = 56 * 1024 * 1024


def _cparams(sem):
    return pltpu.CompilerParams(dimension_semantics=sem, vmem_limit_bytes=VMEM_LIMIT)


def _gelu(x):
    c = math.sqrt(2.0 / math.pi)
    return 0.5 * x * (1.0 + jnp.tanh(c * (x + 0.044715 * (x * x * x))))


def _sigmoid(x):
    return 1.0 / (1.0 + jnp.exp(-x))


def _rmsnorm_rows(x_ref, g_ref, dst_ref, rows, chunk):
    g = g_ref[...]

    def body(c, carry):
        r0 = pl.multiple_of(c * chunk, chunk)
        xf = x_ref[pl.ds(r0, chunk), :]
        r = lax.rsqrt(jnp.mean(xf * xf, axis=-1, keepdims=True) + EPS)
        dst_ref[pl.ds(r0, chunk), :] = (xf * r * g).astype(dst_ref.dtype)
        return carry

    lax.fori_loop(0, rows // chunk, body, 0)


def _ffn_kernel(*refs, tm, d_model, col_chunk, row_chunk, final_norm):
    if final_norm:
        x_ref, gn_ref, wg_ref, wu_ref, wd_ref, gf_ref, out_ref, h_scr = refs
    else:
        x_ref, gn_ref, wg_ref, wu_ref, wd_ref, out_ref, h_scr = refs
    j = pl.program_id(1)

    @pl.when(j == 0)
    def _():
        _rmsnorm_rows(x_ref, gn_ref, h_scr, tm, row_chunk)
        out_ref[...] = jnp.zeros_like(out_ref)

    h = h_scr[...]
    g = jnp.dot(h, wg_ref[...], preferred_element_type=F32)
    u = jnp.dot(h, wu_ref[...], preferred_element_type=F32)
    a = (g * _sigmoid(g) * u).astype(BF16)
    for n in range(d_model // col_chunk):
        sl = slice(n * col_chunk, (n + 1) * col_chunk)
        out_ref[:, sl] += jnp.dot(a, wd_ref[:, sl], preferred_element_type=F32)

    @pl.when(j == pl.num_programs(1) - 1)
    def _():
        def body(c, carry):
            r0 = pl.multiple_of(c * row_chunk, row_chunk)
            y = x_ref[pl.ds(r0, row_chunk), :] + 0.5 * out_ref[pl.ds(r0, row_chunk), :]
            if final_norm:
                r = lax.rsqrt(jnp.mean(y * y, axis=-1, keepdims=True) + EPS)
                y = y * r * gf_ref[...]
            out_ref[pl.ds(r0, row_chunk), :] = y
            return carry

        lax.fori_loop(0, tm // row_chunk, body, 0)


def _ffn(x, gn, wg, wu, wd, gfinal, *, tm, tf):
    m, d = x.shape
    f = wg.shape[1]
    final_norm = gfinal is not None
    row_chunk = min(tm, 64)
    kern = functools.partial(_ffn_kernel, tm=tm, d_model=d, col_chunk=512,
                             row_chunk=row_chunk, final_norm=final_norm)
    in_specs = [
        pl.BlockSpec((tm, d), lambda i, j: (i, 0), pipeline_mode=pl.Buffered(1)),
        pl.BlockSpec((1, d), lambda i, j: (0, 0)),
        pl.BlockSpec((d, tf), lambda i, j: (0, j)),
        pl.BlockSpec((d, tf), lambda i, j: (0, j)),
        pl.BlockSpec((tf, d), lambda i, j: (j, 0)),
    ]
    args = [x, gn, wg, wu, wd]
    if final_norm:
        in_specs.append(pl.BlockSpec((1, d), lambda i, j: (0, 0)))
        args.append(gfinal)
    return pl.pallas_call(
        kern,
        grid=(m // tm, f // tf),
        in_specs=in_specs,
        out_specs=pl.BlockSpec((tm, d), lambda i, j: (i, 0)),
        out_shape=jax.ShapeDtypeStruct((m, d), F32),
        scratch_shapes=[pltpu.VMEM((tm, d), BF16)],
        compiler_params=_cparams(("arbitrary", "arbitrary")),
        name="ffn_final" if final_norm else "ffn",
    )(*args)


def _win_kernel(x_ref, gn_ref, w_ref, out_ref, h_scr, *, tm, tn, row_chunk, gelu_lo, gelu_hi):
    j = pl.program_id(1)

    @pl.when(j == 0)
    def _():
        _rmsnorm_rows(x_ref, gn_ref, h_scr, tm, row_chunk)

    z = jnp.dot(h_scr[...], w_ref[...], preferred_element_type=F32)
    col0 = j * tn

    @pl.when(col0 < gelu_lo)
    def _():
        out_ref[...] = z

    @pl.when(jnp.logical_and(col0 >= gelu_lo, col0 < gelu_hi))
    def _():
        out_ref[...] = _gelu(z)

    @pl.when(col0 >= gelu_hi)
    def _():
        out_ref[...] = _sigmoid(z)


def _win(x, gn, w, *, tm, tn, gelu_lo, gelu_hi):
    m, d = x.shape
    n = w.shape[1]
    kern = functools.partial(_win_kernel, tm=tm, tn=tn, row_chunk=min(tm, 64),
                             gelu_lo=gelu_lo, gelu_hi=gelu_hi)
    return pl.pallas_call(
        kern,
        grid=(m // tm, n // tn),
        in_specs=[
            pl.BlockSpec((tm, d), lambda i, j: (i, 0), pipeline_mode=pl.Buffered(1)),
            pl.BlockSpec((1, d), lambda i, j: (0, 0)),
            pl.BlockSpec((d, tn), lambda i, j: (0, j)),
        ],
        out_specs=pl.BlockSpec((tm, tn), lambda i, j: (i, j)),
        out_shape=jax.ShapeDtypeStruct((m, n), F32),
        scratch_shapes=[pltpu.VMEM((tm, d), BF16)],
        compiler_params=_cparams(("arbitrary", "arbitrary")),
        name="w_in",
    )(x, gn, w)


def _disc_kernel(lre_ref, lim_ref, ldt_ref, bre_ref, bim_ref, are_ref, aim_ref, bbre_ref, bbim_ref):
    lam_re = lre_ref[...]
    lam_im = lim_ref[...]
    dt = jnp.exp(ldt_ref[...])
    mag = jnp.exp(dt * lam_re)
    a_re = mag * jnp.cos(dt * lam_im)
    a_im = mag * jnp.sin(dt * lam_im)
    nr, ni = a_re - 1.0, a_im
    den = lam_re * lam_re + lam_im * lam_im
    coef_re = (nr * lam_re + ni * lam_im) / den
    coef_im = (ni * lam_re - nr * lam_im) / den
    are_ref[...] = a_re
    aim_ref[...] = a_im
    b_re = bre_ref[...]
    b_im = bim_ref[...]
    bbre_ref[...] = coef_re[None] * b_re - coef_im[None] * b_im
    bbim_ref[...] = coef_re[None] * b_im + coef_im[None] * b_re


def _discretise(lam_re, lam_im, log_dt, b_re, b_im):
    g, p = lam_re.shape
    h = b_re.shape[-1]
    b_re_t = jnp.transpose(b_re, (2, 0, 1))
    b_im_t = jnp.transpose(b_im, (2, 0, 1))
    return pl.pallas_call(
        _disc_kernel,
        out_shape=(jax.ShapeDtypeStruct((g, p), F32), jax.ShapeDtypeStruct((g, p), F32),
                   jax.ShapeDtypeStruct((h, g, p), F32), jax.ShapeDtypeStruct((h, g, p), F32)),
        name="s5_discretise",
    )(lam_re, lam_im, log_dt.reshape(g, 1), b_re_t, b_im_t)


def _block_diag_weights(bb_re, bb_im, c_re, c_im):
    h, g, p = bb_re.shape
    nb = g // GROUPS_PER_BLOCK
    eye = jnp.eye(GROUPS_PER_BLOCK, dtype=F32)

    def in_proj(bb):
        x = jnp.transpose(bb, (1, 0, 2)).reshape(nb, GROUPS_PER_BLOCK, h, p)
        x = x[:, :, :, None, :] * eye[None, :, None, :, None]
        return x.reshape(nb, UB, SB)

    def out_proj(c):
        x = jnp.transpose(c.reshape(nb, GROUPS_PER_BLOCK, h, p), (0, 1, 3, 2))
        x = x[:, :, :, None, :] * eye[None, :, None, :, None]
        return x.reshape(nb, SB, UB)

    bw = jnp.concatenate([in_proj(bb_re), in_proj(bb_im)], axis=2).astype(BF16)
    cw = jnp.concatenate([out_proj(c_re), -out_proj(c_im)], axis=1).astype(BF16)
    return bw, cw


def _ssm_prompt_kernel(u_ref, bw_ref, cw_ref, are_ref, aim_ref, d_ref,
                       y_ref, sre_ref, sim_ref,
                       bure, buim, st_re, st_im, *, nbatch):
    tc = pl.program_id(1)

    @pl.when(tc == 0)
    def _():
        st_re[...] = jnp.zeros_like(st_re)
        st_im[...] = jnp.zeros_like(st_im)

    bw = bw_ref[0]
    for b in range(nbatch):
        bu = jnp.dot(u_ref[b].astype(BF16), bw, preferred_element_type=F32)
        r0 = b * SCAN_PITCH
        for c in range(SLABS):
            bure[c, r0:r0 + SCAN_T, :] = bu[:, c * LANES:(c + 1) * LANES]
            buim[c, r0:r0 + SCAN_T, :] = bu[:, SB + c * LANES:SB + (c + 1) * LANES]

    for c0 in range(0, SLABS, SCAN_INTERLEAVE):
        slabs = list(range(c0, c0 + SCAN_INTERLEAVE))
        a_re = [jnp.broadcast_to(are_ref[0, c], (nbatch, LANES)) for c in slabs]
        a_im = [jnp.broadcast_to(aim_ref[0, c], (nbatch, LANES)) for c in slabs]
        init = tuple(st_re[c] for c in slabs) + tuple(st_im[c] for c in slabs)

        def step(t, carry, slabs=slabs, a_re=a_re, a_im=a_im):
            n = len(slabs)
            new_re, new_im = [], []
            for k, c in enumerate(slabs):
                s_r, s_i = carry[k], carry[n + k]
                rows = pl.ds(t, nbatch, stride=SCAN_PITCH)
                n_r = a_re[k] * s_r - a_im[k] * s_i + bure[c, rows, :]
                n_i = a_re[k] * s_i + a_im[k] * s_r + buim[c, rows, :]
                bure[c, rows, :] = n_r
                buim[c, rows, :] = n_i
                new_re.append(n_r)
                new_im.append(n_i)
            return tuple(new_re) + tuple(new_im)

        fin = lax.fori_loop(0, SCAN_T, step, init, unroll=4)
        for k, c in enumerate(slabs):
            st_re[c] = fin[k]
            st_im[c] = fin[len(slabs) + k]

    cw = cw_ref[0]
    d = d_ref[0]
    for b in range(nbatch):
        r0 = b * SCAN_PITCH
        s_cat = jnp.concatenate(
            [bure[c, r0:r0 + SCAN_T, :] for c in range(SLABS)]
            + [buim[c, r0:r0 + SCAN_T, :] for c in range(SLABS)], axis=1).astype(BF16)
        y = jnp.dot(s_cat, cw, preferred_element_type=F32) + d * u_ref[b]
        y_ref[b] = _gelu(y).astype(y_ref.dtype)

    @pl.when(tc == pl.num_programs(1) - 1)
    def _():
        for c in range(SLABS):
            sre_ref[:, c * LANES:(c + 1) * LANES] = st_re[c]
            sim_ref[:, c * LANES:(c + 1) * LANES] = st_im[c]


def _ssm_prompt(z3, bw, cw, a_re, a_im, d_skip):
    nbatch, seq, _ = z3.shape
    nb = bw.shape[0]
    width = nb * UB
    states = nb * SB
    kern = functools.partial(_ssm_prompt_kernel, nbatch=nbatch)
    return pl.pallas_call(
        kern,
        grid=(nb, seq // SCAN_T),
        in_specs=[
            pl.BlockSpec((nbatch, SCAN_T, UB), lambda g, t: (0, t, g)),
            pl.BlockSpec((1, UB, 2 * SB), lambda g, t: (g, 0, 0)),
            pl.BlockSpec((1, 2 * SB, UB), lambda g, t: (g, 0, 0)),
            pl.BlockSpec((1, SLABS, 1, LANES), lambda g, t: (g, 0, 0, 0)),
            pl.BlockSpec((1, SLABS, 1, LANES), lambda g, t: (g, 0, 0, 0)),
            pl.BlockSpec((1, 1, UB), lambda g, t: (g, 0, 0)),
        ],
        out_specs=(
            pl.BlockSpec((nbatch, SCAN_T, UB), lambda g, t: (0, t, g)),
            pl.BlockSpec((nbatch, SB), lambda g, t: (0, g)),
            pl.BlockSpec((nbatch, SB), lambda g, t: (0, g)),
        ),
        out_shape=(
            jax.ShapeDtypeStruct((nbatch, seq, width), BF16),
            jax.ShapeDtypeStruct((nbatch, states), F32),
            jax.ShapeDtypeStruct((nbatch, states), F32),
        ),
        scratch_shapes=[
            pltpu.VMEM((SLABS, nbatch * SCAN_PITCH, LANES), F32),
            pltpu.VMEM((SLABS, nbatch * SCAN_PITCH, LANES), F32),
            pltpu.VMEM((SLABS, nbatch, LANES), F32),
            pltpu.VMEM((SLABS, nbatch, LANES), F32),
        ],
        compiler_params=_cparams(("arbitrary", "arbitrary")),
        name="ssm_prompt",
    )(z3, bw, cw, a_re.reshape(nb, SLABS, 1, LANES), a_im.reshape(nb, SLABS, 1, LANES),
      d_skip.reshape(nb, 1, UB))


def _ssm_sample_kernel(u_ref, bw_ref, cw_ref, are_ref, aim_ref, d_ref, x0re_ref, x0im_ref,
                       y_ref, sre_ref, sim_ref):
    u = u_ref[...]
    bu = jnp.dot(u.astype(BF16), bw_ref[0], preferred_element_type=F32)
    a_re = are_ref[0]
    a_im = aim_ref[0]
    x_re = x0re_ref[...]
    x_im = x0im_ref[...]
    s_re = a_re * x_re - a_im * x_im + bu[:, :SB]
    s_im = a_re * x_im + a_im * x_re + bu[:, SB:]
    sre_ref[...] = s_re
    sim_ref[...] = s_im
    s_cat = jnp.concatenate([s_re, s_im], axis=1).astype(BF16)
    y = jnp.dot(s_cat, cw_ref[0], preferred_element_type=F32) + d_ref[0] * u
    y_ref[...] = _gelu(y).astype(y_ref.dtype)


def _ssm_sample(z, bw, cw, a_re, a_im, d_skip, x0_re, x0_im):
    nbatch = z.shape[0]
    nb = bw.shape[0]
    return pl.pallas_call(
        _ssm_sample_kernel,
        grid=(nb,),
        in_specs=[
            pl.BlockSpec((nbatch, UB), lambda g: (0, g)),
            pl.BlockSpec((1, UB, 2 * SB), lambda g: (g, 0, 0)),
            pl.BlockSpec((1, 2 * SB, UB), lambda g: (g, 0, 0)),
            pl.BlockSpec((1, 1, SB), lambda g: (g, 0, 0)),
            pl.BlockSpec((1, 1, SB), lambda g: (g, 0, 0)),
            pl.BlockSpec((1, 1, UB), lambda g: (g, 0, 0)),
            pl.BlockSpec((nbatch, SB), lambda g: (0, g)),
            pl.BlockSpec((nbatch, SB), lambda g: (0, g)),
        ],
        out_specs=(
            pl.BlockSpec((nbatch, UB), lambda g: (0, g)),
            pl.BlockSpec((nbatch, SB), lambda g: (0, g)),
            pl.BlockSpec((nbatch, SB), lambda g: (0, g)),
        ),
        out_shape=(
            jax.ShapeDtypeStruct((nbatch, nb * UB), BF16),
            jax.ShapeDtypeStruct((nbatch, nb * SB), F32),
            jax.ShapeDtypeStruct((nbatch, nb * SB), F32),
        ),
        compiler_params=_cparams(("arbitrary",)),
        name="ssm_sample",
    )(z, bw, cw, a_re.reshape(nb, 1, SB), a_im.reshape(nb, 1, SB), d_skip.reshape(nb, 1, UB),
      x0_re, x0_im)


def _glu_kernel(y_ref, wa_ref, wb_ref, gate_ref, out_ref):
    y = y_ref[...]
    za = jnp.dot(y, wa_ref[...], preferred_element_type=F32)
    zb = jnp.dot(y, wb_ref[...], preferred_element_type=F32)
    out_ref[...] = gate_ref[...] * (za * _sigmoid(zb))


def _glu(y, w_glu, z, *, gate_col, tm, tn):
    m, k = y.shape
    d = w_glu.shape[1] // 2
    return pl.pallas_call(
        _glu_kernel,
        grid=(m // tm, d // tn),
        in_specs=[
            pl.BlockSpec((tm, k), lambda i, j: (i, 0)),
            pl.BlockSpec((k, tn), lambda i, j: (0, j)),
            pl.BlockSpec((k, tn), lambda i, j: (0, j + d // tn)),
            pl.BlockSpec((tm, tn), lambda i, j: (i, j + gate_col // tn)),
        ],
        out_specs=pl.BlockSpec((tm, tn), lambda i, j: (i, j)),
        out_shape=jax.ShapeDtypeStruct((m, d), F32),
        compiler_params=_cparams(("arbitrary", "arbitrary")),
        name="ssm_glu",
    )(y, w_glu, w_glu, z)


def _gmlp_prompt_kernel(gu_ref, gv_ref, gnv_ref, ws_ref, bias_ref, wo_ref, gate_ref, p_ref,
                        out_ref, s_scr, wt_scr, *, tm, heads):
    i = pl.program_id(0)
    j = pl.program_id(1)

    @pl.when(jnp.logical_and(i == 0, j == 0))
    def _():
        row = lax.broadcasted_iota(jnp.int32, (CHUNK, CHUNK), 0)
        col = lax.broadcasted_iota(jnp.int32, (CHUNK, CHUNK), 1)
        mask = (col <= row).astype(F32)
        for g in range(heads):
            wt_scr[g] = (ws_ref[g] * mask).astype(BF16)

    @pl.when(j == 0)
    def _():
        gnv = gnv_ref[...]

        def body(c, carry):
            r0 = pl.multiple_of(c * CHUNK, CHUNK)
            gv = gv_ref[pl.ds(r0, CHUNK), :]
            r = lax.rsqrt(jnp.mean(gv * gv, axis=-1, keepdims=True) + EPS)
            v = (gv * r * gnv).astype(BF16)
            for g in range(heads):
                sl = slice(g * GMLP_HEAD, (g + 1) * GMLP_HEAD)
                mixed = jnp.dot(wt_scr[g], v[:, sl], preferred_element_type=F32) + bias_ref[:, sl]
                s_scr[pl.ds(r0, CHUNK), sl] = (gu_ref[pl.ds(r0, CHUNK), sl] * mixed).astype(BF16)
            return carry

        lax.fori_loop(0, tm // CHUNK, body, 0)

    yb = jnp.dot(s_scr[...], wo_ref[...], preferred_element_type=F32)
    out_ref[...] = (p_ref[...] + gate_ref[...] * yb).astype(out_ref.dtype)


def _gmlp_prompt(z, gnv, w_s, bias_full, w_gout, p, *, u_col, v_col, gate_col, tm, tn):
    m = z.shape[0]
    width, d = w_gout.shape
    heads = w_s.shape[0]
    kern = functools.partial(_gmlp_prompt_kernel, tm=tm, heads=heads)
    return pl.pallas_call(
        kern,
        grid=(m // tm, d // tn),
        in_specs=[
            pl.BlockSpec((tm, width), lambda i, j: (i, u_col // width)),
            pl.BlockSpec((tm, width), lambda i, j: (i, v_col // width)),
            pl.BlockSpec((1, width), lambda i, j: (0, 0)),
            pl.BlockSpec((heads, CHUNK, CHUNK), lambda i, j: (0, 0, 0)),
            pl.BlockSpec((CHUNK, width), lambda i, j: (0, 0)),
            pl.BlockSpec((width, tn), lambda i, j: (0, j)),
            pl.BlockSpec((tm, tn), lambda i, j: (i, j + gate_col // tn)),
            pl.BlockSpec((tm, tn), lambda i, j: (i, j)),
        ],
        out_specs=pl.BlockSpec((tm, tn), lambda i, j: (i, j)),
        out_shape=jax.ShapeDtypeStruct((m, d), BF16),
        scratch_shapes=[pltpu.VMEM((tm, width), BF16), pltpu.VMEM((heads, CHUNK, CHUNK), BF16)],
        compiler_params=_cparams(("arbitrary", "arbitrary")),
        name="gmlp_prompt",
    )(z, z, gnv, w_s, bias_full, w_gout, z, p)


def _gmlp_sample_kernel(gu_ref, gv_ref, gnv_ref, wdiag_ref, bias_ref, wo_ref, gate_ref, p_ref,
                        out_ref, v_ref, s_scr):
    j = pl.program_id(0)

    @pl.when(j == 0)
    def _():
        gv = gv_ref[...]
        r = lax.rsqrt(jnp.mean(gv * gv, axis=-1, keepdims=True) + EPS)
        v = gv * r * gnv_ref[...]
        v_ref[...] = v
        mixed = wdiag_ref[...] * v + bias_ref[...]
        s_scr[...] = (gu_ref[...] * mixed).astype(BF16)

    yb = jnp.dot(s_scr[...], wo_ref[...], preferred_element_type=F32)
    out_ref[...] = (p_ref[...] + gate_ref[...] * yb).astype(out_ref.dtype)


def _gmlp_sample(z, gnv, wdiag, bias0, w_gout, p, *, u_col, v_col, gate_col, tn):
    m = z.shape[0]
    width, d = w_gout.shape
    return pl.pallas_call(
        _gmlp_sample_kernel,
        grid=(d // tn,),
        in_specs=[
            pl.BlockSpec((m, width), lambda j: (0, u_col // width)),
            pl.BlockSpec((m, width), lambda j: (0, v_col // width)),
            pl.BlockSpec((1, width), lambda j: (0, 0)),
            pl.BlockSpec((1, width), lambda j: (0, 0)),
            pl.BlockSpec((1, width), lambda j: (0, 0)),
            pl.BlockSpec((width, tn), lambda j: (0, j)),
            pl.BlockSpec((m, tn), lambda j: (0, j + gate_col // tn)),
            pl.BlockSpec((m, tn), lambda j: (0, j)),
        ],
        out_specs=(pl.BlockSpec((m, tn), lambda j: (0, j)),
                   pl.BlockSpec((m, width), lambda j: (0, 0))),
        out_shape=(jax.ShapeDtypeStruct((m, d), BF16), jax.ShapeDtypeStruct((m, width), F32)),
        scratch_shapes=[pltpu.VMEM((m, width), BF16)],
        compiler_params=_cparams(("arbitrary",)),
        name="gmlp_sample",
    )(z, z, gnv, wdiag, bias0, w_gout, z, p)


def _outproj_kernel(m_ref, w_ref, x_ref, out_ref):
    out_ref[...] = x_ref[...] + jnp.dot(m_ref[...], w_ref[...], preferred_element_type=F32)


def _outproj(merged, w_out, x, *, tm, tn):
    m, k = merged.shape
    d = w_out.shape[1]
    return pl.pallas_call(
        _outproj_kernel,
        grid=(m // tm, d // tn),
        in_specs=[
            pl.BlockSpec((tm, k), lambda i, j: (i, 0)),
            pl.BlockSpec((k, tn), lambda i, j: (0, j)),
            pl.BlockSpec((tm, tn), lambda i, j: (i, j)),
        ],
        out_specs=pl.BlockSpec((tm, tn), lambda i, j: (i, j)),
        out_shape=jax.ShapeDtypeStruct((m, d), F32),
        compiler_params=_cparams(("arbitrary", "arbitrary")),
        name="out_proj",
    )(merged, w_out, x)


def _layer(x, prm, *, nbatch, x0, gfinal, tm_ffn, tm_mm):
    d_model = x.shape[1]
    ssm_width = prm["bw"].shape[0] * UB
    gmlp_width = prm["w_gout"].shape[0]
    o1 = ssm_width
    o2 = o1 + gmlp_width
    o3 = o2 + gmlp_width
    o4 = o3 + d_model

    x = _ffn(x, prm["n1"], prm["f1g"], prm["f1u"], prm["f1d"], None, tm=tm_ffn, tf=256)
    z = _win(x, prm["nmix"], prm["w_in"], tm=tm_ffn, tn=512, gelu_lo=o1, gelu_hi=o3)

    if x0 is None:
        seq = x.shape[0] // nbatch
        y, s_re, s_im = _ssm_prompt(z.reshape(nbatch, seq, z.shape[1]), prm["bw"], prm["cw"],
                                    prm["a_re"], prm["a_im"], prm["d_skip"])
        y = y.reshape(nbatch * seq, ssm_width)
    else:
        y, s_re, s_im = _ssm_sample(z, prm["bw"], prm["cw"], prm["a_re"], prm["a_im"],
                                    prm["d_skip"], x0[0], x0[1])
    p = _glu(y, prm["w_glu"], z, gate_col=o3, tm=tm_mm, tn=512)

    if x0 is None:
        merged = _gmlp_prompt(z, prm["gnv"], prm["w_s"], prm["bias_full"], prm["w_gout"], p,
                              u_col=o1, v_col=o2, gate_col=o4, tm=tm_mm, tn=512)
        v_rows = None
    else:
        merged, v_rows = _gmlp_sample(z, prm["gnv"], prm["w_diag"], prm["bias0"], prm["w_gout"], p,
                                      u_col=o1, v_col=o2, gate_col=o4, tn=512)
    x = _outproj(merged, prm["w_out"], x, tm=tm_mm, tn=512)
    x = _ffn(x, prm["n2"], prm["f2g"], prm["f2u"], prm["f2d"], gfinal, tm=tm_ffn, tf=256)
    return x, s_re, s_im, v_rows


def kernel(x_prompt, x_sample, state_ssm_re, state_ssm_im, norm_ffn1, ffn1_gate, ffn1_up, ffn1_down, norm_mix, w_in, ssm_lambda_re, ssm_lambda_im, ssm_log_dt, ssm_b_re, ssm_b_im, ssm_c_re, ssm_c_im, ssm_d, ssm_w_glu, gmlp_norm_v, gmlp_w_s, gmlp_b_s, gmlp_w_out, w_out, norm_ffn2, ffn2_gate, ffn2_up, ffn2_down, norm_final):
    depth = w_in.shape[0]
    batch, seq, d_model = x_prompt.shape
    dec_batch, dec_seq, _ = x_sample.shape
    assert dec_seq == 1
    groups, states = ssm_lambda_re.shape[1:]
    gmlp_width = gmlp_norm_v.shape[1]
    heads = gmlp_w_s.shape[1]
    head_dim = gmlp_width // heads
    assert seq % SCAN_T == 0 and seq % CHUNK == 0 and head_dim == GMLP_HEAD
    assert ssm_b_re.shape[-1] == SSM_GROUP and states == SSM_STATE and groups % GROUPS_PER_BLOCK == 0

    yp = x_prompt.reshape(batch * seq, d_model)
    ys = x_sample.reshape(dec_batch, d_model)
    gfin = norm_final.reshape(1, d_model)
    outs = {k: [] for k in ("p_re", "p_im", "s_re", "s_im", "v")}
    for l in range(depth):
        a_re, a_im, bb_re, bb_im = _discretise(ssm_lambda_re[l], ssm_lambda_im[l], ssm_log_dt[l],
                                               ssm_b_re[l], ssm_b_im[l])
        bw, cw = _block_diag_weights(bb_re, bb_im, ssm_c_re[l], ssm_c_im[l])
        prm = dict(
            n1=norm_ffn1[l].reshape(1, d_model),
            f1g=ffn1_gate[l].astype(BF16), f1u=ffn1_up[l].astype(BF16), f1d=ffn1_down[l].astype(BF16),
            nmix=norm_mix[l].reshape(1, d_model), w_in=w_in[l].astype(BF16),
            bw=bw, cw=cw, a_re=a_re, a_im=a_im, d_skip=ssm_d[l],
            w_glu=ssm_w_glu[l].astype(BF16),
            gnv=gmlp_norm_v[l].reshape(1, gmlp_width), w_s=gmlp_w_s[l],
            bias_full=jnp.repeat(gmlp_b_s[l].T, head_dim, axis=1),
            w_diag=jnp.repeat(gmlp_w_s[l][:, 0, 0], head_dim).reshape(1, gmlp_width),
            bias0=jnp.repeat(gmlp_b_s[l][:, 0], head_dim).reshape(1, gmlp_width),
            w_gout=gmlp_w_out[l].astype(BF16), w_out=w_out[l].astype(BF16),
            n2=norm_ffn2[l].reshape(1, d_model),
            f2g=ffn2_gate[l].astype(BF16), f2u=ffn2_up[l].astype(BF16), f2d=ffn2_down[l].astype(BF16),
        )
        last = l == depth - 1
        yp, pr, pi, _ = _layer(yp, prm, nbatch=batch, x0=None, gfinal=gfin if last else None,
                               tm_ffn=512, tm_mm=1024)
        x0 = (state_ssm_re[l].reshape(dec_batch, groups * states),
              state_ssm_im[l].reshape(dec_batch, groups * states))
        ys, sr, si, vr = _layer(ys, prm, nbatch=dec_batch, x0=x0, gfinal=gfin if last else None,
                                tm_ffn=dec_batch, tm_mm=dec_batch)
        outs["p_re"].append(pr.reshape(batch, groups, states))
        outs["p_im"].append(pi.reshape(batch, groups, states))
        outs["s_re"].append(sr.reshape(dec_batch, groups, states))
        outs["s_im"].append(si.reshape(dec_batch, groups, states))
        outs["v"].append(vr.reshape(dec_batch, dec_seq, gmlp_width))
    return (yp.reshape(batch, seq, d_model), ys.reshape(dec_batch, dec_seq, d_model),
            jnp.stack(outs["p_re"]), jnp.stack(outs["p_im"]),
            jnp.stack(outs["s_re"]), jnp.stack(outs["s_im"]), jnp.stack(outs["v"]))
```

```python
import functools
import math

import jax
import jax.numpy as jnp
from jax import lax
from jax.experimental import pallas as pl
from jax.experimental.pallas import tpu as pltpu

F32 = jnp.float32
BF16 = jnp.bfloat16

EPS = 1e-6
LANES = 128
SSM_GROUP = 16
SSM_STATE = 64
GROUPS_PER_BLOCK = 16
UB = GROUPS_PER_BLOCK * SSM_GROUP
SB = GROUPS_PER_BLOCK * SSM_STATE
SLABS = SB // LANES
SCAN_T = 256
SCAN_PITCH = SCAN_T + 8
SCAN_INTERLEAVE = 4
CHUNK = 128
GMLP_HEAD = 128
VMEM_LIMIT = 56 * 1024 * 1024


def _cparams(sem):
    return pltpu.CompilerParams(dimension_semantics=sem, vmem_limit_bytes=VMEM_LIMIT)


def _gelu(x):
    c = math.sqrt(2.0 / math.pi)
    return 0.5 * x * (1.0 + jnp.tanh(c * (x + 0.044715 * (x * x * x))))


def _sigmoid(x):
    return 1.0 / (1.0 + jnp.exp(-x))


def _as_bf16(w_ref, emit_ref):
    w = w_ref[...]
    if w.dtype != BF16:
        w = w.astype(BF16)
    if emit_ref is not None:
        emit_ref[...] = w
    return w


def _ffn_kernel(*refs, tm, d_model, col_chunk, row_chunk, tail, emit):
    x_ref, gn_ref, wg_ref, wu_ref, wd_ref, gt_ref = refs[:6]
    rest = list(refs[6:])
    out_ref = rest.pop(0)
    h2_ref = rest.pop(0) if tail == "norm_out" else None
    emits = [rest.pop(0) for _ in range(3)] if emit else [None] * 3
    (h_scr,) = rest
    j = pl.program_id(1)

    @pl.when(j == 0)
    def _():
        g = gn_ref[...]

        def body(c, carry):
            r0 = pl.multiple_of(c * row_chunk, row_chunk)
            xf = x_ref[pl.ds(r0, row_chunk), :]
            r = lax.rsqrt(jnp.mean(xf * xf, axis=-1, keepdims=True) + EPS)
            h_scr[pl.ds(r0, row_chunk), :] = (xf * r * g).astype(BF16)
            return carry

        lax.fori_loop(0, tm // row_chunk, body, 0)
        out_ref[...] = jnp.zeros_like(out_ref)

    h = h_scr[...]
    g = jnp.dot(h, _as_bf16(wg_ref, emits[0]), preferred_element_type=F32)
    u = jnp.dot(h, _as_bf16(wu_ref, emits[1]), preferred_element_type=F32)
    a = (g * _sigmoid(g) * u).astype(BF16)
    wd = _as_bf16(wd_ref, emits[2])
    for n in range(d_model // col_chunk):
        sl = slice(n * col_chunk, (n + 1) * col_chunk)
        out_ref[:, sl] += jnp.dot(a, wd[:, sl], preferred_element_type=F32)

    @pl.when(j == pl.num_programs(1) - 1)
    def _():
        gt = gt_ref[...]

        def body(c, carry):
            r0 = pl.multiple_of(c * row_chunk, row_chunk)
            y = x_ref[pl.ds(r0, row_chunk), :] + 0.5 * out_ref[pl.ds(r0, row_chunk), :]
            r = lax.rsqrt(jnp.mean(y * y, axis=-1, keepdims=True) + EPS)
            if tail == "norm_out":
                out_ref[pl.ds(r0, row_chunk), :] = y
                h2_ref[pl.ds(r0, row_chunk), :] = (y * r * gt).astype(BF16)
            else:
                out_ref[pl.ds(r0, row_chunk), :] = y * r * gt
            return carry

        lax.fori_loop(0, tm // row_chunk, body, 0)


def _ffn(x, gn, wg, wu, wd, gtail, *, tail, tm, tf, emit):
    m, d = x.shape
    f = wg.shape[1]
    assert not emit or m == tm
    kern = functools.partial(_ffn_kernel, tm=tm, d_model=d, col_chunk=512,
                             row_chunk=min(tm, 64), tail=tail, emit=emit)
    out_specs = [pl.BlockSpec((tm, d), lambda i, j: (i, 0))]
    out_shape = [jax.ShapeDtypeStruct((m, d), F32)]
    if tail == "norm_out":
        out_specs.append(pl.BlockSpec((tm, d), lambda i, j: (i, 0)))
        out_shape.append(jax.ShapeDtypeStruct((m, d), BF16))
    if emit:
        out_specs += [pl.BlockSpec((d, tf), lambda i, j: (0, j)),
                      pl.BlockSpec((d, tf), lambda i, j: (0, j)),
                      pl.BlockSpec((tf, d), lambda i, j: (j, 0))]
        out_shape += [jax.ShapeDtypeStruct(w.shape, BF16) for w in (wg, wu, wd)]
    return pl.pallas_call(
        kern,
        grid=(m // tm, f // tf),
        in_specs=[
            pl.BlockSpec((tm, d), lambda i, j: (i, 0), pipeline_mode=pl.Buffered(1)),
            pl.BlockSpec((1, d), lambda i, j: (0, 0)),
            pl.BlockSpec((d, tf), lambda i, j: (0, j)),
            pl.BlockSpec((d, tf), lambda i, j: (0, j)),
            pl.BlockSpec((tf, d), lambda i, j: (j, 0)),
            pl.BlockSpec((1, d), lambda i, j: (0, 0)),
        ],
        out_specs=out_specs,
        out_shape=out_shape,
        scratch_shapes=[pltpu.VMEM((tm, d), BF16)],
        compiler_params=_cparams(("arbitrary", "arbitrary")),
        name="ffn_" + tail + ("_emit" if emit else ""),
    )(x, gn, wg, wu, wd, gtail)


def _mm_epilogue(kind, accs, extras):
    if kind == "identity":
        return accs[0]
    if kind == "gelu":
        return _gelu(accs[0])
    if kind == "sigmoid":
        return _sigmoid(accs[0])
    if kind == "gated_glu":
        return extras[0] * (accs[0] * _sigmoid(accs[1]))
    if kind == "residual":
        return extras[0] + accs[0]
    raise ValueError(kind)


def _mm_kernel(*refs, n_w, n_extra, kind, emit):
    lhs_ref = refs[0]
    w_refs = refs[1:1 + n_w]
    extra_refs = refs[1 + n_w:1 + n_w + n_extra]
    out_ref = refs[1 + n_w + n_extra]
    emit_refs = refs[2 + n_w + n_extra:] if emit else [None] * n_w
    lhs = lhs_ref[...]
    accs = [jnp.dot(lhs, _as_bf16(w, e), preferred_element_type=F32) for w, e in zip(w_refs, emit_refs)]
    out_ref[...] = _mm_epilogue(kind, accs, [e[...] for e in extra_refs]).astype(out_ref.dtype)


def _mm(lhs, weights, extras, *, kind, n_out, out_dtype, tm, tn, emit, name):
    m, k = lhs.shape
    assert not emit or m == tm
    in_specs = [pl.BlockSpec((tm, k), lambda i, j: (i, 0))]
    args = [lhs]
    for w, c0 in weights:
        in_specs.append(pl.BlockSpec((k, tn), lambda i, j, o=c0 // tn: (0, j + o)))
        args.append(w)
    for e, c0 in extras:
        in_specs.append(pl.BlockSpec((tm, tn), lambda i, j, o=c0 // tn: (i, j + o)))
        args.append(e)
    out_specs = [pl.BlockSpec((tm, tn), lambda i, j: (i, j))]
    out_shape = [jax.ShapeDtypeStruct((m, n_out), out_dtype)]
    if emit:
        out_specs += [pl.BlockSpec((k, tn), lambda i, j: (0, j)) for _ in weights]
        out_shape += [jax.ShapeDtypeStruct((k, n_out), BF16) for _ in weights]
    kern = functools.partial(_mm_kernel, n_w=len(weights), n_extra=len(extras), kind=kind, emit=emit)
    res = pl.pallas_call(
        kern,
        grid=(m // tm, n_out // tn),
        in_specs=in_specs,
        out_specs=out_specs,
        out_shape=out_shape,
        compiler_params=_cparams(("arbitrary", "arbitrary")),
        name=name + ("_emit" if emit else ""),
    )(*args)
    return res[0], list(res[1:])


def _disc_kernel(lre_ref, lim_ref, ldt_ref, bre_ref, bim_ref, are_ref, aim_ref, bbre_ref, bbim_ref):
    lam_re = lre_ref[...]
    lam_im = lim_ref[...]
    dt = jnp.exp(ldt_ref[...])
    mag = jnp.exp(dt * lam_re)
    a_re = mag * jnp.cos(dt * lam_im)
    a_im = mag * jnp.sin(dt * lam_im)
    nr, ni = a_re - 1.0, a_im
    den = lam_re * lam_re + lam_im * lam_im
    coef_re = (nr * lam_re + ni * lam_im) / den
    coef_im = (ni * lam_re - nr * lam_im) / den
    are_ref[...] = a_re
    aim_ref[...] = a_im
    b_re = bre_ref[...]
    b_im = bim_ref[...]
    bbre_ref[...] = coef_re[None] * b_re - coef_im[None] * b_im
    bbim_ref[...] = coef_re[None] * b_im + coef_im[None] * b_re


def _discretise(lam_re, lam_im, log_dt, b_re, b_im):
    g, p = lam_re.shape
    h = b_re.shape[-1]
    b_re_t = jnp.transpose(b_re, (2, 0, 1))
    b_im_t = jnp.transpose(b_im, (2, 0, 1))
    return pl.pallas_call(
        _disc_kernel,
        out_shape=(jax.ShapeDtypeStruct((g, p), F32), jax.ShapeDtypeStruct((g, p), F32),
                   jax.ShapeDtypeStruct((h, g, p), F32), jax.ShapeDtypeStruct((h, g, p), F32)),
        name="s5_discretise",
    )(lam_re, lam_im, log_dt.reshape(g, 1), b_re_t, b_im_t)


def _block_diag_weights(bb_re, bb_im, c_re, c_im):
    h, g, p = bb_re.shape
    nb = g // GROUPS_PER_BLOCK
    eye = jnp.eye(GROUPS_PER_BLOCK, dtype=F32)

    def in_proj(bb):
        x = jnp.transpose(bb, (1, 0, 2)).reshape(nb, GROUPS_PER_BLOCK, h, p)
        x = x[:, :, :, None, :] * eye[None, :, None, :, None]
        return x.reshape(nb, UB, SB)

    def out_proj(c):
        x = jnp.transpose(c.reshape(nb, GROUPS_PER_BLOCK, h, p), (0, 1, 3, 2))
        x = x[:, :, :, None, :] * eye[None, :, None, :, None]
        return x.reshape(nb, SB, UB)

    bw = jnp.concatenate([in_proj(bb_re), in_proj(bb_im)], axis=2).astype(BF16)
    cw = jnp.concatenate([out_proj(c_re), -out_proj(c_im)], axis=1).astype(BF16)
    return bw, cw


def _ssm_prompt_kernel(u_ref, bw_ref, cw_ref, are_ref, aim_ref, d_ref,
                       y_ref, sre_ref, sim_ref,
                       bure, buim, st_re, st_im, *, nbatch):
    tc = pl.program_id(1)

    @pl.when(tc == 0)
    def _():
        st_re[...] = jnp.zeros_like(st_re)
        st_im[...] = jnp.zeros_like(st_im)

    def slab_ref(c):
        return (bure, c) if c < SLABS else (buim, c - SLABS)

    u_all = u_ref[...].reshape(nbatch * SCAN_T, UB)
    u_bf = u_all.astype(BF16)
    for n in range(SLABS):
        bu = jnp.dot(u_bf, bw_ref[0, :, 2 * n * LANES:(2 * n + 2) * LANES], preferred_element_type=F32)
        for h in range(2):
            ref, c = slab_ref(2 * n + h)
            for b in range(nbatch):
                ref[c, b * SCAN_PITCH:b * SCAN_PITCH + SCAN_T, :] = (
                    bu[b * SCAN_T:(b + 1) * SCAN_T, h * LANES:(h + 1) * LANES])

    for c0 in range(0, SLABS, SCAN_INTERLEAVE):
        slabs = list(range(c0, c0 + SCAN_INTERLEAVE))
        a_re = [jnp.broadcast_to(are_ref[0, c], (nbatch, LANES)) for c in slabs]
        a_im = [jnp.broadcast_to(aim_ref[0, c], (nbatch, LANES)) for c in slabs]
        init = tuple(st_re[c] for c in slabs) + tuple(st_im[c] for c in slabs)

        def step(t, carry, slabs=slabs, a_re=a_re, a_im=a_im):
            n = len(slabs)
            new_re, new_im = [], []
            for k, c in enumerate(slabs):
                s_r, s_i = carry[k], carry[n + k]
                rows = pl.ds(t, nbatch, stride=SCAN_PITCH)
                n_r = a_re[k] * s_r - a_im[k] * s_i + bure[c, rows, :]
                n_i = a_re[k] * s_i + a_im[k] * s_r + buim[c, rows, :]
                bure[c, rows, :] = n_r
                buim[c, rows, :] = n_i
                new_re.append(n_r)
                new_im.append(n_i)
            return tuple(new_re) + tuple(new_im)

        fin = lax.fori_loop(0, SCAN_T, step, init, unroll=4)
        for k, c in enumerate(slabs):
            st_re[c] = fin[k]
            st_im[c] = fin[len(slabs) + k]

    def slab_rows(c):
        ref, c = slab_ref(c)
        return jnp.concatenate(
            [ref[c, b * SCAN_PITCH:b * SCAN_PITCH + SCAN_T, :] for b in range(nbatch)], axis=0)

    y = d_ref[0] * u_all
    for n in range(SLABS):
        s_pair = jnp.concatenate([slab_rows(2 * n), slab_rows(2 * n + 1)], axis=1).astype(BF16)
        y = y + jnp.dot(s_pair, cw_ref[0, 2 * n * LANES:(2 * n + 2) * LANES, :], preferred_element_type=F32)
    y_ref[...] = _gelu(y).astype(y_ref.dtype).reshape(nbatch, SCAN_T, UB)

    @pl.when(tc == pl.num_programs(1) - 1)
    def _():
        for c in range(SLABS):
            sre_ref[:, c * LANES:(c + 1) * LANES] = st_re[c]
            sim_ref[:, c * LANES:(c + 1) * LANES] = st_im[c]


def _ssm_prompt(z3, bw, cw, a_re, a_im, d_skip):
    nbatch, seq, width = z3.shape
    nb = bw.shape[0]
    states = nb * SB
    kern = functools.partial(_ssm_prompt_kernel, nbatch=nbatch)
    return pl.pallas_call(
        kern,
        grid=(nb, seq // SCAN_T),
        in_specs=[
            pl.BlockSpec((nbatch, SCAN_T, UB), lambda g, t: (0, t, g)),
            pl.BlockSpec((1, UB, 2 * SB), lambda g, t: (g, 0, 0)),
            pl.BlockSpec((1, 2 * SB, UB), lambda g, t: (g, 0, 0)),
            pl.BlockSpec((1, SLABS, 1, LANES), lambda g, t: (g, 0, 0, 0)),
            pl.BlockSpec((1, SLABS, 1, LANES), lambda g, t: (g, 0, 0, 0)),
            pl.BlockSpec((1, 1, UB), lambda g, t: (g, 0, 0)),
        ],
        out_specs=(
            pl.BlockSpec((nbatch, SCAN_T, UB), lambda g, t: (0, t, g)),
            pl.BlockSpec((nbatch, SB), lambda g, t: (0, g)),
            pl.BlockSpec((nbatch, SB), lambda g, t: (0, g)),
        ),
        out_shape=(
            jax.ShapeDtypeStruct((nbatch, seq, width), BF16),
            jax.ShapeDtypeStruct((nbatch, states), F32),
            jax.ShapeDtypeStruct((nbatch, states), F32),
        ),
        scratch_shapes=[
            pltpu.VMEM((SLABS, nbatch * SCAN_PITCH, LANES), F32),
            pltpu.VMEM((SLABS, nbatch * SCAN_PITCH, LANES), F32),
            pltpu.VMEM((SLABS, nbatch, LANES), F32),
            pltpu.VMEM((SLABS, nbatch, LANES), F32),
        ],
        compiler_params=_cparams(("arbitrary", "arbitrary")),
        name="ssm_prompt",
    )(z3, bw, cw, a_re.reshape(nb, SLABS, 1, LANES), a_im.reshape(nb, SLABS, 1, LANES),
      d_skip.reshape(nb, 1, UB))


def _ssm_sample_kernel(u_ref, bw_ref, cw_ref, are_ref, aim_ref, d_ref, x0re_ref, x0im_ref,
                       y_ref, sre_ref, sim_ref):
    u = u_ref[...]
    bu = jnp.dot(u.astype(BF16), bw_ref[0], preferred_element_type=F32)
    a_re = are_ref[0]
    a_im = aim_ref[0]
    x_re = x0re_ref[...]
    x_im = x0im_ref[...]
    s_re = a_re * x_re - a_im * x_im + bu[:, :SB]
    s_im = a_re * x_im + a_im * x_re + bu[:, SB:]
    sre_ref[...] = s_re
    sim_ref[...] = s_im
    s_cat = jnp.concatenate([s_re, s_im], axis=1).astype(BF16)
    y = jnp.dot(s_cat, cw_ref[0], preferred_element_type=F32) + d_ref[0] * u
    y_ref[...] = _gelu(y).astype(y_ref.dtype)


def _ssm_sample(z, bw, cw, a_re, a_im, d_skip, x0_re, x0_im):
    nbatch = z.shape[0]
    nb = bw.shape[0]
    return pl.pallas_call(
        _ssm_sample_kernel,
        grid=(nb,),
        in_specs=[
            pl.BlockSpec((nbatch, UB), lambda g: (0, g)),
            pl.BlockSpec((1, UB, 2 * SB), lambda g: (g, 0, 0)),
            pl.BlockSpec((1, 2 * SB, UB), lambda g: (g, 0, 0)),
            pl.BlockSpec((1, 1, SB), lambda g: (g, 0, 0)),
            pl.BlockSpec((1, 1, SB), lambda g: (g, 0, 0)),
            pl.BlockSpec((1, 1, UB), lambda g: (g, 0, 0)),
            pl.BlockSpec((nbatch, SB), lambda g: (0, g)),
            pl.BlockSpec((nbatch, SB), lambda g: (0, g)),
        ],
        out_specs=(
            pl.BlockSpec((nbatch, UB), lambda g: (0, g)),
            pl.BlockSpec((nbatch, SB), lambda g: (0, g)),
            pl.BlockSpec((nbatch, SB), lambda g: (0, g)),
        ),
        out_shape=(
            jax.ShapeDtypeStruct((nbatch, nb * UB), BF16),
            jax.ShapeDtypeStruct((nbatch, nb * SB), F32),
            jax.ShapeDtypeStruct((nbatch, nb * SB), F32),
        ),
        compiler_params=_cparams(("arbitrary",)),
        name="ssm_sample",
    )(z, bw, cw, a_re.reshape(nb, 1, SB), a_im.reshape(nb, 1, SB), d_skip.reshape(nb, 1, UB),
      x0_re, x0_im)


def _gmlp_prompt_kernel(gu_ref, gv_ref, gnv_ref, ws_ref, bias_ref, wo_ref, gate_ref, p_ref,
                        out_ref, s_scr, wt_scr, *, tm, heads):
    i = pl.program_id(0)
    j = pl.program_id(1)

    @pl.when(jnp.logical_and(i == 0, j == 0))
    def _():
        row = lax.broadcasted_iota(jnp.int32, (CHUNK, CHUNK), 0)
        col = lax.broadcasted_iota(jnp.int32, (CHUNK, CHUNK), 1)
        mask = (col <= row).astype(F32)
        for g in range(heads):
            wt_scr[g] = (ws_ref[g] * mask).astype(BF16)

    @pl.when(j == 0)
    def _():
        gnv = gnv_ref[...]

        def body(c, carry):
            r0 = pl.multiple_of(c * CHUNK, CHUNK)
            gv = gv_ref[pl.ds(r0, CHUNK), :]
            r = lax.rsqrt(jnp.mean(gv * gv, axis=-1, keepdims=True) + EPS)
            v = (gv * r * gnv).astype(BF16)
            for g in range(heads):
                sl = slice(g * GMLP_HEAD, (g + 1) * GMLP_HEAD)
                mixed = jnp.dot(wt_scr[g], v[:, sl], preferred_element_type=F32) + bias_ref[:, sl]
                s_scr[pl.ds(r0, CHUNK), sl] = (gu_ref[pl.ds(r0, CHUNK), sl] * mixed).astype(BF16)
            return carry

        lax.fori_loop(0, tm // CHUNK, body, 0)

    yb = jnp.dot(s_scr[...], wo_ref[...], preferred_element_type=F32)
    out_ref[...] = (p_ref[...] + gate_ref[...] * yb).astype(out_ref.dtype)


def _gmlp_prompt(guv, gnv, w_s, bias_full, w_gout, gates, p, *, gate_col, tm, tn):
    m = guv.shape[0]
    width, d = w_gout.shape
    heads = w_s.shape[0]
    kern = functools.partial(_gmlp_prompt_kernel, tm=tm, heads=heads)
    return pl.pallas_call(
        kern,
        grid=(m // tm, d // tn),
        in_specs=[
            pl.BlockSpec((tm, width), lambda i, j: (i, 0)),
            pl.BlockSpec((tm, width), lambda i, j: (i, 1)),
            pl.BlockSpec((1, width), lambda i, j: (0, 0)),
            pl.BlockSpec((heads, CHUNK, CHUNK), lambda i, j: (0, 0, 0)),
            pl.BlockSpec((CHUNK, width), lambda i, j: (0, 0)),
            pl.BlockSpec((width, tn), lambda i, j: (0, j)),
            pl.BlockSpec((tm, tn), lambda i, j: (i, j + gate_col // tn)),
            pl.BlockSpec((tm, tn), lambda i, j: (i, j)),
        ],
        out_specs=pl.BlockSpec((tm, tn), lambda i, j: (i, j)),
        out_shape=jax.ShapeDtypeStruct((m, d), BF16),
        scratch_shapes=[pltpu.VMEM((tm, width), BF16), pltpu.VMEM((heads, CHUNK, CHUNK), BF16)],
        compiler_params=_cparams(("arbitrary", "arbitrary")),
        name="gmlp_prompt",
    )(guv, guv, gnv, w_s, bias_full, w_gout, gates, p)


def _gmlp_sample_kernel(gu_ref, gv_ref, gnv_ref, wdiag_ref, bias_ref, wo_ref, gate_ref, p_ref,
                        out_ref, v_ref, wemit_ref, s_scr):
    j = pl.program_id(0)

    @pl.when(j == 0)
    def _():
        gv = gv_ref[...]
        r = lax.rsqrt(jnp.mean(gv * gv, axis=-1, keepdims=True) + EPS)
        v = gv * r * gnv_ref[...]
        v_ref[...] = v
        mixed = wdiag_ref[...] * v + bias_ref[...]
        s_scr[...] = (gu_ref[...] * mixed).astype(BF16)

    yb = jnp.dot(s_scr[...], _as_bf16(wo_ref, wemit_ref), preferred_element_type=F32)
    out_ref[...] = (p_ref[...] + gate_ref[...] * yb).astype(out_ref.dtype)


def _gmlp_sample(guv, gnv, wdiag, bias0, w_gout, gates, p, *, gate_col, tn):
    m = guv.shape[0]
    width, d = w_gout.shape
    return pl.pallas_call(
        _gmlp_sample_kernel,
        grid=(d // tn,),
        in_specs=[
            pl.BlockSpec((m, width), lambda j: (0, 0)),
            pl.BlockSpec((m, width), lambda j: (0, 1)),
            pl.BlockSpec((1, width), lambda j: (0, 0)),
            pl.BlockSpec((1, width), lambda j: (0, 0)),
            pl.BlockSpec((1, width), lambda j: (0, 0)),
            pl.BlockSpec((width, tn), lambda j: (0, j)),
            pl.BlockSpec((m, tn), lambda j: (0, j + gate_col // tn)),
            pl.BlockSpec((m, tn), lambda j: (0, j)),
        ],
        out_specs=(pl.BlockSpec((m, tn), lambda j: (0, j)),
                   pl.BlockSpec((m, width), lambda j: (0, 0)),
                   pl.BlockSpec((width, tn), lambda j: (0, j))),
        out_shape=(jax.ShapeDtypeStruct((m, d), BF16), jax.ShapeDtypeStruct((m, width), F32),
                   jax.ShapeDtypeStruct((width, d), BF16)),
        scratch_shapes=[pltpu.VMEM((m, width), BF16)],
        compiler_params=_cparams(("arbitrary",)),
        name="gmlp_sample_emit",
    )(guv, guv, gnv, wdiag, bias0, w_gout, gates, p)


def _layer(x, prm, w, *, nbatch, x0, gfinal, tm_ffn, tm_mm):
    sample = x0 is not None
    d_model = x.shape[1]
    ssm_width = prm["bw"].shape[0] * UB
    gmlp_width = prm["gnv"].shape[1]
    o1 = ssm_width
    o3 = o1 + 2 * gmlp_width
    tn = 512
    wb = {}

    res = _ffn(x, prm["n1"], w["f1g"], w["f1u"], w["f1d"], prm["nmix"],
               tail="norm_out", tm=tm_ffn, tf=256, emit=sample)
    x, h2 = res[0], res[1]
    if sample:
        wb["f1g"], wb["f1u"], wb["f1d"] = res[2:]

    cols = {}
    if sample:
        w = dict(w, win_ssm=w["w_in"], win_uv=w["w_in"], win_gate=w["w_in"],
                 glu_a=w["w_glu"], glu_b=w["w_glu"])
        cols = dict(win_uv=o1, win_gate=o3, glu_b=d_model)

    def mm(lhs, wkeys, extras, kind, n_out, out_dtype, name):
        out, emitted = _mm(lhs, [(w[k], cols.get(k, 0)) for k in wkeys], extras, kind=kind,
                           n_out=n_out, out_dtype=out_dtype, tm=tm_mm, tn=tn, emit=sample, name=name)
        wb.update(zip(wkeys, emitted))
        return out

    z_ssm = mm(h2, ["win_ssm"], [], "identity", o1, F32, "w_in_ssm")
    guv = mm(h2, ["win_uv"], [], "gelu", 2 * gmlp_width, F32, "w_in_uv")
    gates = mm(h2, ["win_gate"], [], "sigmoid", 2 * d_model, F32, "w_in_gate")

    if sample:
        y, s_re, s_im = _ssm_sample(z_ssm, prm["bw"], prm["cw"], prm["a_re"], prm["a_im"],
                                    prm["d_skip"], x0[0], x0[1])
    else:
        seq = x.shape[0] // nbatch
        y, s_re, s_im = _ssm_prompt(z_ssm.reshape(nbatch, seq, ssm_width), prm["bw"], prm["cw"],
                                    prm["a_re"], prm["a_im"], prm["d_skip"])
        y = y.reshape(nbatch * seq, ssm_width)
    p = mm(y, ["glu_a", "glu_b"], [(gates, 0)], "gated_glu", d_model, F32, "ssm_glu")

    if sample:
        merged, v_rows, wb["w_gout"] = _gmlp_sample(guv, prm["gnv"], prm["w_diag"], prm["bias0"],
                                                    w["w_gout"], gates, p, gate_col=d_model, tn=tn)
    else:
        merged = _gmlp_prompt(guv, prm["gnv"], prm["w_s"], prm["bias_full"], w["w_gout"], gates, p,
                              gate_col=d_model, tm=tm_mm, tn=tn)
        v_rows = None
    x = mm(merged, ["w_out"], [(x, 0)], "residual", d_model, F32, "out_proj")
    res = _ffn(x, prm["n2"], w["f2g"], w["f2u"], w["f2d"], gfinal,
               tail="final", tm=tm_ffn, tf=256, emit=sample)
    if sample:
        wb["f2g"], wb["f2u"], wb["f2d"] = res[1:]
    return res[0], s_re, s_im, v_rows, wb


def kernel(x_prompt, x_sample, state_ssm_re, state_ssm_im, norm_ffn1, ffn1_gate, ffn1_up, ffn1_down, norm_mix, w_in, ssm_lambda_re, ssm_lambda_im, ssm_log_dt, ssm_b_re, ssm_b_im, ssm_c_re, ssm_c_im, ssm_d, ssm_w_glu, gmlp_norm_v, gmlp_w_s, gmlp_b_s, gmlp_w_out, w_out, norm_ffn2, ffn2_gate, ffn2_up, ffn2_down, norm_final):
    depth = w_in.shape[0]
    assert depth == 1, "the final RMSNorm is fused into the last FFN; one layer per step"
    batch, seq, d_model = x_prompt.shape
    dec_batch, dec_seq, _ = x_sample.shape
    assert dec_seq == 1
    groups, states = ssm_lambda_re.shape[1:]
    gmlp_width = gmlp_norm_v.shape[1]
    heads = gmlp_w_s.shape[1]
    head_dim = gmlp_width // heads
    assert seq % SCAN_T == 0 and seq % CHUNK == 0 and head_dim == GMLP_HEAD
    assert ssm_b_re.shape[-1] == SSM_GROUP and states == SSM_STATE and groups % GROUPS_PER_BLOCK == 0

    l = 0
    yp = x_prompt.reshape(batch * seq, d_model)
    ys = x_sample.reshape(dec_batch, d_model)
    gfin = norm_final.reshape(1, d_model)
    a_re, a_im, bb_re, bb_im = _discretise(ssm_lambda_re[l], ssm_lambda_im[l], ssm_log_dt[l],
                                           ssm_b_re[l], ssm_b_im[l])
    bw, cw = _block_diag_weights(bb_re, bb_im, ssm_c_re[l], ssm_c_im[l])
    prm = dict(
        n1=norm_ffn1[l].reshape(1, d_model), nmix=norm_mix[l].reshape(1, d_model),
        n2=norm_ffn2[l].reshape(1, d_model),
        bw=bw, cw=cw, a_re=a_re, a_im=a_im, d_skip=ssm_d[l],
        gnv=gmlp_norm_v[l].reshape(1, gmlp_width), w_s=gmlp_w_s[l],
        bias_full=jnp.repeat(gmlp_b_s[l].T, head_dim, axis=1),
        w_diag=jnp.repeat(gmlp_w_s[l][:, 0, 0], head_dim).reshape(1, gmlp_width),
        bias0=jnp.repeat(gmlp_b_s[l][:, 0], head_dim).reshape(1, gmlp_width),
    )
    w_f32 = dict(f1g=ffn1_gate[l], f1u=ffn1_up[l], f1d=ffn1_down[l], w_in=w_in[l],
                 w_glu=ssm_w_glu[l], w_gout=gmlp_w_out[l], w_out=w_out[l],
                 f2g=ffn2_gate[l], f2u=ffn2_up[l], f2d=ffn2_down[l])
    x0 = (state_ssm_re[l].reshape(dec_batch, groups * states),
          state_ssm_im[l].reshape(dec_batch, groups * states))
    ys, sr, si, vr, w_bf16 = _layer(ys, prm, w_f32, nbatch=dec_batch, x0=x0, gfinal=gfin,
                                    tm_ffn=dec_batch, tm_mm=dec_batch)
    yp, pr, pi, _, _ = _layer(yp, prm, w_bf16, nbatch=batch, x0=None, gfinal=gfin,
                              tm_ffn=512, tm_mm=1024)
    return (yp.reshape(batch, seq, d_model), ys.reshape(dec_batch, dec_seq, d_model),
            pr.reshape(1, batch, groups, states), pi.reshape(1, batch, groups, states),
            sr.reshape(1, dec_batch, groups, states), si.reshape(1, dec_batch, groups, states),
            vr.reshape(1, dec_batch, dec_seq, gmlp_width))
```

```python
import functools
import math

import jax
import jax.numpy as jnp
from jax import lax
from jax.experimental import pallas as pl
from jax.experimental.pallas import tpu as pltpu

F32 = jnp.float32
BF16 = jnp.bfloat16

EPS = 1e-6
LANES = 128
SSM_GROUP = 16
SSM_STATE = 64
GROUPS_PER_BLOCK = 16
UB = GROUPS_PER_BLOCK * SSM_GROUP
SB = GROUPS_PER_BLOCK * SSM_STATE
SLABS = SB // LANES
SCAN_T = 256
SCAN_PITCH = SCAN_T + 8
SCAN_INTERLEAVE = 4
CHUNK = 128
GMLP_HEAD = 128
VMEM_LIMIT = 56 * 1024 * 1024


def _cparams(sem):
    return pltpu.CompilerParams(dimension_semantics=sem, vmem_limit_bytes=VMEM_LIMIT)


def _gelu(x):
    c = math.sqrt(2.0 / math.pi)
    return 0.5 * x * (1.0 + jnp.tanh(c * (x + 0.044715 * (x * x * x))))


def _sigmoid(x):
    return 1.0 / (1.0 + jnp.exp(-x))


def _as_bf16(w_ref, emit_ref):
    w = w_ref[0] if len(w_ref.shape) == 3 else w_ref[...]
    if w.dtype != BF16:
        w = w.astype(BF16)
    if emit_ref is not None:
        if len(emit_ref.shape) == 3:
            emit_ref[0] = w
        else:
            emit_ref[...] = w
    return w


def _col_tile_spec(w, k, tn, col0=0):
    if w.ndim == 3:
        assert w.shape[1:] == (k, tn) and col0 == 0
        return pl.BlockSpec((1, k, tn), lambda i, j: (j, 0, 0))
    return pl.BlockSpec((k, tn), lambda i, j, o=col0 // tn: (0, j + o))


def _ffn_kernel(*refs, tm, d_model, col_chunk, row_chunk, tail, emit):
    x_ref, gn_ref, wg_ref, wu_ref, wd_ref, gt_ref = refs[:6]
    rest = list(refs[6:])
    out_ref = rest.pop(0)
    h2_ref = rest.pop(0) if tail == "norm_out" else None
    emits = [rest.pop(0) for _ in range(3)] if emit else [None] * 3
    (h_scr,) = rest
    j = pl.program_id(1)

    @pl.when(j == 0)
    def _():
        g = gn_ref[...]

        def body(c, carry):
            r0 = pl.multiple_of(c * row_chunk, row_chunk)
            xf = x_ref[pl.ds(r0, row_chunk), :]
            r = lax.rsqrt(jnp.mean(xf * xf, axis=-1, keepdims=True) + EPS)
            h_scr[pl.ds(r0, row_chunk), :] = (xf * r * g).astype(BF16)
            return carry

        lax.fori_loop(0, tm // row_chunk, body, 0)
        out_ref[...] = jnp.zeros_like(out_ref)

    h = h_scr[...]
    g = jnp.dot(h, _as_bf16(wg_ref, emits[0]), preferred_element_type=F32)
    u = jnp.dot(h, _as_bf16(wu_ref, emits[1]), preferred_element_type=F32)
    a = (g * _sigmoid(g) * u).astype(BF16)
    wd = _as_bf16(wd_ref, emits[2])
    for n in range(d_model // col_chunk):
        sl = slice(n * col_chunk, (n + 1) * col_chunk)
        out_ref[:, sl] += jnp.dot(a, wd[:, sl], preferred_element_type=F32)

    @pl.when(j == pl.num_programs(1) - 1)
    def _():
        gt = gt_ref[...]

        def body(c, carry):
            r0 = pl.multiple_of(c * row_chunk, row_chunk)
            y = x_ref[pl.ds(r0, row_chunk), :] + 0.5 * out_ref[pl.ds(r0, row_chunk), :]
            r = lax.rsqrt(jnp.mean(y * y, axis=-1, keepdims=True) + EPS)
            if tail == "norm_out":
                out_ref[pl.ds(r0, row_chunk), :] = y
                h2_ref[pl.ds(r0, row_chunk), :] = (y * r * gt).astype(BF16)
            else:
                out_ref[pl.ds(r0, row_chunk), :] = y * r * gt
            return carry

        lax.fori_loop(0, tm // row_chunk, body, 0)


def _ffn(x, gn, wg, wu, wd, gtail, *, tail, tm, tf, emit):
    m, d = x.shape
    f = wd.shape[0]
    assert not emit or m == tm
    kern = functools.partial(_ffn_kernel, tm=tm, d_model=d, col_chunk=512,
                             row_chunk=min(tm, 64), tail=tail, emit=emit)
    out_specs = [pl.BlockSpec((tm, d), lambda i, j: (i, 0))]
    out_shape = [jax.ShapeDtypeStruct((m, d), F32)]
    if tail == "norm_out":
        out_specs.append(pl.BlockSpec((tm, d), lambda i, j: (i, 0)))
        out_shape.append(jax.ShapeDtypeStruct((m, d), BF16))
    if emit:
        out_specs += [pl.BlockSpec((1, d, tf), lambda i, j: (j, 0, 0)),
                      pl.BlockSpec((1, d, tf), lambda i, j: (j, 0, 0)),
                      pl.BlockSpec((tf, d), lambda i, j: (j, 0))]
        out_shape += [jax.ShapeDtypeStruct((f // tf, d, tf), BF16),
                      jax.ShapeDtypeStruct((f // tf, d, tf), BF16),
                      jax.ShapeDtypeStruct((f, d), BF16)]
    return pl.pallas_call(
        kern,
        grid=(m // tm, f // tf),
        in_specs=[
            pl.BlockSpec((tm, d), lambda i, j: (i, 0), pipeline_mode=pl.Buffered(1)),
            pl.BlockSpec((1, d), lambda i, j: (0, 0)),
            _col_tile_spec(wg, d, tf),
            _col_tile_spec(wu, d, tf),
            pl.BlockSpec((tf, d), lambda i, j: (j, 0)),
            pl.BlockSpec((1, d), lambda i, j: (0, 0)),
        ],
        out_specs=out_specs,
        out_shape=out_shape,
        scratch_shapes=[pltpu.VMEM((tm, d), BF16)],
        compiler_params=_cparams(("arbitrary", "arbitrary")),
        name="ffn_" + tail + ("_emit" if emit else ""),
    )(x, gn, wg, wu, wd, gtail)


def _mm_epilogue(kind, accs, extras):
    if kind == "identity":
        return accs[0]
    if kind == "gelu":
        return _gelu(accs[0])
    if kind == "sigmoid":
        return _sigmoid(accs[0])
    if kind == "gated_glu":
        return extras[0] * (accs[0] * _sigmoid(accs[1]))
    if kind == "residual":
        return extras[0] + accs[0]
    raise ValueError(kind)


def _mm_kernel(*refs, n_w, n_extra, kind, emit):
    lhs_ref = refs[0]
    w_refs = refs[1:1 + n_w]
    extra_refs = refs[1 + n_w:1 + n_w + n_extra]
    out_ref = refs[1 + n_w + n_extra]
    emit_refs = refs[2 + n_w + n_extra:] if emit else [None] * n_w
    lhs = lhs_ref[...]
    accs = [jnp.dot(lhs, _as_bf16(w, e), preferred_element_type=F32) for w, e in zip(w_refs, emit_refs)]
    out_ref[...] = _mm_epilogue(kind, accs, [e[...] for e in extra_refs]).astype(out_ref.dtype)


def _mm(lhs, weights, extras, *, kind, n_out, out_dtype, tm, tn, emit, name):
    m, k = lhs.shape
    assert not emit or m == tm
    in_specs = [pl.BlockSpec((tm, k), lambda i, j: (i, 0))]
    args = [lhs]
    for w, c0 in weights:
        in_specs.append(_col_tile_spec(w, k, tn, c0))
        args.append(w)
    for e, c0 in extras:
        in_specs.append(pl.BlockSpec((tm, tn), lambda i, j, o=c0 // tn: (i, j + o)))
        args.append(e)
    out_specs = [pl.BlockSpec((tm, tn), lambda i, j: (i, j))]
    out_shape = [jax.ShapeDtypeStruct((m, n_out), out_dtype)]
    if emit:
        out_specs += [pl.BlockSpec((1, k, tn), lambda i, j: (j, 0, 0)) for _ in weights]
        out_shape += [jax.ShapeDtypeStruct((n_out // tn, k, tn), BF16) for _ in weights]
    kern = functools.partial(_mm_kernel, n_w=len(weights), n_extra=len(extras), kind=kind, emit=emit)
    res = pl.pallas_call(
        kern,
        grid=(m // tm, n_out // tn),
        in_specs=in_specs,
        out_specs=out_specs,
        out_shape=out_shape,
        compiler_params=_cparams(("arbitrary", "arbitrary")),
        name=name + ("_emit" if emit else ""),
    )(*args)
    return res[0], list(res[1:])


def _disc_kernel(lre_ref, lim_ref, ldt_ref, bre_ref, bim_ref, are_ref, aim_ref, bbre_ref, bbim_ref):
    lam_re = lre_ref[...]
    lam_im = lim_ref[...]
    dt = jnp.exp(ldt_ref[...])
    mag = jnp.exp(dt * lam_re)
    a_re = mag * jnp.cos(dt * lam_im)
    a_im = mag * jnp.sin(dt * lam_im)
    nr, ni = a_re - 1.0, a_im
    den = lam_re * lam_re + lam_im * lam_im
    coef_re = (nr * lam_re + ni * lam_im) / den
    coef_im = (ni * lam_re - nr * lam_im) / den
    are_ref[...] = a_re
    aim_ref[...] = a_im
    b_re = bre_ref[...]
    b_im = bim_ref[...]
    bbre_ref[...] = coef_re[None] * b_re - coef_im[None] * b_im
    bbim_ref[...] = coef_re[None] * b_im + coef_im[None] * b_re


def _discretise(lam_re, lam_im, log_dt, b_re, b_im):
    g, p = lam_re.shape
    h = b_re.shape[-1]
    b_re_t = jnp.transpose(b_re, (2, 0, 1))
    b_im_t = jnp.transpose(b_im, (2, 0, 1))
    return pl.pallas_call(
        _disc_kernel,
        out_shape=(jax.ShapeDtypeStruct((g, p), F32), jax.ShapeDtypeStruct((g, p), F32),
                   jax.ShapeDtypeStruct((h, g, p), F32), jax.ShapeDtypeStruct((h, g, p), F32)),
        name="s5_discretise",
    )(lam_re, lam_im, log_dt.reshape(g, 1), b_re_t, b_im_t)


def _block_diag_weights(bb_re, bb_im, c_re, c_im):
    h, g, p = bb_re.shape
    nb = g // GROUPS_PER_BLOCK
    eye = jnp.eye(GROUPS_PER_BLOCK, dtype=F32)

    def in_proj(bb):
        x = jnp.transpose(bb, (1, 0, 2)).reshape(nb, GROUPS_PER_BLOCK, h, p)
        x = x[:, :, :, None, :] * eye[None, :, None, :, None]
        return x.reshape(nb, UB, SB)

    def out_proj(c):
        x = jnp.transpose(c.reshape(nb, GROUPS_PER_BLOCK, h, p), (0, 1, 3, 2))
        x = x[:, :, :, None, :] * eye[None, :, None, :, None]
        return x.reshape(nb, SB, UB)

    bw = jnp.concatenate([in_proj(bb_re), in_proj(bb_im)], axis=2).astype(BF16)
    cw = jnp.concatenate([out_proj(c_re), -out_proj(c_im)], axis=1).astype(BF16)
    return bw, cw


def _ssm_prompt_kernel(u_ref, bw_ref, cw_ref, are_ref, aim_ref, d_ref,
                       y_ref, sre_ref, sim_ref,
                       bure, buim, st_re, st_im, *, nbatch):
    tc = pl.program_id(1)

    @pl.when(tc == 0)
    def _():
        st_re[...] = jnp.zeros_like(st_re)
        st_im[...] = jnp.zeros_like(st_im)

    def slab_ref(c):
        return (bure, c) if c < SLABS else (buim, c - SLABS)

    u_all = u_ref[...].reshape(nbatch * SCAN_T, UB)
    u_bf = u_all.astype(BF16)
    per_dot = 4
    for n in range(2 * SLABS // per_dot):
        bu = jnp.dot(u_bf, bw_ref[0, :, per_dot * n * LANES:per_dot * (n + 1) * LANES],
                     preferred_element_type=F32)
        for h in range(per_dot):
            ref, c = slab_ref(per_dot * n + h)
            for b in range(nbatch):
                ref[c, b * SCAN_PITCH:b * SCAN_PITCH + SCAN_T, :] = (
                    bu[b * SCAN_T:(b + 1) * SCAN_T, h * LANES:(h + 1) * LANES])

    for c0 in range(0, SLABS, SCAN_INTERLEAVE):
        slabs = list(range(c0, c0 + SCAN_INTERLEAVE))
        a_re = [jnp.broadcast_to(are_ref[0, c], (nbatch, LANES)) for c in slabs]
        a_im = [jnp.broadcast_to(aim_ref[0, c], (nbatch, LANES)) for c in slabs]
        init = tuple(st_re[c] for c in slabs) + tuple(st_im[c] for c in slabs)

        def step(t, carry, slabs=slabs, a_re=a_re, a_im=a_im):
            n = len(slabs)
            new_re, new_im = [], []
            for k, c in enumerate(slabs):
                s_r, s_i = carry[k], carry[n + k]
                rows = pl.ds(t, nbatch, stride=SCAN_PITCH)
                n_r = a_re[k] * s_r - a_im[k] * s_i + bure[c, rows, :]
                n_i = a_re[k] * s_i + a_im[k] * s_r + buim[c, rows, :]
                bure[c, rows, :] = n_r
                buim[c, rows, :] = n_i
                new_re.append(n_r)
                new_im.append(n_i)
            return tuple(new_re) + tuple(new_im)

        fin = lax.fori_loop(0, SCAN_T, step, init, unroll=4)
        for k, c in enumerate(slabs):
            st_re[c] = fin[k]
            st_im[c] = fin[len(slabs) + k]

    d = d_ref[0]
    for b in range(nbatch):
        r0 = b * SCAN_PITCH
        y = d * u_ref[b]
        for n in range(SLABS):
            (ref0, c0), (ref1, c1) = slab_ref(2 * n), slab_ref(2 * n + 1)
            s_pair = jnp.concatenate([ref0[c0, r0:r0 + SCAN_T, :], ref1[c1, r0:r0 + SCAN_T, :]],
                                     axis=1).astype(BF16)
            y = y + jnp.dot(s_pair, cw_ref[0, 2 * n * LANES:(2 * n + 2) * LANES, :],
                            preferred_element_type=F32)
        y_ref[b] = _gelu(y).astype(y_ref.dtype)

    @pl.when(tc == pl.num_programs(1) - 1)
    def _():
        for c in range(SLABS):
            sre_ref[:, c * LANES:(c + 1) * LANES] = st_re[c]
            sim_ref[:, c * LANES:(c + 1) * LANES] = st_im[c]


def _ssm_prompt(z3, bw, cw, a_re, a_im, d_skip):
    nbatch, seq, width = z3.shape
    nb = bw.shape[0]
    states = nb * SB
    kern = functools.partial(_ssm_prompt_kernel, nbatch=nbatch)
    return pl.pallas_call(
        kern,
        grid=(nb, seq // SCAN_T),
        in_specs=[
            pl.BlockSpec((nbatch, SCAN_T, UB), lambda g, t: (0, t, g)),
            pl.BlockSpec((1, UB, 2 * SB), lambda g, t: (g, 0, 0)),
            pl.BlockSpec((1, 2 * SB, UB), lambda g, t: (g, 0, 0)),
            pl.BlockSpec((1, SLABS, 1, LANES), lambda g, t: (g, 0, 0, 0)),
            pl.BlockSpec((1, SLABS, 1, LANES), lambda g, t: (g, 0, 0, 0)),
            pl.BlockSpec((1, 1, UB), lambda g, t: (g, 0, 0)),
        ],
        out_specs=(
            pl.BlockSpec((nbatch, SCAN_T, UB), lambda g, t: (0, t, g)),
            pl.BlockSpec((nbatch, SB), lambda g, t: (0, g)),
            pl.BlockSpec((nbatch, SB), lambda g, t: (0, g)),
        ),
        out_shape=(
            jax.ShapeDtypeStruct((nbatch, seq, width), BF16),
            jax.ShapeDtypeStruct((nbatch, states), F32),
            jax.ShapeDtypeStruct((nbatch, states), F32),
        ),
        scratch_shapes=[
            pltpu.VMEM((SLABS, nbatch * SCAN_PITCH, LANES), F32),
            pltpu.VMEM((SLABS, nbatch * SCAN_PITCH, LANES), F32),
            pltpu.VMEM((SLABS, nbatch, LANES), F32),
            pltpu.VMEM((SLABS, nbatch, LANES), F32),
        ],
        compiler_params=_cparams(("arbitrary", "arbitrary")),
        name="ssm_prompt",
    )(z3, bw, cw, a_re.reshape(nb, SLABS, 1, LANES), a_im.reshape(nb, SLABS, 1, LANES),
      d_skip.reshape(nb, 1, UB))


def _ssm_sample_kernel(u_ref, bw_ref, cw_ref, are_ref, aim_ref, d_ref, x0re_ref, x0im_ref,
                       y_ref, sre_ref, sim_ref):
    u = u_ref[...]
    bu = jnp.dot(u.astype(BF16), bw_ref[0], preferred_element_type=F32)
    a_re = are_ref[0]
    a_im = aim_ref[0]
    x_re = x0re_ref[...]
    x_im = x0im_ref[...]
    s_re = a_re * x_re - a_im * x_im + bu[:, :SB]
    s_im = a_re * x_im + a_im * x_re + bu[:, SB:]
    sre_ref[...] = s_re
    sim_ref[...] = s_im
    s_cat = jnp.concatenate([s_re, s_im], axis=1).astype(BF16)
    y = jnp.dot(s_cat, cw_ref[0], preferred_element_type=F32) + d_ref[0] * u
    y_ref[...] = _gelu(y).astype(y_ref.dtype)


def _ssm_sample(z, bw, cw, a_re, a_im, d_skip, x0_re, x0_im):
    nbatch = z.shape[0]
    nb = bw.shape[0]
    return pl.pallas_call(
        _ssm_sample_kernel,
        grid=(nb,),
        in_specs=[
            pl.BlockSpec((nbatch, UB), lambda g: (0, g)),
            pl.BlockSpec((1, UB, 2 * SB), lambda g: (g, 0, 0)),
            pl.BlockSpec((1, 2 * SB, UB), lambda g: (g, 0, 0)),
            pl.BlockSpec((1, 1, SB), lambda g: (g, 0, 0)),
            pl.BlockSpec((1, 1, SB), lambda g: (g, 0, 0)),
            pl.BlockSpec((1, 1, UB), lambda g: (g, 0, 0)),
            pl.BlockSpec((nbatch, SB), lambda g: (0, g)),
            pl.BlockSpec((nbatch, SB), lambda g: (0, g)),
        ],
        out_specs=(
            pl.BlockSpec((nbatch, UB), lambda g: (0, g)),
            pl.BlockSpec((nbatch, SB), lambda g: (0, g)),
            pl.BlockSpec((nbatch, SB), lambda g: (0, g)),
        ),
        out_shape=(
            jax.ShapeDtypeStruct((nbatch, nb * UB), BF16),
            jax.ShapeDtypeStruct((nbatch, nb * SB), F32),
            jax.ShapeDtypeStruct((nbatch, nb * SB), F32),
        ),
        compiler_params=_cparams(("arbitrary",)),
        name="ssm_sample",
    )(z, bw, cw, a_re.reshape(nb, 1, SB), a_im.reshape(nb, 1, SB), d_skip.reshape(nb, 1, UB),
      x0_re, x0_im)


def _gmlp_prompt_kernel(gu_ref, gv_ref, gnv_ref, ws_ref, bias_ref, wo_ref, gate_ref, p_ref,
                        out_ref, s_scr, wt_scr, *, tm, heads):
    i = pl.program_id(0)
    j = pl.program_id(1)

    @pl.when(jnp.logical_and(i == 0, j == 0))
    def _():
        row = lax.broadcasted_iota(jnp.int32, (CHUNK, CHUNK), 0)
        col = lax.broadcasted_iota(jnp.int32, (CHUNK, CHUNK), 1)
        mask = (col <= row).astype(F32)
        for g in range(heads):
            wt_scr[g] = (ws_ref[g] * mask).astype(BF16)

    @pl.when(j == 0)
    def _():
        gnv = gnv_ref[...]

        def body(c, carry):
            r0 = pl.multiple_of(c * CHUNK, CHUNK)
            gv = gv_ref[pl.ds(r0, CHUNK), :]
            r = lax.rsqrt(jnp.mean(gv * gv, axis=-1, keepdims=True) + EPS)
            v = (gv * r * gnv).astype(BF16)
            for g in range(heads):
                sl = slice(g * GMLP_HEAD, (g + 1) * GMLP_HEAD)
                mixed = jnp.dot(wt_scr[g], v[:, sl], preferred_element_type=F32) + bias_ref[:, sl]
                s_scr[pl.ds(r0, CHUNK), sl] = (gu_ref[pl.ds(r0, CHUNK), sl] * mixed).astype(BF16)
            return carry

        lax.fori_loop(0, tm // CHUNK, body, 0)

    yb = jnp.dot(s_scr[...], wo_ref[0], preferred_element_type=F32)
    out_ref[...] = (p_ref[...] + gate_ref[...] * yb).astype(out_ref.dtype)


def _gmlp_prompt(guv, gnv, w_s, bias_full, w_gout, gates, p, *, gate_col, tm, tn):
    m = guv.shape[0]
    ntiles, width, _ = w_gout.shape
    d = ntiles * tn
    heads = w_s.shape[0]
    kern = functools.partial(_gmlp_prompt_kernel, tm=tm, heads=heads)
    return pl.pallas_call(
        kern,
        grid=(m // tm, d // tn),
        in_specs=[
            pl.BlockSpec((tm, width), lambda i, j: (i, 0)),
            pl.BlockSpec((tm, width), lambda i, j: (i, 1)),
            pl.BlockSpec((1, width), lambda i, j: (0, 0)),
            pl.BlockSpec((heads, CHUNK, CHUNK), lambda i, j: (0, 0, 0)),
            pl.BlockSpec((CHUNK, width), lambda i, j: (0, 0)),
            _col_tile_spec(w_gout, width, tn),
            pl.BlockSpec((tm, tn), lambda i, j: (i, j + gate_col // tn)),
            pl.BlockSpec((tm, tn), lambda i, j: (i, j)),
        ],
        out_specs=pl.BlockSpec((tm, tn), lambda i, j: (i, j)),
        out_shape=jax.ShapeDtypeStruct((m, d), BF16),
        scratch_shapes=[pltpu.VMEM((tm, width), BF16), pltpu.VMEM((heads, CHUNK, CHUNK), BF16)],
        compiler_params=_cparams(("arbitrary", "arbitrary")),
        name="gmlp_prompt",
    )(guv, guv, gnv, w_s, bias_full, w_gout, gates, p)


def _gmlp_sample_kernel(gu_ref, gv_ref, gnv_ref, wdiag_ref, bias_ref, wo_ref, gate_ref, p_ref,
                        out_ref, v_ref, wemit_ref, s_scr):
    j = pl.program_id(0)

    @pl.when(j == 0)
    def _():
        gv = gv_ref[...]
        r = lax.rsqrt(jnp.mean(gv * gv, axis=-1, keepdims=True) + EPS)
        v = gv * r * gnv_ref[...]
        v_ref[...] = v
        mixed = wdiag_ref[...] * v + bias_ref[...]
        s_scr[...] = (gu_ref[...] * mixed).astype(BF16)

    yb = jnp.dot(s_scr[...], _as_bf16(wo_ref, wemit_ref), preferred_element_type=F32)
    out_ref[...] = (p_ref[...] + gate_ref[...] * yb).astype(out_ref.dtype)


def _gmlp_sample(guv, gnv, wdiag, bias0, w_gout, gates, p, *, gate_col, tn):
    m = guv.shape[0]
    width, d = w_gout.shape
    return pl.pallas_call(
        _gmlp_sample_kernel,
        grid=(d // tn,),
        in_specs=[
            pl.BlockSpec((m, width), lambda j: (0, 0)),
            pl.BlockSpec((m, width), lambda j: (0, 1)),
            pl.BlockSpec((1, width), lambda j: (0, 0)),
            pl.BlockSpec((1, width), lambda j: (0, 0)),
            pl.BlockSpec((1, width), lambda j: (0, 0)),
            pl.BlockSpec((width, tn), lambda j: (0, j)),
            pl.BlockSpec((m, tn), lambda j: (0, j + gate_col // tn)),
            pl.BlockSpec((m, tn), lambda j: (0, j)),
        ],
        out_specs=(pl.BlockSpec((m, tn), lambda j: (0, j)),
                   pl.BlockSpec((m, width), lambda j: (0, 0)),
                   pl.BlockSpec((1, width, tn), lambda j: (j, 0, 0))),
        out_shape=(jax.ShapeDtypeStruct((m, d), BF16), jax.ShapeDtypeStruct((m, width), F32),
                   jax.ShapeDtypeStruct((d // tn, width, tn), BF16)),
        scratch_shapes=[pltpu.VMEM((m, width), BF16)],
        compiler_params=_cparams(("arbitrary",)),
        name="gmlp_sample_emit",
    )(guv, guv, gnv, wdiag, bias0, w_gout, gates, p)


def _layer(x, prm, w, *, nbatch, x0, gfinal, tm_ffn, tm_mm):
    sample = x0 is not None
    d_model = x.shape[1]
    ssm_width = prm["bw"].shape[0] * UB
    gmlp_width = prm["gnv"].shape[1]
    o1 = ssm_width
    o3 = o1 + 2 * gmlp_width
    tn = 512
    wb = {}

    res = _ffn(x, prm["n1"], w["f1g"], w["f1u"], w["f1d"], prm["nmix"],
               tail="norm_out", tm=tm_ffn, tf=256, emit=sample)
    x, h2 = res[0], res[1]
    if sample:
        wb["f1g"], wb["f1u"], wb["f1d"] = res[2:]

    cols = {}
    if sample:
        w = dict(w, win_ssm=w["w_in"], win_uv=w["w_in"], win_gate=w["w_in"],
                 glu_a=w["w_glu"], glu_b=w["w_glu"])
        cols = dict(win_uv=o1, win_gate=o3, glu_b=d_model)

    def mm(lhs, wkeys, extras, kind, n_out, out_dtype, name):
        out, emitted = _mm(lhs, [(w[k], cols.get(k, 0)) for k in wkeys], extras, kind=kind,
                           n_out=n_out, out_dtype=out_dtype, tm=tm_mm, tn=tn, emit=sample, name=name)
        wb.update(zip(wkeys, emitted))
        return out

    z_ssm = mm(h2, ["win_ssm"], [], "identity", o1, F32, "w_in_ssm")
    guv = mm(h2, ["win_uv"], [], "gelu", 2 * gmlp_width, F32, "w_in_uv")
    gates = mm(h2, ["win_gate"], [], "sigmoid", 2 * d_model, F32, "w_in_gate")

    if sample:
        y, s_re, s_im = _ssm_sample(z_ssm, prm["bw"], prm["cw"], prm["a_re"], prm["a_im"],
                                    prm["d_skip"], x0[0], x0[1])
    else:
        seq = x.shape[0] // nbatch
        y, s_re, s_im = _ssm_prompt(z_ssm.reshape(nbatch, seq, ssm_width), prm["bw"], prm["cw"],
                                    prm["a_re"], prm["a_im"], prm["d_skip"])
        y = y.reshape(nbatch * seq, ssm_width)
    p = mm(y, ["glu_a", "glu_b"], [(gates, 0)], "gated_glu", d_model, F32, "ssm_glu")

    if sample:
        merged, v_rows, wb["w_gout"] = _gmlp_sample(guv, prm["gnv"], prm["w_diag"], prm["bias0"],
                                                    w["w_gout"], gates, p, gate_col=d_model, tn=tn)
    else:
        merged = _gmlp_prompt(guv, prm["gnv"], prm["w_s"], prm["bias_full"], w["w_gout"], gates, p,
                              gate_col=d_model, tm=tm_mm, tn=tn)
        v_rows = None
    x = mm(merged, ["w_out"], [(x, 0)], "residual", d_model, F32, "out_proj")
    res = _ffn(x, prm["n2"], w["f2g"], w["f2u"], w["f2d"], gfinal,
               tail="final", tm=tm_ffn, tf=256, emit=sample)
    if sample:
        wb["f2g"], wb["f2u"], wb["f2d"] = res[1:]
    return res[0], s_re, s_im, v_rows, wb


def kernel(x_prompt, x_sample, state_ssm_re, state_ssm_im, norm_ffn1, ffn1_gate, ffn1_up, ffn1_down, norm_mix, w_in, ssm_lambda_re, ssm_lambda_im, ssm_log_dt, ssm_b_re, ssm_b_im, ssm_c_re, ssm_c_im, ssm_d, ssm_w_glu, gmlp_norm_v, gmlp_w_s, gmlp_b_s, gmlp_w_out, w_out, norm_ffn2, ffn2_gate, ffn2_up, ffn2_down, norm_final):
    depth = w_in.shape[0]
    assert depth == 1, "the final RMSNorm is fused into the last FFN; one layer per step"
    batch, seq, d_model = x_prompt.shape
    dec_batch, dec_seq, _ = x_sample.shape
    assert dec_seq == 1
    groups, states = ssm_lambda_re.shape[1:]
    gmlp_width = gmlp_norm_v.shape[1]
    heads = gmlp_w_s.shape[1]
    head_dim = gmlp_width // heads
    assert seq % SCAN_T == 0 and seq % CHUNK == 0 and head_dim == GMLP_HEAD
    assert ssm_b_re.shape[-1] == SSM_GROUP and states == SSM_STATE and groups % GROUPS_PER_BLOCK == 0

    l = 0
    yp = x_prompt.reshape(batch * seq, d_model)
    ys = x_sample.reshape(dec_batch, d_model)
    gfin = norm_final.reshape(1, d_model)
    a_re, a_im, bb_re, bb_im = _discretise(ssm_lambda_re[l], ssm_lambda_im[l], ssm_log_dt[l],
                                           ssm_b_re[l], ssm_b_im[l])
    bw, cw = _block_diag_weights(bb_re, bb_im, ssm_c_re[l], ssm_c_im[l])
    prm = dict(
        n1=norm_ffn1[l].reshape(1, d_model), nmix=norm_mix[l].reshape(1, d_model),
        n2=norm_ffn2[l].reshape(1, d_model),
        bw=bw, cw=cw, a_re=a_re, a_im=a_im, d_skip=ssm_d[l],
        gnv=gmlp_norm_v[l].reshape(1, gmlp_width), w_s=gmlp_w_s[l],
        bias_full=jnp.repeat(gmlp_b_s[l].T, head_dim, axis=1),
        w_diag=jnp.repeat(gmlp_w_s[l][:, 0, 0], head_dim).reshape(1, gmlp_width),
        bias0=jnp.repeat(gmlp_b_s[l][:, 0], head_dim).reshape(1, gmlp_width),
    )
    w_f32 = dict(f1g=ffn1_gate[l], f1u=ffn1_up[l], f1d=ffn1_down[l], w_in=w_in[l],
                 w_glu=ssm_w_glu[l], w_gout=gmlp_w_out[l], w_out=w_out[l],
                 f2g=ffn2_gate[l], f2u=ffn2_up[l], f2d=ffn2_down[l])
    x0 = (state_ssm_re[l].reshape(dec_batch, groups * states),
          state_ssm_im[l].reshape(dec_batch, groups * states))
    ys, sr, si, vr, w_bf16 = _layer(ys, prm, w_f32, nbatch=dec_batch, x0=x0, gfinal=gfin,
                                    tm_ffn=dec_batch, tm_mm=dec_batch)
    yp, pr, pi, _, _ = _layer(yp, prm, w_bf16, nbatch=batch, x0=None, gfinal=gfin,
                              tm_ffn=512, tm_mm=1024)
    return (yp.reshape(batch, seq, d_model), ys.reshape(dec_batch, dec_seq, d_model),
            pr.reshape(1, batch, groups, states), pi.reshape(1, batch, groups, states),
            sr.reshape(1, dec_batch, groups, states), si.reshape(1, dec_batch, groups, states),
            vr.reshape(1, dec_batch, dec_seq, gmlp_width))
```

```python
import functools
import math

import jax
import jax.numpy as jnp
from jax import lax
from jax.experimental import pallas as pl
from jax.experimental.pallas import tpu as pltpu

F32 = jnp.float32
BF16 = jnp.bfloat16

EPS = 1e-6
LANES = 128
SSM_GROUP = 16
SSM_STATE = 64
GROUPS_PER_BLOCK = 16
UB = GROUPS_PER_BLOCK * SSM_GROUP
SB = GROUPS_PER_BLOCK * SSM_STATE
SLABS = SB // LANES
SCAN_T = 256
SCAN_PITCH = SCAN_T + 8
SCAN_INTERLEAVE = 4
CHUNK = 128
GMLP_HEAD = 128
VMEM_LIMIT = 56 * 1024 * 1024


def _cparams(sem):
    return pltpu.CompilerParams(dimension_semantics=sem, vmem_limit_bytes=VMEM_LIMIT)


def _gelu(x):
    c = math.sqrt(2.0 / math.pi)
    return 0.5 * x * (1.0 + jnp.tanh(c * (x + 0.044715 * (x * x * x))))


def _sigmoid(x):
    return 1.0 / (1.0 + jnp.exp(-x))


def _as_bf16(w_ref, emit_ref):
    w = w_ref[0] if len(w_ref.shape) == 3 else w_ref[...]
    if w.dtype != BF16:
        w = w.astype(BF16)
    if emit_ref is not None:
        if len(emit_ref.shape) == 3:
            emit_ref[0] = w
        else:
            emit_ref[...] = w
    return w


def _col_tile_spec(w, k, tn, col0=0):
    if w.ndim == 3:
        assert w.shape[1:] == (k, tn) and col0 == 0
        return pl.BlockSpec((1, k, tn), lambda i, j: (j, 0, 0))
    return pl.BlockSpec((k, tn), lambda i, j, o=col0 // tn: (0, j + o))


def _ffn_kernel(*refs, tm, d_model, col_chunk, row_chunk, tail, emit):
    x_ref, gn_ref, wg_ref, wu_ref, wd_ref, gt_ref = refs[:6]
    rest = list(refs[6:])
    out_ref = rest.pop(0)
    h2_ref = rest.pop(0) if tail == "norm_out" else None
    emits = [rest.pop(0) for _ in range(3)] if emit else [None] * 3
    (h_scr,) = rest
    j = pl.program_id(1)

    @pl.when(j == 0)
    def _():
        g = gn_ref[...]

        def body(c, carry):
            r0 = pl.multiple_of(c * row_chunk, row_chunk)
            xf = x_ref[pl.ds(r0, row_chunk), :]
            r = lax.rsqrt(jnp.mean(xf * xf, axis=-1, keepdims=True) + EPS)
            h_scr[pl.ds(r0, row_chunk), :] = (xf * r * g).astype(BF16)
            return carry

        lax.fori_loop(0, tm // row_chunk, body, 0)
        out_ref[...] = jnp.zeros_like(out_ref)

    h = h_scr[...]
    g = jnp.dot(h, _as_bf16(wg_ref, emits[0]), preferred_element_type=F32)
    u = jnp.dot(h, _as_bf16(wu_ref, emits[1]), preferred_element_type=F32)
    a = (g * _sigmoid(g) * u).astype(BF16)
    wd = _as_bf16(wd_ref, emits[2])
    for n in range(d_model // col_chunk):
        sl = slice(n * col_chunk, (n + 1) * col_chunk)
        out_ref[:, sl] += jnp.dot(a, wd[:, sl], preferred_element_type=F32)

    @pl.when(j == pl.num_programs(1) - 1)
    def _():
        gt = gt_ref[...]

        def body(c, carry):
            r0 = pl.multiple_of(c * row_chunk, row_chunk)
            y = x_ref[pl.ds(r0, row_chunk), :] + 0.5 * out_ref[pl.ds(r0, row_chunk), :]
            r = lax.rsqrt(jnp.mean(y * y, axis=-1, keepdims=True) + EPS)
            if tail == "norm_out":
                out_ref[pl.ds(r0, row_chunk), :] = y
                h2_ref[pl.ds(r0, row_chunk), :] = (y * r * gt).astype(BF16)
            else:
                out_ref[pl.ds(r0, row_chunk), :] = y * r * gt
            return carry

        lax.fori_loop(0, tm // row_chunk, body, 0)


def _ffn(x, gn, wg, wu, wd, gtail, *, tail, tm, tf, emit):
    m, d = x.shape
    f = wd.shape[0]
    assert not emit or m == tm
    kern = functools.partial(_ffn_kernel, tm=tm, d_model=d, col_chunk=512,
                             row_chunk=min(tm, 64), tail=tail, emit=emit)
    out_specs = [pl.BlockSpec((tm, d), lambda i, j: (i, 0))]
    out_shape = [jax.ShapeDtypeStruct((m, d), F32)]
    if tail == "norm_out":
        out_specs.append(pl.BlockSpec((tm, d), lambda i, j: (i, 0)))
        out_shape.append(jax.ShapeDtypeStruct((m, d), BF16))
    if emit:
        out_specs += [pl.BlockSpec((1, d, tf), lambda i, j: (j, 0, 0)),
                      pl.BlockSpec((1, d, tf), lambda i, j: (j, 0, 0)),
                      pl.BlockSpec((tf, d), lambda i, j: (j, 0))]
        out_shape += [jax.ShapeDtypeStruct((f // tf, d, tf), BF16),
                      jax.ShapeDtypeStruct((f // tf, d, tf), BF16),
                      jax.ShapeDtypeStruct((f, d), BF16)]
    return pl.pallas_call(
        kern,
        grid=(m // tm, f // tf),
        in_specs=[
            pl.BlockSpec((tm, d), lambda i, j: (i, 0), pipeline_mode=pl.Buffered(1)),
            pl.BlockSpec((1, d), lambda i, j: (0, 0)),
            _col_tile_spec(wg, d, tf),
            _col_tile_spec(wu, d, tf),
            pl.BlockSpec((tf, d), lambda i, j: (j, 0)),
            pl.BlockSpec((1, d), lambda i, j: (0, 0)),
        ],
        out_specs=out_specs,
        out_shape=out_shape,
        scratch_shapes=[pltpu.VMEM((tm, d), BF16)],
        compiler_params=_cparams(("arbitrary", "arbitrary")),
        name="ffn_" + tail + ("_emit" if emit else ""),
    )(x, gn, wg, wu, wd, gtail)


def _mm_epilogue(kind, accs, extras):
    if kind == "identity":
        return accs[0]
    if kind == "gelu":
        return _gelu(accs[0])
    if kind == "sigmoid":
        return _sigmoid(accs[0])
    if kind == "gated_glu":
        return extras[0] * (accs[0] * _sigmoid(accs[1]))
    if kind == "residual":
        return extras[0] + accs[0]
    raise ValueError(kind)


def _mm_kernel(*refs, n_w, n_extra, kind, emit):
    lhs_ref = refs[0]
    w_refs = refs[1:1 + n_w]
    extra_refs = refs[1 + n_w:1 + n_w + n_extra]
    out_ref = refs[1 + n_w + n_extra]
    emit_refs = refs[2 + n_w + n_extra:] if emit else [None] * n_w
    lhs = lhs_ref[...]
    accs = [jnp.dot(lhs, _as_bf16(w, e), preferred_element_type=F32) for w, e in zip(w_refs, emit_refs)]
    out_ref[...] = _mm_epilogue(kind, accs, [e[...] for e in extra_refs]).astype(out_ref.dtype)


def _mm(lhs, weights, extras, *, kind, n_out, out_dtype, tm, tn, emit, name):
    m, k = lhs.shape
    assert not emit or m == tm
    in_specs = [pl.BlockSpec((tm, k), lambda i, j: (i, 0))]
    args = [lhs]
    for w, c0 in weights:
        in_specs.append(_col_tile_spec(w, k, tn, c0))
        args.append(w)
    for e, c0 in extras:
        in_specs.append(pl.BlockSpec((tm, tn), lambda i, j, o=c0 // tn: (i, j + o)))
        args.append(e)
    out_specs = [pl.BlockSpec((tm, tn), lambda i, j: (i, j))]
    out_shape = [jax.ShapeDtypeStruct((m, n_out), out_dtype)]
    if emit:
        out_specs += [pl.BlockSpec((1, k, tn), lambda i, j: (j, 0, 0)) for _ in weights]
        out_shape += [jax.ShapeDtypeStruct((n_out // tn, k, tn), BF16) for _ in weights]
    kern = functools.partial(_mm_kernel, n_w=len(weights), n_extra=len(extras), kind=kind, emit=emit)
    res = pl.pallas_call(
        kern,
        grid=(m // tm, n_out // tn),
        in_specs=in_specs,
        out_specs=out_specs,
        out_shape=out_shape,
        compiler_params=_cparams(("arbitrary", "arbitrary")),
        name=name + ("_emit" if emit else ""),
    )(*args)
    return res[0], list(res[1:])


def _disc_kernel(lre_ref, lim_ref, ldt_ref, bre_ref, bim_ref, are_ref, aim_ref, bbre_ref, bbim_ref):
    lam_re = lre_ref[...]
    lam_im = lim_ref[...]
    dt = jnp.exp(ldt_ref[...])
    mag = jnp.exp(dt * lam_re)
    a_re = mag * jnp.cos(dt * lam_im)
    a_im = mag * jnp.sin(dt * lam_im)
    nr, ni = a_re - 1.0, a_im
    den = lam_re * lam_re + lam_im * lam_im
    coef_re = (nr * lam_re + ni * lam_im) / den
    coef_im = (ni * lam_re - nr * lam_im) / den
    are_ref[...] = a_re
    aim_ref[...] = a_im
    b_re = bre_ref[...]
    b_im = bim_ref[...]
    bbre_ref[...] = coef_re[None] * b_re - coef_im[None] * b_im
    bbim_ref[...] = coef_re[None] * b_im + coef_im[None] * b_re


def _discretise(lam_re, lam_im, log_dt, b_re, b_im):
    g, p = lam_re.shape
    h = b_re.shape[-1]
    b_re_t = jnp.transpose(b_re, (2, 0, 1))
    b_im_t = jnp.transpose(b_im, (2, 0, 1))
    return pl.pallas_call(
        _disc_kernel,
        out_shape=(jax.ShapeDtypeStruct((g, p), F32), jax.ShapeDtypeStruct((g, p), F32),
                   jax.ShapeDtypeStruct((h, g, p), F32), jax.ShapeDtypeStruct((h, g, p), F32)),
        name="s5_discretise",
    )(lam_re, lam_im, log_dt.reshape(g, 1), b_re_t, b_im_t)


def _block_diag_weights(bb_re, bb_im, c_re, c_im):
    h, g, p = bb_re.shape
    nb = g // GROUPS_PER_BLOCK
    eye = jnp.eye(GROUPS_PER_BLOCK, dtype=F32)

    def in_proj(bb):
        x = jnp.transpose(bb, (1, 0, 2)).reshape(nb, GROUPS_PER_BLOCK, h, p)
        x = x[:, :, :, None, :] * eye[None, :, None, :, None]
        return x.reshape(nb, UB, SB)

    def out_proj(c):
        x = jnp.transpose(c.reshape(nb, GROUPS_PER_BLOCK, h, p), (0, 1, 3, 2))
        x = x[:, :, :, None, :] * eye[None, :, None, :, None]
        return x.reshape(nb, SB, UB)

    bw = jnp.concatenate([in_proj(bb_re), in_proj(bb_im)], axis=2).astype(BF16)
    cw = jnp.concatenate([out_proj(c_re), -out_proj(c_im)], axis=1).astype(BF16)
    return bw, cw


def _ssm_prompt_kernel(u_ref, bw_ref, cw_ref, are_ref, aim_ref, d_ref,
                       y_ref, sre_ref, sim_ref,
                       bure, buim, st_re, st_im, *, nbatch):
    tc = pl.program_id(1)

    @pl.when(tc == 0)
    def _():
        st_re[...] = jnp.zeros_like(st_re)
        st_im[...] = jnp.zeros_like(st_im)

    def slab_ref(c):
        return (bure, c) if c < SLABS else (buim, c - SLABS)

    u_all = u_ref[...].reshape(nbatch * SCAN_T, UB)
    u_bf = u_all.astype(BF16)
    per_dot = 4
    for n in range(2 * SLABS // per_dot):
        bu = jnp.dot(u_bf, bw_ref[0, :, per_dot * n * LANES:per_dot * (n + 1) * LANES],
                     preferred_element_type=F32)
        for h in range(per_dot):
            ref, c = slab_ref(per_dot * n + h)
            for b in range(nbatch):
                ref[c, b * SCAN_PITCH:b * SCAN_PITCH + SCAN_T, :] = (
                    bu[b * SCAN_T:(b + 1) * SCAN_T, h * LANES:(h + 1) * LANES])

    for c0 in range(0, SLABS, SCAN_INTERLEAVE):
        slabs = list(range(c0, c0 + SCAN_INTERLEAVE))
        a_re = [jnp.broadcast_to(are_ref[0, c], (nbatch, LANES)) for c in slabs]
        a_im = [jnp.broadcast_to(aim_ref[0, c], (nbatch, LANES)) for c in slabs]
        init = tuple(st_re[c] for c in slabs) + tuple(st_im[c] for c in slabs)

        def step(t, carry, slabs=slabs, a_re=a_re, a_im=a_im):
            n = len(slabs)
            new_re, new_im = [], []
            for k, c in enumerate(slabs):
                s_r, s_i = carry[k], carry[n + k]
                rows = pl.ds(t, nbatch, stride=SCAN_PITCH)
                n_r = a_re[k] * s_r - a_im[k] * s_i + bure[c, rows, :]
                n_i = a_re[k] * s_i + a_im[k] * s_r + buim[c, rows, :]
                bure[c, rows, :] = n_r
                buim[c, rows, :] = n_i
                new_re.append(n_r)
                new_im.append(n_i)
            return tuple(new_re) + tuple(new_im)

        fin = lax.fori_loop(0, SCAN_T, step, init, unroll=4)
        for k, c in enumerate(slabs):
            st_re[c] = fin[k]
            st_im[c] = fin[len(slabs) + k]

    d = d_ref[0]
    for b in range(nbatch):
        r0 = b * SCAN_PITCH
        y = d * u_ref[b]
        for n in range(SLABS):
            (ref0, c0), (ref1, c1) = slab_ref(2 * n), slab_ref(2 * n + 1)
            s_pair = jnp.concatenate([ref0[c0, r0:r0 + SCAN_T, :], ref1[c1, r0:r0 + SCAN_T, :]],
                                     axis=1).astype(BF16)
            y = y + jnp.dot(s_pair, cw_ref[0, 2 * n * LANES:(2 * n + 2) * LANES, :],
                            preferred_element_type=F32)
        y_ref[b] = _gelu(y).astype(y_ref.dtype)

    @pl.when(tc == pl.num_programs(1) - 1)
    def _():
        for c in range(SLABS):
            sre_ref[:, c * LANES:(c + 1) * LANES] = st_re[c]
            sim_ref[:, c * LANES:(c + 1) * LANES] = st_im[c]


def _ssm_prompt(z3, bw, cw, a_re, a_im, d_skip):
    nbatch, seq, width = z3.shape
    nb = bw.shape[0]
    states = nb * SB
    kern = functools.partial(_ssm_prompt_kernel, nbatch=nbatch)
    return pl.pallas_call(
        kern,
        grid=(nb, seq // SCAN_T),
        in_specs=[
            pl.BlockSpec((nbatch, SCAN_T, UB), lambda g, t: (0, t, g)),
            pl.BlockSpec((1, UB, 2 * SB), lambda g, t: (g, 0, 0)),
            pl.BlockSpec((1, 2 * SB, UB), lambda g, t: (g, 0, 0)),
            pl.BlockSpec((1, SLABS, 1, LANES), lambda g, t: (g, 0, 0, 0)),
            pl.BlockSpec((1, SLABS, 1, LANES), lambda g, t: (g, 0, 0, 0)),
            pl.BlockSpec((1, 1, UB), lambda g, t: (g, 0, 0)),
        ],
        out_specs=(
            pl.BlockSpec((nbatch, SCAN_T, UB), lambda g, t: (0, t, g)),
            pl.BlockSpec((nbatch, SB), lambda g, t: (0, g)),
            pl.BlockSpec((nbatch, SB), lambda g, t: (0, g)),
        ),
        out_shape=(
            jax.ShapeDtypeStruct((nbatch, seq, width), BF16),
            jax.ShapeDtypeStruct((nbatch, states), F32),
            jax.ShapeDtypeStruct((nbatch, states), F32),
        ),
        scratch_shapes=[
            pltpu.VMEM((SLABS, nbatch * SCAN_PITCH, LANES), F32),
            pltpu.VMEM((SLABS, nbatch * SCAN_PITCH, LANES), F32),
            pltpu.VMEM((SLABS, nbatch, LANES), F32),
            pltpu.VMEM((SLABS, nbatch, LANES), F32),
        ],
        compiler_params=_cparams(("arbitrary", "arbitrary")),
        name="ssm_prompt",
    )(z3, bw, cw, a_re.reshape(nb, SLABS, 1, LANES), a_im.reshape(nb, SLABS, 1, LANES),
      d_skip.reshape(nb, 1, UB))


def _ssm_sample_kernel(u_ref, bw_ref, cw_ref, are_ref, aim_ref, d_ref, x0re_ref, x0im_ref,
                       y_ref, sre_ref, sim_ref):
    u = u_ref[...]
    bu = jnp.dot(u.astype(BF16), bw_ref[0], preferred_element_type=F32)
    a_re = are_ref[0]
    a_im = aim_ref[0]
    x_re = x0re_ref[...]
    x_im = x0im_ref[...]
    s_re = a_re * x_re - a_im * x_im + bu[:, :SB]
    s_im = a_re * x_im + a_im * x_re + bu[:, SB:]
    sre_ref[...] = s_re
    sim_ref[...] = s_im
    s_cat = jnp.concatenate([s_re, s_im], axis=1).astype(BF16)
    y = jnp.dot(s_cat, cw_ref[0], preferred_element_type=F32) + d_ref[0] * u
    y_ref[...] = _gelu(y).astype(y_ref.dtype)


def _ssm_sample(z, bw, cw, a_re, a_im, d_skip, x0_re, x0_im):
    nbatch = z.shape[0]
    nb = bw.shape[0]
    return pl.pallas_call(
        _ssm_sample_kernel,
        grid=(nb,),
        in_specs=[
            pl.BlockSpec((nbatch, UB), lambda g: (0, g)),
            pl.BlockSpec((1, UB, 2 * SB), lambda g: (g, 0, 0)),
            pl.BlockSpec((1, 2 * SB, UB), lambda g: (g, 0, 0)),
            pl.BlockSpec((1, 1, SB), lambda g: (g, 0, 0)),
            pl.BlockSpec((1, 1, SB), lambda g: (g, 0, 0)),
            pl.BlockSpec((1, 1, UB), lambda g: (g, 0, 0)),
            pl.BlockSpec((nbatch, SB), lambda g: (0, g)),
            pl.BlockSpec((nbatch, SB), lambda g: (0, g)),
        ],
        out_specs=(
            pl.BlockSpec((nbatch, UB), lambda g: (0, g)),
            pl.BlockSpec((nbatch, SB), lambda g: (0, g)),
            pl.BlockSpec((nbatch, SB), lambda g: (0, g)),
        ),
        out_shape=(
            jax.ShapeDtypeStruct((nbatch, nb * UB), BF16),
            jax.ShapeDtypeStruct((nbatch, nb * SB), F32),
            jax.ShapeDtypeStruct((nbatch, nb * SB), F32),
        ),
        compiler_params=_cparams(("arbitrary",)),
        name="ssm_sample",
    )(z, bw, cw, a_re.reshape(nb, 1, SB), a_im.reshape(nb, 1, SB), d_skip.reshape(nb, 1, UB),
      x0_re, x0_im)


def _gmlp_prompt_kernel(gu_ref, gv_ref, gnv_ref, ws_ref, bias_ref, wo_ref, gate_ref, p_ref,
                        out_ref, s_scr, wt_scr, *, tm, heads):
    i = pl.program_id(0)
    j = pl.program_id(1)

    @pl.when(jnp.logical_and(i == 0, j == 0))
    def _():
        row = lax.broadcasted_iota(jnp.int32, (CHUNK, CHUNK), 0)
        col = lax.broadcasted_iota(jnp.int32, (CHUNK, CHUNK), 1)
        mask = (col <= row).astype(F32)
        for g in range(heads):
            wt_scr[g] = (ws_ref[g] * mask).astype(BF16)

    @pl.when(j == 0)
    def _():
        gnv = gnv_ref[...]

        def body(c, carry):
            r0 = pl.multiple_of(c * CHUNK, CHUNK)
            gv = gv_ref[pl.ds(r0, CHUNK), :]
            r = lax.rsqrt(jnp.mean(gv * gv, axis=-1, keepdims=True) + EPS)
            v = (gv * r * gnv).astype(BF16)
            for g in range(heads):
                sl = slice(g * GMLP_HEAD, (g + 1) * GMLP_HEAD)
                mixed = jnp.dot(wt_scr[g], v[:, sl], preferred_element_type=F32) + bias_ref[:, sl]
                s_scr[pl.ds(r0, CHUNK), sl] = (gu_ref[pl.ds(r0, CHUNK), sl] * mixed).astype(BF16)
            return carry

        lax.fori_loop(0, tm // CHUNK, body, 0)

    yb = jnp.dot(s_scr[...], wo_ref[0], preferred_element_type=F32)
    out_ref[...] = (p_ref[...] + gate_ref[...] * yb).astype(out_ref.dtype)


def _gmlp_prompt(guv, gnv, w_s, bias_full, w_gout, gates, p, *, gate_col, tm, tn):
    m = guv.shape[0]
    ntiles, width, _ = w_gout.shape
    d = ntiles * tn
    heads = w_s.shape[0]
    kern = functools.partial(_gmlp_prompt_kernel, tm=tm, heads=heads)
    return pl.pallas_call(
        kern,
        grid=(m // tm, d // tn),
        in_specs=[
            pl.BlockSpec((tm, width), lambda i, j: (i, 0)),
            pl.BlockSpec((tm, width), lambda i, j: (i, 1)),
            pl.BlockSpec((1, width), lambda i, j: (0, 0)),
            pl.BlockSpec((heads, CHUNK, CHUNK), lambda i, j: (0, 0, 0)),
            pl.BlockSpec((CHUNK, width), lambda i, j: (0, 0)),
            _col_tile_spec(w_gout, width, tn),
            pl.BlockSpec((tm, tn), lambda i, j: (i, j + gate_col // tn)),
            pl.BlockSpec((tm, tn), lambda i, j: (i, j)),
        ],
        out_specs=pl.BlockSpec((tm, tn), lambda i, j: (i, j)),
        out_shape=jax.ShapeDtypeStruct((m, d), BF16),
        scratch_shapes=[pltpu.VMEM((tm, width), BF16), pltpu.VMEM((heads, CHUNK, CHUNK), BF16)],
        compiler_params=_cparams(("arbitrary", "arbitrary")),
        name="gmlp_prompt",
    )(guv, guv, gnv, w_s, bias_full, w_gout, gates, p)


def _gmlp_sample_kernel(gu_ref, gv_ref, gnv_ref, wdiag_ref, bias_ref, wo_ref, gate_ref, p_ref,
                        out_ref, v_ref, wemit_ref, s_scr):
    j = pl.program_id(0)

    @pl.when(j == 0)
    def _():
        gv = gv_ref[...]
        r = lax.rsqrt(jnp.mean(gv * gv, axis=-1, keepdims=True) + EPS)
        v = gv * r * gnv_ref[...]
        v_ref[...] = v
        mixed = wdiag_ref[...] * v + bias_ref[...]
        s_scr[...] = (gu_ref[...] * mixed).astype(BF16)

    yb = jnp.dot(s_scr[...], _as_bf16(wo_ref, wemit_ref), preferred_element_type=F32)
    out_ref[...] = (p_ref[...] + gate_ref[...] * yb).astype(out_ref.dtype)


def _gmlp_sample(guv, gnv, wdiag, bias0, w_gout, gates, p, *, gate_col, tn):
    m = guv.shape[0]
    width, d = w_gout.shape
    return pl.pallas_call(
        _gmlp_sample_kernel,
        grid=(d // tn,),
        in_specs=[
            pl.BlockSpec((m, width), lambda j: (0, 0)),
            pl.BlockSpec((m, width), lambda j: (0, 1)),
            pl.BlockSpec((1, width), lambda j: (0, 0)),
            pl.BlockSpec((1, width), lambda j: (0, 0)),
            pl.BlockSpec((1, width), lambda j: (0, 0)),
            pl.BlockSpec((width, tn), lambda j: (0, j)),
            pl.BlockSpec((m, tn), lambda j: (0, j + gate_col // tn)),
            pl.BlockSpec((m, tn), lambda j: (0, j)),
        ],
        out_specs=(pl.BlockSpec((m, tn), lambda j: (0, j)),
                   pl.BlockSpec((m, width), lambda j: (0, 0)),
                   pl.BlockSpec((1, width, tn), lambda j: (j, 0, 0))),
        out_shape=(jax.ShapeDtypeStruct((m, d), BF16), jax.ShapeDtypeStruct((m, width), F32),
                   jax.ShapeDtypeStruct((d // tn, width, tn), BF16)),
        scratch_shapes=[pltpu.VMEM((m, width), BF16)],
        compiler_params=_cparams(("arbitrary",)),
        name="gmlp_sample_emit",
    )(guv, guv, gnv, wdiag, bias0, w_gout, gates, p)


def _layer(x, prm, w, *, nbatch, x0, gfinal, tm_ffn, tm_mm):
    sample = x0 is not None
    d_model = x.shape[1]
    ssm_width = prm["bw"].shape[0] * UB
    gmlp_width = prm["gnv"].shape[1]
    o1 = ssm_width
    o3 = o1 + 2 * gmlp_width
    tn = 512
    wb = {}

    res = _ffn(x, prm["n1"], w["f1g"], w["f1u"], w["f1d"], prm["nmix"],
               tail="norm_out", tm=tm_ffn, tf=256, emit=sample)
    x, h2 = res[0], res[1]
    if sample:
        wb["f1g"], wb["f1u"], wb["f1d"] = res[2:]

    cols = {}
    if sample:
        w = dict(w, win_ssm=w["w_in"], win_uv=w["w_in"], win_gate=w["w_in"],
                 glu_a=w["w_glu"], glu_b=w["w_glu"])
        cols = dict(win_uv=o1, win_gate=o3, glu_b=d_model)

    def mm(lhs, wkeys, extras, kind, n_out, out_dtype, name):
        out, emitted = _mm(lhs, [(w[k], cols.get(k, 0)) for k in wkeys], extras, kind=kind,
                           n_out=n_out, out_dtype=out_dtype, tm=tm_mm, tn=2 * tn, emit=sample, name=name)
        wb.update(zip(wkeys, emitted))
        return out

    z_ssm = mm(h2, ["win_ssm"], [], "identity", o1, F32, "w_in_ssm")
    guv = mm(h2, ["win_uv"], [], "gelu", 2 * gmlp_width, F32, "w_in_uv")
    gates = mm(h2, ["win_gate"], [], "sigmoid", 2 * d_model, F32, "w_in_gate")

    if sample:
        y, s_re, s_im = _ssm_sample(z_ssm, prm["bw"], prm["cw"], prm["a_re"], prm["a_im"],
                                    prm["d_skip"], x0[0], x0[1])
    else:
        seq = x.shape[0] // nbatch
        y, s_re, s_im = _ssm_prompt(z_ssm.reshape(nbatch, seq, ssm_width), prm["bw"], prm["cw"],
                                    prm["a_re"], prm["a_im"], prm["d_skip"])
        y = y.reshape(nbatch * seq, ssm_width)
    p = mm(y, ["glu_a", "glu_b"], [(gates, 0)], "gated_glu", d_model, F32, "ssm_glu")

    if sample:
        merged, v_rows, wb["w_gout"] = _gmlp_sample(guv, prm["gnv"], prm["w_diag"], prm["bias0"],
                                                    w["w_gout"], gates, p, gate_col=d_model, tn=tn)
    else:
        merged = _gmlp_prompt(guv, prm["gnv"], prm["w_s"], prm["bias_full"], w["w_gout"], gates, p,
                              gate_col=d_model, tm=tm_mm, tn=tn)
        v_rows = None
    x = mm(merged, ["w_out"], [(x, 0)], "residual", d_model, F32, "out_proj")
    res = _ffn(x, prm["n2"], w["f2g"], w["f2u"], w["f2d"], gfinal,
               tail="final", tm=tm_ffn, tf=256, emit=sample)
    if sample:
        wb["f2g"], wb["f2u"], wb["f2d"] = res[1:]
    return res[0], s_re, s_im, v_rows, wb


def kernel(x_prompt, x_sample, state_ssm_re, state_ssm_im, norm_ffn1, ffn1_gate, ffn1_up, ffn1_down, norm_mix, w_in, ssm_lambda_re, ssm_lambda_im, ssm_log_dt, ssm_b_re, ssm_b_im, ssm_c_re, ssm_c_im, ssm_d, ssm_w_glu, gmlp_norm_v, gmlp_w_s, gmlp_b_s, gmlp_w_out, w_out, norm_ffn2, ffn2_gate, ffn2_up, ffn2_down, norm_final):
    depth = w_in.shape[0]
    assert depth == 1, "the final RMSNorm is fused into the last FFN; one layer per step"
    batch, seq, d_model = x_prompt.shape
    dec_batch, dec_seq, _ = x_sample.shape
    assert dec_seq == 1
    groups, states = ssm_lambda_re.shape[1:]
    gmlp_width = gmlp_norm_v.shape[1]
    heads = gmlp_w_s.shape[1]
    head_dim = gmlp_width // heads
    assert seq % SCAN_T == 0 and seq % CHUNK == 0 and head_dim == GMLP_HEAD
    assert ssm_b_re.shape[-1] == SSM_GROUP and states == SSM_STATE and groups % GROUPS_PER_BLOCK == 0

    l = 0
    yp = x_prompt.reshape(batch * seq, d_model)
    ys = x_sample.reshape(dec_batch, d_model)
    gfin = norm_final.reshape(1, d_model)
    a_re, a_im, bb_re, bb_im = _discretise(ssm_lambda_re[l], ssm_lambda_im[l], ssm_log_dt[l],
                                           ssm_b_re[l], ssm_b_im[l])
    bw, cw = _block_diag_weights(bb_re, bb_im, ssm_c_re[l], ssm_c_im[l])
    prm = dict(
        n1=norm_ffn1[l].reshape(1, d_model), nmix=norm_mix[l].reshape(1, d_model),
        n2=norm_ffn2[l].reshape(1, d_model),
        bw=bw, cw=cw, a_re=a_re, a_im=a_im, d_skip=ssm_d[l],
        gnv=gmlp_norm_v[l].reshape(1, gmlp_width), w_s=gmlp_w_s[l],
        bias_full=jnp.repeat(gmlp_b_s[l].T, head_dim, axis=1),
        w_diag=jnp.repeat(gmlp_w_s[l][:, 0, 0], head_dim).reshape(1, gmlp_width),
        bias0=jnp.repeat(gmlp_b_s[l][:, 0], head_dim).reshape(1, gmlp_width),
    )
    w_f32 = dict(f1g=ffn1_gate[l], f1u=ffn1_up[l], f1d=ffn1_down[l], w_in=w_in[l],
                 w_glu=ssm_w_glu[l], w_gout=gmlp_w_out[l], w_out=w_out[l],
                 f2g=ffn2_gate[l], f2u=ffn2_up[l], f2d=ffn2_down[l])
    x0 = (state_ssm_re[l].reshape(dec_batch, groups * states),
          state_ssm_im[l].reshape(dec_batch, groups * states))
    ys, sr, si, vr, w_bf16 = _layer(ys, prm, w_f32, nbatch=dec_batch, x0=x0, gfinal=gfin,
                                    tm_ffn=dec_batch, tm_mm=dec_batch)
    yp, pr, pi, _, _ = _layer(yp, prm, w_bf16, nbatch=batch, x0=None, gfinal=gfin,
                              tm_ffn=512, tm_mm=1024)
    return (yp.reshape(batch, seq, d_model), ys.reshape(dec_batch, dec_seq, d_model),
            pr.reshape(1, batch, groups, states), pi.reshape(1, batch, groups, states),
            sr.reshape(1, dec_batch, groups, states), si.reshape(1, dec_batch, groups, states),
            vr.reshape(1, dec_batch, dec_seq, gmlp_width))
```

```python
import functools
import math

import jax
import jax.numpy as jnp
from jax import lax
from jax.experimental import pallas as pl
from jax.experimental.pallas import tpu as pltpu

F32 = jnp.float32
BF16 = jnp.bfloat16

EPS = 1e-6
LANES = 128
SSM_GROUP = 16
SSM_STATE = 64
GROUPS_PER_BLOCK = 16
UB = GROUPS_PER_BLOCK * SSM_GROUP
SB = GROUPS_PER_BLOCK * SSM_STATE
SLABS = SB // LANES
SCAN_T = 256
SCAN_PITCH = SCAN_T + 8
SCAN_INTERLEAVE = 4
CHUNK = 128
GMLP_HEAD = 128
VMEM_LIMIT = 56 * 1024 * 1024


def _cparams(sem):
    return pltpu.CompilerParams(dimension_semantics=sem, vmem_limit_bytes=VMEM_LIMIT)


def _gelu(x):
    c = math.sqrt(2.0 / math.pi)
    return 0.5 * x * (1.0 + jnp.tanh(c * (x + 0.044715 * (x * x * x))))


def _sigmoid(x):
    return 1.0 / (1.0 + jnp.exp(-x))


def _as_bf16(w_ref, emit_ref):
    w = w_ref[0] if len(w_ref.shape) == 3 else w_ref[...]
    if w.dtype != BF16:
        w = w.astype(BF16)
    if emit_ref is not None:
        if len(emit_ref.shape) == 3:
            emit_ref[0] = w
        else:
            emit_ref[...] = w
    return w


def _col_tile_spec(w, k, tn, col0=0):
    if w.ndim == 3:
        assert w.shape[1:] == (k, tn) and col0 == 0
        return pl.BlockSpec((1, k, tn), lambda i, j: (j, 0, 0))
    return pl.BlockSpec((k, tn), lambda i, j, o=col0 // tn: (0, j + o))


def _ffn_kernel(*refs, tm, d_model, col_chunk, row_chunk, tail, emit):
    x_ref, gn_ref, wg_ref, wu_ref, wd_ref, gt_ref = refs[:6]
    rest = list(refs[6:])
    out_ref = rest.pop(0)
    h2_ref = rest.pop(0) if tail == "norm_out" else None
    emits = [rest.pop(0) for _ in range(3)] if emit else [None] * 3
    (h_scr,) = rest
    j = pl.program_id(1)

    @pl.when(j == 0)
    def _():
        g = gn_ref[...]

        def body(c, carry):
            r0 = pl.multiple_of(c * row_chunk, row_chunk)
            xf = x_ref[pl.ds(r0, row_chunk), :]
            r = lax.rsqrt(jnp.mean(xf * xf, axis=-1, keepdims=True) + EPS)
            h_scr[pl.ds(r0, row_chunk), :] = (xf * r * g).astype(BF16)
            return carry

        lax.fori_loop(0, tm // row_chunk, body, 0)
        out_ref[...] = jnp.zeros_like(out_ref)

    h = h_scr[...]
    g = jnp.dot(h, _as_bf16(wg_ref, emits[0]), preferred_element_type=F32)
    u = jnp.dot(h, _as_bf16(wu_ref, emits[1]), preferred_element_type=F32)
    a = (g * _sigmoid(g) * u).astype(BF16)
    wd = _as_bf16(wd_ref, emits[2])
    for n in range(d_model // col_chunk):
        sl = slice(n * col_chunk, (n + 1) * col_chunk)
        out_ref[:, sl] += jnp.dot(a, wd[:, sl], preferred_element_type=F32)

    @pl.when(j == pl.num_programs(1) - 1)
    def _():
        gt = gt_ref[...]

        def body(c, carry):
            r0 = pl.multiple_of(c * row_chunk, row_chunk)
            y = x_ref[pl.ds(r0, row_chunk), :] + 0.5 * out_ref[pl.ds(r0, row_chunk), :]
            r = lax.rsqrt(jnp.mean(y * y, axis=-1, keepdims=True) + EPS)
            if tail == "norm_out":
                out_ref[pl.ds(r0, row_chunk), :] = y
                h2_ref[pl.ds(r0, row_chunk), :] = (y * r * gt).astype(BF16)
            else:
                out_ref[pl.ds(r0, row_chunk), :] = y * r * gt
            return carry

        lax.fori_loop(0, tm // row_chunk, body, 0)


def _ffn(x, gn, wg, wu, wd, gtail, *, tail, tm, tf, emit):
    m, d = x.shape
    f = wd.shape[0]
    assert not emit or m == tm
    kern = functools.partial(_ffn_kernel, tm=tm, d_model=d, col_chunk=512,
                             row_chunk=min(tm, 64), tail=tail, emit=emit)
    out_mode = pl.Buffered(1) if tm * d * 4 > 8 * 1024 * 1024 else None
    out_specs = [pl.BlockSpec((tm, d), lambda i, j: (i, 0), pipeline_mode=out_mode)]
    out_shape = [jax.ShapeDtypeStruct((m, d), F32)]
    if tail == "norm_out":
        out_specs.append(pl.BlockSpec((tm, d), lambda i, j: (i, 0)))
        out_shape.append(jax.ShapeDtypeStruct((m, d), BF16))
    if emit:
        out_specs += [pl.BlockSpec((1, d, tf), lambda i, j: (j, 0, 0)),
                      pl.BlockSpec((1, d, tf), lambda i, j: (j, 0, 0)),
                      pl.BlockSpec((tf, d), lambda i, j: (j, 0))]
        out_shape += [jax.ShapeDtypeStruct((f // tf, d, tf), BF16),
                      jax.ShapeDtypeStruct((f // tf, d, tf), BF16),
                      jax.ShapeDtypeStruct((f, d), BF16)]
    return pl.pallas_call(
        kern,
        grid=(m // tm, f // tf),
        in_specs=[
            pl.BlockSpec((tm, d), lambda i, j: (i, 0), pipeline_mode=pl.Buffered(1)),
            pl.BlockSpec((1, d), lambda i, j: (0, 0)),
            _col_tile_spec(wg, d, tf),
            _col_tile_spec(wu, d, tf),
            pl.BlockSpec((tf, d), lambda i, j: (j, 0)),
            pl.BlockSpec((1, d), lambda i, j: (0, 0)),
        ],
        out_specs=out_specs,
        out_shape=out_shape,
        scratch_shapes=[pltpu.VMEM((tm, d), BF16)],
        compiler_params=_cparams(("arbitrary", "arbitrary")),
        name="ffn_" + tail + ("_emit" if emit else ""),
    )(x, gn, wg, wu, wd, gtail)


def _mm_epilogue(kind, accs, extras):
    if kind == "identity":
        return accs[0]
    if kind == "gelu":
        return _gelu(accs[0])
    if kind == "sigmoid":
        return _sigmoid(accs[0])
    if kind == "gated_glu":
        return extras[0] * (accs[0] * _sigmoid(accs[1]))
    if kind == "residual":
        return extras[0] + accs[0]
    raise ValueError(kind)


def _mm_kernel(*refs, n_w, n_extra, kind, emit):
    lhs_ref = refs[0]
    w_refs = refs[1:1 + n_w]
    extra_refs = refs[1 + n_w:1 + n_w + n_extra]
    out_ref = refs[1 + n_w + n_extra]
    emit_refs = refs[2 + n_w + n_extra:] if emit else [None] * n_w
    lhs = lhs_ref[...]
    accs = [jnp.dot(lhs, _as_bf16(w, e), preferred_element_type=F32) for w, e in zip(w_refs, emit_refs)]
    out_ref[...] = _mm_epilogue(kind, accs, [e[...] for e in extra_refs]).astype(out_ref.dtype)


def _mm(lhs, weights, extras, *, kind, n_out, out_dtype, tm, tn, emit, name):
    m, k = lhs.shape
    assert not emit or m == tm
    in_specs = [pl.BlockSpec((tm, k), lambda i, j: (i, 0))]
    args = [lhs]
    for w, c0 in weights:
        in_specs.append(_col_tile_spec(w, k, tn, c0))
        args.append(w)
    for e, c0 in extras:
        in_specs.append(pl.BlockSpec((tm, tn), lambda i, j, o=c0 // tn: (i, j + o)))
        args.append(e)
    out_specs = [pl.BlockSpec((tm, tn), lambda i, j: (i, j))]
    out_shape = [jax.ShapeDtypeStruct((m, n_out), out_dtype)]
    if emit:
        out_specs += [pl.BlockSpec((1, k, tn), lambda i, j: (j, 0, 0)) for _ in weights]
        out_shape += [jax.ShapeDtypeStruct((n_out // tn, k, tn), BF16) for _ in weights]
    kern = functools.partial(_mm_kernel, n_w=len(weights), n_extra=len(extras), kind=kind, emit=emit)
    res = pl.pallas_call(
        kern,
        grid=(m // tm, n_out // tn),
        in_specs=in_specs,
        out_specs=out_specs,
        out_shape=out_shape,
        compiler_params=_cparams(("arbitrary", "arbitrary")),
        name=name + ("_emit" if emit else ""),
    )(*args)
    return res[0], list(res[1:])


def _disc_kernel(lre_ref, lim_ref, ldt_ref, bre_ref, bim_ref, are_ref, aim_ref, bbre_ref, bbim_ref):
    lam_re = lre_ref[...]
    lam_im = lim_ref[...]
    dt = jnp.exp(ldt_ref[...])
    mag = jnp.exp(dt * lam_re)
    a_re = mag * jnp.cos(dt * lam_im)
    a_im = mag * jnp.sin(dt * lam_im)
    nr, ni = a_re - 1.0, a_im
    den = lam_re * lam_re + lam_im * lam_im
    coef_re = (nr * lam_re + ni * lam_im) / den
    coef_im = (ni * lam_re - nr * lam_im) / den
    are_ref[...] = a_re
    aim_ref[...] = a_im
    b_re = bre_ref[...]
    b_im = bim_ref[...]
    bbre_ref[...] = coef_re[None] * b_re - coef_im[None] * b_im
    bbim_ref[...] = coef_re[None] * b_im + coef_im[None] * b_re


def _discretise(lam_re, lam_im, log_dt, b_re, b_im):
    g, p = lam_re.shape
    h = b_re.shape[-1]
    b_re_t = jnp.transpose(b_re, (2, 0, 1))
    b_im_t = jnp.transpose(b_im, (2, 0, 1))
    return pl.pallas_call(
        _disc_kernel,
        out_shape=(jax.ShapeDtypeStruct((g, p), F32), jax.ShapeDtypeStruct((g, p), F32),
                   jax.ShapeDtypeStruct((h, g, p), F32), jax.ShapeDtypeStruct((h, g, p), F32)),
        name="s5_discretise",
    )(lam_re, lam_im, log_dt.reshape(g, 1), b_re_t, b_im_t)


def _block_diag_weights(bb_re, bb_im, c_re, c_im):
    h, g, p = bb_re.shape
    nb = g // GROUPS_PER_BLOCK
    eye = jnp.eye(GROUPS_PER_BLOCK, dtype=F32)

    def in_proj(bb):
        x = jnp.transpose(bb, (1, 0, 2)).reshape(nb, GROUPS_PER_BLOCK, h, p)
        x = x[:, :, :, None, :] * eye[None, :, None, :, None]
        return x.reshape(nb, UB, SB)

    def out_proj(c):
        x = jnp.transpose(c.reshape(nb, GROUPS_PER_BLOCK, h, p), (0, 1, 3, 2))
        x = x[:, :, :, None, :] * eye[None, :, None, :, None]
        return x.reshape(nb, SB, UB)

    bw = jnp.concatenate([in_proj(bb_re), in_proj(bb_im)], axis=2).astype(BF16)
    cw = jnp.concatenate([out_proj(c_re), -out_proj(c_im)], axis=1).astype(BF16)
    return bw, cw


def _ssm_prompt_kernel(u_ref, bw_ref, cw_ref, are_ref, aim_ref, d_ref,
                       y_ref, sre_ref, sim_ref,
                       bure, buim, st_re, st_im, *, nbatch):
    tc = pl.program_id(1)

    @pl.when(tc == 0)
    def _():
        st_re[...] = jnp.zeros_like(st_re)
        st_im[...] = jnp.zeros_like(st_im)

    def slab_ref(c):
        return (bure, c) if c < SLABS else (buim, c - SLABS)

    u_all = u_ref[...].reshape(nbatch * SCAN_T, UB)
    u_bf = u_all.astype(BF16)
    per_dot = 4
    for n in range(2 * SLABS // per_dot):
        bu = jnp.dot(u_bf, bw_ref[0, :, per_dot * n * LANES:per_dot * (n + 1) * LANES],
                     preferred_element_type=F32)
        for h in range(per_dot):
            ref, c = slab_ref(per_dot * n + h)
            for b in range(nbatch):
                ref[c, b * SCAN_PITCH:b * SCAN_PITCH + SCAN_T, :] = (
                    bu[b * SCAN_T:(b + 1) * SCAN_T, h * LANES:(h + 1) * LANES])

    for c0 in range(0, SLABS, SCAN_INTERLEAVE):
        slabs = list(range(c0, c0 + SCAN_INTERLEAVE))
        a_re = [jnp.broadcast_to(are_ref[0, c], (nbatch, LANES)) for c in slabs]
        a_im = [jnp.broadcast_to(aim_ref[0, c], (nbatch, LANES)) for c in slabs]
        init = tuple(st_re[c] for c in slabs) + tuple(st_im[c] for c in slabs)

        def step(t, carry, slabs=slabs, a_re=a_re, a_im=a_im):
            n = len(slabs)
            new_re, new_im = [], []
            for k, c in enumerate(slabs):
                s_r, s_i = carry[k], carry[n + k]
                rows = pl.ds(t, nbatch, stride=SCAN_PITCH)
                n_r = a_re[k] * s_r - a_im[k] * s_i + bure[c, rows, :]
                n_i = a_re[k] * s_i + a_im[k] * s_r + buim[c, rows, :]
                bure[c, rows, :] = n_r
                buim[c, rows, :] = n_i
                new_re.append(n_r)
                new_im.append(n_i)
            return tuple(new_re) + tuple(new_im)

        fin = lax.fori_loop(0, SCAN_T, step, init, unroll=4)
        for k, c in enumerate(slabs):
            st_re[c] = fin[k]
            st_im[c] = fin[len(slabs) + k]

    d = d_ref[0]
    for b in range(nbatch):
        r0 = b * SCAN_PITCH
        y = d * u_ref[b]
        for n in range(SLABS):
            (ref0, c0), (ref1, c1) = slab_ref(2 * n), slab_ref(2 * n + 1)
            s_pair = jnp.concatenate([ref0[c0, r0:r0 + SCAN_T, :], ref1[c1, r0:r0 + SCAN_T, :]],
                                     axis=1).astype(BF16)
            y = y + jnp.dot(s_pair, cw_ref[0, 2 * n * LANES:(2 * n + 2) * LANES, :],
                            preferred_element_type=F32)
        y_ref[b] = _gelu(y).astype(y_ref.dtype)

    @pl.when(tc == pl.num_programs(1) - 1)
    def _():
        for c in range(SLABS):
            sre_ref[:, c * LANES:(c + 1) * LANES] = st_re[c]
            sim_ref[:, c * LANES:(c + 1) * LANES] = st_im[c]


def _ssm_prompt(z3, bw, cw, a_re, a_im, d_skip):
    nbatch, seq, width = z3.shape
    nb = bw.shape[0]
    states = nb * SB
    kern = functools.partial(_ssm_prompt_kernel, nbatch=nbatch)
    return pl.pallas_call(
        kern,
        grid=(nb, seq // SCAN_T),
        in_specs=[
            pl.BlockSpec((nbatch, SCAN_T, UB), lambda g, t: (0, t, g)),
            pl.BlockSpec((1, UB, 2 * SB), lambda g, t: (g, 0, 0)),
            pl.BlockSpec((1, 2 * SB, UB), lambda g, t: (g, 0, 0)),
            pl.BlockSpec((1, SLABS, 1, LANES), lambda g, t: (g, 0, 0, 0)),
            pl.BlockSpec((1, SLABS, 1, LANES), lambda g, t: (g, 0, 0, 0)),
            pl.BlockSpec((1, 1, UB), lambda g, t: (g, 0, 0)),
        ],
        out_specs=(
            pl.BlockSpec((nbatch, SCAN_T, UB), lambda g, t: (0, t, g)),
            pl.BlockSpec((nbatch, SB), lambda g, t: (0, g)),
            pl.BlockSpec((nbatch, SB), lambda g, t: (0, g)),
        ),
        out_shape=(
            jax.ShapeDtypeStruct((nbatch, seq, width), BF16),
            jax.ShapeDtypeStruct((nbatch, states), F32),
            jax.ShapeDtypeStruct((nbatch, states), F32),
        ),
        scratch_shapes=[
            pltpu.VMEM((SLABS, nbatch * SCAN_PITCH, LANES), F32),
            pltpu.VMEM((SLABS, nbatch * SCAN_PITCH, LANES), F32),
            pltpu.VMEM((SLABS, nbatch, LANES), F32),
            pltpu.VMEM((SLABS, nbatch, LANES), F32),
        ],
        compiler_params=_cparams(("arbitrary", "arbitrary")),
        name="ssm_prompt",
    )(z3, bw, cw, a_re.reshape(nb, SLABS, 1, LANES), a_im.reshape(nb, SLABS, 1, LANES),
      d_skip.reshape(nb, 1, UB))


def _ssm_sample_kernel(u_ref, bw_ref, cw_ref, are_ref, aim_ref, d_ref, x0re_ref, x0im_ref,
                       y_ref, sre_ref, sim_ref):
    u = u_ref[...]
    bu = jnp.dot(u.astype(BF16), bw_ref[0], preferred_element_type=F32)
    a_re = are_ref[0]
    a_im = aim_ref[0]
    x_re = x0re_ref[...]
    x_im = x0im_ref[...]
    s_re = a_re * x_re - a_im * x_im + bu[:, :SB]
    s_im = a_re * x_im + a_im * x_re + bu[:, SB:]
    sre_ref[...] = s_re
    sim_ref[...] = s_im
    s_cat = jnp.concatenate([s_re, s_im], axis=1).astype(BF16)
    y = jnp.dot(s_cat, cw_ref[0], preferred_element_type=F32) + d_ref[0] * u
    y_ref[...] = _gelu(y).astype(y_ref.dtype)


def _ssm_sample(z, bw, cw, a_re, a_im, d_skip, x0_re, x0_im):
    nbatch = z.shape[0]
    nb = bw.shape[0]
    return pl.pallas_call(
        _ssm_sample_kernel,
        grid=(nb,),
        in_specs=[
            pl.BlockSpec((nbatch, UB), lambda g: (0, g)),
            pl.BlockSpec((1, UB, 2 * SB), lambda g: (g, 0, 0)),
            pl.BlockSpec((1, 2 * SB, UB), lambda g: (g, 0, 0)),
            pl.BlockSpec((1, 1, SB), lambda g: (g, 0, 0)),
            pl.BlockSpec((1, 1, SB), lambda g: (g, 0, 0)),
            pl.BlockSpec((1, 1, UB), lambda g: (g, 0, 0)),
            pl.BlockSpec((nbatch, SB), lambda g: (0, g)),
            pl.BlockSpec((nbatch, SB), lambda g: (0, g)),
        ],
        out_specs=(
            pl.BlockSpec((nbatch, UB), lambda g: (0, g)),
            pl.BlockSpec((nbatch, SB), lambda g: (0, g)),
            pl.BlockSpec((nbatch, SB), lambda g: (0, g)),
        ),
        out_shape=(
            jax.ShapeDtypeStruct((nbatch, nb * UB), BF16),
            jax.ShapeDtypeStruct((nbatch, nb * SB), F32),
            jax.ShapeDtypeStruct((nbatch, nb * SB), F32),
        ),
        compiler_params=_cparams(("arbitrary",)),
        name="ssm_sample",
    )(z, bw, cw, a_re.reshape(nb, 1, SB), a_im.reshape(nb, 1, SB), d_skip.reshape(nb, 1, UB),
      x0_re, x0_im)


def _gmlp_prompt_kernel(gu_ref, gv_ref, gnv_ref, ws_ref, bias_ref, wo_ref, gate_ref, p_ref,
                        out_ref, s_scr, wt_scr, *, tm, heads):
    i = pl.program_id(0)
    j = pl.program_id(1)

    @pl.when(jnp.logical_and(i == 0, j == 0))
    def _():
        row = lax.broadcasted_iota(jnp.int32, (CHUNK, CHUNK), 0)
        col = lax.broadcasted_iota(jnp.int32, (CHUNK, CHUNK), 1)
        mask = (col <= row).astype(F32)
        for g in range(heads):
            wt_scr[g] = (ws_ref[g] * mask).astype(BF16)

    @pl.when(j == 0)
    def _():
        gnv = gnv_ref[...]

        def body(c, carry):
            r0 = pl.multiple_of(c * CHUNK, CHUNK)
            gv = gv_ref[pl.ds(r0, CHUNK), :]
            r = lax.rsqrt(jnp.mean(gv * gv, axis=-1, keepdims=True) + EPS)
            v = (gv * r * gnv).astype(BF16)
            for g in range(heads):
                sl = slice(g * GMLP_HEAD, (g + 1) * GMLP_HEAD)
                mixed = jnp.dot(wt_scr[g], v[:, sl], preferred_element_type=F32) + bias_ref[:, sl]
                s_scr[pl.ds(r0, CHUNK), sl] = (gu_ref[pl.ds(r0, CHUNK), sl] * mixed).astype(BF16)
            return carry

        lax.fori_loop(0, tm // CHUNK, body, 0)

    yb = jnp.dot(s_scr[...], wo_ref[0], preferred_element_type=F32)
    out_ref[...] = (p_ref[...] + gate_ref[...] * yb).astype(out_ref.dtype)


def _gmlp_prompt(guv, gnv, w_s, bias_full, w_gout, gates, p, *, gate_col, tm, tn):
    m = guv.shape[0]
    ntiles, width, _ = w_gout.shape
    d = ntiles * tn
    heads = w_s.shape[0]
    kern = functools.partial(_gmlp_prompt_kernel, tm=tm, heads=heads)
    return pl.pallas_call(
        kern,
        grid=(m // tm, d // tn),
        in_specs=[
            pl.BlockSpec((tm, width), lambda i, j: (i, 0)),
            pl.BlockSpec((tm, width), lambda i, j: (i, 1)),
            pl.BlockSpec((1, width), lambda i, j: (0, 0)),
            pl.BlockSpec((heads, CHUNK, CHUNK), lambda i, j: (0, 0, 0)),
            pl.BlockSpec((CHUNK, width), lambda i, j: (0, 0)),
            _col_tile_spec(w_gout, width, tn),
            pl.BlockSpec((tm, tn), lambda i, j: (i, j + gate_col // tn)),
            pl.BlockSpec((tm, tn), lambda i, j: (i, j)),
        ],
        out_specs=pl.BlockSpec((tm, tn), lambda i, j: (i, j)),
        out_shape=jax.ShapeDtypeStruct((m, d), BF16),
        scratch_shapes=[pltpu.VMEM((tm, width), BF16), pltpu.VMEM((heads, CHUNK, CHUNK), BF16)],
        compiler_params=_cparams(("arbitrary", "arbitrary")),
        name="gmlp_prompt",
    )(guv, guv, gnv, w_s, bias_full, w_gout, gates, p)


def _gmlp_sample_kernel(gu_ref, gv_ref, gnv_ref, wdiag_ref, bias_ref, wo_ref, gate_ref, p_ref,
                        out_ref, v_ref, wemit_ref, s_scr):
    j = pl.program_id(0)

    @pl.when(j == 0)
    def _():
        gv = gv_ref[...]
        r = lax.rsqrt(jnp.mean(gv * gv, axis=-1, keepdims=True) + EPS)
        v = gv * r * gnv_ref[...]
        v_ref[...] = v
        mixed = wdiag_ref[...] * v + bias_ref[...]
        s_scr[...] = (gu_ref[...] * mixed).astype(BF16)

    yb = jnp.dot(s_scr[...], _as_bf16(wo_ref, wemit_ref), preferred_element_type=F32)
    out_ref[...] = (p_ref[...] + gate_ref[...] * yb).astype(out_ref.dtype)


def _gmlp_sample(guv, gnv, wdiag, bias0, w_gout, gates, p, *, gate_col, tn):
    m = guv.shape[0]
    width, d = w_gout.shape
    return pl.pallas_call(
        _gmlp_sample_kernel,
        grid=(d // tn,),
        in_specs=[
            pl.BlockSpec((m, width), lambda j: (0, 0)),
            pl.BlockSpec((m, width), lambda j: (0, 1)),
            pl.BlockSpec((1, width), lambda j: (0, 0)),
            pl.BlockSpec((1, width), lambda j: (0, 0)),
            pl.BlockSpec((1, width), lambda j: (0, 0)),
            pl.BlockSpec((width, tn), lambda j: (0, j)),
            pl.BlockSpec((m, tn), lambda j: (0, j + gate_col // tn)),
            pl.BlockSpec((m, tn), lambda j: (0, j)),
        ],
        out_specs=(pl.BlockSpec((m, tn), lambda j: (0, j)),
                   pl.BlockSpec((m, width), lambda j: (0, 0)),
                   pl.BlockSpec((1, width, tn), lambda j: (j, 0, 0))),
        out_shape=(jax.ShapeDtypeStruct((m, d), BF16), jax.ShapeDtypeStruct((m, width), F32),
                   jax.ShapeDtypeStruct((d // tn, width, tn), BF16)),
        scratch_shapes=[pltpu.VMEM((m, width), BF16)],
        compiler_params=_cparams(("arbitrary",)),
        name="gmlp_sample_emit",
    )(guv, guv, gnv, wdiag, bias0, w_gout, gates, p)


def _layer(x, prm, w, *, nbatch, x0, gfinal, tm_ffn, tm_mm):
    sample = x0 is not None
    d_model = x.shape[1]
    ssm_width = prm["bw"].shape[0] * UB
    gmlp_width = prm["gnv"].shape[1]
    o1 = ssm_width
    o3 = o1 + 2 * gmlp_width
    tn = 512
    wb = {}

    res = _ffn(x, prm["n1"], w["f1g"], w["f1u"], w["f1d"], prm["nmix"],
               tail="norm_out", tm=tm_ffn, tf=256, emit=sample)
    x, h2 = res[0], res[1]
    if sample:
        wb["f1g"], wb["f1u"], wb["f1d"] = res[2:]

    cols = {}
    if sample:
        w = dict(w, win_ssm=w["w_in"], win_uv=w["w_in"], win_gate=w["w_in"],
                 glu_a=w["w_glu"], glu_b=w["w_glu"])
        cols = dict(win_uv=o1, win_gate=o3, glu_b=d_model)

    def mm(lhs, wkeys, extras, kind, n_out, out_dtype, name):
        out, emitted = _mm(lhs, [(w[k], cols.get(k, 0)) for k in wkeys], extras, kind=kind,
                           n_out=n_out, out_dtype=out_dtype, tm=tm_mm, tn=2 * tn, emit=sample, name=name)
        wb.update(zip(wkeys, emitted))
        return out

    z_ssm = mm(h2, ["win_ssm"], [], "identity", o1, F32, "w_in_ssm")
    guv = mm(h2, ["win_uv"], [], "gelu", 2 * gmlp_width, F32, "w_in_uv")
    gates = mm(h2, ["win_gate"], [], "sigmoid", 2 * d_model, F32, "w_in_gate")

    if sample:
        y, s_re, s_im = _ssm_sample(z_ssm, prm["bw"], prm["cw"], prm["a_re"], prm["a_im"],
                                    prm["d_skip"], x0[0], x0[1])
    else:
        seq = x.shape[0] // nbatch
        y, s_re, s_im = _ssm_prompt(z_ssm.reshape(nbatch, seq, ssm_width), prm["bw"], prm["cw"],
                                    prm["a_re"], prm["a_im"], prm["d_skip"])
        y = y.reshape(nbatch * seq, ssm_width)
    p = mm(y, ["glu_a", "glu_b"], [(gates, 0)], "gated_glu", d_model, F32, "ssm_glu")

    if sample:
        merged, v_rows, wb["w_gout"] = _gmlp_sample(guv, prm["gnv"], prm["w_diag"], prm["bias0"],
                                                    w["w_gout"], gates, p, gate_col=d_model, tn=tn)
    else:
        merged = _gmlp_prompt(guv, prm["gnv"], prm["w_s"], prm["bias_full"], w["w_gout"], gates, p,
                              gate_col=d_model, tm=tm_mm, tn=tn)
        v_rows = None
    x = mm(merged, ["w_out"], [(x, 0)], "residual", d_model, F32, "out_proj")
    res = _ffn(x, prm["n2"], w["f2g"], w["f2u"], w["f2d"], gfinal,
               tail="final", tm=min(2 * tm_ffn, x.shape[0]), tf=256, emit=sample)
    if sample:
        wb["f2g"], wb["f2u"], wb["f2d"] = res[1:]
    return res[0], s_re, s_im, v_rows, wb


def kernel(x_prompt, x_sample, state_ssm_re, state_ssm_im, norm_ffn1, ffn1_gate, ffn1_up, ffn1_down, norm_mix, w_in, ssm_lambda_re, ssm_lambda_im, ssm_log_dt, ssm_b_re, ssm_b_im, ssm_c_re, ssm_c_im, ssm_d, ssm_w_glu, gmlp_norm_v, gmlp_w_s, gmlp_b_s, gmlp_w_out, w_out, norm_ffn2, ffn2_gate, ffn2_up, ffn2_down, norm_final):
    depth = w_in.shape[0]
    assert depth == 1, "the final RMSNorm is fused into the last FFN; one layer per step"
    batch, seq, d_model = x_prompt.shape
    dec_batch, dec_seq, _ = x_sample.shape
    assert dec_seq == 1
    groups, states = ssm_lambda_re.shape[1:]
    gmlp_width = gmlp_norm_v.shape[1]
    heads = gmlp_w_s.shape[1]
    head_dim = gmlp_width // heads
    assert seq % SCAN_T == 0 and seq % CHUNK == 0 and head_dim == GMLP_HEAD
    assert ssm_b_re.shape[-1] == SSM_GROUP and states == SSM_STATE and groups % GROUPS_PER_BLOCK == 0

    l = 0
    yp = x_prompt.reshape(batch * seq, d_model)
    ys = x_sample.reshape(dec_batch, d_model)
    gfin = norm_final.reshape(1, d_model)
    a_re, a_im, bb_re, bb_im = _discretise(ssm_lambda_re[l], ssm_lambda_im[l], ssm_log_dt[l],
                                           ssm_b_re[l], ssm_b_im[l])
    bw, cw = _block_diag_weights(bb_re, bb_im, ssm_c_re[l], ssm_c_im[l])
    prm = dict(
        n1=norm_ffn1[l].reshape(1, d_model), nmix=norm_mix[l].reshape(1, d_model),
        n2=norm_ffn2[l].reshape(1, d_model),
        bw=bw, cw=cw, a_re=a_re, a_im=a_im, d_skip=ssm_d[l],
        gnv=gmlp_norm_v[l].reshape(1, gmlp_width), w_s=gmlp_w_s[l],
        bias_full=jnp.repeat(gmlp_b_s[l].T, head_dim, axis=1),
        w_diag=jnp.repeat(gmlp_w_s[l][:, 0, 0], head_dim).reshape(1, gmlp_width),
        bias0=jnp.repeat(gmlp_b_s[l][:, 0], head_dim).reshape(1, gmlp_width),
    )
    w_f32 = dict(f1g=ffn1_gate[l], f1u=ffn1_up[l], f1d=ffn1_down[l], w_in=w_in[l],
                 w_glu=ssm_w_glu[l], w_gout=gmlp_w_out[l], w_out=w_out[l],
                 f2g=ffn2_gate[l], f2u=ffn2_up[l], f2d=ffn2_down[l])
    x0 = (state_ssm_re[l].reshape(dec_batch, groups * states),
          state_ssm_im[l].reshape(dec_batch, groups * states))
    ys, sr, si, vr, w_bf16 = _layer(ys, prm, w_f32, nbatch=dec_batch, x0=x0, gfinal=gfin,
                                    tm_ffn=dec_batch, tm_mm=dec_batch)
    yp, pr, pi, _, _ = _layer(yp, prm, w_bf16, nbatch=batch, x0=None, gfinal=gfin,
                              tm_ffn=512, tm_mm=1024)
    return (yp.reshape(batch, seq, d_model), ys.reshape(dec_batch, dec_seq, d_model),
            pr.reshape(1, batch, groups, states), pi.reshape(1, batch, groups, states),
            sr.reshape(1, dec_batch, groups, states), si.reshape(1, dec_batch, groups, states),
            vr.reshape(1, dec_batch, dec_seq, gmlp_width))
```

```python
import functools
import math

import jax
import jax.numpy as jnp
from jax import lax
from jax.experimental import pallas as pl
from jax.experimental.pallas import tpu as pltpu

F32 = jnp.float32
BF16 = jnp.bfloat16

EPS = 1e-6
LANES = 128
SSM_GROUP = 16
SSM_STATE = 64
GROUPS_PER_BLOCK = 16
UB = GROUPS_PER_BLOCK * SSM_GROUP
SB = GROUPS_PER_BLOCK * SSM_STATE
SLABS = SB // LANES
SCAN_T = 256
SCAN_PITCH = SCAN_T + 8
CHUNK = 128
GMLP_HEAD = 128
VMEM_LIMIT = 56 * 1024 * 1024


def _cparams(sem):
    return pltpu.CompilerParams(dimension_semantics=sem, vmem_limit_bytes=VMEM_LIMIT)


def _gelu(x):
    c = math.sqrt(2.0 / math.pi)
    return 0.5 * x * (1.0 + jnp.tanh(c * (x + 0.044715 * (x * x * x))))


def _sigmoid(x):
    return 1.0 / (1.0 + jnp.exp(-x))


def _as_bf16(w_ref, emit_ref):
    w = w_ref[0] if len(w_ref.shape) == 3 else w_ref[...]
    if w.dtype != BF16:
        w = w.astype(BF16)
    if emit_ref is not None:
        if len(emit_ref.shape) == 3:
            emit_ref[0] = w
        else:
            emit_ref[...] = w
    return w


def _col_tile_spec(w, k, tn, col0=0):
    if w.ndim == 3:
        assert w.shape[1:] == (k, tn) and col0 == 0
        return pl.BlockSpec((1, k, tn), lambda i, j: (j, 0, 0))
    return pl.BlockSpec((k, tn), lambda i, j, o=col0 // tn: (0, j + o))


def _ffn_kernel(*refs, tm, d_model, col_chunk, row_chunk, tail, emit):
    x_ref, gn_ref, wg_ref, wu_ref, wd_ref, gt_ref = refs[:6]
    rest = list(refs[6:])
    out_ref = rest.pop(0)
    h2_ref = rest.pop(0) if tail == "norm_out" else None
    emits = [rest.pop(0) for _ in range(3)] if emit else [None] * 3
    h_scr = h2_ref if h2_ref is not None else rest.pop(0)
    assert not rest
    j = pl.program_id(1)

    @pl.when(j == 0)
    def _():
        g = gn_ref[...]

        def body(c, carry):
            r0 = pl.multiple_of(c * row_chunk, row_chunk)
            xf = x_ref[pl.ds(r0, row_chunk), :]
            r = lax.rsqrt(jnp.mean(xf * xf, axis=-1, keepdims=True) + EPS)
            h_scr[pl.ds(r0, row_chunk), :] = (xf * r * g).astype(BF16)
            return carry

        lax.fori_loop(0, tm // row_chunk, body, 0)
        out_ref[...] = jnp.zeros_like(out_ref)

    h = h_scr[...]
    g = jnp.dot(h, _as_bf16(wg_ref, emits[0]), preferred_element_type=F32)
    u = jnp.dot(h, _as_bf16(wu_ref, emits[1]), preferred_element_type=F32)
    a = (g * _sigmoid(g) * u).astype(BF16)
    wd = _as_bf16(wd_ref, emits[2])
    for n in range(d_model // col_chunk):
        sl = slice(n * col_chunk, (n + 1) * col_chunk)
        out_ref[:, sl] += jnp.dot(a, wd[:, sl], preferred_element_type=F32)

    @pl.when(j == pl.num_programs(1) - 1)
    def _():
        gt = gt_ref[...]

        def body(c, carry):
            r0 = pl.multiple_of(c * row_chunk, row_chunk)
            y = x_ref[pl.ds(r0, row_chunk), :] + 0.5 * out_ref[pl.ds(r0, row_chunk), :]
            r = lax.rsqrt(jnp.mean(y * y, axis=-1, keepdims=True) + EPS)
            if tail == "norm_out":
                out_ref[pl.ds(r0, row_chunk), :] = y
                h2_ref[pl.ds(r0, row_chunk), :] = (y * r * gt).astype(BF16)
            else:
                out_ref[pl.ds(r0, row_chunk), :] = y * r * gt
            return carry

        lax.fori_loop(0, tm // row_chunk, body, 0)


def _ffn(x, gn, wg, wu, wd, gtail, *, tail, tm, tf, emit):
    m, d = x.shape
    f = wd.shape[0]
    assert not emit or m == tm
    kern = functools.partial(_ffn_kernel, tm=tm, d_model=d, col_chunk=512,
                             row_chunk=min(tm, 64), tail=tail, emit=emit)
    out_mode = pl.Buffered(1) if tm * d * 4 > 8 * 1024 * 1024 else None
    out_specs = [pl.BlockSpec((tm, d), lambda i, j: (i, 0), pipeline_mode=out_mode)]
    out_shape = [jax.ShapeDtypeStruct((m, d), F32)]
    scratch = [pltpu.VMEM((tm, d), BF16)]
    if tail == "norm_out":
        out_specs.append(pl.BlockSpec((tm, d), lambda i, j: (i, 0), pipeline_mode=out_mode))
        out_shape.append(jax.ShapeDtypeStruct((m, d), BF16))
        scratch = []
    if emit:
        out_specs += [pl.BlockSpec((1, d, tf), lambda i, j: (j, 0, 0)),
                      pl.BlockSpec((1, d, tf), lambda i, j: (j, 0, 0)),
                      pl.BlockSpec((tf, d), lambda i, j: (j, 0))]
        out_shape += [jax.ShapeDtypeStruct((f // tf, d, tf), BF16),
                      jax.ShapeDtypeStruct((f // tf, d, tf), BF16),
                      jax.ShapeDtypeStruct((f, d), BF16)]
    return pl.pallas_call(
        kern,
        grid=(m // tm, f // tf),
        in_specs=[
            pl.BlockSpec((tm, d), lambda i, j: (i, 0), pipeline_mode=pl.Buffered(1)),
            pl.BlockSpec((1, d), lambda i, j: (0, 0)),
            _col_tile_spec(wg, d, tf),
            _col_tile_spec(wu, d, tf),
            pl.BlockSpec((tf, d), lambda i, j: (j, 0)),
            pl.BlockSpec((1, d), lambda i, j: (0, 0)),
        ],
        out_specs=out_specs,
        out_shape=out_shape,
        scratch_shapes=scratch,
        compiler_params=_cparams(("arbitrary", "arbitrary")),
        name="ffn_" + tail + ("_emit" if emit else ""),
    )(x, gn, wg, wu, wd, gtail)


def _mm_epilogue(kind, accs, extras):
    if kind == "identity":
        return accs[0]
    if kind == "gelu":
        return _gelu(accs[0])
    if kind == "sigmoid":
        return _sigmoid(accs[0])
    if kind == "gated_glu":
        return extras[0] * (accs[0] * _sigmoid(accs[1]))
    if kind == "residual":
        return extras[0] + accs[0]
    raise ValueError(kind)


def _mm_kernel(*refs, n_w, n_extra, kind, emit):
    lhs_ref = refs[0]
    w_refs = refs[1:1 + n_w]
    extra_refs = refs[1 + n_w:1 + n_w + n_extra]
    out_ref = refs[1 + n_w + n_extra]
    emit_refs = refs[2 + n_w + n_extra:] if emit else [None] * n_w
    lhs = lhs_ref[...]
    accs = [jnp.dot(lhs, _as_bf16(w, e), preferred_element_type=F32) for w, e in zip(w_refs, emit_refs)]
    out_ref[...] = _mm_epilogue(kind, accs, [e[...] for e in extra_refs]).astype(out_ref.dtype)


def _mm(lhs, weights, extras, *, kind, n_out, out_dtype, tm, tn, emit, name):
    m, k = lhs.shape
    assert not emit or m == tm
    in_specs = [pl.BlockSpec((tm, k), lambda i, j: (i, 0))]
    args = [lhs]
    for w, c0 in weights:
        in_specs.append(_col_tile_spec(w, k, tn, c0))
        args.append(w)
    for e, c0 in extras:
        in_specs.append(pl.BlockSpec((tm, tn), lambda i, j, o=c0 // tn: (i, j + o)))
        args.append(e)
    out_specs = [pl.BlockSpec((tm, tn), lambda i, j: (i, j))]
    out_shape = [jax.ShapeDtypeStruct((m, n_out), out_dtype)]
    if emit:
        out_specs += [pl.BlockSpec((1, k, tn), lambda i, j: (j, 0, 0)) for _ in weights]
        out_shape += [jax.ShapeDtypeStruct((n_out // tn, k, tn), BF16) for _ in weights]
    kern = functools.partial(_mm_kernel, n_w=len(weights), n_extra=len(extras), kind=kind, emit=emit)
    res = pl.pallas_call(
        kern,
        grid=(m // tm, n_out // tn),
        in_specs=in_specs,
        out_specs=out_specs,
        out_shape=out_shape,
        compiler_params=_cparams(("arbitrary", "arbitrary")),
        name=name + ("_emit" if emit else ""),
    )(*args)
    return res[0], list(res[1:])


def _disc_kernel(lre_ref, lim_ref, ldt_ref, bre_ref, bim_ref, are_ref, aim_ref, bbre_ref, bbim_ref):
    lam_re = lre_ref[...]
    lam_im = lim_ref[...]
    dt = jnp.exp(ldt_ref[...])
    mag = jnp.exp(dt * lam_re)
    a_re = mag * jnp.cos(dt * lam_im)
    a_im = mag * jnp.sin(dt * lam_im)
    nr, ni = a_re - 1.0, a_im
    den = lam_re * lam_re + lam_im * lam_im
    coef_re = (nr * lam_re + ni * lam_im) / den
    coef_im = (ni * lam_re - nr * lam_im) / den
    are_ref[...] = a_re
    aim_ref[...] = a_im
    b_re = bre_ref[...]
    b_im = bim_ref[...]
    bbre_ref[...] = coef_re[None] * b_re - coef_im[None] * b_im
    bbim_ref[...] = coef_re[None] * b_im + coef_im[None] * b_re


def _discretise(lam_re, lam_im, log_dt, b_re, b_im):
    g, p = lam_re.shape
    h = b_re.shape[-1]
    b_re_t = jnp.transpose(b_re, (2, 0, 1))
    b_im_t = jnp.transpose(b_im, (2, 0, 1))
    return pl.pallas_call(
        _disc_kernel,
        out_shape=(jax.ShapeDtypeStruct((g, p), F32), jax.ShapeDtypeStruct((g, p), F32),
                   jax.ShapeDtypeStruct((h, g, p), F32), jax.ShapeDtypeStruct((h, g, p), F32)),
        name="s5_discretise",
    )(lam_re, lam_im, log_dt.reshape(g, 1), b_re_t, b_im_t)


def _block_diag_weights(bb_re, bb_im, c_re, c_im):
    h, g, p = bb_re.shape
    nb = g // GROUPS_PER_BLOCK
    eye = jnp.eye(GROUPS_PER_BLOCK, dtype=F32)

    def in_proj(bb):
        x = jnp.transpose(bb, (1, 0, 2)).reshape(nb, GROUPS_PER_BLOCK, h, p)
        x = x[:, :, :, None, :] * eye[None, :, None, :, None]
        return x.reshape(nb, UB, SB)

    def out_proj(c):
        x = jnp.transpose(c.reshape(nb, GROUPS_PER_BLOCK, h, p), (0, 1, 3, 2))
        x = x[:, :, :, None, :] * eye[None, :, None, :, None]
        return x.reshape(nb, SB, UB)

    bw = jnp.concatenate([in_proj(bb_re), in_proj(bb_im)], axis=2).astype(BF16)
    cw = jnp.concatenate([out_proj(c_re), -out_proj(c_im)], axis=1).astype(BF16)
    return bw, cw


def _ssm_prompt_kernel(u_ref, bw_ref, cw_ref, are_ref, aim_ref, d_ref,
                       y_ref, sre_ref, sim_ref,
                       bure, buim, st_re, st_im, *, nbatch):
    tc = pl.program_id(1)

    @pl.when(tc == 0)
    def _():
        st_re[...] = jnp.zeros_like(st_re)
        st_im[...] = jnp.zeros_like(st_im)

    pack = 8 // nbatch
    nq = SLABS // pack

    def slab_rows(c, b):
        ref, c = (bure, c) if c < SLABS else (buim, c - SLABS)
        r0 = ((c // nq) * nbatch + b) * SCAN_PITCH
        return ref, c % nq, slice(r0, r0 + SCAN_T)

    u_all = u_ref[...].reshape(nbatch * SCAN_T, UB)
    u_bf = u_all.astype(BF16)
    per_dot = 4
    for n in range(2 * SLABS // per_dot):
        bu = jnp.dot(u_bf, bw_ref[0, :, per_dot * n * LANES:per_dot * (n + 1) * LANES],
                     preferred_element_type=F32)
        for h in range(per_dot):
            for b in range(nbatch):
                ref, q, rows = slab_rows(per_dot * n + h, b)
                ref[q, rows, :] = bu[b * SCAN_T:(b + 1) * SCAN_T, h * LANES:(h + 1) * LANES]

    def packed(a_ref, q):
        return jnp.concatenate(
            [jnp.broadcast_to(a_ref[0, q + h * nq], (nbatch, LANES)) for h in range(pack)], axis=0)

    a_re = [packed(are_ref, q) for q in range(nq)]
    a_im = [packed(aim_ref, q) for q in range(nq)]
    init = tuple(st_re[q] for q in range(nq)) + tuple(st_im[q] for q in range(nq))

    def step(t, carry):
        rows = pl.ds(t, pack * nbatch, stride=SCAN_PITCH)
        new_re, new_im = [], []
        for q in range(nq):
            s_r, s_i = carry[q], carry[nq + q]
            n_r = a_re[q] * s_r - a_im[q] * s_i + bure[q, rows, :]
            n_i = a_re[q] * s_i + a_im[q] * s_r + buim[q, rows, :]
            bure[q, rows, :] = n_r
            buim[q, rows, :] = n_i
            new_re.append(n_r)
            new_im.append(n_i)
        return tuple(new_re) + tuple(new_im)

    fin = lax.fori_loop(0, SCAN_T, step, init, unroll=4)
    for q in range(nq):
        st_re[q] = fin[q]
        st_im[q] = fin[nq + q]

    d = d_ref[0]
    for b in range(nbatch):
        y = d * u_ref[b]
        for n in range(SLABS):
            (ref0, q0, rows0), (ref1, q1, rows1) = slab_rows(2 * n, b), slab_rows(2 * n + 1, b)
            s_pair = jnp.concatenate([ref0[q0, rows0, :], ref1[q1, rows1, :]], axis=1).astype(BF16)
            y = y + jnp.dot(s_pair, cw_ref[0, 2 * n * LANES:(2 * n + 2) * LANES, :],
                            preferred_element_type=F32)
        y_ref[b] = _gelu(y).astype(y_ref.dtype)

    @pl.when(tc == pl.num_programs(1) - 1)
    def _():
        for c in range(SLABS):
            q, h = c % nq, c // nq
            sre_ref[:, c * LANES:(c + 1) * LANES] = st_re[q, h * nbatch:(h + 1) * nbatch, :]
            sim_ref[:, c * LANES:(c + 1) * LANES] = st_im[q, h * nbatch:(h + 1) * nbatch, :]


def _ssm_prompt(z3, bw, cw, a_re, a_im, d_skip):
    nbatch, seq, width = z3.shape
    nb = bw.shape[0]
    states = nb * SB
    kern = functools.partial(_ssm_prompt_kernel, nbatch=nbatch)
    return pl.pallas_call(
        kern,
        grid=(nb, seq // SCAN_T),
        in_specs=[
            pl.BlockSpec((nbatch, SCAN_T, UB), lambda g, t: (0, t, g)),
            pl.BlockSpec((1, UB, 2 * SB), lambda g, t: (g, 0, 0)),
            pl.BlockSpec((1, 2 * SB, UB), lambda g, t: (g, 0, 0)),
            pl.BlockSpec((1, SLABS, 1, LANES), lambda g, t: (g, 0, 0, 0)),
            pl.BlockSpec((1, SLABS, 1, LANES), lambda g, t: (g, 0, 0, 0)),
            pl.BlockSpec((1, 1, UB), lambda g, t: (g, 0, 0)),
        ],
        out_specs=(
            pl.BlockSpec((nbatch, SCAN_T, UB), lambda g, t: (0, t, g)),
            pl.BlockSpec((nbatch, SB), lambda g, t: (0, g)),
            pl.BlockSpec((nbatch, SB), lambda g, t: (0, g)),
        ),
        out_shape=(
            jax.ShapeDtypeStruct((nbatch, seq, width), BF16),
            jax.ShapeDtypeStruct((nbatch, states), F32),
            jax.ShapeDtypeStruct((nbatch, states), F32),
        ),
        scratch_shapes=[
            pltpu.VMEM((SLABS * nbatch // 8, 8 * SCAN_PITCH, LANES), F32),
            pltpu.VMEM((SLABS * nbatch // 8, 8 * SCAN_PITCH, LANES), F32),
            pltpu.VMEM((SLABS * nbatch // 8, 8, LANES), F32),
            pltpu.VMEM((SLABS * nbatch // 8, 8, LANES), F32),
        ],
        compiler_params=_cparams(("arbitrary", "arbitrary")),
        name="ssm_prompt",
    )(z3, bw, cw, a_re.reshape(nb, SLABS, 1, LANES), a_im.reshape(nb, SLABS, 1, LANES),
      d_skip.reshape(nb, 1, UB))


def _ssm_sample_kernel(u_ref, bw_ref, cw_ref, are_ref, aim_ref, d_ref, x0re_ref, x0im_ref,
                       y_ref, sre_ref, sim_ref):
    u = u_ref[...]
    bu = jnp.dot(u.astype(BF16), bw_ref[0], preferred_element_type=F32)
    a_re = are_ref[0]
    a_im = aim_ref[0]
    x_re = x0re_ref[...]
    x_im = x0im_ref[...]
    s_re = a_re * x_re - a_im * x_im + bu[:, :SB]
    s_im = a_re * x_im + a_im * x_re + bu[:, SB:]
    sre_ref[...] = s_re
    sim_ref[...] = s_im
    s_cat = jnp.concatenate([s_re, s_im], axis=1).astype(BF16)
    y = jnp.dot(s_cat, cw_ref[0], preferred_element_type=F32) + d_ref[0] * u
    y_ref[...] = _gelu(y).astype(y_ref.dtype)


def _ssm_sample(z, bw, cw, a_re, a_im, d_skip, x0_re, x0_im):
    nbatch = z.shape[0]
    nb = bw.shape[0]
    return pl.pallas_call(
        _ssm_sample_kernel,
        grid=(nb,),
        in_specs=[
            pl.BlockSpec((nbatch, UB), lambda g: (0, g)),
            pl.BlockSpec((1, UB, 2 * SB), lambda g: (g, 0, 0)),
            pl.BlockSpec((1, 2 * SB, UB), lambda g: (g, 0, 0)),
            pl.BlockSpec((1, 1, SB), lambda g: (g, 0, 0)),
            pl.BlockSpec((1, 1, SB), lambda g: (g, 0, 0)),
            pl.BlockSpec((1, 1, UB), lambda g: (g, 0, 0)),
            pl.BlockSpec((nbatch, SB), lambda g: (0, g)),
            pl.BlockSpec((nbatch, SB), lambda g: (0, g)),
        ],
        out_specs=(
            pl.BlockSpec((nbatch, UB), lambda g: (0, g)),
            pl.BlockSpec((nbatch, SB), lambda g: (0, g)),
            pl.BlockSpec((nbatch, SB), lambda g: (0, g)),
        ),
        out_shape=(
            jax.ShapeDtypeStruct((nbatch, nb * UB), BF16),
            jax.ShapeDtypeStruct((nbatch, nb * SB), F32),
            jax.ShapeDtypeStruct((nbatch, nb * SB), F32),
        ),
        compiler_params=_cparams(("arbitrary",)),
        name="ssm_sample",
    )(z, bw, cw, a_re.reshape(nb, 1, SB), a_im.reshape(nb, 1, SB), d_skip.reshape(nb, 1, UB),
      x0_re, x0_im)


def _gmlp_prompt_kernel(gu_ref, gv_ref, gnv_ref, ws_ref, bias_ref, wo_ref, gate_ref, p_ref,
                        out_ref, s_scr, wt_scr, *, tm, heads):
    i = pl.program_id(0)
    j = pl.program_id(1)

    @pl.when(jnp.logical_and(i == 0, j == 0))
    def _():
        row = lax.broadcasted_iota(jnp.int32, (CHUNK, CHUNK), 0)
        col = lax.broadcasted_iota(jnp.int32, (CHUNK, CHUNK), 1)
        mask = (col <= row).astype(F32)
        for g in range(heads):
            wt_scr[g] = (ws_ref[g] * mask).astype(BF16)

    @pl.when(j == 0)
    def _():
        gnv = gnv_ref[...]

        def body(c, carry):
            r0 = pl.multiple_of(c * CHUNK, CHUNK)
            gv = gv_ref[pl.ds(r0, CHUNK), :]
            r = lax.rsqrt(jnp.mean(gv * gv, axis=-1, keepdims=True) + EPS)
            v = (gv * r * gnv).astype(BF16)
            for g in range(heads):
                sl = slice(g * GMLP_HEAD, (g + 1) * GMLP_HEAD)
                mixed = jnp.dot(wt_scr[g], v[:, sl], preferred_element_type=F32) + bias_ref[:, sl]
                s_scr[pl.ds(r0, CHUNK), sl] = (gu_ref[pl.ds(r0, CHUNK), sl] * mixed).astype(BF16)
            return carry

        lax.fori_loop(0, tm // CHUNK, body, 0)

    yb = jnp.dot(s_scr[...], wo_ref[0], preferred_element_type=F32)
    out_ref[...] = (p_ref[...] + gate_ref[...] * yb).astype(out_ref.dtype)


def _gmlp_prompt(guv, gnv, w_s, bias_full, w_gout, gates, p, *, gate_col, tm, tn):
    m = guv.shape[0]
    ntiles, width, _ = w_gout.shape
    d = ntiles * tn
    heads = w_s.shape[0]
    kern = functools.partial(_gmlp_prompt_kernel, tm=tm, heads=heads)
    return pl.pallas_call(
        kern,
        grid=(m // tm, d // tn),
        in_specs=[
            pl.BlockSpec((tm, width), lambda i, j: (i, 0)),
            pl.BlockSpec((tm, width), lambda i, j: (i, 1)),
            pl.BlockSpec((1, width), lambda i, j: (0, 0)),
            pl.BlockSpec((heads, CHUNK, CHUNK), lambda i, j: (0, 0, 0)),
            pl.BlockSpec((CHUNK, width), lambda i, j: (0, 0)),
            _col_tile_spec(w_gout, width, tn),
            pl.BlockSpec((tm, tn), lambda i, j: (i, j + gate_col // tn)),
            pl.BlockSpec((tm, tn), lambda i, j: (i, j)),
        ],
        out_specs=pl.BlockSpec((tm, tn), lambda i, j: (i, j)),
        out_shape=jax.ShapeDtypeStruct((m, d), BF16),
        scratch_shapes=[pltpu.VMEM((tm, width), BF16), pltpu.VMEM((heads, CHUNK, CHUNK), BF16)],
        compiler_params=_cparams(("arbitrary", "arbitrary")),
        name="gmlp_prompt",
    )(guv, guv, gnv, w_s, bias_full, w_gout, gates, p)


def _gmlp_sample_kernel(gu_ref, gv_ref, gnv_ref, wdiag_ref, bias_ref, wo_ref, gate_ref, p_ref,
                        out_ref, v_ref, wemit_ref, s_scr):
    j = pl.program_id(0)

    @pl.when(j == 0)
    def _():
        gv = gv_ref[...]
        r = lax.rsqrt(jnp.mean(gv * gv, axis=-1, keepdims=True) + EPS)
        v = gv * r * gnv_ref[...]
        v_ref[...] = v
        mixed = wdiag_ref[...] * v + bias_ref[...]
        s_scr[...] = (gu_ref[...] * mixed).astype(BF16)

    yb = jnp.dot(s_scr[...], _as_bf16(wo_ref, wemit_ref), preferred_element_type=F32)
    out_ref[...] = (p_ref[...] + gate_ref[...] * yb).astype(out_ref.dtype)


def _gmlp_sample(guv, gnv, wdiag, bias0, w_gout, gates, p, *, gate_col, tn):
    m = guv.shape[0]
    width, d = w_gout.shape
    return pl.pallas_call(
        _gmlp_sample_kernel,
        grid=(d // tn,),
        in_specs=[
            pl.BlockSpec((m, width), lambda j: (0, 0)),
            pl.BlockSpec((m, width), lambda j: (0, 1)),
            pl.BlockSpec((1, width), lambda j: (0, 0)),
            pl.BlockSpec((1, width), lambda j: (0, 0)),
            pl.BlockSpec((1, width), lambda j: (0, 0)),
            pl.BlockSpec((width, tn), lambda j: (0, j)),
            pl.BlockSpec((m, tn), lambda j: (0, j + gate_col // tn)),
            pl.BlockSpec((m, tn), lambda j: (0, j)),
        ],
        out_specs=(pl.BlockSpec((m, tn), lambda j: (0, j)),
                   pl.BlockSpec((m, width), lambda j: (0, 0)),
                   pl.BlockSpec((1, width, tn), lambda j: (j, 0, 0))),
        out_shape=(jax.ShapeDtypeStruct((m, d), BF16), jax.ShapeDtypeStruct((m, width), F32),
                   jax.ShapeDtypeStruct((d // tn, width, tn), BF16)),
        scratch_shapes=[pltpu.VMEM((m, width), BF16)],
        compiler_params=_cparams(("arbitrary",)),
        name="gmlp_sample_emit",
    )(guv, guv, gnv, wdiag, bias0, w_gout, gates, p)


def _layer(x, prm, w, *, nbatch, x0, gfinal, tm_ffn, tm_mm):
    sample = x0 is not None
    d_model = x.shape[1]
    ssm_width = prm["bw"].shape[0] * UB
    gmlp_width = prm["gnv"].shape[1]
    o1 = ssm_width
    o3 = o1 + 2 * gmlp_width
    tn = 512
    wb = {}

    res = _ffn(x, prm["n1"], w["f1g"], w["f1u"], w["f1d"], prm["nmix"],
               tail="norm_out", tm=tm_ffn, tf=256, emit=sample)
    x, h2 = res[0], res[1]
    if sample:
        wb["f1g"], wb["f1u"], wb["f1d"] = res[2:]

    cols = {}
    if sample:
        w = dict(w, win_ssm=w["w_in"], win_uv=w["w_in"], win_gate=w["w_in"],
                 glu_a=w["w_glu"], glu_b=w["w_glu"])
        cols = dict(win_uv=o1, win_gate=o3, glu_b=d_model)

    def mm(lhs, wkeys, extras, kind, n_out, out_dtype, name):
        out, emitted = _mm(lhs, [(w[k], cols.get(k, 0)) for k in wkeys], extras, kind=kind,
                           n_out=n_out, out_dtype=out_dtype, tm=tm_mm, tn=2 * tn, emit=sample, name=name)
        wb.update(zip(wkeys, emitted))
        return out

    z_ssm = mm(h2, ["win_ssm"], [], "identity", o1, F32, "w_in_ssm")
    guv = mm(h2, ["win_uv"], [], "gelu", 2 * gmlp_width, F32, "w_in_uv")
    gates = mm(h2, ["win_gate"], [], "sigmoid", 2 * d_model, F32, "w_in_gate")

    if sample:
        y, s_re, s_im = _ssm_sample(z_ssm, prm["bw"], prm["cw"], prm["a_re"], prm["a_im"],
                                    prm["d_skip"], x0[0], x0[1])
    else:
        seq = x.shape[0] // nbatch
        y, s_re, s_im = _ssm_prompt(z_ssm.reshape(nbatch, seq, ssm_width), prm["bw"], prm["cw"],
                                    prm["a_re"], prm["a_im"], prm["d_skip"])
        y = y.reshape(nbatch * seq, ssm_width)
    p = mm(y, ["glu_a", "glu_b"], [(gates, 0)], "gated_glu", d_model, F32, "ssm_glu")

    if sample:
        merged, v_rows, wb["w_gout"] = _gmlp_sample(guv, prm["gnv"], prm["w_diag"], prm["bias0"],
                                                    w["w_gout"], gates, p, gate_col=d_model, tn=tn)
    else:
        merged = _gmlp_prompt(guv, prm["gnv"], prm["w_s"], prm["bias_full"], w["w_gout"], gates, p,
                              gate_col=d_model, tm=tm_mm, tn=tn)
        v_rows = None
    x = mm(merged, ["w_out"], [(x, 0)], "residual", d_model, F32, "out_proj")
    res = _ffn(x, prm["n2"], w["f2g"], w["f2u"], w["f2d"], gfinal,
               tail="final", tm=tm_ffn, tf=256, emit=sample)
    if sample:
        wb["f2g"], wb["f2u"], wb["f2d"] = res[1:]
    return res[0], s_re, s_im, v_rows, wb


def kernel(x_prompt, x_sample, state_ssm_re, state_ssm_im, norm_ffn1, ffn1_gate, ffn1_up, ffn1_down, norm_mix, w_in, ssm_lambda_re, ssm_lambda_im, ssm_log_dt, ssm_b_re, ssm_b_im, ssm_c_re, ssm_c_im, ssm_d, ssm_w_glu, gmlp_norm_v, gmlp_w_s, gmlp_b_s, gmlp_w_out, w_out, norm_ffn2, ffn2_gate, ffn2_up, ffn2_down, norm_final):
    depth = w_in.shape[0]
    assert depth == 1, "the final RMSNorm is fused into the last FFN; one layer per step"
    batch, seq, d_model = x_prompt.shape
    dec_batch, dec_seq, _ = x_sample.shape
    assert dec_seq == 1
    groups, states = ssm_lambda_re.shape[1:]
    gmlp_width = gmlp_norm_v.shape[1]
    heads = gmlp_w_s.shape[1]
    head_dim = gmlp_width // heads
    assert seq % SCAN_T == 0 and seq % CHUNK == 0 and head_dim == GMLP_HEAD
    assert ssm_b_re.shape[-1] == SSM_GROUP and states == SSM_STATE and groups % GROUPS_PER_BLOCK == 0

    l = 0
    yp = x_prompt.reshape(batch * seq, d_model)
    ys = x_sample.reshape(dec_batch, d_model)
    gfin = norm_final.reshape(1, d_model)
    a_re, a_im, bb_re, bb_im = _discretise(ssm_lambda_re[l], ssm_lambda_im[l], ssm_log_dt[l],
                                           ssm_b_re[l], ssm_b_im[l])
    bw, cw = _block_diag_weights(bb_re, bb_im, ssm_c_re[l], ssm_c_im[l])
    prm = dict(
        n1=norm_ffn1[l].reshape(1, d_model), nmix=norm_mix[l].reshape(1, d_model),
        n2=norm_ffn2[l].reshape(1, d_model),
        bw=bw, cw=cw, a_re=a_re, a_im=a_im, d_skip=ssm_d[l],
        gnv=gmlp_norm_v[l].reshape(1, gmlp_width), w_s=gmlp_w_s[l],
        bias_full=jnp.repeat(gmlp_b_s[l].T, head_dim, axis=1),
        w_diag=jnp.repeat(gmlp_w_s[l][:, 0, 0], head_dim).reshape(1, gmlp_width),
        bias0=jnp.repeat(gmlp_b_s[l][:, 0], head_dim).reshape(1, gmlp_width),
    )
    w_f32 = dict(f1g=ffn1_gate[l], f1u=ffn1_up[l], f1d=ffn1_down[l], w_in=w_in[l],
                 w_glu=ssm_w_glu[l], w_gout=gmlp_w_out[l], w_out=w_out[l],
                 f2g=ffn2_gate[l], f2u=ffn2_up[l], f2d=ffn2_down[l])
    x0 = (state_ssm_re[l].reshape(dec_batch, groups * states),
          state_ssm_im[l].reshape(dec_batch, groups * states))
    ys, sr, si, vr, w_bf16 = _layer(ys, prm, w_f32, nbatch=dec_batch, x0=x0, gfinal=gfin,
                                    tm_ffn=dec_batch, tm_mm=dec_batch)
    yp, pr, pi, _, _ = _layer(yp, prm, w_bf16, nbatch=batch, x0=None, gfinal=gfin,
                              tm_ffn=1024, tm_mm=1024)
    return (yp.reshape(batch, seq, d_model), ys.reshape(dec_batch, dec_seq, d_model),
            pr.reshape(1, batch, groups, states), pi.reshape(1, batch, groups, states),
            sr.reshape(1, dec_batch, groups, states), si.reshape(1, dec_batch, groups, states),
            vr.reshape(1, dec_batch, dec_seq, gmlp_width))
```

```python
import functools
import math

import jax
import jax.numpy as jnp
from jax import lax
from jax.experimental import pallas as pl
from jax.experimental.pallas import tpu as pltpu

F32 = jnp.float32
BF16 = jnp.bfloat16

EPS = 1e-6
LANES = 128
SSM_GROUP = 16
SSM_STATE = 64
GROUPS_PER_BLOCK = 16
UB = GROUPS_PER_BLOCK * SSM_GROUP
SB = GROUPS_PER_BLOCK * SSM_STATE
SLABS = SB // LANES
SCAN_T = 256
SCAN_PITCH = SCAN_T + 8
CHUNK = 128
GMLP_HEAD = 128
VMEM_LIMIT = 56 * 1024 * 1024


def _cparams(sem):
    return pltpu.CompilerParams(dimension_semantics=sem, vmem_limit_bytes=VMEM_LIMIT)


def _gelu(x):
    c = math.sqrt(2.0 / math.pi)
    return 0.5 * x * (1.0 + jnp.tanh(c * (x + 0.044715 * (x * x * x))))


def _sigmoid(x):
    return 1.0 / (1.0 + jnp.exp(-x))


def _as_bf16(w_ref, emit_ref):
    w = w_ref[0] if len(w_ref.shape) == 3 else w_ref[...]
    if w.dtype != BF16:
        w = w.astype(BF16)
    if emit_ref is not None:
        if len(emit_ref.shape) == 3:
            emit_ref[0] = w
        else:
            emit_ref[...] = w
    return w


def _col_tile_spec(w, k, tn, col0=0):
    if w.ndim == 3:
        assert w.shape[1:] == (k, tn) and col0 == 0
        return pl.BlockSpec((1, k, tn), lambda i, j: (j, 0, 0))
    return pl.BlockSpec((k, tn), lambda i, j, o=col0 // tn: (0, j + o))


def _ffn_kernel(*refs, tm, d_model, col_chunk, row_chunk, tail, emit, side):
    refs = list(refs)
    x_hbm, gn_ref, wg_ref, wu_ref, wd_ref, gt_ref = refs[:6]
    del refs[:6]
    side_in = [refs.pop(0) for _ in range(3)] if side else []
    out_ref = refs.pop(0)
    h2_ref = refs.pop(0) if tail == "norm_out" else None
    emits = [refs.pop(0) for _ in range(3)] if emit else [None] * 3
    side_out = [refs.pop(0) for _ in range(3)] if side else []
    h_buf = h2_ref if h2_ref is not None else refs.pop(0)
    (x_sem,) = refs
    i = pl.program_id(0)
    j = pl.program_id(1)

    @pl.when(j == 0)
    def _():
        x_copy = pltpu.make_async_copy(x_hbm.at[pl.ds(pl.multiple_of(i * tm, tm), tm), :], out_ref, x_sem)
        x_copy.start()
        x_copy.wait()
        g = gn_ref[...]

        def body(c, carry):
            r0 = pl.multiple_of(c * row_chunk, row_chunk)
            xf = out_ref[pl.ds(r0, row_chunk), :]
            r = lax.rsqrt(jnp.mean(xf * xf, axis=-1, keepdims=True) + EPS)
            h_buf[pl.ds(r0, row_chunk), :] = (xf * r * g).astype(BF16)
            out_ref[pl.ds(r0, row_chunk), :] = 2.0 * xf
            return carry

        lax.fori_loop(0, tm // row_chunk, body, 0)

    h = h_buf[...]
    g = jnp.dot(h, _as_bf16(wg_ref, emits[0]), preferred_element_type=F32)
    u = jnp.dot(h, _as_bf16(wu_ref, emits[1]), preferred_element_type=F32)
    a = (g * _sigmoid(g) * u).astype(BF16)
    wd = _as_bf16(wd_ref, emits[2])
    for n in range(d_model // col_chunk):
        sl = slice(n * col_chunk, (n + 1) * col_chunk)
        out_ref[:, sl] += jnp.dot(a, wd[:, sl], preferred_element_type=F32)

    if side:
        for src, dst in zip(side_in[:2], side_out[:2]):
            tf_dst = dst.shape[2]
            for t in range(dst.shape[0]):
                dst[t] = src[:, t * tf_dst:(t + 1) * tf_dst].astype(BF16)
        side_out[2][...] = side_in[2][...].astype(BF16)

    @pl.when(j == pl.num_programs(1) - 1)
    def _():
        gt = gt_ref[...]

        def body(c, carry):
            r0 = pl.multiple_of(c * row_chunk, row_chunk)
            y = 0.5 * out_ref[pl.ds(r0, row_chunk), :]
            r = lax.rsqrt(jnp.mean(y * y, axis=-1, keepdims=True) + EPS)
            if tail == "norm_out":
                out_ref[pl.ds(r0, row_chunk), :] = y
                h2_ref[pl.ds(r0, row_chunk), :] = (y * r * gt).astype(BF16)
            else:
                out_ref[pl.ds(r0, row_chunk), :] = y * r * gt
            return carry

        lax.fori_loop(0, tm // row_chunk, body, 0)


def _ffn(x, gn, wg, wu, wd, gtail, *, tail, tm, tf, emit=False, side=None, single_out=False):
    m, d = x.shape
    f = wd.shape[0]
    ni, nj = m // tm, f // tf
    assert not emit or ni == 1
    kern = functools.partial(_ffn_kernel, tm=tm, d_model=d, col_chunk=512, row_chunk=min(tm, 64),
                             tail=tail, emit=emit, side=side is not None)
    in_specs = [
        pl.BlockSpec(memory_space=pl.ANY),
        pl.BlockSpec((1, d), lambda i, j: (0, 0)),
        _col_tile_spec(wg, d, tf),
        _col_tile_spec(wu, d, tf),
        pl.BlockSpec((tf, d), lambda i, j: (j, 0)),
        pl.BlockSpec((1, d), lambda i, j: (0, 0)),
    ]
    args = [x, gn, wg, wu, wd, gtail]
    out_specs = [pl.BlockSpec((tm, d), lambda i, j: (i, 0),
                              pipeline_mode=pl.Buffered(1) if single_out else None)]
    out_shape = [jax.ShapeDtypeStruct((m, d), F32)]
    scratch = [pltpu.VMEM((tm, d), BF16)]
    if tail == "norm_out":
        out_specs.append(pl.BlockSpec((tm, d), lambda i, j: (i, 0)))
        out_shape.append(jax.ShapeDtypeStruct((m, d), BF16))
        scratch = []
    if emit:
        out_specs += [pl.BlockSpec((1, d, tf), lambda i, j: (j, 0, 0)),
                      pl.BlockSpec((1, d, tf), lambda i, j: (j, 0, 0)),
                      pl.BlockSpec((tf, d), lambda i, j: (j, 0))]
        out_shape += [jax.ShapeDtypeStruct((nj, d, tf), BF16),
                      jax.ShapeDtypeStruct((nj, d, tf), BF16),
                      jax.ShapeDtypeStruct((f, d), BF16)]
    if side is not None:
        sg, su, sd = side
        nsteps = ni * nj
        rows_gu = 16
        rows_d = f // nsteps
        nblk = d // rows_gu
        assert sg.shape == su.shape == (d, f) and sd.shape == (f, d)
        assert nblk <= nsteps and rows_d * nsteps == f and rows_d % 16 == 0

        def gu_specs(start):
            def blk(i, j):
                return jnp.clip(i * nj + j - start, 0, nblk - 1)
            return (pl.BlockSpec((rows_gu, f), lambda i, j: (blk(i, j), 0)),
                    pl.BlockSpec((nj, rows_gu, tf), lambda i, j: (0, blk(i, j), 0)))

        g_in, g_out = gu_specs(0)
        u_in, u_out = gu_specs(nsteps - nblk)
        in_specs += [g_in, u_in, pl.BlockSpec((rows_d, d), lambda i, j: (i * nj + j, 0))]
        args += [sg, su, sd]
        out_specs += [g_out, u_out, pl.BlockSpec((rows_d, d), lambda i, j: (i * nj + j, 0))]
        out_shape += [jax.ShapeDtypeStruct((nj, d, tf), BF16),
                      jax.ShapeDtypeStruct((nj, d, tf), BF16),
                      jax.ShapeDtypeStruct((f, d), BF16)]
    return pl.pallas_call(
        kern,
        grid=(ni, nj),
        in_specs=in_specs,
        out_specs=out_specs,
        out_shape=out_shape,
        scratch_shapes=scratch + [pltpu.SemaphoreType.DMA],
        compiler_params=_cparams(("arbitrary", "arbitrary")),
        name="ffn_" + tail + ("_emit" if emit else "") + ("_side" if side is not None else ""),
    )(*args)


def _mm_epilogue(kind, accs, extras):
    if kind == "identity":
        return accs[0]
    if kind == "gelu":
        return _gelu(accs[0])
    if kind == "sigmoid":
        return _sigmoid(accs[0])
    if kind == "gated_glu":
        return extras[0] * (accs[0] * _sigmoid(accs[1]))
    if kind == "residual":
        return extras[0] + accs[0]
    raise ValueError(kind)


def _mm_kernel(*refs, n_w, n_extra, kind, emit):
    lhs_ref = refs[0]
    w_refs = refs[1:1 + n_w]
    extra_refs = refs[1 + n_w:1 + n_w + n_extra]
    out_ref = refs[1 + n_w + n_extra]
    emit_refs = refs[2 + n_w + n_extra:] if emit else [None] * n_w
    lhs = lhs_ref[...]
    accs = [jnp.dot(lhs, _as_bf16(w, e), preferred_element_type=F32) for w, e in zip(w_refs, emit_refs)]
    out_ref[...] = _mm_epilogue(kind, accs, [e[...] for e in extra_refs]).astype(out_ref.dtype)


def _mm(lhs, weights, extras, *, kind, n_out, out_dtype, tm, tn, emit, name):
    m, k = lhs.shape
    assert not emit or m == tm
    in_specs = [pl.BlockSpec((tm, k), lambda i, j: (i, 0))]
    args = [lhs]
    for w, c0 in weights:
        in_specs.append(_col_tile_spec(w, k, tn, c0))
        args.append(w)
    for e, c0 in extras:
        in_specs.append(pl.BlockSpec((tm, tn), lambda i, j, o=c0 // tn: (i, j + o)))
        args.append(e)
    out_specs = [pl.BlockSpec((tm, tn), lambda i, j: (i, j))]
    out_shape = [jax.ShapeDtypeStruct((m, n_out), out_dtype)]
    if emit:
        out_specs += [pl.BlockSpec((1, k, tn), lambda i, j: (j, 0, 0)) for _ in weights]
        out_shape += [jax.ShapeDtypeStruct((n_out // tn, k, tn), BF16) for _ in weights]
    kern = functools.partial(_mm_kernel, n_w=len(weights), n_extra=len(extras), kind=kind, emit=emit)
    res = pl.pallas_call(
        kern,
        grid=(m // tm, n_out // tn),
        in_specs=in_specs,
        out_specs=out_specs,
        out_shape=out_shape,
        compiler_params=_cparams(("arbitrary", "arbitrary")),
        name=name + ("_emit" if emit else ""),
    )(*args)
    return res[0], list(res[1:])


def _disc_kernel(lre_ref, lim_ref, ldt_ref, bre_ref, bim_ref, are_ref, aim_ref, bbre_ref, bbim_ref):
    lam_re = lre_ref[...]
    lam_im = lim_ref[...]
    dt = jnp.exp(ldt_ref[...])
    mag = jnp.exp(dt * lam_re)
    a_re = mag * jnp.cos(dt * lam_im)
    a_im = mag * jnp.sin(dt * lam_im)
    nr, ni = a_re - 1.0, a_im
    den = lam_re * lam_re + lam_im * lam_im
    coef_re = (nr * lam_re + ni * lam_im) / den
    coef_im = (ni * lam_re - nr * lam_im) / den
    are_ref[...] = a_re
    aim_ref[...] = a_im
    b_re = bre_ref[...]
    b_im = bim_ref[...]
    bbre_ref[...] = coef_re[None] * b_re - coef_im[None] * b_im
    bbim_ref[...] = coef_re[None] * b_im + coef_im[None] * b_re


def _discretise(lam_re, lam_im, log_dt, b_re, b_im):
    g, p = lam_re.shape
    h = b_re.shape[-1]
    b_re_t = jnp.transpose(b_re, (2, 0, 1))
    b_im_t = jnp.transpose(b_im, (2, 0, 1))
    return pl.pallas_call(
        _disc_kernel,
        out_shape=(jax.ShapeDtypeStruct((g, p), F32), jax.ShapeDtypeStruct((g, p), F32),
                   jax.ShapeDtypeStruct((h, g, p), F32), jax.ShapeDtypeStruct((h, g, p), F32)),
        name="s5_discretise",
    )(lam_re, lam_im, log_dt.reshape(g, 1), b_re_t, b_im_t)


def _block_diag_weights(bb_re, bb_im, c_re, c_im):
    h, g, p = bb_re.shape
    nb = g // GROUPS_PER_BLOCK
    eye = jnp.eye(GROUPS_PER_BLOCK, dtype=F32)

    def in_proj(bb):
        x = jnp.transpose(bb, (1, 0, 2)).reshape(nb, GROUPS_PER_BLOCK, h, p)
        x = x[:, :, :, None, :] * eye[None, :, None, :, None]
        return x.reshape(nb, UB, SB)

    def out_proj(c):
        x = jnp.transpose(c.reshape(nb, GROUPS_PER_BLOCK, h, p), (0, 1, 3, 2))
        x = x[:, :, :, None, :] * eye[None, :, None, :, None]
        return x.reshape(nb, SB, UB)

    bw = jnp.concatenate([in_proj(bb_re), in_proj(bb_im)], axis=2).astype(BF16)
    cw = jnp.concatenate([out_proj(c_re), -out_proj(c_im)], axis=1).astype(BF16)
    return bw, cw


def _ssm_prompt_kernel(u_ref, bw_ref, cw_ref, are_ref, aim_ref, d_ref,
                       y_ref, sre_ref, sim_ref,
                       bure, buim, st_re, st_im, *, nbatch):
    tc = pl.program_id(1)

    @pl.when(tc == 0)
    def _():
        st_re[...] = jnp.zeros_like(st_re)
        st_im[...] = jnp.zeros_like(st_im)

    pack = 8 // nbatch
    nq = SLABS // pack

    def slab_rows(c, b):
        ref, c = (bure, c) if c < SLABS else (buim, c - SLABS)
        r0 = ((c // nq) * nbatch + b) * SCAN_PITCH
        return ref, c % nq, slice(r0, r0 + SCAN_T)

    u_all = u_ref[...].reshape(nbatch * SCAN_T, UB)
    u_bf = u_all.astype(BF16)
    per_dot = 4
    for n in range(2 * SLABS // per_dot):
        bu = jnp.dot(u_bf, bw_ref[0, :, per_dot * n * LANES:per_dot * (n + 1) * LANES],
                     preferred_element_type=F32)
        for h in range(per_dot):
            for b in range(nbatch):
                ref, q, rows = slab_rows(per_dot * n + h, b)
                ref[q, rows, :] = bu[b * SCAN_T:(b + 1) * SCAN_T, h * LANES:(h + 1) * LANES]

    def packed(a_ref, q):
        return jnp.concatenate(
            [jnp.broadcast_to(a_ref[0, q + h * nq], (nbatch, LANES)) for h in range(pack)], axis=0)

    a_re = [packed(are_ref, q) for q in range(nq)]
    a_im = [packed(aim_ref, q) for q in range(nq)]
    init = tuple(st_re[q] for q in range(nq)) + tuple(st_im[q] for q in range(nq))

    def step(t, carry):
        rows = pl.ds(t, pack * nbatch, stride=SCAN_PITCH)
        new_re, new_im = [], []
        for q in range(nq):
            s_r, s_i = carry[q], carry[nq + q]
            n_r = a_re[q] * s_r - a_im[q] * s_i + bure[q, rows, :]
            n_i = a_re[q] * s_i + a_im[q] * s_r + buim[q, rows, :]
            bure[q, rows, :] = n_r
            buim[q, rows, :] = n_i
            new_re.append(n_r)
            new_im.append(n_i)
        return tuple(new_re) + tuple(new_im)

    fin = lax.fori_loop(0, SCAN_T, step, init, unroll=4)
    for q in range(nq):
        st_re[q] = fin[q]
        st_im[q] = fin[nq + q]

    d = d_ref[0]
    for b in range(nbatch):
        y = d * u_ref[b]
        for n in range(SLABS):
            (ref0, q0, rows0), (ref1, q1, rows1) = slab_rows(2 * n, b), slab_rows(2 * n + 1, b)
            s_pair = jnp.concatenate([ref0[q0, rows0, :], ref1[q1, rows1, :]], axis=1).astype(BF16)
            y = y + jnp.dot(s_pair, cw_ref[0, 2 * n * LANES:(2 * n + 2) * LANES, :],
                            preferred_element_type=F32)
        y_ref[b] = _gelu(y).astype(y_ref.dtype)

    @pl.when(tc == pl.num_programs(1) - 1)
    def _():
        for c in range(SLABS):
            q, h = c % nq, c // nq
            sre_ref[:, c * LANES:(c + 1) * LANES] = st_re[q, h * nbatch:(h + 1) * nbatch, :]
            sim_ref[:, c * LANES:(c + 1) * LANES] = st_im[q, h * nbatch:(h + 1) * nbatch, :]


def _ssm_prompt(z3, bw, cw, a_re, a_im, d_skip):
    nbatch, seq, width = z3.shape
    nb = bw.shape[0]
    states = nb * SB
    kern = functools.partial(_ssm_prompt_kernel, nbatch=nbatch)
    return pl.pallas_call(
        kern,
        grid=(nb, seq // SCAN_T),
        in_specs=[
            pl.BlockSpec((nbatch, SCAN_T, UB), lambda g, t: (0, t, g)),
            pl.BlockSpec((1, UB, 2 * SB), lambda g, t: (g, 0, 0)),
            pl.BlockSpec((1, 2 * SB, UB), lambda g, t: (g, 0, 0)),
            pl.BlockSpec((1, SLABS, 1, LANES), lambda g, t: (g, 0, 0, 0)),
            pl.BlockSpec((1, SLABS, 1, LANES), lambda g, t: (g, 0, 0, 0)),
            pl.BlockSpec((1, 1, UB), lambda g, t: (g, 0, 0)),
        ],
        out_specs=(
            pl.BlockSpec((nbatch, SCAN_T, UB), lambda g, t: (0, t, g)),
            pl.BlockSpec((nbatch, SB), lambda g, t: (0, g)),
            pl.BlockSpec((nbatch, SB), lambda g, t: (0, g)),
        ),
        out_shape=(
            jax.ShapeDtypeStruct((nbatch, seq, width), BF16),
            jax.ShapeDtypeStruct((nbatch, states), F32),
            jax.ShapeDtypeStruct((nbatch, states), F32),
        ),
        scratch_shapes=[
            pltpu.VMEM((SLABS * nbatch // 8, 8 * SCAN_PITCH, LANES), F32),
            pltpu.VMEM((SLABS * nbatch // 8, 8 * SCAN_PITCH, LANES), F32),
            pltpu.VMEM((SLABS * nbatch // 8, 8, LANES), F32),
            pltpu.VMEM((SLABS * nbatch // 8, 8, LANES), F32),
        ],
        compiler_params=_cparams(("arbitrary", "arbitrary")),
        name="ssm_prompt",
    )(z3, bw, cw, a_re.reshape(nb, SLABS, 1, LANES), a_im.reshape(nb, SLABS, 1, LANES),
      d_skip.reshape(nb, 1, UB))


def _ssm_sample_kernel(u_ref, bw_ref, cw_ref, are_ref, aim_ref, d_ref, x0re_ref, x0im_ref,
                       y_ref, sre_ref, sim_ref):
    u = u_ref[...]
    bu = jnp.dot(u.astype(BF16), bw_ref[0], preferred_element_type=F32)
    a_re = are_ref[0]
    a_im = aim_ref[0]
    x_re = x0re_ref[...]
    x_im = x0im_ref[...]
    s_re = a_re * x_re - a_im * x_im + bu[:, :SB]
    s_im = a_re * x_im + a_im * x_re + bu[:, SB:]
    sre_ref[...] = s_re
    sim_ref[...] = s_im
    s_cat = jnp.concatenate([s_re, s_im], axis=1).astype(BF16)
    y = jnp.dot(s_cat, cw_ref[0], preferred_element_type=F32) + d_ref[0] * u
    y_ref[...] = _gelu(y).astype(y_ref.dtype)


def _ssm_sample(z, bw, cw, a_re, a_im, d_skip, x0_re, x0_im):
    nbatch = z.shape[0]
    nb = bw.shape[0]
    return pl.pallas_call(
        _ssm_sample_kernel,
        grid=(nb,),
        in_specs=[
            pl.BlockSpec((nbatch, UB), lambda g: (0, g)),
            pl.BlockSpec((1, UB, 2 * SB), lambda g: (g, 0, 0)),
            pl.BlockSpec((1, 2 * SB, UB), lambda g: (g, 0, 0)),
            pl.BlockSpec((1, 1, SB), lambda g: (g, 0, 0)),
            pl.BlockSpec((1, 1, SB), lambda g: (g, 0, 0)),
            pl.BlockSpec((1, 1, UB), lambda g: (g, 0, 0)),
            pl.BlockSpec((nbatch, SB), lambda g: (0, g)),
            pl.BlockSpec((nbatch, SB), lambda g: (0, g)),
        ],
        out_specs=(
            pl.BlockSpec((nbatch, UB), lambda g: (0, g)),
            pl.BlockSpec((nbatch, SB), lambda g: (0, g)),
            pl.BlockSpec((nbatch, SB), lambda g: (0, g)),
        ),
        out_shape=(
            jax.ShapeDtypeStruct((nbatch, nb * UB), BF16),
            jax.ShapeDtypeStruct((nbatch, nb * SB), F32),
            jax.ShapeDtypeStruct((nbatch, nb * SB), F32),
        ),
        compiler_params=_cparams(("arbitrary",)),
        name="ssm_sample",
    )(z, bw, cw, a_re.reshape(nb, 1, SB), a_im.reshape(nb, 1, SB), d_skip.reshape(nb, 1, UB),
      x0_re, x0_im)


def _gmlp_prompt_kernel(gu_ref, gv_ref, gnv_ref, ws_ref, bias_ref, wo_ref, gate_ref, p_ref,
                        out_ref, s_scr, wt_scr, *, tm, heads):
    i = pl.program_id(0)
    j = pl.program_id(1)

    @pl.when(jnp.logical_and(i == 0, j == 0))
    def _():
        row = lax.broadcasted_iota(jnp.int32, (CHUNK, CHUNK), 0)
        col = lax.broadcasted_iota(jnp.int32, (CHUNK, CHUNK), 1)
        mask = (col <= row).astype(F32)
        for g in range(heads):
            wt_scr[g] = (ws_ref[g] * mask).astype(BF16)

    @pl.when(j == 0)
    def _():
        gnv = gnv_ref[...]

        def body(c, carry):
            r0 = pl.multiple_of(c * CHUNK, CHUNK)
            gv = gv_ref[pl.ds(r0, CHUNK), :]
            r = lax.rsqrt(jnp.mean(gv * gv, axis=-1, keepdims=True) + EPS)
            v = (gv * r * gnv).astype(BF16)
            for g in range(heads):
                sl = slice(g * GMLP_HEAD, (g + 1) * GMLP_HEAD)
                mixed = jnp.dot(wt_scr[g], v[:, sl], preferred_element_type=F32) + bias_ref[:, sl]
                s_scr[pl.ds(r0, CHUNK), sl] = (gu_ref[pl.ds(r0, CHUNK), sl] * mixed).astype(BF16)
            return carry

        lax.fori_loop(0, tm // CHUNK, body, 0)

    yb = jnp.dot(s_scr[...], wo_ref[0], preferred_element_type=F32)
    out_ref[...] = (p_ref[...] + gate_ref[...] * yb).astype(out_ref.dtype)


def _gmlp_prompt(guv, gnv, w_s, bias_full, w_gout, gates, p, *, gate_col, tm, tn):
    m = guv.shape[0]
    ntiles, width, _ = w_gout.shape
    d = ntiles * tn
    heads = w_s.shape[0]
    kern = functools.partial(_gmlp_prompt_kernel, tm=tm, heads=heads)
    return pl.pallas_call(
        kern,
        grid=(m // tm, d // tn),
        in_specs=[
            pl.BlockSpec((tm, width), lambda i, j: (i, 0)),
            pl.BlockSpec((tm, width), lambda i, j: (i, 1)),
            pl.BlockSpec((1, width), lambda i, j: (0, 0)),
            pl.BlockSpec((heads, CHUNK, CHUNK), lambda i, j: (0, 0, 0)),
            pl.BlockSpec((CHUNK, width), lambda i, j: (0, 0)),
            _col_tile_spec(w_gout, width, tn),
            pl.BlockSpec((tm, tn), lambda i, j: (i, j + gate_col // tn)),
            pl.BlockSpec((tm, tn), lambda i, j: (i, j)),
        ],
        out_specs=pl.BlockSpec((tm, tn), lambda i, j: (i, j)),
        out_shape=jax.ShapeDtypeStruct((m, d), BF16),
        scratch_shapes=[pltpu.VMEM((tm, width), BF16), pltpu.VMEM((heads, CHUNK, CHUNK), BF16)],
        compiler_params=_cparams(("arbitrary", "arbitrary")),
        name="gmlp_prompt",
    )(guv, guv, gnv, w_s, bias_full, w_gout, gates, p)


def _gmlp_sample_kernel(gu_ref, gv_ref, gnv_ref, wdiag_ref, bias_ref, wo_ref, gate_ref, p_ref,
                        out_ref, v_ref, wemit_ref, s_scr):
    j = pl.program_id(0)

    @pl.when(j == 0)
    def _():
        gv = gv_ref[...]
        r = lax.rsqrt(jnp.mean(gv * gv, axis=-1, keepdims=True) + EPS)
        v = gv * r * gnv_ref[...]
        v_ref[...] = v
        mixed = wdiag_ref[...] * v + bias_ref[...]
        s_scr[...] = (gu_ref[...] * mixed).astype(BF16)

    yb = jnp.dot(s_scr[...], _as_bf16(wo_ref, wemit_ref), preferred_element_type=F32)
    out_ref[...] = (p_ref[...] + gate_ref[...] * yb).astype(out_ref.dtype)


def _gmlp_sample(guv, gnv, wdiag, bias0, w_gout, gates, p, *, gate_col, tn):
    m = guv.shape[0]
    width, d = w_gout.shape
    return pl.pallas_call(
        _gmlp_sample_kernel,
        grid=(d // tn,),
        in_specs=[
            pl.BlockSpec((m, width), lambda j: (0, 0)),
            pl.BlockSpec((m, width), lambda j: (0, 1)),
            pl.BlockSpec((1, width), lambda j: (0, 0)),
            pl.BlockSpec((1, width), lambda j: (0, 0)),
            pl.BlockSpec((1, width), lambda j: (0, 0)),
            pl.BlockSpec((width, tn), lambda j: (0, j)),
            pl.BlockSpec((m, tn), lambda j: (0, j + gate_col // tn)),
            pl.BlockSpec((m, tn), lambda j: (0, j)),
        ],
        out_specs=(pl.BlockSpec((m, tn), lambda j: (0, j)),
                   pl.BlockSpec((m, width), lambda j: (0, 0)),
                   pl.BlockSpec((1, width, tn), lambda j: (j, 0, 0))),
        out_shape=(jax.ShapeDtypeStruct((m, d), BF16), jax.ShapeDtypeStruct((m, width), F32),
                   jax.ShapeDtypeStruct((d // tn, width, tn), BF16)),
        scratch_shapes=[pltpu.VMEM((m, width), BF16)],
        compiler_params=_cparams(("arbitrary",)),
        name="gmlp_sample_emit",
    )(guv, guv, gnv, wdiag, bias0, w_gout, gates, p)


def _mixer(x, h2, prm, w, *, nbatch, x0, tm_mm):
    sample = x0 is not None
    d_model = x.shape[1]
    ssm_width = prm["bw"].shape[0] * UB
    gmlp_width = prm["gnv"].shape[1]
    o1 = ssm_width
    o3 = o1 + 2 * gmlp_width
    tn = 512
    wb = {}

    cols = {}
    if sample:
        w = dict(w, win_ssm=w["w_in"], win_uv=w["w_in"], win_gate=w["w_in"],
                 glu_a=w["w_glu"], glu_b=w["w_glu"])
        cols = dict(win_uv=o1, win_gate=o3, glu_b=d_model)

    def mm(lhs, wkeys, extras, kind, n_out, out_dtype, name):
        out, emitted = _mm(lhs, [(w[k], cols.get(k, 0)) for k in wkeys], extras, kind=kind,
                           n_out=n_out, out_dtype=out_dtype, tm=tm_mm, tn=2 * tn, emit=sample, name=name)
        wb.update(zip(wkeys, emitted))
        return out

    z_ssm = mm(h2, ["win_ssm"], [], "identity", o1, F32, "w_in_ssm")
    guv = mm(h2, ["win_uv"], [], "gelu", 2 * gmlp_width, F32, "w_in_uv")
    gates = mm(h2, ["win_gate"], [], "sigmoid", 2 * d_model, F32, "w_in_gate")

    if sample:
        y, s_re, s_im = _ssm_sample(z_ssm, prm["bw"], prm["cw"], prm["a_re"], prm["a_im"],
                                    prm["d_skip"], x0[0], x0[1])
    else:
        seq = x.shape[0] // nbatch
        y, s_re, s_im = _ssm_prompt(z_ssm.reshape(nbatch, seq, ssm_width), prm["bw"], prm["cw"],
                                    prm["a_re"], prm["a_im"], prm["d_skip"])
        y = y.reshape(nbatch * seq, ssm_width)
    p = mm(y, ["glu_a", "glu_b"], [(gates, 0)], "gated_glu", d_model, F32, "ssm_glu")

    if sample:
        merged, v_rows, wb["w_gout"] = _gmlp_sample(guv, prm["gnv"], prm["w_diag"], prm["bias0"],
                                                    w["w_gout"], gates, p, gate_col=d_model, tn=tn)
    else:
        merged = _gmlp_prompt(guv, prm["gnv"], prm["w_s"], prm["bias_full"], w["w_gout"], gates, p,
                              gate_col=d_model, tm=tm_mm, tn=tn)
        v_rows = None
    x = mm(merged, ["w_out"], [(x, 0)], "residual", d_model, F32, "out_proj")
    return x, s_re, s_im, v_rows, wb


def kernel(x_prompt, x_sample, state_ssm_re, state_ssm_im, norm_ffn1, ffn1_gate, ffn1_up, ffn1_down, norm_mix, w_in, ssm_lambda_re, ssm_lambda_im, ssm_log_dt, ssm_b_re, ssm_b_im, ssm_c_re, ssm_c_im, ssm_d, ssm_w_glu, gmlp_norm_v, gmlp_w_s, gmlp_b_s, gmlp_w_out, w_out, norm_ffn2, ffn2_gate, ffn2_up, ffn2_down, norm_final):
    depth = w_in.shape[0]
    assert depth == 1, "the final RMSNorm is fused into the last FFN; one layer per step"
    batch, seq, d_model = x_prompt.shape
    dec_batch, dec_seq, _ = x_sample.shape
    assert dec_seq == 1
    groups, states = ssm_lambda_re.shape[1:]
    gmlp_width = gmlp_norm_v.shape[1]
    heads = gmlp_w_s.shape[1]
    head_dim = gmlp_width // heads
    assert seq % SCAN_T == 0 and seq % CHUNK == 0 and head_dim == GMLP_HEAD
    assert ssm_b_re.shape[-1] == SSM_GROUP and states == SSM_STATE and groups % GROUPS_PER_BLOCK == 0

    l = 0
    yp = x_prompt.reshape(batch * seq, d_model)
    ys = x_sample.reshape(dec_batch, d_model)
    gfin = norm_final.reshape(1, d_model)
    a_re, a_im, bb_re, bb_im = _discretise(ssm_lambda_re[l], ssm_lambda_im[l], ssm_log_dt[l],
                                           ssm_b_re[l], ssm_b_im[l])
    bw, cw = _block_diag_weights(bb_re, bb_im, ssm_c_re[l], ssm_c_im[l])
    prm = dict(
        n1=norm_ffn1[l].reshape(1, d_model), nmix=norm_mix[l].reshape(1, d_model),
        n2=norm_ffn2[l].reshape(1, d_model),
        bw=bw, cw=cw, a_re=a_re, a_im=a_im, d_skip=ssm_d[l],
        gnv=gmlp_norm_v[l].reshape(1, gmlp_width), w_s=gmlp_w_s[l],
        bias_full=jnp.repeat(gmlp_b_s[l].T, head_dim, axis=1),
        w_diag=jnp.repeat(gmlp_w_s[l][:, 0, 0], head_dim).reshape(1, gmlp_width),
        bias0=jnp.repeat(gmlp_b_s[l][:, 0], head_dim).reshape(1, gmlp_width),
    )
    w_f32 = dict(w_in=w_in[l], w_glu=ssm_w_glu[l], w_gout=gmlp_w_out[l], w_out=w_out[l])
    x0 = (state_ssm_re[l].reshape(dec_batch, groups * states),
          state_ssm_im[l].reshape(dec_batch, groups * states))
    tm, tf = 1024, 256

    ys, hs, *f1 = _ffn(ys, prm["n1"], ffn1_gate[l], ffn1_up[l], ffn1_down[l], prm["nmix"],
                       tail="norm_out", tm=dec_batch, tf=tf, emit=True)
    yp, hp, *f2 = _ffn(yp, prm["n1"], *f1, prm["nmix"], tail="norm_out", tm=tm, tf=tf,
                       side=(ffn2_gate[l], ffn2_up[l], ffn2_down[l]), single_out=True)
    ys, sr, si, vr, w_bf16 = _mixer(ys, hs, prm, w_f32, nbatch=dec_batch, x0=x0, tm_mm=dec_batch)
    yp, pr, pi, _, _ = _mixer(yp, hp, prm, w_bf16, nbatch=batch, x0=None, tm_mm=tm)
    (ys,) = _ffn(ys, prm["n2"], *f2, gfin, tail="final", tm=dec_batch, tf=tf)
    (yp,) = _ffn(yp, prm["n2"], *f2, gfin, tail="final", tm=tm, tf=tf)
    return (yp.reshape(batch, seq, d_model), ys.reshape(dec_batch, dec_seq, d_model),
            pr.reshape(1, batch, groups, states), pi.reshape(1, batch, groups, states),
            sr.reshape(1, dec_batch, groups, states), si.reshape(1, dec_batch, groups, states),
            vr.reshape(1, dec_batch, dec_seq, gmlp_width))
```

```python
import functools
import math

import jax
import jax.numpy as jnp
from jax import lax
from jax.experimental import pallas as pl
from jax.experimental.pallas import tpu as pltpu

F32 = jnp.float32
BF16 = jnp.bfloat16

EPS = 1e-6
LANES = 128
SSM_GROUP = 16
SSM_STATE = 64
GROUPS_PER_BLOCK = 16
UB = GROUPS_PER_BLOCK * SSM_GROUP
SB = GROUPS_PER_BLOCK * SSM_STATE
SLABS = SB // LANES
SCAN_T = 256
SCAN_PITCH = SCAN_T + 8
CHUNK = 128
GMLP_HEAD = 128
VMEM_LIMIT = 56 * 1024 * 1024


def _cparams(sem):
    return pltpu.CompilerParams(dimension_semantics=sem, vmem_limit_bytes=VMEM_LIMIT)


def _gelu(x):
    c = math.sqrt(2.0 / math.pi)
    return 0.5 * x * (1.0 + jnp.tanh(c * (x + 0.044715 * (x * x * x))))


def _sigmoid(x):
    return 1.0 / (1.0 + jnp.exp(-x))


def _as_bf16(w_ref, emit_ref):
    w = w_ref[0] if len(w_ref.shape) == 3 else w_ref[...]
    if w.dtype != BF16:
        w = w.astype(BF16)
    if emit_ref is not None:
        if len(emit_ref.shape) == 3:
            emit_ref[0] = w
        else:
            emit_ref[...] = w
    return w


def _col_tile_spec(w, k, tn, col0=0):
    if w.ndim == 3:
        assert w.shape[1:] == (k, tn) and col0 == 0
        return pl.BlockSpec((1, k, tn), lambda i, j: (j, 0, 0))
    return pl.BlockSpec((k, tn), lambda i, j, o=col0 // tn: (0, j + o))


def _ffn_kernel(*refs, tm, d_model, col_chunk, row_chunk, tail, emit, side):
    refs = list(refs)
    x_hbm, gn_ref, wg_ref, wu_ref, wd_ref, gt_ref = refs[:6]
    del refs[:6]
    side_in = [refs.pop(0) for _ in range(3)] if side else []
    out_ref = refs.pop(0)
    h2_ref = refs.pop(0) if tail == "norm_out" else None
    emits = [refs.pop(0) for _ in range(3)] if emit else [None] * 3
    side_out = [refs.pop(0) for _ in range(3)] if side else []
    h_buf = h2_ref if h2_ref is not None else refs.pop(0)
    (x_sem,) = refs
    i = pl.program_id(0)
    j = pl.program_id(1)

    @pl.when(j == 0)
    def _():
        x_copy = pltpu.make_async_copy(x_hbm.at[pl.ds(pl.multiple_of(i * tm, tm), tm), :], out_ref, x_sem)
        x_copy.start()
        x_copy.wait()
        g = gn_ref[...]

        def body(c, carry):
            r0 = pl.multiple_of(c * row_chunk, row_chunk)
            xf = out_ref[pl.ds(r0, row_chunk), :]
            r = lax.rsqrt(jnp.mean(xf * xf, axis=-1, keepdims=True) + EPS)
            h_buf[pl.ds(r0, row_chunk), :] = (xf * r * g).astype(BF16)
            out_ref[pl.ds(r0, row_chunk), :] = 2.0 * xf
            return carry

        lax.fori_loop(0, tm // row_chunk, body, 0)

    h = h_buf[...]
    g = jnp.dot(h, _as_bf16(wg_ref, emits[0]), preferred_element_type=F32)
    u = jnp.dot(h, _as_bf16(wu_ref, emits[1]), preferred_element_type=F32)
    a = (g * _sigmoid(g) * u).astype(BF16)
    wd = _as_bf16(wd_ref, emits[2])
    for n in range(d_model // col_chunk):
        sl = slice(n * col_chunk, (n + 1) * col_chunk)
        out_ref[:, sl] += jnp.dot(a, wd[:, sl], preferred_element_type=F32)

    if side:
        for src, dst in zip(side_in[:2], side_out[:2]):
            tf_dst = dst.shape[2]
            for t in range(dst.shape[0]):
                dst[t] = src[:, t * tf_dst:(t + 1) * tf_dst].astype(BF16)
        side_out[2][...] = side_in[2][...].astype(BF16)

    @pl.when(j == pl.num_programs(1) - 1)
    def _():
        gt = gt_ref[...]

        def body(c, carry):
            r0 = pl.multiple_of(c * row_chunk, row_chunk)
            y = 0.5 * out_ref[pl.ds(r0, row_chunk), :]
            r = lax.rsqrt(jnp.mean(y * y, axis=-1, keepdims=True) + EPS)
            if tail == "norm_out":
                out_ref[pl.ds(r0, row_chunk), :] = y
                h2_ref[pl.ds(r0, row_chunk), :] = (y * r * gt).astype(BF16)
            else:
                out_ref[pl.ds(r0, row_chunk), :] = y * r * gt
            return carry

        lax.fori_loop(0, tm // row_chunk, body, 0)


def _ffn(x, gn, wg, wu, wd, gtail, *, tail, tm, tf, emit=False, side=None, single_out=False):
    m, d = x.shape
    f = wd.shape[0]
    ni, nj = m // tm, f // tf
    assert not emit or ni == 1
    kern = functools.partial(_ffn_kernel, tm=tm, d_model=d, col_chunk=512, row_chunk=min(tm, 64),
                             tail=tail, emit=emit, side=side is not None)
    in_specs = [
        pl.BlockSpec(memory_space=pl.ANY),
        pl.BlockSpec((1, d), lambda i, j: (0, 0)),
        _col_tile_spec(wg, d, tf),
        _col_tile_spec(wu, d, tf),
        pl.BlockSpec((tf, d), lambda i, j: (j, 0)),
        pl.BlockSpec((1, d), lambda i, j: (0, 0)),
    ]
    args = [x, gn, wg, wu, wd, gtail]
    out_specs = [pl.BlockSpec((tm, d), lambda i, j: (i, 0),
                              pipeline_mode=pl.Buffered(1) if single_out else None)]
    out_shape = [jax.ShapeDtypeStruct((m, d), F32)]
    scratch = [pltpu.VMEM((tm, d), BF16)]
    if tail == "norm_out":
        out_specs.append(pl.BlockSpec((tm, d), lambda i, j: (i, 0)))
        out_shape.append(jax.ShapeDtypeStruct((m, d), BF16))
        scratch = []
    if emit:
        out_specs += [pl.BlockSpec((1, d, tf), lambda i, j: (j, 0, 0)),
                      pl.BlockSpec((1, d, tf), lambda i, j: (j, 0, 0)),
                      pl.BlockSpec((tf, d), lambda i, j: (j, 0))]
        out_shape += [jax.ShapeDtypeStruct((nj, d, tf), BF16),
                      jax.ShapeDtypeStruct((nj, d, tf), BF16),
                      jax.ShapeDtypeStruct((f, d), BF16)]
    if side is not None:
        sg, su, sd = side
        nsteps = ni * nj
        rows_gu = 16
        rows_d = f // nsteps
        nblk = d // rows_gu
        assert sg.shape == su.shape == (d, f) and sd.shape == (f, d)
        assert nblk <= nsteps and rows_d * nsteps == f and rows_d % 16 == 0

        def gu_specs(start):
            def blk(i, j):
                return jnp.clip(i * nj + j - start, 0, nblk - 1)
            return (pl.BlockSpec((rows_gu, f), lambda i, j: (blk(i, j), 0)),
                    pl.BlockSpec((nj, rows_gu, tf), lambda i, j: (0, blk(i, j), 0)))

        g_in, g_out = gu_specs(0)
        u_in, u_out = gu_specs(nsteps - nblk)
        in_specs += [g_in, u_in, pl.BlockSpec((rows_d, d), lambda i, j: (i * nj + j, 0))]
        args += [sg, su, sd]
        out_specs += [g_out, u_out, pl.BlockSpec((rows_d, d), lambda i, j: (i * nj + j, 0))]
        out_shape += [jax.ShapeDtypeStruct((nj, d, tf), BF16),
                      jax.ShapeDtypeStruct((nj, d, tf), BF16),
                      jax.ShapeDtypeStruct((f, d), BF16)]
    return pl.pallas_call(
        kern,
        grid=(ni, nj),
        in_specs=in_specs,
        out_specs=out_specs,
        out_shape=out_shape,
        scratch_shapes=scratch + [pltpu.SemaphoreType.DMA],
        compiler_params=_cparams(("arbitrary", "arbitrary")),
        name="ffn_" + tail + ("_emit" if emit else "") + ("_side" if side is not None else ""),
    )(*args)


def _mm_epilogue(kind, accs, extras):
    if kind == "identity":
        return accs[0]
    if kind == "gelu":
        return _gelu(accs[0])
    if kind == "sigmoid":
        return _sigmoid(accs[0])
    if kind == "gated_glu":
        return extras[0] * (accs[0] * _sigmoid(accs[1]))
    if kind == "residual":
        return extras[0] + accs[0]
    raise ValueError(kind)


def _mm_kernel(*refs, n_w, n_extra, kind, emit):
    lhs_ref = refs[0]
    w_refs = refs[1:1 + n_w]
    extra_refs = refs[1 + n_w:1 + n_w + n_extra]
    out_ref = refs[1 + n_w + n_extra]
    emit_refs = refs[2 + n_w + n_extra:] if emit else [None] * n_w
    lhs = lhs_ref[...]
    accs = [jnp.dot(lhs, _as_bf16(w, e), preferred_element_type=F32) for w, e in zip(w_refs, emit_refs)]
    out_ref[...] = _mm_epilogue(kind, accs, [e[...] for e in extra_refs]).astype(out_ref.dtype)


def _mm(lhs, weights, extras, *, kind, n_out, out_dtype, tm, tn, emit, name):
    m, k = lhs.shape
    assert not emit or m == tm
    in_specs = [pl.BlockSpec((tm, k), lambda i, j: (i, 0))]
    args = [lhs]
    for w, c0 in weights:
        in_specs.append(_col_tile_spec(w, k, tn, c0))
        args.append(w)
    for e, c0 in extras:
        in_specs.append(pl.BlockSpec((tm, tn), lambda i, j, o=c0 // tn: (i, j + o)))
        args.append(e)
    out_specs = [pl.BlockSpec((tm, tn), lambda i, j: (i, j))]
    out_shape = [jax.ShapeDtypeStruct((m, n_out), out_dtype)]
    if emit:
        out_specs += [pl.BlockSpec((1, k, tn), lambda i, j: (j, 0, 0)) for _ in weights]
        out_shape += [jax.ShapeDtypeStruct((n_out // tn, k, tn), BF16) for _ in weights]
    kern = functools.partial(_mm_kernel, n_w=len(weights), n_extra=len(extras), kind=kind, emit=emit)
    res = pl.pallas_call(
        kern,
        grid=(m // tm, n_out // tn),
        in_specs=in_specs,
        out_specs=out_specs,
        out_shape=out_shape,
        compiler_params=_cparams(("arbitrary", "arbitrary")),
        name=name + ("_emit" if emit else ""),
    )(*args)
    return res[0], list(res[1:])


def _zoh(lam_re, lam_im, log_dt):
    dt = jnp.exp(log_dt)
    mag = jnp.exp(dt * lam_re)
    a_re = mag * jnp.cos(dt * lam_im)
    a_im = mag * jnp.sin(dt * lam_im)
    nr, ni = a_re - 1.0, a_im
    den = lam_re * lam_re + lam_im * lam_im
    coef_re = (nr * lam_re + ni * lam_im) / den
    coef_im = (ni * lam_re - nr * lam_im) / den
    return a_re, a_im, coef_re, coef_im


def _spread_lanes(x, reps):
    w = x.shape[1]
    assert w & (w - 1) == 0
    src = lax.broadcasted_iota(jnp.int32, (w, w * reps), 0)
    dst = lax.broadcasted_iota(jnp.int32, (w, w * reps), 1)
    sel = ((dst & (w - 1)) == src).astype(BF16)
    return jnp.dot(x.astype(BF16), sel, preferred_element_type=F32)


def _disc_kernel(lre_ref, lim_ref, ldt_ref, lre_col, lim_col, ldt_col, bre_ref, bim_ref, cre_ref, cim_ref,
                 are_ref, aim_ref, bw_ref, cw_ref):
    a_re, a_im, _, _ = _zoh(lre_ref[...], lim_ref[...], ldt_ref[...])
    are_ref[...] = a_re
    aim_ref[...] = a_im

    _, _, coef_re, coef_im = _zoh(lre_col[...], lim_col[...], ldt_col[...])
    b_re = bre_ref[...]
    b_im = bim_ref[...]
    bb_re = coef_re * b_re - coef_im * b_im
    bb_im = coef_re * b_im + coef_im * b_re
    state_shift = SSM_STATE.bit_length() - 1
    chan_shift = SSM_GROUP.bit_length() - 1
    row_g = lax.broadcasted_iota(jnp.int32, (SB, UB), 0) >> state_shift
    col_g = lax.broadcasted_iota(jnp.int32, (SB, UB), 1) >> chan_shift
    diag = row_g == col_g
    for part, bb in enumerate((bb_re, bb_im)):
        wt = jnp.where(diag, _spread_lanes(bb, GROUPS_PER_BLOCK), 0.0)
        bw_ref[0, :, part * SB:(part + 1) * SB] = wt.T.astype(BF16)

    row_g = lax.broadcasted_iota(jnp.int32, (UB, SB), 0) >> chan_shift
    col_g = lax.broadcasted_iota(jnp.int32, (UB, SB), 1) >> state_shift
    diag = row_g == col_g
    for part, c in enumerate((cre_ref[...], -cim_ref[...])):
        wt = jnp.where(diag, _spread_lanes(c, GROUPS_PER_BLOCK), 0.0)
        cw_ref[0, part * SB:(part + 1) * SB, :] = wt.T.astype(BF16)


def _discretise(lam_re, lam_im, log_dt, b_re, b_im, c_re, c_im):
    g, p = lam_re.shape
    h = b_re.shape[-1]
    nb = g // GROUPS_PER_BLOCK
    col = lambda x: x.reshape(g * p, 1)
    blk = lambda rows, cols: pl.BlockSpec((rows, cols), lambda i: (i, 0))
    return pl.pallas_call(
        _disc_kernel,
        grid=(nb,),
        in_specs=[blk(GROUPS_PER_BLOCK, p), blk(GROUPS_PER_BLOCK, p), blk(GROUPS_PER_BLOCK, 1),
                  blk(SB, 1), blk(SB, 1), blk(SB, 1),
                  blk(SB, h), blk(SB, h), blk(UB, p), blk(UB, p)],
        out_specs=(blk(GROUPS_PER_BLOCK, p), blk(GROUPS_PER_BLOCK, p),
                   pl.BlockSpec((1, UB, 2 * SB), lambda i: (i, 0, 0)),
                   pl.BlockSpec((1, 2 * SB, UB), lambda i: (i, 0, 0))),
        out_shape=(jax.ShapeDtypeStruct((g, p), F32), jax.ShapeDtypeStruct((g, p), F32),
                   jax.ShapeDtypeStruct((nb, UB, 2 * SB), BF16),
                   jax.ShapeDtypeStruct((nb, 2 * SB, UB), BF16)),
        compiler_params=_cparams(("arbitrary",)),
        name="s5_discretise",
    )(lam_re, lam_im, log_dt.reshape(g, 1),
      col(lam_re), col(lam_im), col(jnp.repeat(log_dt, p)),
      b_re.reshape(g * p, h), b_im.reshape(g * p, h), c_re.reshape(g * h, p), c_im.reshape(g * h, p))


def _ssm_prompt_kernel(u_ref, bw_ref, cw_ref, are_ref, aim_ref, d_ref,
                       y_ref, sre_ref, sim_ref,
                       bure, buim, st_re, st_im, *, nbatch):
    tc = pl.program_id(1)

    @pl.when(tc == 0)
    def _():
        st_re[...] = jnp.zeros_like(st_re)
        st_im[...] = jnp.zeros_like(st_im)

    pack = 8 // nbatch
    nq = SLABS // pack

    def slab_rows(c, b):
        ref, c = (bure, c) if c < SLABS else (buim, c - SLABS)
        r0 = ((c // nq) * nbatch + b) * SCAN_PITCH
        return ref, c % nq, slice(r0, r0 + SCAN_T)

    u_all = u_ref[...].reshape(nbatch * SCAN_T, UB)
    u_bf = u_all.astype(BF16)
    per_dot = 4
    for n in range(2 * SLABS // per_dot):
        bu = jnp.dot(u_bf, bw_ref[0, :, per_dot * n * LANES:per_dot * (n + 1) * LANES],
                     preferred_element_type=F32)
        for h in range(per_dot):
            for b in range(nbatch):
                ref, q, rows = slab_rows(per_dot * n + h, b)
                ref[q, rows, :] = bu[b * SCAN_T:(b + 1) * SCAN_T, h * LANES:(h + 1) * LANES]

    def packed(a_ref, q):
        return jnp.concatenate(
            [jnp.broadcast_to(a_ref[0, q + h * nq], (nbatch, LANES)) for h in range(pack)], axis=0)

    a_re = [packed(are_ref, q) for q in range(nq)]
    a_im = [packed(aim_ref, q) for q in range(nq)]
    init = tuple(st_re[q] for q in range(nq)) + tuple(st_im[q] for q in range(nq))

    def step(t, carry):
        rows = pl.ds(t, pack * nbatch, stride=SCAN_PITCH)
        new_re, new_im = [], []
        for q in range(nq):
            s_r, s_i = carry[q], carry[nq + q]
            n_r = a_re[q] * s_r - a_im[q] * s_i + bure[q, rows, :]
            n_i = a_re[q] * s_i + a_im[q] * s_r + buim[q, rows, :]
            bure[q, rows, :] = n_r
            buim[q, rows, :] = n_i
            new_re.append(n_r)
            new_im.append(n_i)
        return tuple(new_re) + tuple(new_im)

    fin = lax.fori_loop(0, SCAN_T, step, init, unroll=4)
    for q in range(nq):
        st_re[q] = fin[q]
        st_im[q] = fin[nq + q]

    d = d_ref[0]
    for b in range(nbatch):
        y = d * u_ref[b]
        for n in range(SLABS):
            (ref0, q0, rows0), (ref1, q1, rows1) = slab_rows(2 * n, b), slab_rows(2 * n + 1, b)
            s_pair = jnp.concatenate([ref0[q0, rows0, :], ref1[q1, rows1, :]], axis=1).astype(BF16)
            y = y + jnp.dot(s_pair, cw_ref[0, 2 * n * LANES:(2 * n + 2) * LANES, :],
                            preferred_element_type=F32)
        y_ref[b] = _gelu(y).astype(y_ref.dtype)

    @pl.when(tc == pl.num_programs(1) - 1)
    def _():
        for c in range(SLABS):
            q, h = c % nq, c // nq
            sre_ref[:, c * LANES:(c + 1) * LANES] = st_re[q, h * nbatch:(h + 1) * nbatch, :]
            sim_ref[:, c * LANES:(c + 1) * LANES] = st_im[q, h * nbatch:(h + 1) * nbatch, :]


def _ssm_prompt(z3, bw, cw, a_re, a_im, d_skip):
    nbatch, seq, width = z3.shape
    nb = bw.shape[0]
    states = nb * SB
    kern = functools.partial(_ssm_prompt_kernel, nbatch=nbatch)
    return pl.pallas_call(
        kern,
        grid=(nb, seq // SCAN_T),
        in_specs=[
            pl.BlockSpec((nbatch, SCAN_T, UB), lambda g, t: (0, t, g)),
            pl.BlockSpec((1, UB, 2 * SB), lambda g, t: (g, 0, 0)),
            pl.BlockSpec((1, 2 * SB, UB), lambda g, t: (g, 0, 0)),
            pl.BlockSpec((1, SLABS, 1, LANES), lambda g, t: (g, 0, 0, 0)),
            pl.BlockSpec((1, SLABS, 1, LANES), lambda g, t: (g, 0, 0, 0)),
            pl.BlockSpec((1, 1, UB), lambda g, t: (g, 0, 0)),
        ],
        out_specs=(
            pl.BlockSpec((nbatch, SCAN_T, UB), lambda g, t: (0, t, g)),
            pl.BlockSpec((nbatch, SB), lambda g, t: (0, g)),
            pl.BlockSpec((nbatch, SB), lambda g, t: (0, g)),
        ),
        out_shape=(
            jax.ShapeDtypeStruct((nbatch, seq, width), BF16),
            jax.ShapeDtypeStruct((nbatch, states), F32),
            jax.ShapeDtypeStruct((nbatch, states), F32),
        ),
        scratch_shapes=[
            pltpu.VMEM((SLABS * nbatch // 8, 8 * SCAN_PITCH, LANES), F32),
            pltpu.VMEM((SLABS * nbatch // 8, 8 * SCAN_PITCH, LANES), F32),
            pltpu.VMEM((SLABS * nbatch // 8, 8, LANES), F32),
            pltpu.VMEM((SLABS * nbatch // 8, 8, LANES), F32),
        ],
        compiler_params=_cparams(("arbitrary", "arbitrary")),
        name="ssm_prompt",
    )(z3, bw, cw, a_re.reshape(nb, SLABS, 1, LANES), a_im.reshape(nb, SLABS, 1, LANES),
      d_skip.reshape(nb, 1, UB))


def _ssm_sample_kernel(u_ref, bw_ref, cw_ref, are_ref, aim_ref, d_ref, x0re_ref, x0im_ref,
                       y_ref, sre_ref, sim_ref):
    u = u_ref[...]
    bu = jnp.dot(u.astype(BF16), bw_ref[0], preferred_element_type=F32)
    a_re = are_ref[0]
    a_im = aim_ref[0]
    x_re = x0re_ref[...]
    x_im = x0im_ref[...]
    s_re = a_re * x_re - a_im * x_im + bu[:, :SB]
    s_im = a_re * x_im + a_im * x_re + bu[:, SB:]
    sre_ref[...] = s_re
    sim_ref[...] = s_im
    s_cat = jnp.concatenate([s_re, s_im], axis=1).astype(BF16)
    y = jnp.dot(s_cat, cw_ref[0], preferred_element_type=F32) + d_ref[0] * u
    y_ref[...] = _gelu(y).astype(y_ref.dtype)


def _ssm_sample(z, bw, cw, a_re, a_im, d_skip, x0_re, x0_im):
    nbatch = z.shape[0]
    nb = bw.shape[0]
    return pl.pallas_call(
        _ssm_sample_kernel,
        grid=(nb,),
        in_specs=[
            pl.BlockSpec((nbatch, UB), lambda g: (0, g)),
            pl.BlockSpec((1, UB, 2 * SB), lambda g: (g, 0, 0)),
            pl.BlockSpec((1, 2 * SB, UB), lambda g: (g, 0, 0)),
            pl.BlockSpec((1, 1, SB), lambda g: (g, 0, 0)),
            pl.BlockSpec((1, 1, SB), lambda g: (g, 0, 0)),
            pl.BlockSpec((1, 1, UB), lambda g: (g, 0, 0)),
            pl.BlockSpec((nbatch, SB), lambda g: (0, g)),
            pl.BlockSpec((nbatch, SB), lambda g: (0, g)),
        ],
        out_specs=(
            pl.BlockSpec((nbatch, UB), lambda g: (0, g)),
            pl.BlockSpec((nbatch, SB), lambda g: (0, g)),
            pl.BlockSpec((nbatch, SB), lambda g: (0, g)),
        ),
        out_shape=(
            jax.ShapeDtypeStruct((nbatch, nb * UB), BF16),
            jax.ShapeDtypeStruct((nbatch, nb * SB), F32),
            jax.ShapeDtypeStruct((nbatch, nb * SB), F32),
        ),
        compiler_params=_cparams(("arbitrary",)),
        name="ssm_sample",
    )(z, bw, cw, a_re.reshape(nb, 1, SB), a_im.reshape(nb, 1, SB), d_skip.reshape(nb, 1, UB),
      x0_re, x0_im)


def _gmlp_prompt_kernel(gu_ref, gv_ref, gnv_ref, ws_ref, bias_ref, wo_ref, gate_ref, p_ref,
                        out_ref, s_scr, wt_scr, *, tm, heads):
    i = pl.program_id(0)
    j = pl.program_id(1)

    @pl.when(jnp.logical_and(i == 0, j == 0))
    def _():
        row = lax.broadcasted_iota(jnp.int32, (CHUNK, CHUNK), 0)
        col = lax.broadcasted_iota(jnp.int32, (CHUNK, CHUNK), 1)
        mask = (col <= row).astype(F32)
        for g in range(heads):
            wt_scr[g] = (ws_ref[g] * mask).astype(BF16)

    @pl.when(j == 0)
    def _():
        gnv = gnv_ref[...]

        def body(c, carry):
            r0 = pl.multiple_of(c * CHUNK, CHUNK)
            gv = gv_ref[pl.ds(r0, CHUNK), :]
            r = lax.rsqrt(jnp.mean(gv * gv, axis=-1, keepdims=True) + EPS)
            v = (gv * r * gnv).astype(BF16)
            for g in range(heads):
                sl = slice(g * GMLP_HEAD, (g + 1) * GMLP_HEAD)
                mixed = jnp.dot(wt_scr[g], v[:, sl], preferred_element_type=F32) + bias_ref[:, sl]
                s_scr[pl.ds(r0, CHUNK), sl] = (gu_ref[pl.ds(r0, CHUNK), sl] * mixed).astype(BF16)
            return carry

        lax.fori_loop(0, tm // CHUNK, body, 0)

    yb = jnp.dot(s_scr[...], wo_ref[0], preferred_element_type=F32)
    out_ref[...] = (p_ref[...] + gate_ref[...] * yb).astype(out_ref.dtype)


def _gmlp_prompt(guv, gnv, w_s, bias_full, w_gout, gates, p, *, gate_col, tm, tn):
    m = guv.shape[0]
    ntiles, width, _ = w_gout.shape
    d = ntiles * tn
    heads = w_s.shape[0]
    kern = functools.partial(_gmlp_prompt_kernel, tm=tm, heads=heads)
    return pl.pallas_call(
        kern,
        grid=(m // tm, d // tn),
        in_specs=[
            pl.BlockSpec((tm, width), lambda i, j: (i, 0)),
            pl.BlockSpec((tm, width), lambda i, j: (i, 1)),
            pl.BlockSpec((1, width), lambda i, j: (0, 0)),
            pl.BlockSpec((heads, CHUNK, CHUNK), lambda i, j: (0, 0, 0)),
            pl.BlockSpec((CHUNK, width), lambda i, j: (0, 0)),
            _col_tile_spec(w_gout, width, tn),
            pl.BlockSpec((tm, tn), lambda i, j: (i, j + gate_col // tn)),
            pl.BlockSpec((tm, tn), lambda i, j: (i, j)),
        ],
        out_specs=pl.BlockSpec((tm, tn), lambda i, j: (i, j)),
        out_shape=jax.ShapeDtypeStruct((m, d), BF16),
        scratch_shapes=[pltpu.VMEM((tm, width), BF16), pltpu.VMEM((heads, CHUNK, CHUNK), BF16)],
        compiler_params=_cparams(("arbitrary", "arbitrary")),
        name="gmlp_prompt",
    )(guv, guv, gnv, w_s, bias_full, w_gout, gates, p)


def _gmlp_sample_kernel(gu_ref, gv_ref, gnv_ref, wdiag_ref, bias_ref, wo_ref, gate_ref, p_ref,
                        out_ref, v_ref, wemit_ref, s_scr):
    j = pl.program_id(0)

    @pl.when(j == 0)
    def _():
        gv = gv_ref[...]
        r = lax.rsqrt(jnp.mean(gv * gv, axis=-1, keepdims=True) + EPS)
        v = gv * r * gnv_ref[...]
        v_ref[...] = v
        mixed = wdiag_ref[...] * v + bias_ref[...]
        s_scr[...] = (gu_ref[...] * mixed).astype(BF16)

    yb = jnp.dot(s_scr[...], _as_bf16(wo_ref, wemit_ref), preferred_element_type=F32)
    out_ref[...] = (p_ref[...] + gate_ref[...] * yb).astype(out_ref.dtype)


def _gmlp_sample(guv, gnv, wdiag, bias0, w_gout, gates, p, *, gate_col, tn):
    m = guv.shape[0]
    width, d = w_gout.shape
    return pl.pallas_call(
        _gmlp_sample_kernel,
        grid=(d // tn,),
        in_specs=[
            pl.BlockSpec((m, width), lambda j: (0, 0)),
            pl.BlockSpec((m, width), lambda j: (0, 1)),
            pl.BlockSpec((1, width), lambda j: (0, 0)),
            pl.BlockSpec((1, width), lambda j: (0, 0)),
            pl.BlockSpec((1, width), lambda j: (0, 0)),
            pl.BlockSpec((width, tn), lambda j: (0, j)),
            pl.BlockSpec((m, tn), lambda j: (0, j + gate_col // tn)),
            pl.BlockSpec((m, tn), lambda j: (0, j)),
        ],
        out_specs=(pl.BlockSpec((m, tn), lambda j: (0, j)),
                   pl.BlockSpec((m, width), lambda j: (0, 0)),
                   pl.BlockSpec((1, width, tn), lambda j: (j, 0, 0))),
        out_shape=(jax.ShapeDtypeStruct((m, d), BF16), jax.ShapeDtypeStruct((m, width), F32),
                   jax.ShapeDtypeStruct((d // tn, width, tn), BF16)),
        scratch_shapes=[pltpu.VMEM((m, width), BF16)],
        compiler_params=_cparams(("arbitrary",)),
        name="gmlp_sample_emit",
    )(guv, guv, gnv, wdiag, bias0, w_gout, gates, p)


def _mixer(x, h2, prm, w, *, nbatch, x0, tm_mm):
    sample = x0 is not None
    d_model = x.shape[1]
    ssm_width = prm["bw"].shape[0] * UB
    gmlp_width = prm["gnv"].shape[1]
    o1 = ssm_width
    o3 = o1 + 2 * gmlp_width
    tn = 512
    wb = {}

    cols = {}
    if sample:
        w = dict(w, win_ssm=w["w_in"], win_uv=w["w_in"], win_gate=w["w_in"],
                 glu_a=w["w_glu"], glu_b=w["w_glu"])
        cols = dict(win_uv=o1, win_gate=o3, glu_b=d_model)

    def mm(lhs, wkeys, extras, kind, n_out, out_dtype, name):
        out, emitted = _mm(lhs, [(w[k], cols.get(k, 0)) for k in wkeys], extras, kind=kind,
                           n_out=n_out, out_dtype=out_dtype, tm=tm_mm, tn=2 * tn, emit=sample, name=name)
        wb.update(zip(wkeys, emitted))
        return out

    z_ssm = mm(h2, ["win_ssm"], [], "identity", o1, F32, "w_in_ssm")
    guv = mm(h2, ["win_uv"], [], "gelu", 2 * gmlp_width, F32, "w_in_uv")
    gates = mm(h2, ["win_gate"], [], "sigmoid", 2 * d_model, F32, "w_in_gate")

    if sample:
        y, s_re, s_im = _ssm_sample(z_ssm, prm["bw"], prm["cw"], prm["a_re"], prm["a_im"],
                                    prm["d_skip"], x0[0], x0[1])
    else:
        seq = x.shape[0] // nbatch
        y, s_re, s_im = _ssm_prompt(z_ssm.reshape(nbatch, seq, ssm_width), prm["bw"], prm["cw"],
                                    prm["a_re"], prm["a_im"], prm["d_skip"])
        y = y.reshape(nbatch * seq, ssm_width)
    p = mm(y, ["glu_a", "glu_b"], [(gates, 0)], "gated_glu", d_model, F32, "ssm_glu")

    if sample:
        merged, v_rows, wb["w_gout"] = _gmlp_sample(guv, prm["gnv"], prm["w_diag"], prm["bias0"],
                                                    w["w_gout"], gates, p, gate_col=d_model, tn=tn)
    else:
        merged = _gmlp_prompt(guv, prm["gnv"], prm["w_s"], prm["bias_full"], w["w_gout"], gates, p,
                              gate_col=d_model, tm=tm_mm, tn=tn)
        v_rows = None
    x = mm(merged, ["w_out"], [(x, 0)], "residual", d_model, F32, "out_proj")
    return x, s_re, s_im, v_rows, wb


def kernel(x_prompt, x_sample, state_ssm_re, state_ssm_im, norm_ffn1, ffn1_gate, ffn1_up, ffn1_down, norm_mix, w_in, ssm_lambda_re, ssm_lambda_im, ssm_log_dt, ssm_b_re, ssm_b_im, ssm_c_re, ssm_c_im, ssm_d, ssm_w_glu, gmlp_norm_v, gmlp_w_s, gmlp_b_s, gmlp_w_out, w_out, norm_ffn2, ffn2_gate, ffn2_up, ffn2_down, norm_final):
    depth = w_in.shape[0]
    assert depth == 1, "the final RMSNorm is fused into the last FFN; one layer per step"
    batch, seq, d_model = x_prompt.shape
    dec_batch, dec_seq, _ = x_sample.shape
    assert dec_seq == 1
    groups, states = ssm_lambda_re.shape[1:]
    gmlp_width = gmlp_norm_v.shape[1]
    heads = gmlp_w_s.shape[1]
    head_dim = gmlp_width // heads
    assert seq % SCAN_T == 0 and seq % CHUNK == 0 and head_dim == GMLP_HEAD
    assert ssm_b_re.shape[-1] == SSM_GROUP and states == SSM_STATE and groups % GROUPS_PER_BLOCK == 0

    l = 0
    yp = x_prompt.reshape(batch * seq, d_model)
    ys = x_sample.reshape(dec_batch, d_model)
    gfin = norm_final.reshape(1, d_model)
    a_re, a_im, bw, cw = _discretise(ssm_lambda_re[l], ssm_lambda_im[l], ssm_log_dt[l],
                                     ssm_b_re[l], ssm_b_im[l], ssm_c_re[l], ssm_c_im[l])
    prm = dict(
        n1=norm_ffn1[l].reshape(1, d_model), nmix=norm_mix[l].reshape(1, d_model),
        n2=norm_ffn2[l].reshape(1, d_model),
        bw=bw, cw=cw, a_re=a_re, a_im=a_im, d_skip=ssm_d[l],
        gnv=gmlp_norm_v[l].reshape(1, gmlp_width), w_s=gmlp_w_s[l],
        bias_full=jnp.repeat(gmlp_b_s[l].T, head_dim, axis=1),
        w_diag=jnp.repeat(gmlp_w_s[l][:, 0, 0], head_dim).reshape(1, gmlp_width),
        bias0=jnp.repeat(gmlp_b_s[l][:, 0], head_dim).reshape(1, gmlp_width),
    )
    w_f32 = dict(w_in=w_in[l], w_glu=ssm_w_glu[l], w_gout=gmlp_w_out[l], w_out=w_out[l])
    x0 = (state_ssm_re[l].reshape(dec_batch, groups * states),
          state_ssm_im[l].reshape(dec_batch, groups * states))
    tm, tf = 1024, 256

    ys, hs, *f1 = _ffn(ys, prm["n1"], ffn1_gate[l], ffn1_up[l], ffn1_down[l], prm["nmix"],
                       tail="norm_out", tm=dec_batch, tf=tf, emit=True)
    yp, hp, *f2 = _ffn(yp, prm["n1"], *f1, prm["nmix"], tail="norm_out", tm=tm, tf=tf,
                       side=(ffn2_gate[l], ffn2_up[l], ffn2_down[l]), single_out=True)
    ys, sr, si, vr, w_bf16 = _mixer(ys, hs, prm, w_f32, nbatch=dec_batch, x0=x0, tm_mm=dec_batch)
    yp, pr, pi, _, _ = _mixer(yp, hp, prm, w_bf16, nbatch=batch, x0=None, tm_mm=tm)
    (ys,) = _ffn(ys, prm["n2"], *f2, gfin, tail="final", tm=dec_batch, tf=tf)
    (yp,) = _ffn(yp, prm["n2"], *f2, gfin, tail="final", tm=tm, tf=tf)
    return (yp.reshape(batch, seq, d_model), ys.reshape(dec_batch, dec_seq, d_model),
            pr.reshape(1, batch, groups, states), pi.reshape(1, batch, groups, states),
            sr.reshape(1, dec_batch, groups, states), si.reshape(1, dec_batch, groups, states),
            vr.reshape(1, dec_batch, dec_seq, gmlp_width))
```

```python
import functools
import math

import jax
import jax.numpy as jnp
from jax import lax
from jax.experimental import pallas as pl
from jax.experimental.pallas import tpu as pltpu

F32 = jnp.float32
BF16 = jnp.bfloat16

EPS = 1e-6
LANES = 128
SSM_GROUP = 16
SSM_STATE = 64
GROUPS_PER_BLOCK = 16
UB = GROUPS_PER_BLOCK * SSM_GROUP
SB = GROUPS_PER_BLOCK * SSM_STATE
SLABS = SB // LANES
SCAN_T = 256
SCAN_SKEW = 4
SCAN_PITCH = SCAN_T + SCAN_SKEW
CHUNK = 128
GMLP_HEAD = 128
VMEM_LIMIT = 56 * 1024 * 1024


def _cparams(sem):
    return pltpu.CompilerParams(dimension_semantics=sem, vmem_limit_bytes=VMEM_LIMIT)


def _gelu(x):
    c = math.sqrt(2.0 / math.pi)
    return 0.5 * x * (1.0 + jnp.tanh(c * (x + 0.044715 * (x * x * x))))


def _sigmoid(x):
    return 1.0 / (1.0 + jnp.exp(-x))


def _as_bf16(w_ref, emit_ref):
    w = w_ref[0] if len(w_ref.shape) == 3 else w_ref[...]
    if w.dtype != BF16:
        w = w.astype(BF16)
    if emit_ref is not None:
        if len(emit_ref.shape) == 3:
            emit_ref[0] = w
        else:
            emit_ref[...] = w
    return w


def _col_tile_spec(w, k, tn, col0=0):
    if w.ndim == 3:
        assert w.shape[1:] == (k, tn) and col0 == 0
        return pl.BlockSpec((1, k, tn), lambda i, j: (j, 0, 0))
    return pl.BlockSpec((k, tn), lambda i, j, o=col0 // tn: (0, j + o))


def _ffn_kernel(*refs, tm, d_model, col_chunk, row_chunk, tail, emit, side):
    refs = list(refs)
    x_hbm, gn_ref, wg_ref, wu_ref, wd_ref, gt_ref = refs[:6]
    del refs[:6]
    side_in = [refs.pop(0) for _ in range(3)] if side else []
    out_ref = refs.pop(0)
    h2_ref = refs.pop(0) if tail == "norm_out" else None
    emits = [refs.pop(0) for _ in range(3)] if emit else [None] * 3
    side_out = [refs.pop(0) for _ in range(3)] if side else []
    h_buf = h2_ref if h2_ref is not None else refs.pop(0)
    (x_sem,) = refs
    i = pl.program_id(0)
    j = pl.program_id(1)

    @pl.when(j == 0)
    def _():
        x_copy = pltpu.make_async_copy(x_hbm.at[pl.ds(pl.multiple_of(i * tm, tm), tm), :], out_ref, x_sem)
        x_copy.start()
        x_copy.wait()
        g = gn_ref[...]

        def body(c, carry):
            r0 = pl.multiple_of(c * row_chunk, row_chunk)
            xf = out_ref[pl.ds(r0, row_chunk), :]
            r = lax.rsqrt(jnp.mean(xf * xf, axis=-1, keepdims=True) + EPS)
            h_buf[pl.ds(r0, row_chunk), :] = (xf * r * g).astype(BF16)
            out_ref[pl.ds(r0, row_chunk), :] = 2.0 * xf
            return carry

        lax.fori_loop(0, tm // row_chunk, body, 0)

    h = h_buf[...]
    g = jnp.dot(h, _as_bf16(wg_ref, emits[0]), preferred_element_type=F32)
    u = jnp.dot(h, _as_bf16(wu_ref, emits[1]), preferred_element_type=F32)
    a = (g * _sigmoid(g) * u).astype(BF16)
    wd = _as_bf16(wd_ref, emits[2])
    for n in range(d_model // col_chunk):
        sl = slice(n * col_chunk, (n + 1) * col_chunk)
        out_ref[:, sl] += jnp.dot(a, wd[:, sl], preferred_element_type=F32)

    if side:
        for src, dst in zip(side_in[:2], side_out[:2]):
            tf_dst = dst.shape[2]
            for t in range(dst.shape[0]):
                dst[t] = src[:, t * tf_dst:(t + 1) * tf_dst].astype(BF16)
        side_out[2][...] = side_in[2][...].astype(BF16)

    @pl.when(j == pl.num_programs(1) - 1)
    def _():
        gt = gt_ref[...]

        def body(c, carry):
            r0 = pl.multiple_of(c * row_chunk, row_chunk)
            y = 0.5 * out_ref[pl.ds(r0, row_chunk), :]
            r = lax.rsqrt(jnp.mean(y * y, axis=-1, keepdims=True) + EPS)
            if tail == "norm_out":
                out_ref[pl.ds(r0, row_chunk), :] = y
                h2_ref[pl.ds(r0, row_chunk), :] = (y * r * gt).astype(BF16)
            else:
                out_ref[pl.ds(r0, row_chunk), :] = y * r * gt
            return carry

        lax.fori_loop(0, tm // row_chunk, body, 0)


def _ffn(x, gn, wg, wu, wd, gtail, *, tail, tm, tf, emit=False, side=None, single_out=False):
    m, d = x.shape
    f = wd.shape[0]
    ni, nj = m // tm, f // tf
    assert not emit or ni == 1
    kern = functools.partial(_ffn_kernel, tm=tm, d_model=d, col_chunk=512, row_chunk=min(tm, 64),
                             tail=tail, emit=emit, side=side is not None)
    in_specs = [
        pl.BlockSpec(memory_space=pl.ANY),
        pl.BlockSpec((1, d), lambda i, j: (0, 0)),
        _col_tile_spec(wg, d, tf),
        _col_tile_spec(wu, d, tf),
        pl.BlockSpec((tf, d), lambda i, j: (j, 0)),
        pl.BlockSpec((1, d), lambda i, j: (0, 0)),
    ]
    args = [x, gn, wg, wu, wd, gtail]
    out_specs = [pl.BlockSpec((tm, d), lambda i, j: (i, 0),
                              pipeline_mode=pl.Buffered(1) if single_out else None)]
    out_shape = [jax.ShapeDtypeStruct((m, d), F32)]
    scratch = [pltpu.VMEM((tm, d), BF16)]
    if tail == "norm_out":
        out_specs.append(pl.BlockSpec((tm, d), lambda i, j: (i, 0)))
        out_shape.append(jax.ShapeDtypeStruct((m, d), BF16))
        scratch = []
    if emit:
        out_specs += [pl.BlockSpec((1, d, tf), lambda i, j: (j, 0, 0)),
                      pl.BlockSpec((1, d, tf), lambda i, j: (j, 0, 0)),
                      pl.BlockSpec((tf, d), lambda i, j: (j, 0))]
        out_shape += [jax.ShapeDtypeStruct((nj, d, tf), BF16),
                      jax.ShapeDtypeStruct((nj, d, tf), BF16),
                      jax.ShapeDtypeStruct((f, d), BF16)]
    if side is not None:
        sg, su, sd = side
        nsteps = ni * nj
        rows_gu = 16
        rows_d = f // nsteps
        nblk = d // rows_gu
        assert sg.shape == su.shape == (d, f) and sd.shape == (f, d)
        assert nblk <= nsteps and rows_d * nsteps == f and rows_d % 16 == 0

        def gu_specs(start):
            def blk(i, j):
                return jnp.clip(i * nj + j - start, 0, nblk - 1)
            return (pl.BlockSpec((rows_gu, f), lambda i, j: (blk(i, j), 0)),
                    pl.BlockSpec((nj, rows_gu, tf), lambda i, j: (0, blk(i, j), 0)))

        g_in, g_out = gu_specs(0)
        u_in, u_out = gu_specs(nsteps - nblk)
        in_specs += [g_in, u_in, pl.BlockSpec((rows_d, d), lambda i, j: (i * nj + j, 0))]
        args += [sg, su, sd]
        out_specs += [g_out, u_out, pl.BlockSpec((rows_d, d), lambda i, j: (i * nj + j, 0))]
        out_shape += [jax.ShapeDtypeStruct((nj, d, tf), BF16),
                      jax.ShapeDtypeStruct((nj, d, tf), BF16),
                      jax.ShapeDtypeStruct((f, d), BF16)]
    return pl.pallas_call(
        kern,
        grid=(ni, nj),
        in_specs=in_specs,
        out_specs=out_specs,
        out_shape=out_shape,
        scratch_shapes=scratch + [pltpu.SemaphoreType.DMA],
        compiler_params=_cparams(("arbitrary", "arbitrary")),
        name="ffn_" + tail + ("_emit" if emit else "") + ("_side" if side is not None else ""),
    )(*args)


def _mm_epilogue(kind, accs, extras):
    if kind == "identity":
        return accs[0]
    if kind == "gelu":
        return _gelu(accs[0])
    if kind == "sigmoid":
        return _sigmoid(accs[0])
    if kind == "gated_glu":
        return extras[0].astype(F32) * (accs[0] * _sigmoid(accs[1]))
    if kind == "residual":
        return extras[0].astype(F32) + accs[0]
    raise ValueError(kind)


def _mm_kernel(*refs, n_w, n_extra, kind, emit):
    lhs_ref = refs[0]
    w_refs = refs[1:1 + n_w]
    extra_refs = refs[1 + n_w:1 + n_w + n_extra]
    out_ref = refs[1 + n_w + n_extra]
    emit_refs = refs[2 + n_w + n_extra:] if emit else [None] * n_w
    lhs = lhs_ref[...]
    accs = [jnp.dot(lhs, _as_bf16(w, e), preferred_element_type=F32) for w, e in zip(w_refs, emit_refs)]
    out_ref[...] = _mm_epilogue(kind, accs, [e[...] for e in extra_refs]).astype(out_ref.dtype)


def _mm(lhs, weights, extras, *, kind, n_out, out_dtype, tm, tn, emit, name):
    m, k = lhs.shape
    assert not emit or m == tm
    in_specs = [pl.BlockSpec((tm, k), lambda i, j: (i, 0))]
    args = [lhs]
    for w, c0 in weights:
        in_specs.append(_col_tile_spec(w, k, tn, c0))
        args.append(w)
    for e, c0 in extras:
        in_specs.append(pl.BlockSpec((tm, tn), lambda i, j, o=c0 // tn: (i, j + o)))
        args.append(e)
    out_specs = [pl.BlockSpec((tm, tn), lambda i, j: (i, j))]
    out_shape = [jax.ShapeDtypeStruct((m, n_out), out_dtype)]
    if emit:
        out_specs += [pl.BlockSpec((1, k, tn), lambda i, j: (j, 0, 0)) for _ in weights]
        out_shape += [jax.ShapeDtypeStruct((n_out // tn, k, tn), BF16) for _ in weights]
    kern = functools.partial(_mm_kernel, n_w=len(weights), n_extra=len(extras), kind=kind, emit=emit)
    res = pl.pallas_call(
        kern,
        grid=(m // tm, n_out // tn),
        in_specs=in_specs,
        out_specs=out_specs,
        out_shape=out_shape,
        compiler_params=_cparams(("arbitrary", "arbitrary")),
        name=name + ("_emit" if emit else ""),
    )(*args)
    return res[0], list(res[1:])


def _zoh(lam_re, lam_im, log_dt):
    dt = jnp.exp(log_dt)
    mag = jnp.exp(dt * lam_re)
    a_re = mag * jnp.cos(dt * lam_im)
    a_im = mag * jnp.sin(dt * lam_im)
    nr, ni = a_re - 1.0, a_im
    den = lam_re * lam_re + lam_im * lam_im
    coef_re = (nr * lam_re + ni * lam_im) / den
    coef_im = (ni * lam_re - nr * lam_im) / den
    return a_re, a_im, coef_re, coef_im


def _spread_lanes(x, reps):
    w = x.shape[1]
    assert w & (w - 1) == 0
    src = lax.broadcasted_iota(jnp.int32, (w, w * reps), 0)
    dst = lax.broadcasted_iota(jnp.int32, (w, w * reps), 1)
    sel = ((dst & (w - 1)) == src).astype(BF16)
    return jnp.dot(x.astype(BF16), sel, preferred_element_type=F32)


def _disc_kernel(lre_ref, lim_ref, ldt_ref, lre_col, lim_col, ldt_col, bre_ref, bim_ref, cre_ref, cim_ref,
                 are_ref, aim_ref, bw_ref, cw_ref):
    a_re, a_im, _, _ = _zoh(lre_ref[...], lim_ref[...], ldt_ref[...])
    are_ref[...] = a_re
    aim_ref[...] = a_im

    _, _, coef_re, coef_im = _zoh(lre_col[...], lim_col[...], ldt_col[...])
    b_re = bre_ref[...]
    b_im = bim_ref[...]
    bb_re = coef_re * b_re - coef_im * b_im
    bb_im = coef_re * b_im + coef_im * b_re
    state_shift = SSM_STATE.bit_length() - 1
    chan_shift = SSM_GROUP.bit_length() - 1
    row_g = lax.broadcasted_iota(jnp.int32, (SB, UB), 0) >> state_shift
    col_g = lax.broadcasted_iota(jnp.int32, (SB, UB), 1) >> chan_shift
    diag = row_g == col_g
    for part, bb in enumerate((bb_re, bb_im)):
        wt = jnp.where(diag, _spread_lanes(bb, GROUPS_PER_BLOCK), 0.0)
        bw_ref[0, :, part * SB:(part + 1) * SB] = wt.T.astype(BF16)

    row_g = lax.broadcasted_iota(jnp.int32, (UB, SB), 0) >> chan_shift
    col_g = lax.broadcasted_iota(jnp.int32, (UB, SB), 1) >> state_shift
    diag = row_g == col_g
    for part, c in enumerate((cre_ref[...], -cim_ref[...])):
        wt = jnp.where(diag, _spread_lanes(c, GROUPS_PER_BLOCK), 0.0)
        cw_ref[0, part * SB:(part + 1) * SB, :] = wt.T.astype(BF16)


def _discretise(lam_re, lam_im, log_dt, b_re, b_im, c_re, c_im):
    g, p = lam_re.shape
    h = b_re.shape[-1]
    nb = g // GROUPS_PER_BLOCK
    col = lambda x: x.reshape(g * p, 1)
    blk = lambda rows, cols: pl.BlockSpec((rows, cols), lambda i: (i, 0))
    return pl.pallas_call(
        _disc_kernel,
        grid=(nb,),
        in_specs=[blk(GROUPS_PER_BLOCK, p), blk(GROUPS_PER_BLOCK, p), blk(GROUPS_PER_BLOCK, 1),
                  blk(SB, 1), blk(SB, 1), blk(SB, 1),
                  blk(SB, h), blk(SB, h), blk(UB, p), blk(UB, p)],
        out_specs=(blk(GROUPS_PER_BLOCK, p), blk(GROUPS_PER_BLOCK, p),
                   pl.BlockSpec((1, UB, 2 * SB), lambda i: (i, 0, 0)),
                   pl.BlockSpec((1, 2 * SB, UB), lambda i: (i, 0, 0))),
        out_shape=(jax.ShapeDtypeStruct((g, p), F32), jax.ShapeDtypeStruct((g, p), F32),
                   jax.ShapeDtypeStruct((nb, UB, 2 * SB), BF16),
                   jax.ShapeDtypeStruct((nb, 2 * SB, UB), BF16)),
        compiler_params=_cparams(("arbitrary",)),
        name="s5_discretise",
    )(lam_re, lam_im, log_dt.reshape(g, 1),
      col(lam_re), col(lam_im), col(jnp.repeat(log_dt, p)),
      b_re.reshape(g * p, h), b_im.reshape(g * p, h), c_re.reshape(g * h, p), c_im.reshape(g * h, p))


def _ssm_prompt_kernel(u_ref, bw_ref, cw_ref, are_ref, aim_ref, d_ref,
                       y_ref, sre_ref, sim_ref,
                       bure, buim, st_re, st_im, *, nbatch):
    tc = pl.program_id(1)

    @pl.when(tc == 0)
    def _():
        st_re[...] = jnp.zeros_like(st_re)
        st_im[...] = jnp.zeros_like(st_im)

    pack = 8 // nbatch
    nq = SLABS // pack

    def slab_rows(c, b):
        ref, c = (bure, c) if c < SLABS else (buim, c - SLABS)
        v = (c // nq) * nbatch + b
        r0 = v * SCAN_PITCH + (v % 2) * SCAN_SKEW
        return ref, c % nq, slice(r0, r0 + SCAN_T)

    for ref in (bure, buim):
        for q in range(nq):
            for v in range(0, 8, 2):
                ref[q, v * SCAN_PITCH + SCAN_T:v * SCAN_PITCH + SCAN_T + 2 * SCAN_SKEW, :] = (
                    jnp.zeros((2 * SCAN_SKEW, LANES), F32))

    u_all = u_ref[...].reshape(nbatch * SCAN_T, UB)
    u_bf = u_all.astype(BF16)
    per_dot = 4
    for n in range(2 * SLABS // per_dot):
        bu = jnp.dot(u_bf, bw_ref[0, :, per_dot * n * LANES:per_dot * (n + 1) * LANES],
                     preferred_element_type=F32)
        for h in range(per_dot):
            for b in range(nbatch):
                ref, q, rows = slab_rows(per_dot * n + h, b)
                ref[q, rows, :] = bu[b * SCAN_T:(b + 1) * SCAN_T, h * LANES:(h + 1) * LANES]

    def packed(a_ref, q):
        return jnp.concatenate(
            [jnp.broadcast_to(a_ref[0, q + h * nq], (nbatch, LANES)) for h in range(pack)], axis=0)

    a_re = [packed(are_ref, q) for q in range(nq)]
    a_im = [packed(aim_ref, q) for q in range(nq)]
    init = tuple(st_re[q] for q in range(nq)) + tuple(st_im[q] for q in range(nq))

    def step(t, carry, active=None):
        rows = pl.ds(t, pack * nbatch, stride=SCAN_PITCH)
        new_re, new_im = [], []
        for q in range(nq):
            s_r, s_i = carry[q], carry[nq + q]
            b_r, b_i = bure[q, rows, :], buim[q, rows, :]
            n_r = a_re[q] * s_r - a_im[q] * s_i + b_r
            n_i = a_re[q] * s_i + a_im[q] * s_r + b_i
            if active is not None:
                b_r, b_i = jnp.where(active, n_r, b_r), jnp.where(active, n_i, b_i)
                n_r, n_i = jnp.where(active, n_r, s_r), jnp.where(active, n_i, s_i)
                bure[q, rows, :] = b_r
                buim[q, rows, :] = b_i
            else:
                bure[q, rows, :] = n_r
                buim[q, rows, :] = n_i
            new_re.append(n_r)
            new_im.append(n_i)
        return tuple(new_re) + tuple(new_im)

    odd = (lax.broadcasted_iota(jnp.int32, (8, LANES), 0) & 1) == 1
    carry = init
    for t in range(SCAN_SKEW):
        carry = step(t, carry, active=jnp.logical_not(odd))
    carry = lax.fori_loop(SCAN_SKEW, SCAN_T, step, carry, unroll=4)
    for t in range(SCAN_T, SCAN_T + SCAN_SKEW):
        carry = step(t, carry, active=odd)
    fin = carry
    for q in range(nq):
        st_re[q] = fin[q]
        st_im[q] = fin[nq + q]

    d = d_ref[0]
    for b in range(nbatch):
        y = d * u_ref[b]
        for n in range(SLABS):
            (ref0, q0, rows0), (ref1, q1, rows1) = slab_rows(2 * n, b), slab_rows(2 * n + 1, b)
            s_pair = jnp.concatenate([ref0[q0, rows0, :], ref1[q1, rows1, :]], axis=1).astype(BF16)
            y = y + jnp.dot(s_pair, cw_ref[0, 2 * n * LANES:(2 * n + 2) * LANES, :],
                            preferred_element_type=F32)
        y_ref[b] = _gelu(y).astype(y_ref.dtype)

    @pl.when(tc == pl.num_programs(1) - 1)
    def _():
        for c in range(SLABS):
            q, h = c % nq, c // nq
            sre_ref[:, c * LANES:(c + 1) * LANES] = st_re[q, h * nbatch:(h + 1) * nbatch, :]
            sim_ref[:, c * LANES:(c + 1) * LANES] = st_im[q, h * nbatch:(h + 1) * nbatch, :]


def _ssm_prompt(z3, bw, cw, a_re, a_im, d_skip):
    nbatch, seq, width = z3.shape
    nb = bw.shape[0]
    states = nb * SB
    kern = functools.partial(_ssm_prompt_kernel, nbatch=nbatch)
    return pl.pallas_call(
        kern,
        grid=(nb, seq // SCAN_T),
        in_specs=[
            pl.BlockSpec((nbatch, SCAN_T, UB), lambda g, t: (0, t, g)),
            pl.BlockSpec((1, UB, 2 * SB), lambda g, t: (g, 0, 0)),
            pl.BlockSpec((1, 2 * SB, UB), lambda g, t: (g, 0, 0)),
            pl.BlockSpec((1, SLABS, 1, LANES), lambda g, t: (g, 0, 0, 0)),
            pl.BlockSpec((1, SLABS, 1, LANES), lambda g, t: (g, 0, 0, 0)),
            pl.BlockSpec((1, 1, UB), lambda g, t: (g, 0, 0)),
        ],
        out_specs=(
            pl.BlockSpec((nbatch, SCAN_T, UB), lambda g, t: (0, t, g)),
            pl.BlockSpec((nbatch, SB), lambda g, t: (0, g)),
            pl.BlockSpec((nbatch, SB), lambda g, t: (0, g)),
        ),
        out_shape=(
            jax.ShapeDtypeStruct((nbatch, seq, width), BF16),
            jax.ShapeDtypeStruct((nbatch, states), F32),
            jax.ShapeDtypeStruct((nbatch, states), F32),
        ),
        scratch_shapes=[
            pltpu.VMEM((SLABS * nbatch // 8, 8 * SCAN_PITCH, LANES), F32),
            pltpu.VMEM((SLABS * nbatch // 8, 8 * SCAN_PITCH, LANES), F32),
            pltpu.VMEM((SLABS * nbatch // 8, 8, LANES), F32),
            pltpu.VMEM((SLABS * nbatch // 8, 8, LANES), F32),
        ],
        compiler_params=_cparams(("arbitrary", "arbitrary")),
        name="ssm_prompt",
    )(z3, bw, cw, a_re.reshape(nb, SLABS, 1, LANES), a_im.reshape(nb, SLABS, 1, LANES),
      d_skip.reshape(nb, 1, UB))


def _ssm_sample_kernel(u_ref, bw_ref, cw_ref, are_ref, aim_ref, d_ref, x0re_ref, x0im_ref,
                       y_ref, sre_ref, sim_ref):
    u = u_ref[...]
    bu = jnp.dot(u.astype(BF16), bw_ref[0], preferred_element_type=F32)
    a_re = are_ref[0]
    a_im = aim_ref[0]
    x_re = x0re_ref[...]
    x_im = x0im_ref[...]
    s_re = a_re * x_re - a_im * x_im + bu[:, :SB]
    s_im = a_re * x_im + a_im * x_re + bu[:, SB:]
    sre_ref[...] = s_re
    sim_ref[...] = s_im
    s_cat = jnp.concatenate([s_re, s_im], axis=1).astype(BF16)
    y = jnp.dot(s_cat, cw_ref[0], preferred_element_type=F32) + d_ref[0] * u
    y_ref[...] = _gelu(y).astype(y_ref.dtype)


def _ssm_sample(z, bw, cw, a_re, a_im, d_skip, x0_re, x0_im):
    nbatch = z.shape[0]
    nb = bw.shape[0]
    return pl.pallas_call(
        _ssm_sample_kernel,
        grid=(nb,),
        in_specs=[
            pl.BlockSpec((nbatch, UB), lambda g: (0, g)),
            pl.BlockSpec((1, UB, 2 * SB), lambda g: (g, 0, 0)),
            pl.BlockSpec((1, 2 * SB, UB), lambda g: (g, 0, 0)),
            pl.BlockSpec((1, 1, SB), lambda g: (g, 0, 0)),
            pl.BlockSpec((1, 1, SB), lambda g: (g, 0, 0)),
            pl.BlockSpec((1, 1, UB), lambda g: (g, 0, 0)),
            pl.BlockSpec((nbatch, SB), lambda g: (0, g)),
            pl.BlockSpec((nbatch, SB), lambda g: (0, g)),
        ],
        out_specs=(
            pl.BlockSpec((nbatch, UB), lambda g: (0, g)),
            pl.BlockSpec((nbatch, SB), lambda g: (0, g)),
            pl.BlockSpec((nbatch, SB), lambda g: (0, g)),
        ),
        out_shape=(
            jax.ShapeDtypeStruct((nbatch, nb * UB), BF16),
            jax.ShapeDtypeStruct((nbatch, nb * SB), F32),
            jax.ShapeDtypeStruct((nbatch, nb * SB), F32),
        ),
        compiler_params=_cparams(("arbitrary",)),
        name="ssm_sample",
    )(z, bw, cw, a_re.reshape(nb, 1, SB), a_im.reshape(nb, 1, SB), d_skip.reshape(nb, 1, UB),
      x0_re, x0_im)


def _gmlp_prompt_kernel(gu_ref, gv_ref, gnv_ref, ws_ref, bias_ref, wo_ref, gate_ref, p_ref,
                        out_ref, s_scr, wt_scr, *, tm, heads):
    i = pl.program_id(0)
    j = pl.program_id(1)

    @pl.when(jnp.logical_and(i == 0, j == 0))
    def _():
        row = lax.broadcasted_iota(jnp.int32, (CHUNK, CHUNK), 0)
        col = lax.broadcasted_iota(jnp.int32, (CHUNK, CHUNK), 1)
        mask = (col <= row).astype(F32)
        for g in range(heads):
            wt_scr[g] = (ws_ref[g] * mask).astype(BF16)

    @pl.when(j == 0)
    def _():
        gnv = gnv_ref[...]

        def body(c, carry):
            r0 = pl.multiple_of(c * CHUNK, CHUNK)
            gv = gv_ref[pl.ds(r0, CHUNK), :].astype(F32)
            r = lax.rsqrt(jnp.mean(gv * gv, axis=-1, keepdims=True) + EPS)
            v = (gv * r * gnv).astype(BF16)
            for g in range(heads):
                sl = slice(g * GMLP_HEAD, (g + 1) * GMLP_HEAD)
                mixed = jnp.dot(wt_scr[g], v[:, sl], preferred_element_type=F32) + bias_ref[:, sl]
                gu = gu_ref[pl.ds(r0, CHUNK), sl].astype(F32)
                s_scr[pl.ds(r0, CHUNK), sl] = (gu * mixed).astype(BF16)
            return carry

        lax.fori_loop(0, tm // CHUNK, body, 0)

    yb = jnp.dot(s_scr[...], wo_ref[0], preferred_element_type=F32)
    out_ref[...] = (p_ref[...].astype(F32) + gate_ref[...].astype(F32) * yb).astype(out_ref.dtype)


def _gmlp_prompt(guv, gnv, w_s, bias_full, w_gout, gates, p, *, gate_col, tm, tn):
    m = guv.shape[0]
    ntiles, width, _ = w_gout.shape
    d = ntiles * tn
    heads = w_s.shape[0]
    kern = functools.partial(_gmlp_prompt_kernel, tm=tm, heads=heads)
    return pl.pallas_call(
        kern,
        grid=(m // tm, d // tn),
        in_specs=[
            pl.BlockSpec((tm, width), lambda i, j: (i, 0)),
            pl.BlockSpec((tm, width), lambda i, j: (i, 1)),
            pl.BlockSpec((1, width), lambda i, j: (0, 0)),
            pl.BlockSpec((heads, CHUNK, CHUNK), lambda i, j: (0, 0, 0)),
            pl.BlockSpec((CHUNK, width), lambda i, j: (0, 0)),
            _col_tile_spec(w_gout, width, tn),
            pl.BlockSpec((tm, tn), lambda i, j: (i, j + gate_col // tn)),
            pl.BlockSpec((tm, tn), lambda i, j: (i, j)),
        ],
        out_specs=pl.BlockSpec((tm, tn), lambda i, j: (i, j)),
        out_shape=jax.ShapeDtypeStruct((m, d), BF16),
        scratch_shapes=[pltpu.VMEM((tm, width), BF16), pltpu.VMEM((heads, CHUNK, CHUNK), BF16)],
        compiler_params=_cparams(("arbitrary", "arbitrary")),
        name="gmlp_prompt",
    )(guv, guv, gnv, w_s, bias_full, w_gout, gates, p)


def _gmlp_sample_kernel(gu_ref, gv_ref, gnv_ref, wdiag_ref, bias_ref, wo_ref, gate_ref, p_ref,
                        out_ref, v_ref, wemit_ref, s_scr):
    j = pl.program_id(0)

    @pl.when(j == 0)
    def _():
        gv = gv_ref[...]
        r = lax.rsqrt(jnp.mean(gv * gv, axis=-1, keepdims=True) + EPS)
        v = gv * r * gnv_ref[...]
        v_ref[...] = v
        mixed = wdiag_ref[...] * v + bias_ref[...]
        s_scr[...] = (gu_ref[...] * mixed).astype(BF16)

    yb = jnp.dot(s_scr[...], _as_bf16(wo_ref, wemit_ref), preferred_element_type=F32)
    out_ref[...] = (p_ref[...] + gate_ref[...] * yb).astype(out_ref.dtype)


def _gmlp_sample(guv, gnv, wdiag, bias0, w_gout, gates, p, *, gate_col, tn):
    m = guv.shape[0]
    width, d = w_gout.shape
    return pl.pallas_call(
        _gmlp_sample_kernel,
        grid=(d // tn,),
        in_specs=[
            pl.BlockSpec((m, width), lambda j: (0, 0)),
            pl.BlockSpec((m, width), lambda j: (0, 1)),
            pl.BlockSpec((1, width), lambda j: (0, 0)),
            pl.BlockSpec((1, width), lambda j: (0, 0)),
            pl.BlockSpec((1, width), lambda j: (0, 0)),
            pl.BlockSpec((width, tn), lambda j: (0, j)),
            pl.BlockSpec((m, tn), lambda j: (0, j + gate_col // tn)),
            pl.BlockSpec((m, tn), lambda j: (0, j)),
        ],
        out_specs=(pl.BlockSpec((m, tn), lambda j: (0, j)),
                   pl.BlockSpec((m, width), lambda j: (0, 0)),
                   pl.BlockSpec((1, width, tn), lambda j: (j, 0, 0))),
        out_shape=(jax.ShapeDtypeStruct((m, d), BF16), jax.ShapeDtypeStruct((m, width), F32),
                   jax.ShapeDtypeStruct((d // tn, width, tn), BF16)),
        scratch_shapes=[pltpu.VMEM((m, width), BF16)],
        compiler_params=_cparams(("arbitrary",)),
        name="gmlp_sample_emit",
    )(guv, guv, gnv, wdiag, bias0, w_gout, gates, p)


def _mixer(x, h2, prm, w, *, nbatch, x0, tm_mm):
    sample = x0 is not None
    d_model = x.shape[1]
    ssm_width = prm["bw"].shape[0] * UB
    gmlp_width = prm["gnv"].shape[1]
    o1 = ssm_width
    o3 = o1 + 2 * gmlp_width
    tn = 512
    wb = {}

    cols = {}
    if sample:
        w = dict(w, win_ssm=w["w_in"], win_uv=w["w_in"], win_gate=w["w_in"],
                 glu_a=w["w_glu"], glu_b=w["w_glu"])
        cols = dict(win_uv=o1, win_gate=o3, glu_b=d_model)

    def mm(lhs, wkeys, extras, kind, n_out, out_dtype, name):
        out, emitted = _mm(lhs, [(w[k], cols.get(k, 0)) for k in wkeys], extras, kind=kind,
                           n_out=n_out, out_dtype=out_dtype, tm=tm_mm, tn=2 * tn, emit=sample, name=name)
        wb.update(zip(wkeys, emitted))
        return out

    act = F32 if sample else BF16
    z_ssm = mm(h2, ["win_ssm"], [], "identity", o1, F32, "w_in_ssm")
    guv = mm(h2, ["win_uv"], [], "gelu", 2 * gmlp_width, act, "w_in_uv")
    gates = mm(h2, ["win_gate"], [], "sigmoid", 2 * d_model, act, "w_in_gate")

    if sample:
        y, s_re, s_im = _ssm_sample(z_ssm, prm["bw"], prm["cw"], prm["a_re"], prm["a_im"],
                                    prm["d_skip"], x0[0], x0[1])
    else:
        seq = x.shape[0] // nbatch
        y, s_re, s_im = _ssm_prompt(z_ssm.reshape(nbatch, seq, ssm_width), prm["bw"], prm["cw"],
                                    prm["a_re"], prm["a_im"], prm["d_skip"])
        y = y.reshape(nbatch * seq, ssm_width)
    p = mm(y, ["glu_a", "glu_b"], [(gates, 0)], "gated_glu", d_model, act, "ssm_glu")

    if sample:
        merged, v_rows, wb["w_gout"] = _gmlp_sample(guv, prm["gnv"], prm["w_diag"], prm["bias0"],
                                                    w["w_gout"], gates, p, gate_col=d_model, tn=tn)
    else:
        merged = _gmlp_prompt(guv, prm["gnv"], prm["w_s"], prm["bias_full"], w["w_gout"], gates, p,
                              gate_col=d_model, tm=tm_mm, tn=tn)
        v_rows = None
    x = mm(merged, ["w_out"], [(x, 0)], "residual", d_model, F32, "out_proj")
    return x, s_re, s_im, v_rows, wb


def kernel(x_prompt, x_sample, state_ssm_re, state_ssm_im, norm_ffn1, ffn1_gate, ffn1_up, ffn1_down, norm_mix, w_in, ssm_lambda_re, ssm_lambda_im, ssm_log_dt, ssm_b_re, ssm_b_im, ssm_c_re, ssm_c_im, ssm_d, ssm_w_glu, gmlp_norm_v, gmlp_w_s, gmlp_b_s, gmlp_w_out, w_out, norm_ffn2, ffn2_gate, ffn2_up, ffn2_down, norm_final):
    depth = w_in.shape[0]
    assert depth == 1, "the final RMSNorm is fused into the last FFN; one layer per step"
    batch, seq, d_model = x_prompt.shape
    dec_batch, dec_seq, _ = x_sample.shape
    assert dec_seq == 1
    groups, states = ssm_lambda_re.shape[1:]
    gmlp_width = gmlp_norm_v.shape[1]
    heads = gmlp_w_s.shape[1]
    head_dim = gmlp_width // heads
    assert seq % SCAN_T == 0 and seq % CHUNK == 0 and head_dim == GMLP_HEAD
    assert ssm_b_re.shape[-1] == SSM_GROUP and states == SSM_STATE and groups % GROUPS_PER_BLOCK == 0

    l = 0
    yp = x_prompt.reshape(batch * seq, d_model)
    ys = x_sample.reshape(dec_batch, d_model)
    gfin = norm_final.reshape(1, d_model)
    a_re, a_im, bw, cw = _discretise(ssm_lambda_re[l], ssm_lambda_im[l], ssm_log_dt[l],
                                     ssm_b_re[l], ssm_b_im[l], ssm_c_re[l], ssm_c_im[l])
    prm = dict(
        n1=norm_ffn1[l].reshape(1, d_model), nmix=norm_mix[l].reshape(1, d_model),
        n2=norm_ffn2[l].reshape(1, d_model),
        bw=bw, cw=cw, a_re=a_re, a_im=a_im, d_skip=ssm_d[l],
        gnv=gmlp_norm_v[l].reshape(1, gmlp_width), w_s=gmlp_w_s[l],
        bias_full=jnp.repeat(gmlp_b_s[l].T, head_dim, axis=1),
        w_diag=jnp.repeat(gmlp_w_s[l][:, 0, 0], head_dim).reshape(1, gmlp_width),
        bias0=jnp.repeat(gmlp_b_s[l][:, 0], head_dim).reshape(1, gmlp_width),
    )
    w_f32 = dict(w_in=w_in[l], w_glu=ssm_w_glu[l], w_gout=gmlp_w_out[l], w_out=w_out[l])
    x0 = (state_ssm_re[l].reshape(dec_batch, groups * states),
          state_ssm_im[l].reshape(dec_batch, groups * states))
    tm, tf = 1024, 256

    ys, hs, *f1 = _ffn(ys, prm["n1"], ffn1_gate[l], ffn1_up[l], ffn1_down[l], prm["nmix"],
                       tail="norm_out", tm=dec_batch, tf=tf, emit=True)
    yp, hp, *f2 = _ffn(yp, prm["n1"], *f1, prm["nmix"], tail="norm_out", tm=tm, tf=tf,
                       side=(ffn2_gate[l], ffn2_up[l], ffn2_down[l]), single_out=True)
    ys, sr, si, vr, w_bf16 = _mixer(ys, hs, prm, w_f32, nbatch=dec_batch, x0=x0, tm_mm=dec_batch)
    yp, pr, pi, _, _ = _mixer(yp, hp, prm, w_bf16, nbatch=batch, x0=None, tm_mm=tm)
    (ys,) = _ffn(ys, prm["n2"], *f2, gfin, tail="final", tm=dec_batch, tf=tf)
    (yp,) = _ffn(yp, prm["n2"], *f2, gfin, tail="final", tm=tm, tf=tf)
    return (yp.reshape(batch, seq, d_model), ys.reshape(dec_batch, dec_seq, d_model),
            pr.reshape(1, batch, groups, states), pi.reshape(1, batch, groups, states),
            sr.reshape(1, dec_batch, groups, states), si.reshape(1, dec_batch, groups, states),
            vr.reshape(1, dec_batch, dec_seq, gmlp_width))
```

```python
import functools
import math

import jax
import jax.numpy as jnp
from jax import lax
from jax.experimental import pallas as pl
from jax.experimental.pallas import tpu as pltpu

F32 = jnp.float32
BF16 = jnp.bfloat16

EPS = 1e-6
LANES = 128
SSM_GROUP = 16
SSM_STATE = 64
GROUPS_PER_BLOCK = 16
UB = GROUPS_PER_BLOCK * SSM_GROUP
SB = GROUPS_PER_BLOCK * SSM_STATE
SLABS = SB // LANES
SCAN_T = 256
SCAN_SKEW = 4
SCAN_PITCH = SCAN_T + SCAN_SKEW
CHUNK = 128
GMLP_HEAD = 128
VMEM_LIMIT = 56 * 1024 * 1024


def _cparams(sem):
    return pltpu.CompilerParams(dimension_semantics=sem, vmem_limit_bytes=VMEM_LIMIT)


def _gelu(x):
    c = math.sqrt(2.0 / math.pi)
    return 0.5 * x * (1.0 + jnp.tanh(c * (x + 0.044715 * (x * x * x))))


def _sigmoid(x):
    return 1.0 / (1.0 + jnp.exp(-x))


def _as_bf16(w_ref, emit_ref):
    w = w_ref[0] if len(w_ref.shape) == 3 else w_ref[...]
    if w.dtype != BF16:
        w = w.astype(BF16)
    if emit_ref is not None:
        if len(emit_ref.shape) == 3:
            emit_ref[0] = w
        else:
            emit_ref[...] = w
    return w


def _col_tile_spec(w, k, tn, col0=0):
    if w.ndim == 3:
        assert w.shape[1:] == (k, tn) and col0 == 0
        return pl.BlockSpec((1, k, tn), lambda i, j: (j, 0, 0))
    return pl.BlockSpec((k, tn), lambda i, j, o=col0 // tn: (0, j + o))


def _ffn_kernel(*refs, tm, d_model, col_chunk, row_chunk, tail, emit, side):
    refs = list(refs)
    x_hbm, gn_ref, wg_ref, wu_ref, wd_ref, gt_ref = refs[:6]
    del refs[:6]
    side_in = [refs.pop(0) for _ in range(3)] if side else []
    out_ref = refs.pop(0)
    h2_ref = refs.pop(0) if tail == "norm_out" else None
    emits = [refs.pop(0) for _ in range(3)] if emit else [None] * 3
    side_out = [refs.pop(0) for _ in range(3)] if side else []
    h_buf = h2_ref if h2_ref is not None else refs.pop(0)
    (x_sem,) = refs
    i = pl.program_id(0)
    j = pl.program_id(1)

    @pl.when(j == 0)
    def _():
        x_copy = pltpu.make_async_copy(x_hbm.at[pl.ds(pl.multiple_of(i * tm, tm), tm), :], out_ref, x_sem)
        x_copy.start()
        x_copy.wait()
        g = gn_ref[...]

        def body(c, carry):
            r0 = pl.multiple_of(c * row_chunk, row_chunk)
            xf = out_ref[pl.ds(r0, row_chunk), :]
            r = lax.rsqrt(jnp.mean(xf * xf, axis=-1, keepdims=True) + EPS)
            h_buf[pl.ds(r0, row_chunk), :] = (xf * r * g).astype(BF16)
            out_ref[pl.ds(r0, row_chunk), :] = 2.0 * xf
            return carry

        lax.fori_loop(0, tm // row_chunk, body, 0)

    h = h_buf[...]
    g = jnp.dot(h, _as_bf16(wg_ref, emits[0]), preferred_element_type=F32)
    u = jnp.dot(h, _as_bf16(wu_ref, emits[1]), preferred_element_type=F32)
    a = (g * _sigmoid(g) * u).astype(BF16)
    wd = _as_bf16(wd_ref, emits[2])
    for n in range(d_model // col_chunk):
        sl = slice(n * col_chunk, (n + 1) * col_chunk)
        out_ref[:, sl] += jnp.dot(a, wd[:, sl], preferred_element_type=F32)

    if side:
        for src, dst in zip(side_in[:2], side_out[:2]):
            tf_dst = dst.shape[2]
            for t in range(dst.shape[0]):
                dst[t] = src[:, t * tf_dst:(t + 1) * tf_dst].astype(BF16)
        side_out[2][...] = side_in[2][...].astype(BF16)

    @pl.when(j == pl.num_programs(1) - 1)
    def _():
        gt = gt_ref[...]

        def body(c, carry):
            r0 = pl.multiple_of(c * row_chunk, row_chunk)
            y = 0.5 * out_ref[pl.ds(r0, row_chunk), :]
            r = lax.rsqrt(jnp.mean(y * y, axis=-1, keepdims=True) + EPS)
            if tail == "norm_out":
                out_ref[pl.ds(r0, row_chunk), :] = y
                h2_ref[pl.ds(r0, row_chunk), :] = (y * r * gt).astype(BF16)
            else:
                out_ref[pl.ds(r0, row_chunk), :] = y * r * gt
            return carry

        lax.fori_loop(0, tm // row_chunk, body, 0)


def _ffn(x, gn, wg, wu, wd, gtail, *, tail, tm, tf, emit=False, side=None, single_out=False):
    m, d = x.shape
    f = wd.shape[0]
    ni, nj = m // tm, f // tf
    assert not emit or ni == 1
    kern = functools.partial(_ffn_kernel, tm=tm, d_model=d, col_chunk=512, row_chunk=min(tm, 64),
                             tail=tail, emit=emit, side=side is not None)
    in_specs = [
        pl.BlockSpec(memory_space=pl.ANY),
        pl.BlockSpec((1, d), lambda i, j: (0, 0)),
        _col_tile_spec(wg, d, tf),
        _col_tile_spec(wu, d, tf),
        pl.BlockSpec((tf, d), lambda i, j: (j, 0)),
        pl.BlockSpec((1, d), lambda i, j: (0, 0)),
    ]
    args = [x, gn, wg, wu, wd, gtail]
    out_specs = [pl.BlockSpec((tm, d), lambda i, j: (i, 0),
                              pipeline_mode=pl.Buffered(1) if single_out else None)]
    out_shape = [jax.ShapeDtypeStruct((m, d), F32)]
    scratch = [pltpu.VMEM((tm, d), BF16)]
    if tail == "norm_out":
        out_specs.append(pl.BlockSpec((tm, d), lambda i, j: (i, 0)))
        out_shape.append(jax.ShapeDtypeStruct((m, d), BF16))
        scratch = []
    if emit:
        out_specs += [pl.BlockSpec((1, d, tf), lambda i, j: (j, 0, 0)),
                      pl.BlockSpec((1, d, tf), lambda i, j: (j, 0, 0)),
                      pl.BlockSpec((tf, d), lambda i, j: (j, 0))]
        out_shape += [jax.ShapeDtypeStruct((nj, d, tf), BF16),
                      jax.ShapeDtypeStruct((nj, d, tf), BF16),
                      jax.ShapeDtypeStruct((f, d), BF16)]
    if side is not None:
        sg, su, sd = side
        nsteps = ni * nj
        rows_gu = 16
        rows_d = f // nsteps
        nblk = d // rows_gu
        assert sg.shape == su.shape == (d, f) and sd.shape == (f, d)
        assert nblk <= nsteps and rows_d * nsteps == f and rows_d % 16 == 0

        def gu_specs(start):
            def blk(i, j):
                return jnp.clip(i * nj + j - start, 0, nblk - 1)
            return (pl.BlockSpec((rows_gu, f), lambda i, j: (blk(i, j), 0)),
                    pl.BlockSpec((nj, rows_gu, tf), lambda i, j: (0, blk(i, j), 0)))

        g_in, g_out = gu_specs(0)
        u_in, u_out = gu_specs(nsteps - nblk)
        in_specs += [g_in, u_in, pl.BlockSpec((rows_d, d), lambda i, j: (i * nj + j, 0))]
        args += [sg, su, sd]
        out_specs += [g_out, u_out, pl.BlockSpec((rows_d, d), lambda i, j: (i * nj + j, 0))]
        out_shape += [jax.ShapeDtypeStruct((nj, d, tf), BF16),
                      jax.ShapeDtypeStruct((nj, d, tf), BF16),
                      jax.ShapeDtypeStruct((f, d), BF16)]
    return pl.pallas_call(
        kern,
        grid=(ni, nj),
        in_specs=in_specs,
        out_specs=out_specs,
        out_shape=out_shape,
        scratch_shapes=scratch + [pltpu.SemaphoreType.DMA],
        compiler_params=_cparams(("arbitrary", "arbitrary")),
        name="ffn_" + tail + ("_emit" if emit else "") + ("_side" if side is not None else ""),
    )(*args)


def _mm_epilogue(kind, accs, extras):
    if kind == "identity":
        return accs[0]
    if kind == "gelu":
        return _gelu(accs[0])
    if kind == "sigmoid":
        return _sigmoid(accs[0])
    if kind == "gated_glu":
        return extras[0].astype(F32) * (accs[0] * _sigmoid(accs[1]))
    if kind == "residual":
        return extras[0].astype(F32) + accs[0]
    raise ValueError(kind)


def _mm_kernel(*refs, n_w, n_extra, kind, emit):
    lhs_ref = refs[0]
    w_refs = refs[1:1 + n_w]
    extra_refs = refs[1 + n_w:1 + n_w + n_extra]
    out_ref = refs[1 + n_w + n_extra]
    emit_refs = refs[2 + n_w + n_extra:] if emit else [None] * n_w
    lhs = lhs_ref[...]
    accs = [jnp.dot(lhs, _as_bf16(w, e), preferred_element_type=F32) for w, e in zip(w_refs, emit_refs)]
    out_ref[...] = _mm_epilogue(kind, accs, [e[...] for e in extra_refs]).astype(out_ref.dtype)


def _mm(lhs, weights, extras, *, kind, n_out, out_dtype, tm, tn, emit, name):
    m, k = lhs.shape
    assert not emit or m == tm
    in_specs = [pl.BlockSpec((tm, k), lambda i, j: (i, 0))]
    args = [lhs]
    for w, c0 in weights:
        in_specs.append(_col_tile_spec(w, k, tn, c0))
        args.append(w)
    for e, c0 in extras:
        in_specs.append(pl.BlockSpec((tm, tn), lambda i, j, o=c0 // tn: (i, j + o)))
        args.append(e)
    out_specs = [pl.BlockSpec((tm, tn), lambda i, j: (i, j))]
    out_shape = [jax.ShapeDtypeStruct((m, n_out), out_dtype)]
    if emit:
        out_specs += [pl.BlockSpec((1, k, tn), lambda i, j: (j, 0, 0)) for _ in weights]
        out_shape += [jax.ShapeDtypeStruct((n_out // tn, k, tn), BF16) for _ in weights]
    kern = functools.partial(_mm_kernel, n_w=len(weights), n_extra=len(extras), kind=kind, emit=emit)
    res = pl.pallas_call(
        kern,
        grid=(m // tm, n_out // tn),
        in_specs=in_specs,
        out_specs=out_specs,
        out_shape=out_shape,
        compiler_params=_cparams(("arbitrary", "arbitrary")),
        name=name + ("_emit" if emit else ""),
    )(*args)
    return res[0], list(res[1:])


def _zoh(lam_re, lam_im, log_dt):
    dt = jnp.exp(log_dt)
    mag = jnp.exp(dt * lam_re)
    a_re = mag * jnp.cos(dt * lam_im)
    a_im = mag * jnp.sin(dt * lam_im)
    nr, ni = a_re - 1.0, a_im
    den = lam_re * lam_re + lam_im * lam_im
    coef_re = (nr * lam_re + ni * lam_im) / den
    coef_im = (ni * lam_re - nr * lam_im) / den
    return a_re, a_im, coef_re, coef_im


def _spread_lanes(x, reps):
    w = x.shape[1]
    assert w & (w - 1) == 0
    src = lax.broadcasted_iota(jnp.int32, (w, w * reps), 0)
    dst = lax.broadcasted_iota(jnp.int32, (w, w * reps), 1)
    sel = ((dst & (w - 1)) == src).astype(BF16)
    return jnp.dot(x.astype(BF16), sel, preferred_element_type=F32)


def _disc_kernel(lre_ref, lim_ref, ldt_ref, lre_col, lim_col, ldt_col, bre_ref, bim_ref, cre_ref, cim_ref,
                 are_ref, aim_ref, bw_ref, cw_ref):
    a_re, a_im, _, _ = _zoh(lre_ref[...], lim_ref[...], ldt_ref[...])
    are_ref[...] = a_re
    aim_ref[...] = a_im

    _, _, coef_re, coef_im = _zoh(lre_col[...], lim_col[...], ldt_col[...])
    b_re = bre_ref[...]
    b_im = bim_ref[...]
    bb_re = coef_re * b_re - coef_im * b_im
    bb_im = coef_re * b_im + coef_im * b_re
    state_shift = SSM_STATE.bit_length() - 1
    chan_shift = SSM_GROUP.bit_length() - 1
    row_g = lax.broadcasted_iota(jnp.int32, (SB, UB), 0) >> state_shift
    col_g = lax.broadcasted_iota(jnp.int32, (SB, UB), 1) >> chan_shift
    diag = row_g == col_g
    for part, bb in enumerate((bb_re, bb_im)):
        wt = jnp.where(diag, _spread_lanes(bb, GROUPS_PER_BLOCK), 0.0)
        bw_ref[0, :, part * SB:(part + 1) * SB] = wt.T.astype(BF16)

    row_g = lax.broadcasted_iota(jnp.int32, (UB, SB), 0) >> chan_shift
    col_g = lax.broadcasted_iota(jnp.int32, (UB, SB), 1) >> state_shift
    diag = row_g == col_g
    for part, c in enumerate((cre_ref[...], -cim_ref[...])):
        wt = jnp.where(diag, _spread_lanes(c, GROUPS_PER_BLOCK), 0.0)
        cw_ref[0, part * SB:(part + 1) * SB, :] = wt.T.astype(BF16)


def _discretise(lam_re, lam_im, log_dt, b_re, b_im, c_re, c_im):
    g, p = lam_re.shape
    h = b_re.shape[-1]
    nb = g // GROUPS_PER_BLOCK
    col = lambda x: x.reshape(g * p, 1)
    blk = lambda rows, cols: pl.BlockSpec((rows, cols), lambda i: (i, 0))
    return pl.pallas_call(
        _disc_kernel,
        grid=(nb,),
        in_specs=[blk(GROUPS_PER_BLOCK, p), blk(GROUPS_PER_BLOCK, p), blk(GROUPS_PER_BLOCK, 1),
                  blk(SB, 1), blk(SB, 1), blk(SB, 1),
                  blk(SB, h), blk(SB, h), blk(UB, p), blk(UB, p)],
        out_specs=(blk(GROUPS_PER_BLOCK, p), blk(GROUPS_PER_BLOCK, p),
                   pl.BlockSpec((1, UB, 2 * SB), lambda i: (i, 0, 0)),
                   pl.BlockSpec((1, 2 * SB, UB), lambda i: (i, 0, 0))),
        out_shape=(jax.ShapeDtypeStruct((g, p), F32), jax.ShapeDtypeStruct((g, p), F32),
                   jax.ShapeDtypeStruct((nb, UB, 2 * SB), BF16),
                   jax.ShapeDtypeStruct((nb, 2 * SB, UB), BF16)),
        compiler_params=_cparams(("arbitrary",)),
        name="s5_discretise",
    )(lam_re, lam_im, log_dt.reshape(g, 1),
      col(lam_re), col(lam_im), col(jnp.repeat(log_dt, p)),
      b_re.reshape(g * p, h), b_im.reshape(g * p, h), c_re.reshape(g * h, p), c_im.reshape(g * h, p))


def _ssm_prompt_kernel(*refs, nbatch):
    parity = pl.program_id(1) % 2
    for fill in (0, 1):
        pl.when(parity == fill)(functools.partial(_ssm_prompt_step, *refs, nbatch=nbatch, fill=fill))


def _ssm_prompt_step(un_ref, u_ref, bw_ref, cw_ref, are_ref, aim_ref, d_ref,
                     y_ref, sre_ref, sim_ref,
                     bure2, buim2, st_re, st_im, *, nbatch, fill):
    tc = pl.program_id(1)
    scan = 1 - fill

    @pl.when(tc == 0)
    def _():
        st_re[...] = jnp.zeros_like(st_re)
        st_im[...] = jnp.zeros_like(st_im)
        bure2[1] = jnp.zeros(bure2.shape[1:], F32)
        buim2[1] = jnp.zeros(buim2.shape[1:], F32)

    bure_f, buim_f = bure2.at[fill], buim2.at[fill]
    bure, buim = bure2.at[scan], buim2.at[scan]

    pack = 8 // nbatch
    nq = SLABS // pack

    def slab_rows(c, b):
        part, c = (0, c) if c < SLABS else (1, c - SLABS)
        v = (c // nq) * nbatch + b
        r0 = v * SCAN_PITCH + (v % 2) * SCAN_SKEW
        return part, c % nq, slice(r0, r0 + SCAN_T)

    for ref in (bure_f, buim_f):
        for q in range(nq):
            for v in range(0, 8, 2):
                ref[q, v * SCAN_PITCH + SCAN_T:v * SCAN_PITCH + SCAN_T + 2 * SCAN_SKEW, :] = (
                    jnp.zeros((2 * SCAN_SKEW, LANES), F32))

    u_bf = un_ref[...].reshape(nbatch * SCAN_T, UB).astype(BF16)
    per_dot = 4
    for n in range(2 * SLABS // per_dot):
        bu = jnp.dot(u_bf, bw_ref[0, :, per_dot * n * LANES:per_dot * (n + 1) * LANES],
                     preferred_element_type=F32)
        for h in range(per_dot):
            for b in range(nbatch):
                part, q, rows = slab_rows(per_dot * n + h, b)
                (bure_f, buim_f)[part][q, rows, :] = bu[b * SCAN_T:(b + 1) * SCAN_T, h * LANES:(h + 1) * LANES]

    def packed(a_ref, q):
        return jnp.concatenate(
            [jnp.broadcast_to(a_ref[0, q + h * nq], (nbatch, LANES)) for h in range(pack)], axis=0)

    a_re = [packed(are_ref, q) for q in range(nq)]
    a_im = [packed(aim_ref, q) for q in range(nq)]
    init = tuple(st_re[q] for q in range(nq)) + tuple(st_im[q] for q in range(nq))

    def step(t, carry, active=None):
        rows = pl.ds(t, pack * nbatch, stride=SCAN_PITCH)
        new_re, new_im = [], []
        for q in range(nq):
            s_r, s_i = carry[q], carry[nq + q]
            b_r, b_i = bure[q, rows, :], buim[q, rows, :]
            n_r = a_re[q] * s_r - a_im[q] * s_i + b_r
            n_i = a_re[q] * s_i + a_im[q] * s_r + b_i
            if active is not None:
                b_r, b_i = jnp.where(active, n_r, b_r), jnp.where(active, n_i, b_i)
                n_r, n_i = jnp.where(active, n_r, s_r), jnp.where(active, n_i, s_i)
                bure[q, rows, :] = b_r
                buim[q, rows, :] = b_i
            else:
                bure[q, rows, :] = n_r
                buim[q, rows, :] = n_i
            new_re.append(n_r)
            new_im.append(n_i)
        return tuple(new_re) + tuple(new_im)

    odd = (lax.broadcasted_iota(jnp.int32, (8, LANES), 0) & 1) == 1
    carry = init
    for t in range(SCAN_SKEW):
        carry = step(t, carry, active=jnp.logical_not(odd))
    for t in range(SCAN_SKEW, SCAN_T):
        carry = step(t, carry)
    for t in range(SCAN_T, SCAN_T + SCAN_SKEW):
        carry = step(t, carry, active=odd)
    fin = carry
    for q in range(nq):
        st_re[q] = fin[q]
        st_im[q] = fin[nq + q]

    d = d_ref[0]
    for b in range(nbatch):
        y = d * u_ref[b]
        for n in range(SLABS):
            (p0, q0, rows0), (p1, q1, rows1) = slab_rows(2 * n, b), slab_rows(2 * n + 1, b)
            s_pair = jnp.concatenate([(bure, buim)[p0][q0, rows0, :], (bure, buim)[p1][q1, rows1, :]],
                                     axis=1).astype(BF16)
            y = y + jnp.dot(s_pair, cw_ref[0, 2 * n * LANES:(2 * n + 2) * LANES, :],
                            preferred_element_type=F32)
        y_ref[b] = _gelu(y).astype(y_ref.dtype)

    @pl.when(tc == pl.num_programs(1) - 1)
    def _():
        for c in range(SLABS):
            q, h = c % nq, c // nq
            sre_ref[:, c * LANES:(c + 1) * LANES] = st_re[q, h * nbatch:(h + 1) * nbatch, :]
            sim_ref[:, c * LANES:(c + 1) * LANES] = st_im[q, h * nbatch:(h + 1) * nbatch, :]


def _ssm_prompt(z3, bw, cw, a_re, a_im, d_skip):
    nbatch, seq, width = z3.shape
    nb = bw.shape[0]
    states = nb * SB
    kern = functools.partial(_ssm_prompt_kernel, nbatch=nbatch)
    nt = seq // SCAN_T
    assert 8 % nbatch == 0
    return pl.pallas_call(
        kern,
        grid=(nb, nt + 1),
        in_specs=[
            pl.BlockSpec((nbatch, SCAN_T, UB), lambda g, t: (0, jnp.minimum(t, nt - 1), g)),
            pl.BlockSpec((nbatch, SCAN_T, UB), lambda g, t: (0, jnp.maximum(t - 1, 0), g)),
            pl.BlockSpec((1, UB, 2 * SB), lambda g, t: (g, 0, 0)),
            pl.BlockSpec((1, 2 * SB, UB), lambda g, t: (g, 0, 0)),
            pl.BlockSpec((1, SLABS, 1, LANES), lambda g, t: (g, 0, 0, 0)),
            pl.BlockSpec((1, SLABS, 1, LANES), lambda g, t: (g, 0, 0, 0)),
            pl.BlockSpec((1, 1, UB), lambda g, t: (g, 0, 0)),
        ],
        out_specs=(
            pl.BlockSpec((nbatch, SCAN_T, UB), lambda g, t: (0, jnp.maximum(t - 1, 0), g)),
            pl.BlockSpec((nbatch, SB), lambda g, t: (0, g)),
            pl.BlockSpec((nbatch, SB), lambda g, t: (0, g)),
        ),
        out_shape=(
            jax.ShapeDtypeStruct((nbatch, seq, width), BF16),
            jax.ShapeDtypeStruct((nbatch, states), F32),
            jax.ShapeDtypeStruct((nbatch, states), F32),
        ),
        scratch_shapes=[
            pltpu.VMEM((2, SLABS * nbatch // 8, 8 * SCAN_PITCH, LANES), F32),
            pltpu.VMEM((2, SLABS * nbatch // 8, 8 * SCAN_PITCH, LANES), F32),
            pltpu.VMEM((SLABS * nbatch // 8, 8, LANES), F32),
            pltpu.VMEM((SLABS * nbatch // 8, 8, LANES), F32),
        ],
        compiler_params=_cparams(("arbitrary", "arbitrary")),
        name="ssm_prompt",
    )(z3, z3, bw, cw, a_re.reshape(nb, SLABS, 1, LANES), a_im.reshape(nb, SLABS, 1, LANES),
      d_skip.reshape(nb, 1, UB))


def _ssm_sample_kernel(u_ref, bw_ref, cw_ref, are_ref, aim_ref, d_ref, x0re_ref, x0im_ref,
                       y_ref, sre_ref, sim_ref):
    u = u_ref[...]
    bu = jnp.dot(u.astype(BF16), bw_ref[0], preferred_element_type=F32)
    a_re = are_ref[0]
    a_im = aim_ref[0]
    x_re = x0re_ref[...]
    x_im = x0im_ref[...]
    s_re = a_re * x_re - a_im * x_im + bu[:, :SB]
    s_im = a_re * x_im + a_im * x_re + bu[:, SB:]
    sre_ref[...] = s_re
    sim_ref[...] = s_im
    s_cat = jnp.concatenate([s_re, s_im], axis=1).astype(BF16)
    y = jnp.dot(s_cat, cw_ref[0], preferred_element_type=F32) + d_ref[0] * u
    y_ref[...] = _gelu(y).astype(y_ref.dtype)


def _ssm_sample(z, bw, cw, a_re, a_im, d_skip, x0_re, x0_im):
    nbatch = z.shape[0]
    nb = bw.shape[0]
    return pl.pallas_call(
        _ssm_sample_kernel,
        grid=(nb,),
        in_specs=[
            pl.BlockSpec((nbatch, UB), lambda g: (0, g)),
            pl.BlockSpec((1, UB, 2 * SB), lambda g: (g, 0, 0)),
            pl.BlockSpec((1, 2 * SB, UB), lambda g: (g, 0, 0)),
            pl.BlockSpec((1, 1, SB), lambda g: (g, 0, 0)),
            pl.BlockSpec((1, 1, SB), lambda g: (g, 0, 0)),
            pl.BlockSpec((1, 1, UB), lambda g: (g, 0, 0)),
            pl.BlockSpec((nbatch, SB), lambda g: (0, g)),
            pl.BlockSpec((nbatch, SB), lambda g: (0, g)),
        ],
        out_specs=(
            pl.BlockSpec((nbatch, UB), lambda g: (0, g)),
            pl.BlockSpec((nbatch, SB), lambda g: (0, g)),
            pl.BlockSpec((nbatch, SB), lambda g: (0, g)),
        ),
        out_shape=(
            jax.ShapeDtypeStruct((nbatch, nb * UB), BF16),
            jax.ShapeDtypeStruct((nbatch, nb * SB), F32),
            jax.ShapeDtypeStruct((nbatch, nb * SB), F32),
        ),
        compiler_params=_cparams(("arbitrary",)),
        name="ssm_sample",
    )(z, bw, cw, a_re.reshape(nb, 1, SB), a_im.reshape(nb, 1, SB), d_skip.reshape(nb, 1, UB),
      x0_re, x0_im)


def _gmlp_prompt_kernel(gu_ref, gv_ref, gnv_ref, ws_ref, bias_ref, wo_ref, gate_ref, p_ref,
                        out_ref, s_scr, wt_scr, *, tm, heads):
    i = pl.program_id(0)
    j = pl.program_id(1)

    @pl.when(jnp.logical_and(i == 0, j == 0))
    def _():
        row = lax.broadcasted_iota(jnp.int32, (CHUNK, CHUNK), 0)
        col = lax.broadcasted_iota(jnp.int32, (CHUNK, CHUNK), 1)
        mask = (col <= row).astype(F32)
        for g in range(heads):
            wt_scr[g] = (ws_ref[g] * mask).astype(BF16)

    @pl.when(j == 0)
    def _():
        gnv = gnv_ref[...]

        def body(c, carry):
            r0 = pl.multiple_of(c * CHUNK, CHUNK)
            gv = gv_ref[pl.ds(r0, CHUNK), :].astype(F32)
            r = lax.rsqrt(jnp.mean(gv * gv, axis=-1, keepdims=True) + EPS)
            v = (gv * r * gnv).astype(BF16)
            for g in range(heads):
                sl = slice(g * GMLP_HEAD, (g + 1) * GMLP_HEAD)
                mixed = jnp.dot(wt_scr[g], v[:, sl], preferred_element_type=F32) + bias_ref[:, sl]
                gu = gu_ref[pl.ds(r0, CHUNK), sl].astype(F32)
                s_scr[pl.ds(r0, CHUNK), sl] = (gu * mixed).astype(BF16)
            return carry

        lax.fori_loop(0, tm // CHUNK, body, 0)

    yb = jnp.dot(s_scr[...], wo_ref[0], preferred_element_type=F32)
    out_ref[...] = (p_ref[...].astype(F32) + gate_ref[...].astype(F32) * yb).astype(out_ref.dtype)


def _gmlp_prompt(guv, gnv, w_s, bias_full, w_gout, gates, p, *, gate_col, tm, tn):
    m = guv.shape[0]
    ntiles, width, _ = w_gout.shape
    d = ntiles * tn
    heads = w_s.shape[0]
    kern = functools.partial(_gmlp_prompt_kernel, tm=tm, heads=heads)
    return pl.pallas_call(
        kern,
        grid=(m // tm, d // tn),
        in_specs=[
            pl.BlockSpec((tm, width), lambda i, j: (i, 0)),
            pl.BlockSpec((tm, width), lambda i, j: (i, 1)),
            pl.BlockSpec((1, width), lambda i, j: (0, 0)),
            pl.BlockSpec((heads, CHUNK, CHUNK), lambda i, j: (0, 0, 0)),
            pl.BlockSpec((CHUNK, width), lambda i, j: (0, 0)),
            _col_tile_spec(w_gout, width, tn),
            pl.BlockSpec((tm, tn), lambda i, j: (i, j + gate_col // tn)),
            pl.BlockSpec((tm, tn), lambda i, j: (i, j)),
        ],
        out_specs=pl.BlockSpec((tm, tn), lambda i, j: (i, j)),
        out_shape=jax.ShapeDtypeStruct((m, d), BF16),
        scratch_shapes=[pltpu.VMEM((tm, width), BF16), pltpu.VMEM((heads, CHUNK, CHUNK), BF16)],
        compiler_params=_cparams(("arbitrary", "arbitrary")),
        name="gmlp_prompt",
    )(guv, guv, gnv, w_s, bias_full, w_gout, gates, p)


def _gmlp_sample_kernel(gu_ref, gv_ref, gnv_ref, wdiag_ref, bias_ref, wo_ref, gate_ref, p_ref,
                        out_ref, v_ref, wemit_ref, s_scr):
    j = pl.program_id(0)

    @pl.when(j == 0)
    def _():
        gv = gv_ref[...]
        r = lax.rsqrt(jnp.mean(gv * gv, axis=-1, keepdims=True) + EPS)
        v = gv * r * gnv_ref[...]
        v_ref[...] = v
        mixed = wdiag_ref[...] * v + bias_ref[...]
        s_scr[...] = (gu_ref[...] * mixed).astype(BF16)

    yb = jnp.dot(s_scr[...], _as_bf16(wo_ref, wemit_ref), preferred_element_type=F32)
    out_ref[...] = (p_ref[...] + gate_ref[...] * yb).astype(out_ref.dtype)


def _gmlp_sample(guv, gnv, wdiag, bias0, w_gout, gates, p, *, gate_col, tn):
    m = guv.shape[0]
    width, d = w_gout.shape
    return pl.pallas_call(
        _gmlp_sample_kernel,
        grid=(d // tn,),
        in_specs=[
            pl.BlockSpec((m, width), lambda j: (0, 0)),
            pl.BlockSpec((m, width), lambda j: (0, 1)),
            pl.BlockSpec((1, width), lambda j: (0, 0)),
            pl.BlockSpec((1, width), lambda j: (0, 0)),
            pl.BlockSpec((1, width), lambda j: (0, 0)),
            pl.BlockSpec((width, tn), lambda j: (0, j)),
            pl.BlockSpec((m, tn), lambda j: (0, j + gate_col // tn)),
            pl.BlockSpec((m, tn), lambda j: (0, j)),
        ],
        out_specs=(pl.BlockSpec((m, tn), lambda j: (0, j)),
                   pl.BlockSpec((m, width), lambda j: (0, 0)),
                   pl.BlockSpec((1, width, tn), lambda j: (j, 0, 0))),
        out_shape=(jax.ShapeDtypeStruct((m, d), BF16), jax.ShapeDtypeStruct((m, width), F32),
                   jax.ShapeDtypeStruct((d // tn, width, tn), BF16)),
        scratch_shapes=[pltpu.VMEM((m, width), BF16)],
        compiler_params=_cparams(("arbitrary",)),
        name="gmlp_sample_emit",
    )(guv, guv, gnv, wdiag, bias0, w_gout, gates, p)


def _mixer(x, h2, prm, w, *, nbatch, x0, tm_mm):
    sample = x0 is not None
    d_model = x.shape[1]
    ssm_width = prm["bw"].shape[0] * UB
    gmlp_width = prm["gnv"].shape[1]
    o1 = ssm_width
    o3 = o1 + 2 * gmlp_width
    tn = 512
    wb = {}

    cols = {}
    if sample:
        w = dict(w, win_ssm=w["w_in"], win_uv=w["w_in"], win_gate=w["w_in"],
                 glu_a=w["w_glu"], glu_b=w["w_glu"])
        cols = dict(win_uv=o1, win_gate=o3, glu_b=d_model)

    def mm(lhs, wkeys, extras, kind, n_out, out_dtype, name):
        out, emitted = _mm(lhs, [(w[k], cols.get(k, 0)) for k in wkeys], extras, kind=kind,
                           n_out=n_out, out_dtype=out_dtype, tm=tm_mm, tn=2 * tn, emit=sample, name=name)
        wb.update(zip(wkeys, emitted))
        return out

    act = F32 if sample else BF16
    z_ssm = mm(h2, ["win_ssm"], [], "identity", o1, F32, "w_in_ssm")
    guv = mm(h2, ["win_uv"], [], "gelu", 2 * gmlp_width, act, "w_in_uv")
    gates = mm(h2, ["win_gate"], [], "sigmoid", 2 * d_model, act, "w_in_gate")

    if sample:
        y, s_re, s_im = _ssm_sample(z_ssm, prm["bw"], prm["cw"], prm["a_re"], prm["a_im"],
                                    prm["d_skip"], x0[0], x0[1])
    else:
        seq = x.shape[0] // nbatch
        y, s_re, s_im = _ssm_prompt(z_ssm.reshape(nbatch, seq, ssm_width), prm["bw"], prm["cw"],
                                    prm["a_re"], prm["a_im"], prm["d_skip"])
        y = y.reshape(nbatch * seq, ssm_width)
    p = mm(y, ["glu_a", "glu_b"], [(gates, 0)], "gated_glu", d_model, act, "ssm_glu")

    if sample:
        merged, v_rows, wb["w_gout"] = _gmlp_sample(guv, prm["gnv"], prm["w_diag"], prm["bias0"],
                                                    w["w_gout"], gates, p, gate_col=d_model, tn=tn)
    else:
        merged = _gmlp_prompt(guv, prm["gnv"], prm["w_s"], prm["bias_full"], w["w_gout"], gates, p,
                              gate_col=d_model, tm=tm_mm, tn=tn)
        v_rows = None
    x = mm(merged, ["w_out"], [(x, 0)], "residual", d_model, F32, "out_proj")
    return x, s_re, s_im, v_rows, wb


def kernel(x_prompt, x_sample, state_ssm_re, state_ssm_im, norm_ffn1, ffn1_gate, ffn1_up, ffn1_down, norm_mix, w_in, ssm_lambda_re, ssm_lambda_im, ssm_log_dt, ssm_b_re, ssm_b_im, ssm_c_re, ssm_c_im, ssm_d, ssm_w_glu, gmlp_norm_v, gmlp_w_s, gmlp_b_s, gmlp_w_out, w_out, norm_ffn2, ffn2_gate, ffn2_up, ffn2_down, norm_final):
    depth = w_in.shape[0]
    assert depth == 1, "the final RMSNorm is fused into the last FFN; one layer per step"
    batch, seq, d_model = x_prompt.shape
    dec_batch, dec_seq, _ = x_sample.shape
    assert dec_seq == 1
    groups, states = ssm_lambda_re.shape[1:]
    gmlp_width = gmlp_norm_v.shape[1]
    heads = gmlp_w_s.shape[1]
    head_dim = gmlp_width // heads
    assert seq % SCAN_T == 0 and seq % CHUNK == 0 and head_dim == GMLP_HEAD
    assert ssm_b_re.shape[-1] == SSM_GROUP and states == SSM_STATE and groups % GROUPS_PER_BLOCK == 0

    l = 0
    yp = x_prompt.reshape(batch * seq, d_model)
    ys = x_sample.reshape(dec_batch, d_model)
    gfin = norm_final.reshape(1, d_model)
    a_re, a_im, bw, cw = _discretise(ssm_lambda_re[l], ssm_lambda_im[l], ssm_log_dt[l],
                                     ssm_b_re[l], ssm_b_im[l], ssm_c_re[l], ssm_c_im[l])
    prm = dict(
        n1=norm_ffn1[l].reshape(1, d_model), nmix=norm_mix[l].reshape(1, d_model),
        n2=norm_ffn2[l].reshape(1, d_model),
        bw=bw, cw=cw, a_re=a_re, a_im=a_im, d_skip=ssm_d[l],
        gnv=gmlp_norm_v[l].reshape(1, gmlp_width), w_s=gmlp_w_s[l],
        bias_full=jnp.repeat(gmlp_b_s[l].T, head_dim, axis=1),
        w_diag=jnp.repeat(gmlp_w_s[l][:, 0, 0], head_dim).reshape(1, gmlp_width),
        bias0=jnp.repeat(gmlp_b_s[l][:, 0], head_dim).reshape(1, gmlp_width),
    )
    w_f32 = dict(w_in=w_in[l], w_glu=ssm_w_glu[l], w_gout=gmlp_w_out[l], w_out=w_out[l])
    x0 = (state_ssm_re[l].reshape(dec_batch, groups * states),
          state_ssm_im[l].reshape(dec_batch, groups * states))
    tm, tf = 1024, 256

    ys, hs, *f1 = _ffn(ys, prm["n1"], ffn1_gate[l], ffn1_up[l], ffn1_down[l], prm["nmix"],
                       tail="norm_out", tm=dec_batch, tf=tf, emit=True)
    yp, hp, *f2 = _ffn(yp, prm["n1"], *f1, prm["nmix"], tail="norm_out", tm=tm, tf=tf,
                       side=(ffn2_gate[l], ffn2_up[l], ffn2_down[l]), single_out=True)
    ys, sr, si, vr, w_bf16 = _mixer(ys, hs, prm, w_f32, nbatch=dec_batch, x0=x0, tm_mm=dec_batch)
    yp, pr, pi, _, _ = _mixer(yp, hp, prm, w_bf16, nbatch=batch, x0=None, tm_mm=tm)
    (ys,) = _ffn(ys, prm["n2"], *f2, gfin, tail="final", tm=dec_batch, tf=tf)
    (yp,) = _ffn(yp, prm["n2"], *f2, gfin, tail="final", tm=tm, tf=tf)
    return (yp.reshape(batch, seq, d_model), ys.reshape(dec_batch, dec_seq, d_model),
            pr.reshape(1, batch, groups, states), pi.reshape(1, batch, groups, states),
            sr.reshape(1, dec_batch, groups, states), si.reshape(1, dec_batch, groups, states),
            vr.reshape(1, dec_batch, dec_seq, gmlp_width))
```

```python
import functools
import math

import jax
import jax.numpy as jnp
from jax import lax
from jax.experimental import pallas as pl
from jax.experimental.pallas import tpu as pltpu

F32 = jnp.float32
BF16 = jnp.bfloat16

EPS = 1e-6
LANES = 128
SSM_GROUP = 16
SSM_STATE = 64
GROUPS_PER_BLOCK = 16
UB = GROUPS_PER_BLOCK * SSM_GROUP
SB = GROUPS_PER_BLOCK * SSM_STATE
SLABS = SB // LANES
SCAN_T = 256
SCAN_SKEW = 4
SCAN_PITCH = SCAN_T + SCAN_SKEW
CHUNK = 128
GMLP_HEAD = 128
VMEM_LIMIT = 56 * 1024 * 1024


def _cparams(sem):
    return pltpu.CompilerParams(dimension_semantics=sem, vmem_limit_bytes=VMEM_LIMIT)


def _gelu(x):
    c = math.sqrt(2.0 / math.pi)
    return 0.5 * x * (1.0 + jnp.tanh(c * (x + 0.044715 * (x * x * x))))


def _sigmoid(x):
    return 1.0 / (1.0 + jnp.exp(-x))


def _as_bf16(w_ref, emit_ref):
    w = w_ref[0] if len(w_ref.shape) == 3 else w_ref[...]
    if w.dtype != BF16:
        w = w.astype(BF16)
    if emit_ref is not None:
        if len(emit_ref.shape) == 3:
            emit_ref[0] = w
        else:
            emit_ref[...] = w
    return w


def _col_tile_spec(w, k, tn, col0=0):
    if w.ndim == 3:
        assert w.shape[1:] == (k, tn) and col0 == 0
        return pl.BlockSpec((1, k, tn), lambda i, j: (j, 0, 0))
    return pl.BlockSpec((k, tn), lambda i, j, o=col0 // tn: (0, j + o))


def _ffn_kernel(*refs, tm, d_model, col_chunk, row_chunk, tail, emit, side, tail_rows):
    refs = list(refs)
    x_hbm = refs.pop(0)
    xt_hbm = refs.pop(0) if tail_rows else None
    gn_ref, wg_ref, wu_ref, wd_ref, gt_ref = refs[:5]
    del refs[:5]
    side_in = [refs.pop(0) for _ in range(3)] if side else []
    out_ref = refs.pop(0)
    outt_ref = refs.pop(0) if tail_rows else None
    h2_ref = refs.pop(0) if tail == "norm_out" else None
    emits = [refs.pop(0) for _ in range(3)] if emit else [None] * 3
    side_out = [refs.pop(0) for _ in range(3)] if side else []
    h_buf = h2_ref if h2_ref is not None else refs.pop(0)
    x_sem, xt_sem = refs
    i = pl.program_id(0)
    j = pl.program_id(1)
    last = pl.num_programs(0) - 1
    head_rows = tm - tail_rows

    def copy_in(src, dst, sem):
        cp = pltpu.make_async_copy(src, dst, sem)
        cp.start()
        return cp

    @pl.when(j == 0)
    def _():
        def whole_tile():
            copy_in(x_hbm.at[pl.ds(pl.multiple_of(i * tm, tm), tm), :], out_ref, x_sem).wait()

        if tail_rows:
            pl.when(i < last)(whole_tile)

            @pl.when(i == last)
            def _():
                n_x = x_hbm.shape[0]
                head = copy_in(x_hbm.at[pl.ds(n_x - head_rows, head_rows), :],
                               out_ref.at[pl.ds(0, head_rows), :], x_sem)
                rest = copy_in(xt_hbm, out_ref.at[pl.ds(head_rows, tail_rows), :], xt_sem)
                head.wait()
                rest.wait()
        else:
            whole_tile()
        g = gn_ref[...]

        def body(c, carry):
            r0 = pl.multiple_of(c * row_chunk, row_chunk)
            xf = out_ref[pl.ds(r0, row_chunk), :]
            r = lax.rsqrt(jnp.mean(xf * xf, axis=-1, keepdims=True) + EPS)
            h_buf[pl.ds(r0, row_chunk), :] = (xf * r * g).astype(BF16)
            out_ref[pl.ds(r0, row_chunk), :] = 2.0 * xf
            return carry

        lax.fori_loop(0, tm // row_chunk, body, 0)

    h = h_buf[...]
    g = jnp.dot(h, _as_bf16(wg_ref, emits[0]), preferred_element_type=F32)
    u = jnp.dot(h, _as_bf16(wu_ref, emits[1]), preferred_element_type=F32)
    a = (g * _sigmoid(g) * u).astype(BF16)
    wd = _as_bf16(wd_ref, emits[2])
    for n in range(d_model // col_chunk):
        sl = slice(n * col_chunk, (n + 1) * col_chunk)
        out_ref[:, sl] += jnp.dot(a, wd[:, sl], preferred_element_type=F32)

    if side:
        for src, dst in zip(side_in[:2], side_out[:2]):
            tf_dst = dst.shape[2]
            for t in range(dst.shape[0]):
                dst[t] = src[:, t * tf_dst:(t + 1) * tf_dst].astype(BF16)
        side_out[2][...] = side_in[2][...].astype(BF16)

    @pl.when(j == pl.num_programs(1) - 1)
    def _():
        gt = gt_ref[...]

        def body(c, carry):
            r0 = pl.multiple_of(c * row_chunk, row_chunk)
            y = 0.5 * out_ref[pl.ds(r0, row_chunk), :]
            r = lax.rsqrt(jnp.mean(y * y, axis=-1, keepdims=True) + EPS)
            if tail == "norm_out":
                out_ref[pl.ds(r0, row_chunk), :] = y
                h2_ref[pl.ds(r0, row_chunk), :] = (y * r * gt).astype(BF16)
            else:
                out_ref[pl.ds(r0, row_chunk), :] = y * r * gt
            return carry

        lax.fori_loop(0, tm // row_chunk, body, 0)
        if tail_rows:
            @pl.when(i == last)
            def _():
                outt_ref[...] = out_ref[pl.ds(head_rows, tail_rows), :]


def _ffn(x, gn, wg, wu, wd, gtail, *, tail, tm, tf, emit=False, side=None, single_out=False, x_tail=None):
    n_x, d = x.shape
    f = wd.shape[0]
    tail_rows = 0 if x_tail is None else x_tail.shape[0]
    m = n_x + tail_rows
    ni, nj = m // tm, f // tf
    assert ni * tm == m and not (emit and ni > 1)
    assert tail_rows == 0 or (tail == "final" and 0 < tail_rows < tm and (tm - tail_rows) % 16 == 0)
    row_chunk = max(c for c in range(16, 81, 16) if tm % c == 0)
    kern = functools.partial(_ffn_kernel, tm=tm, d_model=d, col_chunk=512, row_chunk=row_chunk,
                             tail=tail, emit=emit, side=side is not None, tail_rows=tail_rows)
    in_specs = [pl.BlockSpec(memory_space=pl.ANY)] * (2 if tail_rows else 1) + [
        pl.BlockSpec((1, d), lambda i, j: (0, 0)),
        _col_tile_spec(wg, d, tf),
        _col_tile_spec(wu, d, tf),
        pl.BlockSpec((tf, d), lambda i, j: (j, 0)),
        pl.BlockSpec((1, d), lambda i, j: (0, 0)),
    ]
    args = [x] + ([x_tail] if tail_rows else []) + [gn, wg, wu, wd, gtail]
    out_specs = [pl.BlockSpec((tm, d), lambda i, j: (i, 0),
                              pipeline_mode=pl.Buffered(1) if single_out else None)]
    out_shape = [jax.ShapeDtypeStruct((n_x, d), F32)]
    if tail_rows:
        out_specs.append(pl.BlockSpec((tail_rows, d), lambda i, j: (0, 0)))
        out_shape.append(jax.ShapeDtypeStruct((tail_rows, d), F32))
    scratch = [pltpu.VMEM((tm, d), BF16)]
    if tail == "norm_out":
        out_specs.append(pl.BlockSpec((tm, d), lambda i, j: (i, 0)))
        out_shape.append(jax.ShapeDtypeStruct((m, d), BF16))
        scratch = []
    if emit:
        out_specs += [pl.BlockSpec((1, d, tf), lambda i, j: (j, 0, 0)),
                      pl.BlockSpec((1, d, tf), lambda i, j: (j, 0, 0)),
                      pl.BlockSpec((tf, d), lambda i, j: (j, 0))]
        out_shape += [jax.ShapeDtypeStruct((nj, d, tf), BF16),
                      jax.ShapeDtypeStruct((nj, d, tf), BF16),
                      jax.ShapeDtypeStruct((f, d), BF16)]
    if side is not None:
        sg, su, sd = side
        nsteps = ni * nj
        rows_gu = 16
        rows_d = f // nsteps
        nblk = d // rows_gu
        assert sg.shape == su.shape == (d, f) and sd.shape == (f, d)
        assert nblk <= nsteps and rows_d * nsteps == f and rows_d % 16 == 0

        def gu_specs(start):
            def blk(i, j):
                return jnp.clip(i * nj + j - start, 0, nblk - 1)
            return (pl.BlockSpec((rows_gu, f), lambda i, j: (blk(i, j), 0)),
                    pl.BlockSpec((nj, rows_gu, tf), lambda i, j: (0, blk(i, j), 0)))

        g_in, g_out = gu_specs(0)
        u_in, u_out = gu_specs(nsteps - nblk)
        in_specs += [g_in, u_in, pl.BlockSpec((rows_d, d), lambda i, j: (i * nj + j, 0))]
        args += [sg, su, sd]
        out_specs += [g_out, u_out, pl.BlockSpec((rows_d, d), lambda i, j: (i * nj + j, 0))]
        out_shape += [jax.ShapeDtypeStruct((nj, d, tf), BF16),
                      jax.ShapeDtypeStruct((nj, d, tf), BF16),
                      jax.ShapeDtypeStruct((f, d), BF16)]
    return pl.pallas_call(
        kern,
        grid=(ni, nj),
        in_specs=in_specs,
        out_specs=out_specs,
        out_shape=out_shape,
        scratch_shapes=scratch + [pltpu.SemaphoreType.DMA, pltpu.SemaphoreType.DMA],
        compiler_params=_cparams(("arbitrary", "arbitrary")),
        name="ffn_" + tail + ("_emit" if emit else "") + ("_side" if side is not None else ""),
    )(*args)


def _mm_epilogue(kind, accs, extras):
    if kind == "identity":
        return accs[0]
    if kind == "gelu":
        return _gelu(accs[0])
    if kind == "sigmoid":
        return _sigmoid(accs[0])
    if kind == "gated_glu":
        return extras[0].astype(F32) * (accs[0] * _sigmoid(accs[1]))
    if kind == "residual":
        return extras[0].astype(F32) + accs[0]
    raise ValueError(kind)


def _mm_kernel(*refs, n_w, n_extra, kind, emit):
    lhs_ref = refs[0]
    w_refs = refs[1:1 + n_w]
    extra_refs = refs[1 + n_w:1 + n_w + n_extra]
    out_ref = refs[1 + n_w + n_extra]
    emit_refs = refs[2 + n_w + n_extra:] if emit else [None] * n_w
    lhs = lhs_ref[...]
    accs = [jnp.dot(lhs, _as_bf16(w, e), preferred_element_type=F32) for w, e in zip(w_refs, emit_refs)]
    out_ref[...] = _mm_epilogue(kind, accs, [e[...] for e in extra_refs]).astype(out_ref.dtype)


def _mm(lhs, weights, extras, *, kind, n_out, out_dtype, tm, tn, emit, name):
    m, k = lhs.shape
    assert not emit or m == tm
    in_specs = [pl.BlockSpec((tm, k), lambda i, j: (i, 0))]
    args = [lhs]
    for w, c0 in weights:
        in_specs.append(_col_tile_spec(w, k, tn, c0))
        args.append(w)
    for e, c0 in extras:
        in_specs.append(pl.BlockSpec((tm, tn), lambda i, j, o=c0 // tn: (i, j + o)))
        args.append(e)
    out_specs = [pl.BlockSpec((tm, tn), lambda i, j: (i, j))]
    out_shape = [jax.ShapeDtypeStruct((m, n_out), out_dtype)]
    if emit:
        out_specs += [pl.BlockSpec((1, k, tn), lambda i, j: (j, 0, 0)) for _ in weights]
        out_shape += [jax.ShapeDtypeStruct((n_out // tn, k, tn), BF16) for _ in weights]
    kern = functools.partial(_mm_kernel, n_w=len(weights), n_extra=len(extras), kind=kind, emit=emit)
    res = pl.pallas_call(
        kern,
        grid=(m // tm, n_out // tn),
        in_specs=in_specs,
        out_specs=out_specs,
        out_shape=out_shape,
        compiler_params=_cparams(("arbitrary", "arbitrary")),
        name=name + ("_emit" if emit else ""),
    )(*args)
    return res[0], list(res[1:])


def _zoh(lam_re, lam_im, log_dt):
    dt = jnp.exp(log_dt)
    mag = jnp.exp(dt * lam_re)
    a_re = mag * jnp.cos(dt * lam_im)
    a_im = mag * jnp.sin(dt * lam_im)
    nr, ni = a_re - 1.0, a_im
    den = lam_re * lam_re + lam_im * lam_im
    coef_re = (nr * lam_re + ni * lam_im) / den
    coef_im = (ni * lam_re - nr * lam_im) / den
    return a_re, a_im, coef_re, coef_im


def _spread_lanes(x, reps):
    w = x.shape[1]
    assert w & (w - 1) == 0
    src = lax.broadcasted_iota(jnp.int32, (w, w * reps), 0)
    dst = lax.broadcasted_iota(jnp.int32, (w, w * reps), 1)
    sel = ((dst & (w - 1)) == src).astype(BF16)
    return jnp.dot(x.astype(BF16), sel, preferred_element_type=F32)


def _disc_kernel(lre_ref, lim_ref, ldt_ref, lre_col, lim_col, ldt_col, bre_ref, bim_ref, cre_ref, cim_ref,
                 are_ref, aim_ref, bw_ref, cw_ref):
    a_re, a_im, _, _ = _zoh(lre_ref[...], lim_ref[...], ldt_ref[...])
    are_ref[...] = a_re
    aim_ref[...] = a_im

    _, _, coef_re, coef_im = _zoh(lre_col[...], lim_col[...], ldt_col[...])
    b_re = bre_ref[...]
    b_im = bim_ref[...]
    bb_re = coef_re * b_re - coef_im * b_im
    bb_im = coef_re * b_im + coef_im * b_re
    state_shift = SSM_STATE.bit_length() - 1
    chan_shift = SSM_GROUP.bit_length() - 1
    row_g = lax.broadcasted_iota(jnp.int32, (SB, UB), 0) >> state_shift
    col_g = lax.broadcasted_iota(jnp.int32, (SB, UB), 1) >> chan_shift
    diag = row_g == col_g
    for part, bb in enumerate((bb_re, bb_im)):
        wt = jnp.where(diag, _spread_lanes(bb, GROUPS_PER_BLOCK), 0.0)
        bw_ref[0, :, part * SB:(part + 1) * SB] = wt.T.astype(BF16)

    row_g = lax.broadcasted_iota(jnp.int32, (UB, SB), 0) >> chan_shift
    col_g = lax.broadcasted_iota(jnp.int32, (UB, SB), 1) >> state_shift
    diag = row_g == col_g
    for part, c in enumerate((cre_ref[...], -cim_ref[...])):
        wt = jnp.where(diag, _spread_lanes(c, GROUPS_PER_BLOCK), 0.0)
        cw_ref[0, part * SB:(part + 1) * SB, :] = wt.T.astype(BF16)


def _discretise(lam_re, lam_im, log_dt, b_re, b_im, c_re, c_im):
    g, p = lam_re.shape
    h = b_re.shape[-1]
    nb = g // GROUPS_PER_BLOCK
    col = lambda x: x.reshape(g * p, 1)
    blk = lambda rows, cols: pl.BlockSpec((rows, cols), lambda i: (i, 0))
    return pl.pallas_call(
        _disc_kernel,
        grid=(nb,),
        in_specs=[blk(GROUPS_PER_BLOCK, p), blk(GROUPS_PER_BLOCK, p), blk(GROUPS_PER_BLOCK, 1),
                  blk(SB, 1), blk(SB, 1), blk(SB, 1),
                  blk(SB, h), blk(SB, h), blk(UB, p), blk(UB, p)],
        out_specs=(blk(GROUPS_PER_BLOCK, p), blk(GROUPS_PER_BLOCK, p),
                   pl.BlockSpec((1, UB, 2 * SB), lambda i: (i, 0, 0)),
                   pl.BlockSpec((1, 2 * SB, UB), lambda i: (i, 0, 0))),
        out_shape=(jax.ShapeDtypeStruct((g, p), F32), jax.ShapeDtypeStruct((g, p), F32),
                   jax.ShapeDtypeStruct((nb, UB, 2 * SB), BF16),
                   jax.ShapeDtypeStruct((nb, 2 * SB, UB), BF16)),
        compiler_params=_cparams(("arbitrary",)),
        name="s5_discretise",
    )(lam_re, lam_im, log_dt.reshape(g, 1),
      col(lam_re), col(lam_im), col(jnp.repeat(log_dt, p)),
      b_re.reshape(g * p, h), b_im.reshape(g * p, h), c_re.reshape(g * h, p), c_im.reshape(g * h, p))


def _ssm_prompt_kernel(*refs, nbatch):
    parity = pl.program_id(1) % 2
    for fill in (0, 1):
        pl.when(parity == fill)(functools.partial(_ssm_prompt_step, *refs, nbatch=nbatch, fill=fill))


def _ssm_prompt_step(un_ref, u_ref, bw_ref, cw_ref, are_ref, aim_ref, d_ref,
                     y_ref, sre_ref, sim_ref,
                     bure2, buim2, st_re, st_im, *, nbatch, fill):
    tc = pl.program_id(1)
    scan = 1 - fill

    @pl.when(tc == 0)
    def _():
        st_re[...] = jnp.zeros_like(st_re)
        st_im[...] = jnp.zeros_like(st_im)
        bure2[1] = jnp.zeros(bure2.shape[1:], F32)
        buim2[1] = jnp.zeros(buim2.shape[1:], F32)

    bure_f, buim_f = bure2.at[fill], buim2.at[fill]
    bure, buim = bure2.at[scan], buim2.at[scan]

    pack = 8 // nbatch
    nq = SLABS // pack

    def slab_rows(c, b):
        part, c = (0, c) if c < SLABS else (1, c - SLABS)
        v = (c // nq) * nbatch + b
        r0 = v * SCAN_PITCH + (v % 2) * SCAN_SKEW
        return part, c % nq, slice(r0, r0 + SCAN_T)

    for ref in (bure_f, buim_f):
        for q in range(nq):
            for v in range(0, 8, 2):
                ref[q, v * SCAN_PITCH + SCAN_T:v * SCAN_PITCH + SCAN_T + 2 * SCAN_SKEW, :] = (
                    jnp.zeros((2 * SCAN_SKEW, LANES), F32))

    u_bf = un_ref[...].reshape(nbatch * SCAN_T, UB).astype(BF16)
    per_dot = 4
    for n in range(2 * SLABS // per_dot):
        bu = jnp.dot(u_bf, bw_ref[0, :, per_dot * n * LANES:per_dot * (n + 1) * LANES],
                     preferred_element_type=F32)
        for h in range(per_dot):
            for b in range(nbatch):
                part, q, rows = slab_rows(per_dot * n + h, b)
                (bure_f, buim_f)[part][q, rows, :] = bu[b * SCAN_T:(b + 1) * SCAN_T, h * LANES:(h + 1) * LANES]

    def packed(a_ref, q):
        return jnp.concatenate(
            [jnp.broadcast_to(a_ref[0, q + h * nq], (nbatch, LANES)) for h in range(pack)], axis=0)

    a_re = [packed(are_ref, q) for q in range(nq)]
    a_im = [packed(aim_ref, q) for q in range(nq)]
    init = tuple(st_re[q] for q in range(nq)) + tuple(st_im[q] for q in range(nq))

    def step(t, carry, active=None):
        rows = pl.ds(t, pack * nbatch, stride=SCAN_PITCH)
        new_re, new_im = [], []
        for q in range(nq):
            s_r, s_i = carry[q], carry[nq + q]
            b_r, b_i = bure[q, rows, :], buim[q, rows, :]
            n_r = a_re[q] * s_r - a_im[q] * s_i + b_r
            n_i = a_re[q] * s_i + a_im[q] * s_r + b_i
            if active is not None:
                b_r, b_i = jnp.where(active, n_r, b_r), jnp.where(active, n_i, b_i)
                n_r, n_i = jnp.where(active, n_r, s_r), jnp.where(active, n_i, s_i)
                bure[q, rows, :] = b_r
                buim[q, rows, :] = b_i
            else:
                bure[q, rows, :] = n_r
                buim[q, rows, :] = n_i
            new_re.append(n_r)
            new_im.append(n_i)
        return tuple(new_re) + tuple(new_im)

    odd = (lax.broadcasted_iota(jnp.int32, (8, LANES), 0) & 1) == 1
    carry = init
    for t in range(SCAN_SKEW):
        carry = step(t, carry, active=jnp.logical_not(odd))
    for t in range(SCAN_SKEW, SCAN_T):
        carry = step(t, carry)
    for t in range(SCAN_T, SCAN_T + SCAN_SKEW):
        carry = step(t, carry, active=odd)
    fin = carry
    for q in range(nq):
        st_re[q] = fin[q]
        st_im[q] = fin[nq + q]

    d = d_ref[0]
    for b in range(nbatch):
        y = d * u_ref[b]
        for n in range(SLABS):
            (p0, q0, rows0), (p1, q1, rows1) = slab_rows(2 * n, b), slab_rows(2 * n + 1, b)
            s_pair = jnp.concatenate([(bure, buim)[p0][q0, rows0, :], (bure, buim)[p1][q1, rows1, :]],
                                     axis=1).astype(BF16)
            y = y + jnp.dot(s_pair, cw_ref[0, 2 * n * LANES:(2 * n + 2) * LANES, :],
                            preferred_element_type=F32)
        y_ref[b] = _gelu(y).astype(y_ref.dtype)

    @pl.when(tc == pl.num_programs(1) - 1)
    def _():
        for c in range(SLABS):
            q, h = c % nq, c // nq
            sre_ref[:, c * LANES:(c + 1) * LANES] = st_re[q, h * nbatch:(h + 1) * nbatch, :]
            sim_ref[:, c * LANES:(c + 1) * LANES] = st_im[q, h * nbatch:(h + 1) * nbatch, :]


def _ssm_prompt(z3, bw, cw, a_re, a_im, d_skip):
    nbatch, seq, width = z3.shape
    nb = bw.shape[0]
    states = nb * SB
    kern = functools.partial(_ssm_prompt_kernel, nbatch=nbatch)
    nt = seq // SCAN_T
    assert 8 % nbatch == 0
    return pl.pallas_call(
        kern,
        grid=(nb, nt + 1),
        in_specs=[
            pl.BlockSpec((nbatch, SCAN_T, UB), lambda g, t: (0, jnp.minimum(t, nt - 1), g)),
            pl.BlockSpec((nbatch, SCAN_T, UB), lambda g, t: (0, jnp.maximum(t - 1, 0), g)),
            pl.BlockSpec((1, UB, 2 * SB), lambda g, t: (g, 0, 0)),
            pl.BlockSpec((1, 2 * SB, UB), lambda g, t: (g, 0, 0)),
            pl.BlockSpec((1, SLABS, 1, LANES), lambda g, t: (g, 0, 0, 0)),
            pl.BlockSpec((1, SLABS, 1, LANES), lambda g, t: (g, 0, 0, 0)),
            pl.BlockSpec((1, 1, UB), lambda g, t: (g, 0, 0)),
        ],
        out_specs=(
            pl.BlockSpec((nbatch, SCAN_T, UB), lambda g, t: (0, jnp.maximum(t - 1, 0), g)),
            pl.BlockSpec((nbatch, SB), lambda g, t: (0, g)),
            pl.BlockSpec((nbatch, SB), lambda g, t: (0, g)),
        ),
        out_shape=(
            jax.ShapeDtypeStruct((nbatch, seq, width), BF16),
            jax.ShapeDtypeStruct((nbatch, states), F32),
            jax.ShapeDtypeStruct((nbatch, states), F32),
        ),
        scratch_shapes=[
            pltpu.VMEM((2, SLABS * nbatch // 8, 8 * SCAN_PITCH, LANES), F32),
            pltpu.VMEM((2, SLABS * nbatch // 8, 8 * SCAN_PITCH, LANES), F32),
            pltpu.VMEM((SLABS * nbatch // 8, 8, LANES), F32),
            pltpu.VMEM((SLABS * nbatch // 8, 8, LANES), F32),
        ],
        compiler_params=_cparams(("arbitrary", "arbitrary")),
        name="ssm_prompt",
    )(z3, z3, bw, cw, a_re.reshape(nb, SLABS, 1, LANES), a_im.reshape(nb, SLABS, 1, LANES),
      d_skip.reshape(nb, 1, UB))


def _ssm_sample_kernel(u_ref, bw_ref, cw_ref, are_ref, aim_ref, d_ref, x0re_ref, x0im_ref,
                       y_ref, sre_ref, sim_ref):
    u = u_ref[...]
    bu = jnp.dot(u.astype(BF16), bw_ref[0], preferred_element_type=F32)
    a_re = are_ref[0]
    a_im = aim_ref[0]
    x_re = x0re_ref[...]
    x_im = x0im_ref[...]
    s_re = a_re * x_re - a_im * x_im + bu[:, :SB]
    s_im = a_re * x_im + a_im * x_re + bu[:, SB:]
    sre_ref[...] = s_re
    sim_ref[...] = s_im
    s_cat = jnp.concatenate([s_re, s_im], axis=1).astype(BF16)
    y = jnp.dot(s_cat, cw_ref[0], preferred_element_type=F32) + d_ref[0] * u
    y_ref[...] = _gelu(y).astype(y_ref.dtype)


def _ssm_sample(z, bw, cw, a_re, a_im, d_skip, x0_re, x0_im):
    nbatch = z.shape[0]
    nb = bw.shape[0]
    return pl.pallas_call(
        _ssm_sample_kernel,
        grid=(nb,),
        in_specs=[
            pl.BlockSpec((nbatch, UB), lambda g: (0, g)),
            pl.BlockSpec((1, UB, 2 * SB), lambda g: (g, 0, 0)),
            pl.BlockSpec((1, 2 * SB, UB), lambda g: (g, 0, 0)),
            pl.BlockSpec((1, 1, SB), lambda g: (g, 0, 0)),
            pl.BlockSpec((1, 1, SB), lambda g: (g, 0, 0)),
            pl.BlockSpec((1, 1, UB), lambda g: (g, 0, 0)),
            pl.BlockSpec((nbatch, SB), lambda g: (0, g)),
            pl.BlockSpec((nbatch, SB), lambda g: (0, g)),
        ],
        out_specs=(
            pl.BlockSpec((nbatch, UB), lambda g: (0, g)),
            pl.BlockSpec((nbatch, SB), lambda g: (0, g)),
            pl.BlockSpec((nbatch, SB), lambda g: (0, g)),
        ),
        out_shape=(
            jax.ShapeDtypeStruct((nbatch, nb * UB), BF16),
            jax.ShapeDtypeStruct((nbatch, nb * SB), F32),
            jax.ShapeDtypeStruct((nbatch, nb * SB), F32),
        ),
        compiler_params=_cparams(("arbitrary",)),
        name="ssm_sample",
    )(z, bw, cw, a_re.reshape(nb, 1, SB), a_im.reshape(nb, 1, SB), d_skip.reshape(nb, 1, UB),
      x0_re, x0_im)


def _gmlp_prompt_kernel(gu_ref, gv_ref, gnv_ref, ws_ref, bias_ref, wo_ref, gate_ref, p_ref,
                        out_ref, s_scr, wt_scr, *, tm, heads):
    i = pl.program_id(0)
    j = pl.program_id(1)

    @pl.when(jnp.logical_and(i == 0, j == 0))
    def _():
        row = lax.broadcasted_iota(jnp.int32, (CHUNK, CHUNK), 0)
        col = lax.broadcasted_iota(jnp.int32, (CHUNK, CHUNK), 1)
        mask = (col <= row).astype(F32)
        for g in range(heads):
            wt_scr[g] = (ws_ref[g] * mask).astype(BF16)

    @pl.when(j == 0)
    def _():
        gnv = gnv_ref[...]

        def body(c, carry):
            r0 = pl.multiple_of(c * CHUNK, CHUNK)
            gv = gv_ref[pl.ds(r0, CHUNK), :].astype(F32)
            r = lax.rsqrt(jnp.mean(gv * gv, axis=-1, keepdims=True) + EPS)
            v = (gv * r * gnv).astype(BF16)
            for g in range(heads):
                sl = slice(g * GMLP_HEAD, (g + 1) * GMLP_HEAD)
                mixed = jnp.dot(wt_scr[g], v[:, sl], preferred_element_type=F32) + bias_ref[:, sl]
                gu = gu_ref[pl.ds(r0, CHUNK), sl].astype(F32)
                s_scr[pl.ds(r0, CHUNK), sl] = (gu * mixed).astype(BF16)
            return carry

        lax.fori_loop(0, tm // CHUNK, body, 0)

    yb = jnp.dot(s_scr[...], wo_ref[0], preferred_element_type=F32)
    out_ref[...] = (p_ref[...].astype(F32) + gate_ref[...].astype(F32) * yb).astype(out_ref.dtype)


def _gmlp_prompt(guv, gnv, w_s, bias_full, w_gout, gates, p, *, gate_col, tm, tn):
    m = guv.shape[0]
    ntiles, width, _ = w_gout.shape
    d = ntiles * tn
    heads = w_s.shape[0]
    kern = functools.partial(_gmlp_prompt_kernel, tm=tm, heads=heads)
    return pl.pallas_call(
        kern,
        grid=(m // tm, d // tn),
        in_specs=[
            pl.BlockSpec((tm, width), lambda i, j: (i, 0)),
            pl.BlockSpec((tm, width), lambda i, j: (i, 1)),
            pl.BlockSpec((1, width), lambda i, j: (0, 0)),
            pl.BlockSpec((heads, CHUNK, CHUNK), lambda i, j: (0, 0, 0)),
            pl.BlockSpec((CHUNK, width), lambda i, j: (0, 0)),
            _col_tile_spec(w_gout, width, tn),
            pl.BlockSpec((tm, tn), lambda i, j: (i, j + gate_col // tn)),
            pl.BlockSpec((tm, tn), lambda i, j: (i, j)),
        ],
        out_specs=pl.BlockSpec((tm, tn), lambda i, j: (i, j)),
        out_shape=jax.ShapeDtypeStruct((m, d), BF16),
        scratch_shapes=[pltpu.VMEM((tm, width), BF16), pltpu.VMEM((heads, CHUNK, CHUNK), BF16)],
        compiler_params=_cparams(("arbitrary", "arbitrary")),
        name="gmlp_prompt",
    )(guv, guv, gnv, w_s, bias_full, w_gout, gates, p)


def _gmlp_sample_kernel(gu_ref, gv_ref, gnv_ref, wdiag_ref, bias_ref, wo_ref, gate_ref, p_ref,
                        out_ref, v_ref, wemit_ref, s_scr):
    j = pl.program_id(0)

    @pl.when(j == 0)
    def _():
        gv = gv_ref[...]
        r = lax.rsqrt(jnp.mean(gv * gv, axis=-1, keepdims=True) + EPS)
        v = gv * r * gnv_ref[...]
        v_ref[...] = v
        mixed = wdiag_ref[...] * v + bias_ref[...]
        s_scr[...] = (gu_ref[...] * mixed).astype(BF16)

    yb = jnp.dot(s_scr[...], _as_bf16(wo_ref, wemit_ref), preferred_element_type=F32)
    out_ref[...] = (p_ref[...] + gate_ref[...] * yb).astype(out_ref.dtype)


def _gmlp_sample(guv, gnv, wdiag, bias0, w_gout, gates, p, *, gate_col, tn):
    m = guv.shape[0]
    width, d = w_gout.shape
    return pl.pallas_call(
        _gmlp_sample_kernel,
        grid=(d // tn,),
        in_specs=[
            pl.BlockSpec((m, width), lambda j: (0, 0)),
            pl.BlockSpec((m, width), lambda j: (0, 1)),
            pl.BlockSpec((1, width), lambda j: (0, 0)),
            pl.BlockSpec((1, width), lambda j: (0, 0)),
            pl.BlockSpec((1, width), lambda j: (0, 0)),
            pl.BlockSpec((width, tn), lambda j: (0, j)),
            pl.BlockSpec((m, tn), lambda j: (0, j + gate_col // tn)),
            pl.BlockSpec((m, tn), lambda j: (0, j)),
        ],
        out_specs=(pl.BlockSpec((m, tn), lambda j: (0, j)),
                   pl.BlockSpec((m, width), lambda j: (0, 0)),
                   pl.BlockSpec((1, width, tn), lambda j: (j, 0, 0))),
        out_shape=(jax.ShapeDtypeStruct((m, d), BF16), jax.ShapeDtypeStruct((m, width), F32),
                   jax.ShapeDtypeStruct((d // tn, width, tn), BF16)),
        scratch_shapes=[pltpu.VMEM((m, width), BF16)],
        compiler_params=_cparams(("arbitrary",)),
        name="gmlp_sample_emit",
    )(guv, guv, gnv, wdiag, bias0, w_gout, gates, p)


def _mixer(x, h2, prm, w, *, nbatch, x0, tm_mm):
    sample = x0 is not None
    d_model = x.shape[1]
    ssm_width = prm["bw"].shape[0] * UB
    gmlp_width = prm["gnv"].shape[1]
    o1 = ssm_width
    o3 = o1 + 2 * gmlp_width
    tn = 512
    wb = {}

    cols = {}
    if sample:
        w = dict(w, win_ssm=w["w_in"], win_uv=w["w_in"], win_gate=w["w_in"],
                 glu_a=w["w_glu"], glu_b=w["w_glu"])
        cols = dict(win_uv=o1, win_gate=o3, glu_b=d_model)

    def mm(lhs, wkeys, extras, kind, n_out, out_dtype, name):
        out, emitted = _mm(lhs, [(w[k], cols.get(k, 0)) for k in wkeys], extras, kind=kind,
                           n_out=n_out, out_dtype=out_dtype, tm=tm_mm, tn=2 * tn, emit=sample, name=name)
        wb.update(zip(wkeys, emitted))
        return out

    act = F32 if sample else BF16
    z_ssm = mm(h2, ["win_ssm"], [], "identity", o1, F32, "w_in_ssm")
    guv = mm(h2, ["win_uv"], [], "gelu", 2 * gmlp_width, act, "w_in_uv")
    gates = mm(h2, ["win_gate"], [], "sigmoid", 2 * d_model, act, "w_in_gate")

    if sample:
        y, s_re, s_im = _ssm_sample(z_ssm, prm["bw"], prm["cw"], prm["a_re"], prm["a_im"],
                                    prm["d_skip"], x0[0], x0[1])
    else:
        seq = x.shape[0] // nbatch
        y, s_re, s_im = _ssm_prompt(z_ssm.reshape(nbatch, seq, ssm_width), prm["bw"], prm["cw"],
                                    prm["a_re"], prm["a_im"], prm["d_skip"])
        y = y.reshape(nbatch * seq, ssm_width)
    p = mm(y, ["glu_a", "glu_b"], [(gates, 0)], "gated_glu", d_model, act, "ssm_glu")

    if sample:
        merged, v_rows, wb["w_gout"] = _gmlp_sample(guv, prm["gnv"], prm["w_diag"], prm["bias0"],
                                                    w["w_gout"], gates, p, gate_col=d_model, tn=2 * tn)
    else:
        merged = _gmlp_prompt(guv, prm["gnv"], prm["w_s"], prm["bias_full"], w["w_gout"], gates, p,
                              gate_col=d_model, tm=tm_mm, tn=2 * tn)
        v_rows = None
    x = mm(merged, ["w_out"], [(x, 0)], "residual", d_model, F32, "out_proj")
    return x, s_re, s_im, v_rows, wb


def kernel(x_prompt, x_sample, state_ssm_re, state_ssm_im, norm_ffn1, ffn1_gate, ffn1_up, ffn1_down, norm_mix, w_in, ssm_lambda_re, ssm_lambda_im, ssm_log_dt, ssm_b_re, ssm_b_im, ssm_c_re, ssm_c_im, ssm_d, ssm_w_glu, gmlp_norm_v, gmlp_w_s, gmlp_b_s, gmlp_w_out, w_out, norm_ffn2, ffn2_gate, ffn2_up, ffn2_down, norm_final):
    depth = w_in.shape[0]
    assert depth == 1, "the final RMSNorm is fused into the last FFN; one layer per step"
    batch, seq, d_model = x_prompt.shape
    dec_batch, dec_seq, _ = x_sample.shape
    assert dec_seq == 1
    groups, states = ssm_lambda_re.shape[1:]
    gmlp_width = gmlp_norm_v.shape[1]
    heads = gmlp_w_s.shape[1]
    head_dim = gmlp_width // heads
    assert seq % SCAN_T == 0 and seq % CHUNK == 0 and head_dim == GMLP_HEAD
    assert ssm_b_re.shape[-1] == SSM_GROUP and states == SSM_STATE and groups % GROUPS_PER_BLOCK == 0

    l = 0
    yp = x_prompt.reshape(batch * seq, d_model)
    ys = x_sample.reshape(dec_batch, d_model)
    gfin = norm_final.reshape(1, d_model)
    a_re, a_im, bw, cw = _discretise(ssm_lambda_re[l], ssm_lambda_im[l], ssm_log_dt[l],
                                     ssm_b_re[l], ssm_b_im[l], ssm_c_re[l], ssm_c_im[l])
    prm = dict(
        n1=norm_ffn1[l].reshape(1, d_model), nmix=norm_mix[l].reshape(1, d_model),
        n2=norm_ffn2[l].reshape(1, d_model),
        bw=bw, cw=cw, a_re=a_re, a_im=a_im, d_skip=ssm_d[l],
        gnv=gmlp_norm_v[l].reshape(1, gmlp_width), w_s=gmlp_w_s[l],
        bias_full=jnp.repeat(gmlp_b_s[l].T, head_dim, axis=1),
        w_diag=jnp.repeat(gmlp_w_s[l][:, 0, 0], head_dim).reshape(1, gmlp_width),
        bias0=jnp.repeat(gmlp_b_s[l][:, 0], head_dim).reshape(1, gmlp_width),
    )
    w_f32 = dict(w_in=w_in[l], w_glu=ssm_w_glu[l], w_gout=gmlp_w_out[l], w_out=w_out[l])
    x0 = (state_ssm_re[l].reshape(dec_batch, groups * states),
          state_ssm_im[l].reshape(dec_batch, groups * states))
    tm, tf = 1024, 256

    ys, hs, *f1 = _ffn(ys, prm["n1"], ffn1_gate[l], ffn1_up[l], ffn1_down[l], prm["nmix"],
                       tail="norm_out", tm=dec_batch, tf=tf, emit=True)
    yp, hp, *f2 = _ffn(yp, prm["n1"], *f1, prm["nmix"], tail="norm_out", tm=tm, tf=tf,
                       side=(ffn2_gate[l], ffn2_up[l], ffn2_down[l]), single_out=True)
    ys, sr, si, vr, w_bf16 = _mixer(ys, hs, prm, w_f32, nbatch=dec_batch, x0=x0, tm_mm=dec_batch)
    yp, pr, pi, _, _ = _mixer(yp, hp, prm, w_bf16, nbatch=batch, x0=None, tm_mm=tm)
    n_tiles = yp.shape[0] // tm
    yp, ys = _ffn(yp, prm["n2"], *f2, gfin, tail="final", tm=(yp.shape[0] + dec_batch) // n_tiles, tf=tf,
                  x_tail=ys, single_out=True)
    return (yp.reshape(batch, seq, d_model), ys.reshape(dec_batch, dec_seq, d_model),
            pr.reshape(1, batch, groups, states), pi.reshape(1, batch, groups, states),
            sr.reshape(1, dec_batch, groups, states), si.reshape(1, dec_batch, groups, states),
            vr.reshape(1, dec_batch, dec_seq, gmlp_width))
```

```python
import functools
import math

import jax
import jax.numpy as jnp
from jax import lax
from jax.experimental import pallas as pl
from jax.experimental.pallas import tpu as pltpu

F32 = jnp.float32
BF16 = jnp.bfloat16

EPS = 1e-6
LANES = 128
SSM_GROUP = 16
SSM_STATE = 64
GROUPS_PER_BLOCK = 16
UB = GROUPS_PER_BLOCK * SSM_GROUP
SB = GROUPS_PER_BLOCK * SSM_STATE
SLABS = SB // LANES
SCAN_T = 256
SCAN_SKEW = 4
SCAN_PITCH = SCAN_T + SCAN_SKEW
CHUNK = 128
GMLP_HEAD = 128
VMEM_LIMIT = 56 * 1024 * 1024


def _cparams(sem):
    return pltpu.CompilerParams(dimension_semantics=sem, vmem_limit_bytes=VMEM_LIMIT)


def _gelu(x):
    c = math.sqrt(2.0 / math.pi)
    return 0.5 * x * (1.0 + jnp.tanh(c * (x + 0.044715 * (x * x * x))))


def _sigmoid(x):
    return 1.0 / (1.0 + jnp.exp(-x))


def _as_bf16(w_ref, emit_ref):
    w = w_ref[0] if len(w_ref.shape) == 3 else w_ref[...]
    if w.dtype != BF16:
        w = w.astype(BF16)
    if emit_ref is not None:
        if len(emit_ref.shape) == 3:
            emit_ref[0] = w
        else:
            emit_ref[...] = w
    return w


def _col_tile_spec(w, k, tn, col0=0):
    if w.ndim == 3:
        assert w.shape[1:] == (k, tn) and col0 == 0
        return pl.BlockSpec((1, k, tn), lambda i, j: (j, 0, 0))
    return pl.BlockSpec((k, tn), lambda i, j, o=col0 // tn: (0, j + o))


def _ffn_kernel(*refs, tm, d_model, col_chunk, row_chunk, tail, tail_rows):
    refs = list(refs)
    x_hbm = refs.pop(0)
    xt_hbm = refs.pop(0) if tail_rows else None
    gn_ref, wg_ref, wu_ref, wd_ref, gt_ref = refs[:5]
    del refs[:5]
    out_ref = refs.pop(0)
    outt_ref = refs.pop(0) if tail_rows else None
    h2_ref = refs.pop(0) if tail == "norm_out" else None
    h2t_ref = refs.pop(0) if tail == "norm_out" and tail_rows else None
    h_buf = h2_ref if h2_ref is not None else refs.pop(0)
    x_sem, xt_sem = refs
    i = pl.program_id(0)
    j = pl.program_id(1)
    last = pl.num_programs(0) - 1
    head_rows = tm - tail_rows

    def copy_in(src, dst, sem):
        cp = pltpu.make_async_copy(src, dst, sem)
        cp.start()
        return cp

    @pl.when(j == 0)
    def _():
        def whole_tile():
            copy_in(x_hbm.at[pl.ds(pl.multiple_of(i * tm, tm), tm), :], out_ref, x_sem).wait()

        if tail_rows:
            pl.when(i < last)(whole_tile)

            @pl.when(i == last)
            def _():
                n_x = x_hbm.shape[0]
                head = copy_in(x_hbm.at[pl.ds(n_x - head_rows, head_rows), :],
                               out_ref.at[pl.ds(0, head_rows), :], x_sem)
                rest = copy_in(xt_hbm, out_ref.at[pl.ds(head_rows, tail_rows), :], xt_sem)
                head.wait()
                rest.wait()
        else:
            whole_tile()
        g = gn_ref[...]

        def body(c, carry):
            r0 = pl.multiple_of(c * row_chunk, row_chunk)
            xf = out_ref[pl.ds(r0, row_chunk), :]
            r = lax.rsqrt(jnp.mean(xf * xf, axis=-1, keepdims=True) + EPS)
            h_buf[pl.ds(r0, row_chunk), :] = (xf * r * g).astype(BF16)
            out_ref[pl.ds(r0, row_chunk), :] = 2.0 * xf
            return carry

        lax.fori_loop(0, tm // row_chunk, body, 0)

    h = h_buf[...]
    g = jnp.dot(h, _as_bf16(wg_ref, None), preferred_element_type=F32)
    u = jnp.dot(h, _as_bf16(wu_ref, None), preferred_element_type=F32)
    a = (g * _sigmoid(g) * u).astype(BF16)
    for n in range(d_model // col_chunk):
        sl = slice(n * col_chunk, (n + 1) * col_chunk)
        out_ref[:, sl] += jnp.dot(a, wd_ref[:, sl].astype(BF16), preferred_element_type=F32)

    @pl.when(j == pl.num_programs(1) - 1)
    def _():
        gt = gt_ref[...]

        def body(c, carry):
            r0 = pl.multiple_of(c * row_chunk, row_chunk)
            y = 0.5 * out_ref[pl.ds(r0, row_chunk), :]
            r = lax.rsqrt(jnp.mean(y * y, axis=-1, keepdims=True) + EPS)
            if tail == "norm_out":
                out_ref[pl.ds(r0, row_chunk), :] = y
                h2_ref[pl.ds(r0, row_chunk), :] = (y * r * gt).astype(BF16)
            else:
                out_ref[pl.ds(r0, row_chunk), :] = y * r * gt
            return carry

        lax.fori_loop(0, tm // row_chunk, body, 0)
        if tail_rows:
            @pl.when(i == last)
            def _():
                outt_ref[...] = out_ref[pl.ds(head_rows, tail_rows), :]
                if h2t_ref is not None:
                    h2t_ref[...] = h2_ref[pl.ds(head_rows, tail_rows), :]


def _ffn(x, x_tail, gn, wg, wu, wd, gtail, *, tail, row_tiles, tf):
    n_x, d = x.shape
    f = wd.shape[0]
    tail_rows = x_tail.shape[0]
    tm = (n_x + tail_rows) // row_tiles
    ni, nj = row_tiles, f // tf
    assert ni * tm == n_x + tail_rows and nj * tf == f
    assert 0 < tail_rows < tm and (tm - tail_rows) % 16 == 0
    row_chunk = max(c for c in range(16, 81, 16) if tm % c == 0)
    kern = functools.partial(_ffn_kernel, tm=tm, d_model=d, col_chunk=512, row_chunk=row_chunk,
                             tail=tail, tail_rows=tail_rows)
    once = pl.Buffered(1)
    in_specs = [
        pl.BlockSpec(memory_space=pl.ANY),
        pl.BlockSpec(memory_space=pl.ANY),
        pl.BlockSpec((1, d), lambda i, j: (0, 0)),
        pl.BlockSpec((d, tf), lambda i, j: (0, j)),
        pl.BlockSpec((d, tf), lambda i, j: (0, j)),
        pl.BlockSpec((tf, d), lambda i, j: (j, 0)),
        pl.BlockSpec((1, d), lambda i, j: (0, 0)),
    ]
    args = [x, x_tail, gn, wg, wu, wd, gtail]
    out_specs = [pl.BlockSpec((tm, d), lambda i, j: (i, 0), pipeline_mode=once),
                 pl.BlockSpec((tail_rows, d), lambda i, j: (0, 0), pipeline_mode=once)]
    out_shape = [jax.ShapeDtypeStruct((n_x, d), F32), jax.ShapeDtypeStruct((tail_rows, d), F32)]
    scratch = [pltpu.VMEM((tm, d), BF16)]
    if tail == "norm_out":
        out_specs += [pl.BlockSpec((tm, d), lambda i, j: (i, 0), pipeline_mode=once),
                      pl.BlockSpec((tail_rows, d), lambda i, j: (0, 0), pipeline_mode=once)]
        out_shape += [jax.ShapeDtypeStruct((n_x, d), BF16), jax.ShapeDtypeStruct((tail_rows, d), BF16)]
        scratch = []
    return pl.pallas_call(
        kern,
        grid=(ni, nj),
        in_specs=in_specs,
        out_specs=out_specs,
        out_shape=out_shape,
        scratch_shapes=scratch + [pltpu.SemaphoreType.DMA, pltpu.SemaphoreType.DMA],
        compiler_params=_cparams(("arbitrary", "arbitrary")),
        name="ffn_" + tail,
    )(*args)


def _mm_epilogue(kind, accs, extras):
    if kind == "identity":
        return accs[0]
    if kind == "gelu":
        return _gelu(accs[0])
    if kind == "sigmoid":
        return _sigmoid(accs[0])
    if kind == "gated_glu":
        return extras[0].astype(F32) * (accs[0] * _sigmoid(accs[1]))
    if kind == "residual":
        return extras[0].astype(F32) + accs[0]
    raise ValueError(kind)


def _mm_kernel(*refs, n_w, n_extra, kind, emit):
    lhs_ref = refs[0]
    w_refs = refs[1:1 + n_w]
    extra_refs = refs[1 + n_w:1 + n_w + n_extra]
    out_ref = refs[1 + n_w + n_extra]
    emit_refs = refs[2 + n_w + n_extra:] if emit else [None] * n_w
    lhs = lhs_ref[...]
    accs = [jnp.dot(lhs, _as_bf16(w, e), preferred_element_type=F32) for w, e in zip(w_refs, emit_refs)]
    out_ref[...] = _mm_epilogue(kind, accs, [e[...] for e in extra_refs]).astype(out_ref.dtype)


def _mm(lhs, weights, extras, *, kind, n_out, out_dtype, tm, tn, emit, name):
    m, k = lhs.shape
    assert not emit or m == tm
    in_specs = [pl.BlockSpec((tm, k), lambda i, j: (i, 0))]
    args = [lhs]
    for w, c0 in weights:
        in_specs.append(_col_tile_spec(w, k, tn, c0))
        args.append(w)
    for e, c0 in extras:
        in_specs.append(pl.BlockSpec((tm, tn), lambda i, j, o=c0 // tn: (i, j + o)))
        args.append(e)
    out_specs = [pl.BlockSpec((tm, tn), lambda i, j: (i, j))]
    out_shape = [jax.ShapeDtypeStruct((m, n_out), out_dtype)]
    if emit:
        out_specs += [pl.BlockSpec((1, k, tn), lambda i, j: (j, 0, 0)) for _ in weights]
        out_shape += [jax.ShapeDtypeStruct((n_out // tn, k, tn), BF16) for _ in weights]
    kern = functools.partial(_mm_kernel, n_w=len(weights), n_extra=len(extras), kind=kind, emit=emit)
    res = pl.pallas_call(
        kern,
        grid=(m // tm, n_out // tn),
        in_specs=in_specs,
        out_specs=out_specs,
        out_shape=out_shape,
        compiler_params=_cparams(("arbitrary", "arbitrary")),
        name=name + ("_emit" if emit else ""),
    )(*args)
    return res[0], list(res[1:])


def _zoh(lam_re, lam_im, log_dt):
    dt = jnp.exp(log_dt)
    mag = jnp.exp(dt * lam_re)
    a_re = mag * jnp.cos(dt * lam_im)
    a_im = mag * jnp.sin(dt * lam_im)
    nr, ni = a_re - 1.0, a_im
    den = lam_re * lam_re + lam_im * lam_im
    coef_re = (nr * lam_re + ni * lam_im) / den
    coef_im = (ni * lam_re - nr * lam_im) / den
    return a_re, a_im, coef_re, coef_im


def _spread_lanes(x, reps):
    w = x.shape[1]
    assert w & (w - 1) == 0
    src = lax.broadcasted_iota(jnp.int32, (w, w * reps), 0)
    dst = lax.broadcasted_iota(jnp.int32, (w, w * reps), 1)
    sel = ((dst & (w - 1)) == src).astype(BF16)
    return jnp.dot(x.astype(BF16), sel, preferred_element_type=F32)


def _disc_kernel(lre_ref, lim_ref, ldt_ref, lre_col, lim_col, ldt_col, bre_ref, bim_ref, cre_ref, cim_ref,
                 are_ref, aim_ref, bw_ref, cw_ref):
    a_re, a_im, _, _ = _zoh(lre_ref[...], lim_ref[...], ldt_ref[...])
    are_ref[...] = a_re
    aim_ref[...] = a_im

    _, _, coef_re, coef_im = _zoh(lre_col[...], lim_col[...], ldt_col[...])
    b_re = bre_ref[...]
    b_im = bim_ref[...]
    bb_re = coef_re * b_re - coef_im * b_im
    bb_im = coef_re * b_im + coef_im * b_re
    state_shift = SSM_STATE.bit_length() - 1
    chan_shift = SSM_GROUP.bit_length() - 1
    row_g = lax.broadcasted_iota(jnp.int32, (SB, UB), 0) >> state_shift
    col_g = lax.broadcasted_iota(jnp.int32, (SB, UB), 1) >> chan_shift
    diag = row_g == col_g
    for part, bb in enumerate((bb_re, bb_im)):
        wt = jnp.where(diag, _spread_lanes(bb, GROUPS_PER_BLOCK), 0.0)
        bw_ref[0, :, part * SB:(part + 1) * SB] = wt.T.astype(BF16)

    row_g = lax.broadcasted_iota(jnp.int32, (UB, SB), 0) >> chan_shift
    col_g = lax.broadcasted_iota(jnp.int32, (UB, SB), 1) >> state_shift
    diag = row_g == col_g
    for part, c in enumerate((cre_ref[...], -cim_ref[...])):
        wt = jnp.where(diag, _spread_lanes(c, GROUPS_PER_BLOCK), 0.0)
        cw_ref[0, part * SB:(part + 1) * SB, :] = wt.T.astype(BF16)


def _discretise(lam_re, lam_im, log_dt, b_re, b_im, c_re, c_im):
    g, p = lam_re.shape
    h = b_re.shape[-1]
    nb = g // GROUPS_PER_BLOCK
    col = lambda x: x.reshape(g * p, 1)
    blk = lambda rows, cols: pl.BlockSpec((rows, cols), lambda i: (i, 0))
    return pl.pallas_call(
        _disc_kernel,
        grid=(nb,),
        in_specs=[blk(GROUPS_PER_BLOCK, p), blk(GROUPS_PER_BLOCK, p), blk(GROUPS_PER_BLOCK, 1),
                  blk(SB, 1), blk(SB, 1), blk(SB, 1),
                  blk(SB, h), blk(SB, h), blk(UB, p), blk(UB, p)],
        out_specs=(blk(GROUPS_PER_BLOCK, p), blk(GROUPS_PER_BLOCK, p),
                   pl.BlockSpec((1, UB, 2 * SB), lambda i: (i, 0, 0)),
                   pl.BlockSpec((1, 2 * SB, UB), lambda i: (i, 0, 0))),
        out_shape=(jax.ShapeDtypeStruct((g, p), F32), jax.ShapeDtypeStruct((g, p), F32),
                   jax.ShapeDtypeStruct((nb, UB, 2 * SB), BF16),
                   jax.ShapeDtypeStruct((nb, 2 * SB, UB), BF16)),
        compiler_params=_cparams(("arbitrary",)),
        name="s5_discretise",
    )(lam_re, lam_im, log_dt.reshape(g, 1),
      col(lam_re), col(lam_im), col(jnp.repeat(log_dt, p)),
      b_re.reshape(g * p, h), b_im.reshape(g * p, h), c_re.reshape(g * h, p), c_im.reshape(g * h, p))


def _ssm_prompt_kernel(*refs, nbatch):
    parity = pl.program_id(1) % 2
    for fill in (0, 1):
        pl.when(parity == fill)(functools.partial(_ssm_prompt_step, *refs, nbatch=nbatch, fill=fill))


def _ssm_prompt_step(un_ref, u_ref, bw_ref, cw_ref, are_ref, aim_ref, d_ref,
                     y_ref, sre_ref, sim_ref,
                     bure2, buim2, st_re, st_im, *, nbatch, fill):
    tc = pl.program_id(1)
    scan = 1 - fill

    @pl.when(tc == 0)
    def _():
        st_re[...] = jnp.zeros_like(st_re)
        st_im[...] = jnp.zeros_like(st_im)
        bure2[1] = jnp.zeros(bure2.shape[1:], F32)
        buim2[1] = jnp.zeros(buim2.shape[1:], F32)

    bure_f, buim_f = bure2.at[fill], buim2.at[fill]
    bure, buim = bure2.at[scan], buim2.at[scan]

    pack = 8 // nbatch
    nq = SLABS // pack

    def slab_rows(c, b):
        part, c = (0, c) if c < SLABS else (1, c - SLABS)
        v = (c // nq) * nbatch + b
        r0 = v * SCAN_PITCH + (v % 2) * SCAN_SKEW
        return part, c % nq, slice(r0, r0 + SCAN_T)

    for ref in (bure_f, buim_f):
        for q in range(nq):
            for v in range(0, 8, 2):
                ref[q, v * SCAN_PITCH + SCAN_T:v * SCAN_PITCH + SCAN_T + 2 * SCAN_SKEW, :] = (
                    jnp.zeros((2 * SCAN_SKEW, LANES), F32))

    u_bf = un_ref[...].reshape(nbatch * SCAN_T, UB).astype(BF16)
    per_dot = 4
    for n in range(2 * SLABS // per_dot):
        bu = jnp.dot(u_bf, bw_ref[0, :, per_dot * n * LANES:per_dot * (n + 1) * LANES],
                     preferred_element_type=F32)
        for h in range(per_dot):
            for b in range(nbatch):
                part, q, rows = slab_rows(per_dot * n + h, b)
                (bure_f, buim_f)[part][q, rows, :] = bu[b * SCAN_T:(b + 1) * SCAN_T, h * LANES:(h + 1) * LANES]

    def packed(a_ref, q):
        return jnp.concatenate(
            [jnp.broadcast_to(a_ref[0, q + h * nq], (nbatch, LANES)) for h in range(pack)], axis=0)

    a_re = [packed(are_ref, q) for q in range(nq)]
    a_im = [packed(aim_ref, q) for q in range(nq)]
    init = tuple(st_re[q] for q in range(nq)) + tuple(st_im[q] for q in range(nq))

    def step(t, carry, active=None):
        rows = pl.ds(t, pack * nbatch, stride=SCAN_PITCH)
        new_re, new_im = [], []
        for q in range(nq):
            s_r, s_i = carry[q], carry[nq + q]
            b_r, b_i = bure[q, rows, :], buim[q, rows, :]
            n_r = a_re[q] * s_r - a_im[q] * s_i + b_r
            n_i = a_re[q] * s_i + a_im[q] * s_r + b_i
            if active is not None:
                b_r, b_i = jnp.where(active, n_r, b_r), jnp.where(active, n_i, b_i)
                n_r, n_i = jnp.where(active, n_r, s_r), jnp.where(active, n_i, s_i)
                bure[q, rows, :] = b_r
                buim[q, rows, :] = b_i
            else:
                bure[q, rows, :] = n_r
                buim[q, rows, :] = n_i
            new_re.append(n_r)
            new_im.append(n_i)
        return tuple(new_re) + tuple(new_im)

    odd = (lax.broadcasted_iota(jnp.int32, (8, LANES), 0) & 1) == 1
    carry = init
    for t in range(SCAN_SKEW):
        carry = step(t, carry, active=jnp.logical_not(odd))
    for t in range(SCAN_SKEW, SCAN_T):
        carry = step(t, carry)
    for t in range(SCAN_T, SCAN_T + SCAN_SKEW):
        carry = step(t, carry, active=odd)
    fin = carry
    for q in range(nq):
        st_re[q] = fin[q]
        st_im[q] = fin[nq + q]

    d = d_ref[0]
    for b in range(nbatch):
        y = d * u_ref[b]
        for n in range(SLABS):
            (p0, q0, rows0), (p1, q1, rows1) = slab_rows(2 * n, b), slab_rows(2 * n + 1, b)
            s_pair = jnp.concatenate([(bure, buim)[p0][q0, rows0, :], (bure, buim)[p1][q1, rows1, :]],
                                     axis=1).astype(BF16)
            y = y + jnp.dot(s_pair, cw_ref[0, 2 * n * LANES:(2 * n + 2) * LANES, :],
                            preferred_element_type=F32)
        y_ref[b] = _gelu(y).astype(y_ref.dtype)

    @pl.when(tc == pl.num_programs(1) - 1)
    def _():
        for c in range(SLABS):
            q, h = c % nq, c // nq
            sre_ref[:, c * LANES:(c + 1) * LANES] = st_re[q, h * nbatch:(h + 1) * nbatch, :]
            sim_ref[:, c * LANES:(c + 1) * LANES] = st_im[q, h * nbatch:(h + 1) * nbatch, :]


def _ssm_prompt(z3, bw, cw, a_re, a_im, d_skip):
    nbatch, seq, width = z3.shape
    nb = bw.shape[0]
    states = nb * SB
    kern = functools.partial(_ssm_prompt_kernel, nbatch=nbatch)
    nt = seq // SCAN_T
    assert 8 % nbatch == 0
    return pl.pallas_call(
        kern,
        grid=(nb, nt + 1),
        in_specs=[
            pl.BlockSpec((nbatch, SCAN_T, UB), lambda g, t: (0, jnp.minimum(t, nt - 1), g)),
            pl.BlockSpec((nbatch, SCAN_T, UB), lambda g, t: (0, jnp.maximum(t - 1, 0), g)),
            pl.BlockSpec((1, UB, 2 * SB), lambda g, t: (g, 0, 0)),
            pl.BlockSpec((1, 2 * SB, UB), lambda g, t: (g, 0, 0)),
            pl.BlockSpec((1, SLABS, 1, LANES), lambda g, t: (g, 0, 0, 0)),
            pl.BlockSpec((1, SLABS, 1, LANES), lambda g, t: (g, 0, 0, 0)),
            pl.BlockSpec((1, 1, UB), lambda g, t: (g, 0, 0)),
        ],
        out_specs=(
            pl.BlockSpec((nbatch, SCAN_T, UB), lambda g, t: (0, jnp.maximum(t - 1, 0), g)),
            pl.BlockSpec((nbatch, SB), lambda g, t: (0, g)),
            pl.BlockSpec((nbatch, SB), lambda g, t: (0, g)),
        ),
        out_shape=(
            jax.ShapeDtypeStruct((nbatch, seq, width), BF16),
            jax.ShapeDtypeStruct((nbatch, states), F32),
            jax.ShapeDtypeStruct((nbatch, states), F32),
        ),
        scratch_shapes=[
            pltpu.VMEM((2, SLABS * nbatch // 8, 8 * SCAN_PITCH, LANES), F32),
            pltpu.VMEM((2, SLABS * nbatch // 8, 8 * SCAN_PITCH, LANES), F32),
            pltpu.VMEM((SLABS * nbatch // 8, 8, LANES), F32),
            pltpu.VMEM((SLABS * nbatch // 8, 8, LANES), F32),
        ],
        compiler_params=_cparams(("arbitrary", "arbitrary")),
        name="ssm_prompt",
    )(z3, z3, bw, cw, a_re.reshape(nb, SLABS, 1, LANES), a_im.reshape(nb, SLABS, 1, LANES),
      d_skip.reshape(nb, 1, UB))


def _ssm_sample_kernel(u_ref, bw_ref, cw_ref, are_ref, aim_ref, d_ref, x0re_ref, x0im_ref,
                       y_ref, sre_ref, sim_ref):
    u = u_ref[...]
    bu = jnp.dot(u.astype(BF16), bw_ref[0], preferred_element_type=F32)
    a_re = are_ref[0]
    a_im = aim_ref[0]
    x_re = x0re_ref[...]
    x_im = x0im_ref[...]
    s_re = a_re * x_re - a_im * x_im + bu[:, :SB]
    s_im = a_re * x_im + a_im * x_re + bu[:, SB:]
    sre_ref[...] = s_re
    sim_ref[...] = s_im
    s_cat = jnp.concatenate([s_re, s_im], axis=1).astype(BF16)
    y = jnp.dot(s_cat, cw_ref[0], preferred_element_type=F32) + d_ref[0] * u
    y_ref[...] = _gelu(y).astype(y_ref.dtype)


def _ssm_sample(z, bw, cw, a_re, a_im, d_skip, x0_re, x0_im):
    nbatch = z.shape[0]
    nb = bw.shape[0]
    return pl.pallas_call(
        _ssm_sample_kernel,
        grid=(nb,),
        in_specs=[
            pl.BlockSpec((nbatch, UB), lambda g: (0, g)),
            pl.BlockSpec((1, UB, 2 * SB), lambda g: (g, 0, 0)),
            pl.BlockSpec((1, 2 * SB, UB), lambda g: (g, 0, 0)),
            pl.BlockSpec((1, 1, SB), lambda g: (g, 0, 0)),
            pl.BlockSpec((1, 1, SB), lambda g: (g, 0, 0)),
            pl.BlockSpec((1, 1, UB), lambda g: (g, 0, 0)),
            pl.BlockSpec((nbatch, SB), lambda g: (0, g)),
            pl.BlockSpec((nbatch, SB), lambda g: (0, g)),
        ],
        out_specs=(
            pl.BlockSpec((nbatch, UB), lambda g: (0, g)),
            pl.BlockSpec((nbatch, SB), lambda g: (0, g)),
            pl.BlockSpec((nbatch, SB), lambda g: (0, g)),
        ),
        out_shape=(
            jax.ShapeDtypeStruct((nbatch, nb * UB), BF16),
            jax.ShapeDtypeStruct((nbatch, nb * SB), F32),
            jax.ShapeDtypeStruct((nbatch, nb * SB), F32),
        ),
        compiler_params=_cparams(("arbitrary",)),
        name="ssm_sample",
    )(z, bw, cw, a_re.reshape(nb, 1, SB), a_im.reshape(nb, 1, SB), d_skip.reshape(nb, 1, UB),
      x0_re, x0_im)


def _gmlp_prompt_kernel(gu_ref, gv_ref, gnv_ref, ws_ref, bias_ref, wo_ref, gate_ref, p_ref,
                        out_ref, s_scr, wt_scr, *, tm, heads):
    i = pl.program_id(0)
    j = pl.program_id(1)

    @pl.when(jnp.logical_and(i == 0, j == 0))
    def _():
        row = lax.broadcasted_iota(jnp.int32, (CHUNK, CHUNK), 0)
        col = lax.broadcasted_iota(jnp.int32, (CHUNK, CHUNK), 1)
        mask = (col <= row).astype(F32)
        for g in range(heads):
            wt_scr[g] = (ws_ref[g] * mask).astype(BF16)

    @pl.when(j == 0)
    def _():
        gnv = gnv_ref[...]

        def body(c, carry):
            r0 = pl.multiple_of(c * CHUNK, CHUNK)
            gv = gv_ref[pl.ds(r0, CHUNK), :].astype(F32)
            r = lax.rsqrt(jnp.mean(gv * gv, axis=-1, keepdims=True) + EPS)
            v = (gv * r * gnv).astype(BF16)
            for g in range(heads):
                sl = slice(g * GMLP_HEAD, (g + 1) * GMLP_HEAD)
                mixed = jnp.dot(wt_scr[g], v[:, sl], preferred_element_type=F32) + bias_ref[:, sl]
                gu = gu_ref[pl.ds(r0, CHUNK), sl].astype(F32)
                s_scr[pl.ds(r0, CHUNK), sl] = (gu * mixed).astype(BF16)
            return carry

        lax.fori_loop(0, tm // CHUNK, body, 0)

    yb = jnp.dot(s_scr[...], wo_ref[0], preferred_element_type=F32)
    out_ref[...] = (p_ref[...].astype(F32) + gate_ref[...].astype(F32) * yb).astype(out_ref.dtype)


def _gmlp_prompt(guv, gnv, w_s, bias_full, w_gout, gates, p, *, gate_col, tm, tn):
    m = guv.shape[0]
    ntiles, width, _ = w_gout.shape
    d = ntiles * tn
    heads = w_s.shape[0]
    kern = functools.partial(_gmlp_prompt_kernel, tm=tm, heads=heads)
    return pl.pallas_call(
        kern,
        grid=(m // tm, d // tn),
        in_specs=[
            pl.BlockSpec((tm, width), lambda i, j: (i, 0)),
            pl.BlockSpec((tm, width), lambda i, j: (i, 1)),
            pl.BlockSpec((1, width), lambda i, j: (0, 0)),
            pl.BlockSpec((heads, CHUNK, CHUNK), lambda i, j: (0, 0, 0)),
            pl.BlockSpec((CHUNK, width), lambda i, j: (0, 0)),
            _col_tile_spec(w_gout, width, tn),
            pl.BlockSpec((tm, tn), lambda i, j: (i, j + gate_col // tn)),
            pl.BlockSpec((tm, tn), lambda i, j: (i, j)),
        ],
        out_specs=pl.BlockSpec((tm, tn), lambda i, j: (i, j)),
        out_shape=jax.ShapeDtypeStruct((m, d), BF16),
        scratch_shapes=[pltpu.VMEM((tm, width), BF16), pltpu.VMEM((heads, CHUNK, CHUNK), BF16)],
        compiler_params=_cparams(("arbitrary", "arbitrary")),
        name="gmlp_prompt",
    )(guv, guv, gnv, w_s, bias_full, w_gout, gates, p)


def _gmlp_sample_kernel(gu_ref, gv_ref, gnv_ref, wdiag_ref, bias_ref, wo_ref, gate_ref, p_ref,
                        out_ref, v_ref, wemit_ref, s_scr):
    j = pl.program_id(0)

    @pl.when(j == 0)
    def _():
        gv = gv_ref[...]
        r = lax.rsqrt(jnp.mean(gv * gv, axis=-1, keepdims=True) + EPS)
        v = gv * r * gnv_ref[...]
        v_ref[...] = v
        mixed = wdiag_ref[...] * v + bias_ref[...]
        s_scr[...] = (gu_ref[...] * mixed).astype(BF16)

    yb = jnp.dot(s_scr[...], _as_bf16(wo_ref, wemit_ref), preferred_element_type=F32)
    out_ref[...] = (p_ref[...] + gate_ref[...] * yb).astype(out_ref.dtype)


def _gmlp_sample(guv, gnv, wdiag, bias0, w_gout, gates, p, *, gate_col, tn):
    m = guv.shape[0]
    width, d = w_gout.shape
    return pl.pallas_call(
        _gmlp_sample_kernel,
        grid=(d // tn,),
        in_specs=[
            pl.BlockSpec((m, width), lambda j: (0, 0)),
            pl.BlockSpec((m, width), lambda j: (0, 1)),
            pl.BlockSpec((1, width), lambda j: (0, 0)),
            pl.BlockSpec((1, width), lambda j: (0, 0)),
            pl.BlockSpec((1, width), lambda j: (0, 0)),
            pl.BlockSpec((width, tn), lambda j: (0, j)),
            pl.BlockSpec((m, tn), lambda j: (0, j + gate_col // tn)),
            pl.BlockSpec((m, tn), lambda j: (0, j)),
        ],
        out_specs=(pl.BlockSpec((m, tn), lambda j: (0, j)),
                   pl.BlockSpec((m, width), lambda j: (0, 0)),
                   pl.BlockSpec((1, width, tn), lambda j: (j, 0, 0))),
        out_shape=(jax.ShapeDtypeStruct((m, d), BF16), jax.ShapeDtypeStruct((m, width), F32),
                   jax.ShapeDtypeStruct((d // tn, width, tn), BF16)),
        scratch_shapes=[pltpu.VMEM((m, width), BF16)],
        compiler_params=_cparams(("arbitrary",)),
        name="gmlp_sample_emit",
    )(guv, guv, gnv, wdiag, bias0, w_gout, gates, p)


def _mixer(x, h2, prm, w, *, nbatch, x0, tm_mm):
    sample = x0 is not None
    d_model = x.shape[1]
    ssm_width = prm["bw"].shape[0] * UB
    gmlp_width = prm["gnv"].shape[1]
    o1 = ssm_width
    o3 = o1 + 2 * gmlp_width
    tn = 512
    wb = {}

    cols = {}
    if sample:
        w = dict(w, win_ssm=w["w_in"], win_uv=w["w_in"], win_gate=w["w_in"],
                 glu_a=w["w_glu"], glu_b=w["w_glu"])
        cols = dict(win_uv=o1, win_gate=o3, glu_b=d_model)

    def mm(lhs, wkeys, extras, kind, n_out, out_dtype, name):
        out, emitted = _mm(lhs, [(w[k], cols.get(k, 0)) for k in wkeys], extras, kind=kind,
                           n_out=n_out, out_dtype=out_dtype, tm=tm_mm, tn=2 * tn, emit=sample, name=name)
        wb.update(zip(wkeys, emitted))
        return out

    act = F32 if sample else BF16
    z_ssm = mm(h2, ["win_ssm"], [], "identity", o1, F32, "w_in_ssm")
    guv = mm(h2, ["win_uv"], [], "gelu", 2 * gmlp_width, act, "w_in_uv")
    gates = mm(h2, ["win_gate"], [], "sigmoid", 2 * d_model, act, "w_in_gate")

    if sample:
        y, s_re, s_im = _ssm_sample(z_ssm, prm["bw"], prm["cw"], prm["a_re"], prm["a_im"],
                                    prm["d_skip"], x0[0], x0[1])
    else:
        seq = x.shape[0] // nbatch
        y, s_re, s_im = _ssm_prompt(z_ssm.reshape(nbatch, seq, ssm_width), prm["bw"], prm["cw"],
                                    prm["a_re"], prm["a_im"], prm["d_skip"])
        y = y.reshape(nbatch * seq, ssm_width)
    p = mm(y, ["glu_a", "glu_b"], [(gates, 0)], "gated_glu", d_model, act, "ssm_glu")

    if sample:
        merged, v_rows, wb["w_gout"] = _gmlp_sample(guv, prm["gnv"], prm["w_diag"], prm["bias0"],
                                                    w["w_gout"], gates, p, gate_col=d_model, tn=2 * tn)
    else:
        merged = _gmlp_prompt(guv, prm["gnv"], prm["w_s"], prm["bias_full"], w["w_gout"], gates, p,
                              gate_col=d_model, tm=tm_mm, tn=2 * tn)
        v_rows = None
    x = mm(merged, ["w_out"], [(x, 0)], "residual", d_model, F32, "out_proj")
    return x, s_re, s_im, v_rows, wb


def kernel(x_prompt, x_sample, state_ssm_re, state_ssm_im, norm_ffn1, ffn1_gate, ffn1_up, ffn1_down, norm_mix, w_in, ssm_lambda_re, ssm_lambda_im, ssm_log_dt, ssm_b_re, ssm_b_im, ssm_c_re, ssm_c_im, ssm_d, ssm_w_glu, gmlp_norm_v, gmlp_w_s, gmlp_b_s, gmlp_w_out, w_out, norm_ffn2, ffn2_gate, ffn2_up, ffn2_down, norm_final):
    depth = w_in.shape[0]
    assert depth == 1, "the final RMSNorm is fused into the last FFN; one layer per step"
    batch, seq, d_model = x_prompt.shape
    dec_batch, dec_seq, _ = x_sample.shape
    assert dec_seq == 1
    groups, states = ssm_lambda_re.shape[1:]
    gmlp_width = gmlp_norm_v.shape[1]
    heads = gmlp_w_s.shape[1]
    head_dim = gmlp_width // heads
    assert seq % SCAN_T == 0 and seq % CHUNK == 0 and head_dim == GMLP_HEAD
    assert ssm_b_re.shape[-1] == SSM_GROUP and states == SSM_STATE and groups % GROUPS_PER_BLOCK == 0

    l = 0
    yp = x_prompt.reshape(batch * seq, d_model)
    ys = x_sample.reshape(dec_batch, d_model)
    gfin = norm_final.reshape(1, d_model)
    a_re, a_im, bw, cw = _discretise(ssm_lambda_re[l], ssm_lambda_im[l], ssm_log_dt[l],
                                     ssm_b_re[l], ssm_b_im[l], ssm_c_re[l], ssm_c_im[l])
    prm = dict(
        n1=norm_ffn1[l].reshape(1, d_model), nmix=norm_mix[l].reshape(1, d_model),
        n2=norm_ffn2[l].reshape(1, d_model),
        bw=bw, cw=cw, a_re=a_re, a_im=a_im, d_skip=ssm_d[l],
        gnv=gmlp_norm_v[l].reshape(1, gmlp_width), w_s=gmlp_w_s[l],
        bias_full=jnp.repeat(gmlp_b_s[l].T, head_dim, axis=1),
        w_diag=jnp.repeat(gmlp_w_s[l][:, 0, 0], head_dim).reshape(1, gmlp_width),
        bias0=jnp.repeat(gmlp_b_s[l][:, 0], head_dim).reshape(1, gmlp_width),
    )
    w_f32 = dict(w_in=w_in[l], w_glu=ssm_w_glu[l], w_gout=gmlp_w_out[l], w_out=w_out[l])
    x0 = (state_ssm_re[l].reshape(dec_batch, groups * states),
          state_ssm_im[l].reshape(dec_batch, groups * states))
    tm, tf, row_tiles = 1024, 256, 8

    yp, ys, hp, hs = _ffn(yp, ys, prm["n1"], ffn1_gate[l], ffn1_up[l], ffn1_down[l], prm["nmix"],
                          tail="norm_out", row_tiles=row_tiles, tf=tf)
    ys, sr, si, vr, w_bf16 = _mixer(ys, hs, prm, w_f32, nbatch=dec_batch, x0=x0, tm_mm=dec_batch)
    yp, pr, pi, _, _ = _mixer(yp, hp, prm, w_bf16, nbatch=batch, x0=None, tm_mm=tm)
    yp, ys = _ffn(yp, ys, prm["n2"], ffn2_gate[l], ffn2_up[l], ffn2_down[l], gfin,
                  tail="final", row_tiles=row_tiles, tf=tf)
    return (yp.reshape(batch, seq, d_model), ys.reshape(dec_batch, dec_seq, d_model),
            pr.reshape(1, batch, groups, states), pi.reshape(1, batch, groups, states),
            sr.reshape(1, dec_batch, groups, states), si.reshape(1, dec_batch, groups, states),
            vr.reshape(1, dec_batch, dec_seq, gmlp_width))
```

```python
import functools
import math

import jax
import jax.numpy as jnp
from jax import lax
from jax.experimental import pallas as pl
from jax.experimental.pallas import tpu as pltpu

F32 = jnp.float32
BF16 = jnp.bfloat16

EPS = 1e-6
LANES = 128
SSM_GROUP = 16
SSM_STATE = 64
GROUPS_PER_BLOCK = 16
UB = GROUPS_PER_BLOCK * SSM_GROUP
SB = GROUPS_PER_BLOCK * SSM_STATE
SLABS = SB // LANES
SCAN_T = 256
SCAN_SKEW = 4
SCAN_PITCH = SCAN_T + SCAN_SKEW
CHUNK = 128
GMLP_HEAD = 128
VMEM_LIMIT = 62 * 1024 * 1024


def _cparams(sem):
    return pltpu.CompilerParams(dimension_semantics=sem, vmem_limit_bytes=VMEM_LIMIT)


def _gelu(x):
    c = math.sqrt(2.0 / math.pi)
    return 0.5 * x * (1.0 + jnp.tanh(c * (x + 0.044715 * (x * x * x))))


def _sigmoid(x):
    return 1.0 / (1.0 + jnp.exp(-x))


def _as_bf16(w_ref, emit_ref):
    w = w_ref[0] if len(w_ref.shape) == 3 else w_ref[...]
    if w.dtype != BF16:
        w = w.astype(BF16)
    if emit_ref is not None:
        if len(emit_ref.shape) == 3:
            emit_ref[0] = w
        else:
            emit_ref[...] = w
    return w


def _col_tile_spec(w, k, tn, col0=0):
    if w.ndim == 3:
        assert w.shape[1:] == (k, tn) and col0 == 0
        return pl.BlockSpec((1, k, tn), lambda i, j: (j, 0, 0))
    return pl.BlockSpec((k, tn), lambda i, j, o=col0 // tn: (0, j + o))


def _ffn_kernel(*refs, tm, d_model, col_chunk, row_chunk, tail, tail_rows):
    refs = list(refs)
    x_hbm = refs.pop(0)
    xt_hbm = refs.pop(0) if tail_rows else None
    gn_ref, wg_ref, wu_ref, wd_ref, gt_ref = refs[:5]
    del refs[:5]
    out_ref = refs.pop(0)
    outt_ref = refs.pop(0) if tail_rows else None
    h2_ref = refs.pop(0) if tail == "norm_out" else None
    h2t_ref = refs.pop(0) if tail == "norm_out" and tail_rows else None
    h_buf = h2_ref if h2_ref is not None else refs.pop(0)
    x_sem, xt_sem = refs
    i = pl.program_id(0)
    j = pl.program_id(1)
    last = pl.num_programs(0) - 1
    head_rows = tm - tail_rows

    def copy_in(src, dst, sem):
        cp = pltpu.make_async_copy(src, dst, sem)
        cp.start()
        return cp

    @pl.when(j == 0)
    def _():
        def whole_tile():
            copy_in(x_hbm.at[pl.ds(pl.multiple_of(i * tm, tm), tm), :], out_ref, x_sem).wait()

        if tail_rows:
            pl.when(i < last)(whole_tile)

            @pl.when(i == last)
            def _():
                n_x = x_hbm.shape[0]
                head = copy_in(x_hbm.at[pl.ds(n_x - head_rows, head_rows), :],
                               out_ref.at[pl.ds(0, head_rows), :], x_sem)
                rest = copy_in(xt_hbm, out_ref.at[pl.ds(head_rows, tail_rows), :], xt_sem)
                head.wait()
                rest.wait()
        else:
            whole_tile()
        g = gn_ref[...]

        def body(c, carry):
            r0 = pl.multiple_of(c * row_chunk, row_chunk)
            xf = out_ref[pl.ds(r0, row_chunk), :]
            r = lax.rsqrt(jnp.mean(xf * xf, axis=-1, keepdims=True) + EPS)
            h_buf[pl.ds(r0, row_chunk), :] = (xf * r * g).astype(BF16)
            out_ref[pl.ds(r0, row_chunk), :] = 2.0 * xf
            return carry

        lax.fori_loop(0, tm // row_chunk, body, 0)

    h = h_buf[...]
    g = jnp.dot(h, _as_bf16(wg_ref, None), preferred_element_type=F32)
    u = jnp.dot(h, _as_bf16(wu_ref, None), preferred_element_type=F32)
    a = (g * _sigmoid(g) * u).astype(BF16)
    for n in range(d_model // col_chunk):
        sl = slice(n * col_chunk, (n + 1) * col_chunk)
        out_ref[:, sl] += jnp.dot(a, wd_ref[:, sl].astype(BF16), preferred_element_type=F32)

    @pl.when(j == pl.num_programs(1) - 1)
    def _():
        gt = gt_ref[...]

        def body(c, carry):
            r0 = pl.multiple_of(c * row_chunk, row_chunk)
            y = 0.5 * out_ref[pl.ds(r0, row_chunk), :]
            r = lax.rsqrt(jnp.mean(y * y, axis=-1, keepdims=True) + EPS)
            if tail == "norm_out":
                out_ref[pl.ds(r0, row_chunk), :] = y
                h2_ref[pl.ds(r0, row_chunk), :] = (y * r * gt).astype(BF16)
            else:
                out_ref[pl.ds(r0, row_chunk), :] = y * r * gt
            return carry

        lax.fori_loop(0, tm // row_chunk, body, 0)
        if tail_rows:
            @pl.when(i == last)
            def _():
                outt_ref[...] = out_ref[pl.ds(head_rows, tail_rows), :]
                if h2t_ref is not None:
                    h2t_ref[...] = h2_ref[pl.ds(head_rows, tail_rows), :]


def _ffn(x, x_tail, gn, wg, wu, wd, gtail, *, tail, row_tiles, tf):
    n_x, d = x.shape
    f = wd.shape[0]
    tail_rows = x_tail.shape[0]
    tm = (n_x + tail_rows) // row_tiles
    ni, nj = row_tiles, f // tf
    assert ni * tm == n_x + tail_rows and nj * tf == f
    assert 0 < tail_rows < tm and (tm - tail_rows) % 16 == 0
    row_chunk = max(c for c in range(16, 81, 16) if tm % c == 0)
    kern = functools.partial(_ffn_kernel, tm=tm, d_model=d, col_chunk=512, row_chunk=row_chunk,
                             tail=tail, tail_rows=tail_rows)
    once = pl.Buffered(1)
    in_specs = [
        pl.BlockSpec(memory_space=pl.ANY),
        pl.BlockSpec(memory_space=pl.ANY),
        pl.BlockSpec((1, d), lambda i, j: (0, 0)),
        pl.BlockSpec((d, tf), lambda i, j: (0, j)),
        pl.BlockSpec((d, tf), lambda i, j: (0, j)),
        pl.BlockSpec((tf, d), lambda i, j: (j, 0)),
        pl.BlockSpec((1, d), lambda i, j: (0, 0)),
    ]
    args = [x, x_tail, gn, wg, wu, wd, gtail]
    out_specs = [pl.BlockSpec((tm, d), lambda i, j: (i, 0), pipeline_mode=once),
                 pl.BlockSpec((tail_rows, d), lambda i, j: (0, 0), pipeline_mode=once)]
    out_shape = [jax.ShapeDtypeStruct((n_x, d), F32), jax.ShapeDtypeStruct((tail_rows, d), F32)]
    scratch = [pltpu.VMEM((tm, d), BF16)]
    if tail == "norm_out":
        out_specs += [pl.BlockSpec((tm, d), lambda i, j: (i, 0), pipeline_mode=once),
                      pl.BlockSpec((tail_rows, d), lambda i, j: (0, 0), pipeline_mode=once)]
        out_shape += [jax.ShapeDtypeStruct((n_x, d), BF16), jax.ShapeDtypeStruct((tail_rows, d), BF16)]
        scratch = []
    return pl.pallas_call(
        kern,
        grid=(ni, nj),
        in_specs=in_specs,
        out_specs=out_specs,
        out_shape=out_shape,
        scratch_shapes=scratch + [pltpu.SemaphoreType.DMA, pltpu.SemaphoreType.DMA],
        compiler_params=_cparams(("arbitrary", "arbitrary")),
        name="ffn_" + tail,
    )(*args)


def _mm_epilogue(kind, accs, extras):
    if kind == "identity":
        return accs[0]
    if kind == "gelu":
        return _gelu(accs[0])
    if kind == "sigmoid":
        return _sigmoid(accs[0])
    if kind == "gated_glu":
        return extras[0].astype(F32) * (accs[0] * _sigmoid(accs[1]))
    if kind == "residual":
        return extras[0].astype(F32) + accs[0]
    raise ValueError(kind)


def _mm_kernel(*refs, n_w, n_extra, kind, emit):
    lhs_ref = refs[0]
    w_refs = refs[1:1 + n_w]
    extra_refs = refs[1 + n_w:1 + n_w + n_extra]
    out_ref = refs[1 + n_w + n_extra]
    emit_refs = refs[2 + n_w + n_extra:] if emit else [None] * n_w
    lhs = lhs_ref[...]
    accs = [jnp.dot(lhs, _as_bf16(w, e), preferred_element_type=F32) for w, e in zip(w_refs, emit_refs)]
    out_ref[...] = _mm_epilogue(kind, accs, [e[...] for e in extra_refs]).astype(out_ref.dtype)


def _mm(lhs, weights, extras, *, kind, n_out, out_dtype, tm, tn, emit, name):
    m, k = lhs.shape
    assert not emit or m == tm
    in_specs = [pl.BlockSpec((tm, k), lambda i, j: (i, 0))]
    args = [lhs]
    for w, c0 in weights:
        in_specs.append(_col_tile_spec(w, k, tn, c0))
        args.append(w)
    for e, c0 in extras:
        in_specs.append(pl.BlockSpec((tm, tn), lambda i, j, o=c0 // tn: (i, j + o)))
        args.append(e)
    out_specs = [pl.BlockSpec((tm, tn), lambda i, j: (i, j))]
    out_shape = [jax.ShapeDtypeStruct((m, n_out), out_dtype)]
    if emit:
        out_specs += [pl.BlockSpec((1, k, tn), lambda i, j: (j, 0, 0)) for _ in weights]
        out_shape += [jax.ShapeDtypeStruct((n_out // tn, k, tn), BF16) for _ in weights]
    kern = functools.partial(_mm_kernel, n_w=len(weights), n_extra=len(extras), kind=kind, emit=emit)
    res = pl.pallas_call(
        kern,
        grid=(m // tm, n_out // tn),
        in_specs=in_specs,
        out_specs=out_specs,
        out_shape=out_shape,
        compiler_params=_cparams(("arbitrary", "arbitrary")),
        name=name + ("_emit" if emit else ""),
    )(*args)
    return res[0], list(res[1:])


def _zoh(lam_re, lam_im, log_dt):
    dt = jnp.exp(log_dt)
    mag = jnp.exp(dt * lam_re)
    a_re = mag * jnp.cos(dt * lam_im)
    a_im = mag * jnp.sin(dt * lam_im)
    nr, ni = a_re - 1.0, a_im
    den = lam_re * lam_re + lam_im * lam_im
    coef_re = (nr * lam_re + ni * lam_im) / den
    coef_im = (ni * lam_re - nr * lam_im) / den
    return a_re, a_im, coef_re, coef_im


def _spread_lanes(x, reps):
    w = x.shape[1]
    assert w & (w - 1) == 0
    src = lax.broadcasted_iota(jnp.int32, (w, w * reps), 0)
    dst = lax.broadcasted_iota(jnp.int32, (w, w * reps), 1)
    sel = ((dst & (w - 1)) == src).astype(BF16)
    return jnp.dot(x.astype(BF16), sel, preferred_element_type=F32)


def _disc_kernel(lre_ref, lim_ref, ldt_ref, lre_col, lim_col, ldt_col, bre_ref, bim_ref, cre_ref, cim_ref,
                 are_ref, aim_ref, bw_ref, cw_ref):
    a_re, a_im, _, _ = _zoh(lre_ref[...], lim_ref[...], ldt_ref[...])
    are_ref[...] = a_re
    aim_ref[...] = a_im

    _, _, coef_re, coef_im = _zoh(lre_col[...], lim_col[...], ldt_col[...])
    b_re = bre_ref[...]
    b_im = bim_ref[...]
    bb_re = coef_re * b_re - coef_im * b_im
    bb_im = coef_re * b_im + coef_im * b_re
    state_shift = SSM_STATE.bit_length() - 1
    chan_shift = SSM_GROUP.bit_length() - 1
    row_g = lax.broadcasted_iota(jnp.int32, (SB, UB), 0) >> state_shift
    col_g = lax.broadcasted_iota(jnp.int32, (SB, UB), 1) >> chan_shift
    diag = row_g == col_g
    for part, bb in enumerate((bb_re, bb_im)):
        wt = jnp.where(diag, _spread_lanes(bb, GROUPS_PER_BLOCK), 0.0)
        bw_ref[0, :, part * SB:(part + 1) * SB] = wt.T.astype(BF16)

    row_g = lax.broadcasted_iota(jnp.int32, (UB, SB), 0) >> chan_shift
    col_g = lax.broadcasted_iota(jnp.int32, (UB, SB), 1) >> state_shift
    diag = row_g == col_g
    for part, c in enumerate((cre_ref[...], -cim_ref[...])):
        wt = jnp.where(diag, _spread_lanes(c, GROUPS_PER_BLOCK), 0.0)
        cw_ref[0, part * SB:(part + 1) * SB, :] = wt.T.astype(BF16)


def _discretise(lam_re, lam_im, log_dt, b_re, b_im, c_re, c_im):
    g, p = lam_re.shape
    h = b_re.shape[-1]
    nb = g // GROUPS_PER_BLOCK
    col = lambda x: x.reshape(g * p, 1)
    blk = lambda rows, cols: pl.BlockSpec((rows, cols), lambda i: (i, 0))
    return pl.pallas_call(
        _disc_kernel,
        grid=(nb,),
        in_specs=[blk(GROUPS_PER_BLOCK, p), blk(GROUPS_PER_BLOCK, p), blk(GROUPS_PER_BLOCK, 1),
                  blk(SB, 1), blk(SB, 1), blk(SB, 1),
                  blk(SB, h), blk(SB, h), blk(UB, p), blk(UB, p)],
        out_specs=(blk(GROUPS_PER_BLOCK, p), blk(GROUPS_PER_BLOCK, p),
                   pl.BlockSpec((1, UB, 2 * SB), lambda i: (i, 0, 0)),
                   pl.BlockSpec((1, 2 * SB, UB), lambda i: (i, 0, 0))),
        out_shape=(jax.ShapeDtypeStruct((g, p), F32), jax.ShapeDtypeStruct((g, p), F32),
                   jax.ShapeDtypeStruct((nb, UB, 2 * SB), BF16),
                   jax.ShapeDtypeStruct((nb, 2 * SB, UB), BF16)),
        compiler_params=_cparams(("arbitrary",)),
        name="s5_discretise",
    )(lam_re, lam_im, log_dt.reshape(g, 1),
      col(lam_re), col(lam_im), col(jnp.repeat(log_dt, p)),
      b_re.reshape(g * p, h), b_im.reshape(g * p, h), c_re.reshape(g * h, p), c_im.reshape(g * h, p))


def _ssm_prompt_kernel(*refs, nbatch):
    parity = pl.program_id(1) % 2
    for fill in (0, 1):
        pl.when(parity == fill)(functools.partial(_ssm_prompt_step, *refs, nbatch=nbatch, fill=fill))


def _ssm_prompt_step(un_ref, u_ref, bw_ref, cw_ref, are_ref, aim_ref, d_ref,
                     y_ref, sre_ref, sim_ref,
                     bure2, buim2, st_re, st_im, *, nbatch, fill):
    tc = pl.program_id(1)
    scan = 1 - fill

    @pl.when(tc == 0)
    def _():
        st_re[...] = jnp.zeros_like(st_re)
        st_im[...] = jnp.zeros_like(st_im)
        bure2[1] = jnp.zeros(bure2.shape[1:], F32)
        buim2[1] = jnp.zeros(buim2.shape[1:], F32)

    bure_f, buim_f = bure2.at[fill], buim2.at[fill]
    bure, buim = bure2.at[scan], buim2.at[scan]

    pack = 8 // nbatch
    nq = SLABS // pack

    def slab_rows(c, b):
        part, c = (0, c) if c < SLABS else (1, c - SLABS)
        v = (c // nq) * nbatch + b
        r0 = v * SCAN_PITCH + (v % 2) * SCAN_SKEW
        return part, c % nq, slice(r0, r0 + SCAN_T)

    for ref in (bure_f, buim_f):
        for q in range(nq):
            for v in range(0, 8, 2):
                ref[q, v * SCAN_PITCH + SCAN_T:v * SCAN_PITCH + SCAN_T + 2 * SCAN_SKEW, :] = (
                    jnp.zeros((2 * SCAN_SKEW, LANES), F32))

    u_bf = un_ref[...].reshape(nbatch * SCAN_T, UB).astype(BF16)
    per_dot = 4
    for n in range(2 * SLABS // per_dot):
        bu = jnp.dot(u_bf, bw_ref[0, :, per_dot * n * LANES:per_dot * (n + 1) * LANES],
                     preferred_element_type=F32)
        for h in range(per_dot):
            for b in range(nbatch):
                part, q, rows = slab_rows(per_dot * n + h, b)
                (bure_f, buim_f)[part][q, rows, :] = bu[b * SCAN_T:(b + 1) * SCAN_T, h * LANES:(h + 1) * LANES]

    def packed(a_ref, q):
        return jnp.concatenate(
            [jnp.broadcast_to(a_ref[0, q + h * nq], (nbatch, LANES)) for h in range(pack)], axis=0)

    a_re = [packed(are_ref, q) for q in range(nq)]
    a_im = [packed(aim_ref, q) for q in range(nq)]
    init = tuple(st_re[q] for q in range(nq)) + tuple(st_im[q] for q in range(nq))

    def step(t, carry, active=None):
        rows = pl.ds(t, pack * nbatch, stride=SCAN_PITCH)
        new_re, new_im = [], []
        for q in range(nq):
            s_r, s_i = carry[q], carry[nq + q]
            b_r, b_i = bure[q, rows, :], buim[q, rows, :]
            n_r = a_re[q] * s_r - a_im[q] * s_i + b_r
            n_i = a_re[q] * s_i + a_im[q] * s_r + b_i
            if active is not None:
                b_r, b_i = jnp.where(active, n_r, b_r), jnp.where(active, n_i, b_i)
                n_r, n_i = jnp.where(active, n_r, s_r), jnp.where(active, n_i, s_i)
                bure[q, rows, :] = b_r
                buim[q, rows, :] = b_i
            else:
                bure[q, rows, :] = n_r
                buim[q, rows, :] = n_i
            new_re.append(n_r)
            new_im.append(n_i)
        return tuple(new_re) + tuple(new_im)

    odd = (lax.broadcasted_iota(jnp.int32, (8, LANES), 0) & 1) == 1
    carry = init
    for t in range(SCAN_SKEW):
        carry = step(t, carry, active=jnp.logical_not(odd))
    for t in range(SCAN_SKEW, SCAN_T):
        carry = step(t, carry)
    for t in range(SCAN_T, SCAN_T + SCAN_SKEW):
        carry = step(t, carry, active=odd)
    fin = carry
    for q in range(nq):
        st_re[q] = fin[q]
        st_im[q] = fin[nq + q]

    d = d_ref[0]
    for b in range(nbatch):
        y = d * u_ref[b]
        for n in range(SLABS):
            (p0, q0, rows0), (p1, q1, rows1) = slab_rows(2 * n, b), slab_rows(2 * n + 1, b)
            s_pair = jnp.concatenate([(bure, buim)[p0][q0, rows0, :], (bure, buim)[p1][q1, rows1, :]],
                                     axis=1).astype(BF16)
            y = y + jnp.dot(s_pair, cw_ref[0, 2 * n * LANES:(2 * n + 2) * LANES, :],
                            preferred_element_type=F32)
        y_ref[b] = _gelu(y).astype(y_ref.dtype)

    @pl.when(tc == pl.num_programs(1) - 1)
    def _():
        for c in range(SLABS):
            q, h = c % nq, c // nq
            sre_ref[:, c * LANES:(c + 1) * LANES] = st_re[q, h * nbatch:(h + 1) * nbatch, :]
            sim_ref[:, c * LANES:(c + 1) * LANES] = st_im[q, h * nbatch:(h + 1) * nbatch, :]


def _ssm_prompt(z3, bw, cw, a_re, a_im, d_skip):
    nbatch, seq, width = z3.shape
    nb = bw.shape[0]
    states = nb * SB
    kern = functools.partial(_ssm_prompt_kernel, nbatch=nbatch)
    nt = seq // SCAN_T
    assert 8 % nbatch == 0
    return pl.pallas_call(
        kern,
        grid=(nb, nt + 1),
        in_specs=[
            pl.BlockSpec((nbatch, SCAN_T, UB), lambda g, t: (0, jnp.minimum(t, nt - 1), g)),
            pl.BlockSpec((nbatch, SCAN_T, UB), lambda g, t: (0, jnp.maximum(t - 1, 0), g)),
            pl.BlockSpec((1, UB, 2 * SB), lambda g, t: (g, 0, 0)),
            pl.BlockSpec((1, 2 * SB, UB), lambda g, t: (g, 0, 0)),
            pl.BlockSpec((1, SLABS, 1, LANES), lambda g, t: (g, 0, 0, 0)),
            pl.BlockSpec((1, SLABS, 1, LANES), lambda g, t: (g, 0, 0, 0)),
            pl.BlockSpec((1, 1, UB), lambda g, t: (g, 0, 0)),
        ],
        out_specs=(
            pl.BlockSpec((nbatch, SCAN_T, UB), lambda g, t: (0, jnp.maximum(t - 1, 0), g)),
            pl.BlockSpec((nbatch, SB), lambda g, t: (0, g)),
            pl.BlockSpec((nbatch, SB), lambda g, t: (0, g)),
        ),
        out_shape=(
            jax.ShapeDtypeStruct((nbatch, seq, width), BF16),
            jax.ShapeDtypeStruct((nbatch, states), F32),
            jax.ShapeDtypeStruct((nbatch, states), F32),
        ),
        scratch_shapes=[
            pltpu.VMEM((2, SLABS * nbatch // 8, 8 * SCAN_PITCH, LANES), F32),
            pltpu.VMEM((2, SLABS * nbatch // 8, 8 * SCAN_PITCH, LANES), F32),
            pltpu.VMEM((SLABS * nbatch // 8, 8, LANES), F32),
            pltpu.VMEM((SLABS * nbatch // 8, 8, LANES), F32),
        ],
        compiler_params=_cparams(("arbitrary", "arbitrary")),
        name="ssm_prompt",
    )(z3, z3, bw, cw, a_re.reshape(nb, SLABS, 1, LANES), a_im.reshape(nb, SLABS, 1, LANES),
      d_skip.reshape(nb, 1, UB))


def _ssm_sample_kernel(u_ref, bw_ref, cw_ref, are_ref, aim_ref, d_ref, x0re_ref, x0im_ref,
                       y_ref, sre_ref, sim_ref):
    u = u_ref[...]
    bu = jnp.dot(u.astype(BF16), bw_ref[0], preferred_element_type=F32)
    a_re = are_ref[0]
    a_im = aim_ref[0]
    x_re = x0re_ref[...]
    x_im = x0im_ref[...]
    s_re = a_re * x_re - a_im * x_im + bu[:, :SB]
    s_im = a_re * x_im + a_im * x_re + bu[:, SB:]
    sre_ref[...] = s_re
    sim_ref[...] = s_im
    s_cat = jnp.concatenate([s_re, s_im], axis=1).astype(BF16)
    y = jnp.dot(s_cat, cw_ref[0], preferred_element_type=F32) + d_ref[0] * u
    y_ref[...] = _gelu(y).astype(y_ref.dtype)


def _ssm_sample(z, bw, cw, a_re, a_im, d_skip, x0_re, x0_im):
    nbatch = z.shape[0]
    nb = bw.shape[0]
    return pl.pallas_call(
        _ssm_sample_kernel,
        grid=(nb,),
        in_specs=[
            pl.BlockSpec((nbatch, UB), lambda g: (0, g)),
            pl.BlockSpec((1, UB, 2 * SB), lambda g: (g, 0, 0)),
            pl.BlockSpec((1, 2 * SB, UB), lambda g: (g, 0, 0)),
            pl.BlockSpec((1, 1, SB), lambda g: (g, 0, 0)),
            pl.BlockSpec((1, 1, SB), lambda g: (g, 0, 0)),
            pl.BlockSpec((1, 1, UB), lambda g: (g, 0, 0)),
            pl.BlockSpec((nbatch, SB), lambda g: (0, g)),
            pl.BlockSpec((nbatch, SB), lambda g: (0, g)),
        ],
        out_specs=(
            pl.BlockSpec((nbatch, UB), lambda g: (0, g)),
            pl.BlockSpec((nbatch, SB), lambda g: (0, g)),
            pl.BlockSpec((nbatch, SB), lambda g: (0, g)),
        ),
        out_shape=(
            jax.ShapeDtypeStruct((nbatch, nb * UB), BF16),
            jax.ShapeDtypeStruct((nbatch, nb * SB), F32),
            jax.ShapeDtypeStruct((nbatch, nb * SB), F32),
        ),
        compiler_params=_cparams(("arbitrary",)),
        name="ssm_sample",
    )(z, bw, cw, a_re.reshape(nb, 1, SB), a_im.reshape(nb, 1, SB), d_skip.reshape(nb, 1, UB),
      x0_re, x0_im)


def _gmlp_prompt_kernel(gu_ref, gv_ref, gnv_ref, ws_ref, bias_ref, wo_ref, gate_ref, p_ref,
                        out_ref, s_scr, wt_scr, *, tm, heads):
    i = pl.program_id(0)
    j = pl.program_id(1)

    @pl.when(jnp.logical_and(i == 0, j == 0))
    def _():
        row = lax.broadcasted_iota(jnp.int32, (CHUNK, CHUNK), 0)
        col = lax.broadcasted_iota(jnp.int32, (CHUNK, CHUNK), 1)
        mask = (col <= row).astype(F32)
        for g in range(heads):
            wt_scr[g] = (ws_ref[g] * mask).astype(BF16)

    @pl.when(j == 0)
    def _():
        gnv = gnv_ref[...]

        def body(c, carry):
            r0 = pl.multiple_of(c * CHUNK, CHUNK)
            gv = gv_ref[pl.ds(r0, CHUNK), :].astype(F32)
            r = lax.rsqrt(jnp.mean(gv * gv, axis=-1, keepdims=True) + EPS)
            v = (gv * r * gnv).astype(BF16)
            for g in range(heads):
                sl = slice(g * GMLP_HEAD, (g + 1) * GMLP_HEAD)
                mixed = jnp.dot(wt_scr[g], v[:, sl], preferred_element_type=F32) + bias_ref[:, sl]
                gu = gu_ref[pl.ds(r0, CHUNK), sl].astype(F32)
                s_scr[pl.ds(r0, CHUNK), sl] = (gu * mixed).astype(BF16)
            return carry

        lax.fori_loop(0, tm // CHUNK, body, 0)

    yb = jnp.dot(s_scr[...], _as_bf16(wo_ref, None), preferred_element_type=F32)
    out_ref[...] = (p_ref[...].astype(F32) + gate_ref[...].astype(F32) * yb).astype(out_ref.dtype)


def _gmlp_prompt(guv, gnv, w_s, bias_full, w_gout, gates, p, *, gate_col, tm, tn):
    m = guv.shape[0]
    width, d = w_gout.shape
    heads = w_s.shape[0]
    kern = functools.partial(_gmlp_prompt_kernel, tm=tm, heads=heads)
    return pl.pallas_call(
        kern,
        grid=(m // tm, d // tn),
        in_specs=[
            pl.BlockSpec((tm, width), lambda i, j: (i, 0)),
            pl.BlockSpec((tm, width), lambda i, j: (i, 1)),
            pl.BlockSpec((1, width), lambda i, j: (0, 0)),
            pl.BlockSpec((heads, CHUNK, CHUNK), lambda i, j: (0, 0, 0)),
            pl.BlockSpec((CHUNK, width), lambda i, j: (0, 0)),
            _col_tile_spec(w_gout, width, tn),
            pl.BlockSpec((tm, tn), lambda i, j: (i, j + gate_col // tn)),
            pl.BlockSpec((tm, tn), lambda i, j: (i, j)),
        ],
        out_specs=pl.BlockSpec((tm, tn), lambda i, j: (i, j)),
        out_shape=jax.ShapeDtypeStruct((m, d), BF16),
        scratch_shapes=[pltpu.VMEM((tm, width), BF16), pltpu.VMEM((heads, CHUNK, CHUNK), BF16)],
        compiler_params=_cparams(("arbitrary", "arbitrary")),
        name="gmlp_prompt",
    )(guv, guv, gnv, w_s, bias_full, w_gout, gates, p)


def _gmlp_sample_kernel(gu_ref, gv_ref, gnv_ref, wdiag_ref, bias_ref, wo_ref, gate_ref, p_ref,
                        out_ref, v_ref, s_scr):
    j = pl.program_id(0)

    @pl.when(j == 0)
    def _():
        gv = gv_ref[...]
        r = lax.rsqrt(jnp.mean(gv * gv, axis=-1, keepdims=True) + EPS)
        v = gv * r * gnv_ref[...]
        v_ref[...] = v
        mixed = wdiag_ref[...] * v + bias_ref[...]
        s_scr[...] = (gu_ref[...] * mixed).astype(BF16)

    yb = jnp.dot(s_scr[...], _as_bf16(wo_ref, None), preferred_element_type=F32)
    out_ref[...] = (p_ref[...] + gate_ref[...] * yb).astype(out_ref.dtype)


def _gmlp_sample(guv, gnv, wdiag, bias0, w_gout, gates, p, *, gate_col, tn):
    m = guv.shape[0]
    width, d = w_gout.shape
    return pl.pallas_call(
        _gmlp_sample_kernel,
        grid=(d // tn,),
        in_specs=[
            pl.BlockSpec((m, width), lambda j: (0, 0)),
            pl.BlockSpec((m, width), lambda j: (0, 1)),
            pl.BlockSpec((1, width), lambda j: (0, 0)),
            pl.BlockSpec((1, width), lambda j: (0, 0)),
            pl.BlockSpec((1, width), lambda j: (0, 0)),
            pl.BlockSpec((width, tn), lambda j: (0, j)),
            pl.BlockSpec((m, tn), lambda j: (0, j + gate_col // tn)),
            pl.BlockSpec((m, tn), lambda j: (0, j)),
        ],
        out_specs=(pl.BlockSpec((m, tn), lambda j: (0, j)),
                   pl.BlockSpec((m, width), lambda j: (0, 0))),
        out_shape=(jax.ShapeDtypeStruct((m, d), BF16), jax.ShapeDtypeStruct((m, width), F32)),
        scratch_shapes=[pltpu.VMEM((m, width), BF16)],
        compiler_params=_cparams(("arbitrary",)),
        name="gmlp_sample",
    )(guv, guv, gnv, wdiag, bias0, w_gout, gates, p)


def _mixer(x, h2, prm, w, *, nbatch, x0, tm_mm):
    sample = x0 is not None
    d_model = x.shape[1]
    ssm_width = prm["bw"].shape[0] * UB
    gmlp_width = prm["gnv"].shape[1]
    o1 = ssm_width
    o3 = o1 + 2 * gmlp_width
    wide, narrow = 1024, 512

    w = dict(w, win_ssm=w["w_in"], win_uv=w["w_in"], win_gate=w["w_in"], glu_a=w["w_glu"], glu_b=w["w_glu"])
    cols = dict(win_uv=o1, win_gate=o3, glu_b=d_model)

    def mm(lhs, wkeys, extras, kind, n_out, out_dtype, name, tn=wide):
        out, _ = _mm(lhs, [(w[k], cols.get(k, 0)) for k in wkeys], extras, kind=kind, n_out=n_out,
                     out_dtype=out_dtype, tm=tm_mm, tn=wide if sample else tn, emit=False, name=name)
        return out

    act = F32 if sample else BF16
    z_ssm = mm(h2, ["win_ssm"], [], "identity", o1, F32, "w_in_ssm", tn=narrow)
    guv = mm(h2, ["win_uv"], [], "gelu", 2 * gmlp_width, act, "w_in_uv")
    gates = mm(h2, ["win_gate"], [], "sigmoid", 2 * d_model, act, "w_in_gate")

    if sample:
        y, s_re, s_im = _ssm_sample(z_ssm, prm["bw"], prm["cw"], prm["a_re"], prm["a_im"],
                                    prm["d_skip"], x0[0], x0[1])
    else:
        seq = x.shape[0] // nbatch
        y, s_re, s_im = _ssm_prompt(z_ssm.reshape(nbatch, seq, ssm_width), prm["bw"], prm["cw"],
                                    prm["a_re"], prm["a_im"], prm["d_skip"])
        y = y.reshape(nbatch * seq, ssm_width)
    p = mm(y, ["glu_a", "glu_b"], [(gates, 0)], "gated_glu", d_model, act, "ssm_glu")

    if sample:
        merged, v_rows = _gmlp_sample(guv, prm["gnv"], prm["w_diag"], prm["bias0"],
                                      w["w_gout"], gates, p, gate_col=d_model, tn=wide)
    else:
        merged = _gmlp_prompt(guv, prm["gnv"], prm["w_s"], prm["bias_full"], w["w_gout"], gates, p,
                              gate_col=d_model, tm=tm_mm, tn=wide)
        v_rows = None
    x = mm(merged, ["w_out"], [(x, 0)], "residual", d_model, F32, "out_proj", tn=narrow)
    return x, s_re, s_im, v_rows


def kernel(x_prompt, x_sample, state_ssm_re, state_ssm_im, norm_ffn1, ffn1_gate, ffn1_up, ffn1_down, norm_mix, w_in, ssm_lambda_re, ssm_lambda_im, ssm_log_dt, ssm_b_re, ssm_b_im, ssm_c_re, ssm_c_im, ssm_d, ssm_w_glu, gmlp_norm_v, gmlp_w_s, gmlp_b_s, gmlp_w_out, w_out, norm_ffn2, ffn2_gate, ffn2_up, ffn2_down, norm_final):
    depth = w_in.shape[0]
    assert depth == 1, "the final RMSNorm is fused into the last FFN; one layer per step"
    batch, seq, d_model = x_prompt.shape
    dec_batch, dec_seq, _ = x_sample.shape
    assert dec_seq == 1
    groups, states = ssm_lambda_re.shape[1:]
    gmlp_width = gmlp_norm_v.shape[1]
    heads = gmlp_w_s.shape[1]
    head_dim = gmlp_width // heads
    assert seq % SCAN_T == 0 and seq % CHUNK == 0 and head_dim == GMLP_HEAD
    assert ssm_b_re.shape[-1] == SSM_GROUP and states == SSM_STATE and groups % GROUPS_PER_BLOCK == 0

    l = 0
    yp = x_prompt.reshape(batch * seq, d_model)
    ys = x_sample.reshape(dec_batch, d_model)
    gfin = norm_final.reshape(1, d_model)
    a_re, a_im, bw, cw = _discretise(ssm_lambda_re[l], ssm_lambda_im[l], ssm_log_dt[l],
                                     ssm_b_re[l], ssm_b_im[l], ssm_c_re[l], ssm_c_im[l])
    prm = dict(
        n1=norm_ffn1[l].reshape(1, d_model), nmix=norm_mix[l].reshape(1, d_model),
        n2=norm_ffn2[l].reshape(1, d_model),
        bw=bw, cw=cw, a_re=a_re, a_im=a_im, d_skip=ssm_d[l],
        gnv=gmlp_norm_v[l].reshape(1, gmlp_width), w_s=gmlp_w_s[l],
        bias_full=jnp.repeat(gmlp_b_s[l].T, head_dim, axis=1),
        w_diag=jnp.repeat(gmlp_w_s[l][:, 0, 0], head_dim).reshape(1, gmlp_width),
        bias0=jnp.repeat(gmlp_b_s[l][:, 0], head_dim).reshape(1, gmlp_width),
    )
    w_f32 = dict(w_in=w_in[l], w_glu=ssm_w_glu[l], w_gout=gmlp_w_out[l], w_out=w_out[l])
    x0 = (state_ssm_re[l].reshape(dec_batch, groups * states),
          state_ssm_im[l].reshape(dec_batch, groups * states))
    tm, tf, row_tiles = 1024, 256, 8

    yp, ys, hp, hs = _ffn(yp, ys, prm["n1"], ffn1_gate[l], ffn1_up[l], ffn1_down[l], prm["nmix"],
                          tail="norm_out", row_tiles=row_tiles, tf=tf)
    ys, sr, si, vr = _mixer(ys, hs, prm, w_f32, nbatch=dec_batch, x0=x0, tm_mm=dec_batch)
    yp, pr, pi, _ = _mixer(yp, hp, prm, w_f32, nbatch=batch, x0=None, tm_mm=tm)
    yp, ys = _ffn(yp, ys, prm["n2"], ffn2_gate[l], ffn2_up[l], ffn2_down[l], gfin,
                  tail="final", row_tiles=row_tiles, tf=tf)
    return (yp.reshape(batch, seq, d_model), ys.reshape(dec_batch, dec_seq, d_model),
            pr.reshape(1, batch, groups, states), pi.reshape(1, batch, groups, states),
            sr.reshape(1, dec_batch, groups, states), si.reshape(1, dec_batch, groups, states),
            vr.reshape(1, dec_batch, dec_seq, gmlp_width))
```

```python
import functools
import math

import jax
import jax.numpy as jnp
from jax import lax
from jax.experimental import pallas as pl
from jax.experimental.pallas import tpu as pltpu

F32 = jnp.float32
BF16 = jnp.bfloat16

EPS = 1e-6
LANES = 128
SSM_GROUP = 16
SSM_STATE = 64
GROUPS_PER_BLOCK = 16
UB = GROUPS_PER_BLOCK * SSM_GROUP
SB = GROUPS_PER_BLOCK * SSM_STATE
SLABS = SB // LANES
SCAN_T = 256
SCAN_SKEW = 4
SCAN_PITCH = SCAN_T + SCAN_SKEW
CHUNK = 128
GMLP_HEAD = 128
VMEM_LIMIT = 62 * 1024 * 1024


def _cparams(sem):
    return pltpu.CompilerParams(dimension_semantics=sem, vmem_limit_bytes=VMEM_LIMIT)


def _gelu(x):
    c = math.sqrt(2.0 / math.pi)
    return 0.5 * x * (1.0 + jnp.tanh(c * (x + 0.044715 * (x * x * x))))


def _sigmoid(x):
    return 1.0 / (1.0 + jnp.exp(-x))


def _as_bf16(w_ref, emit_ref):
    w = w_ref[0] if len(w_ref.shape) == 3 else w_ref[...]
    if w.dtype != BF16:
        w = w.astype(BF16)
    if emit_ref is not None:
        if len(emit_ref.shape) == 3:
            emit_ref[0] = w
        else:
            emit_ref[...] = w
    return w


def _col_tile_spec(w, k, tn, col0=0):
    if w.ndim == 3:
        assert w.shape[1:] == (k, tn) and col0 == 0
        return pl.BlockSpec((1, k, tn), lambda i, j: (j, 0, 0))
    return pl.BlockSpec((k, tn), lambda i, j, o=col0 // tn: (0, j + o))


def _ffn_kernel(*refs, tm, d_model, col_chunk, row_chunk, tail, tail_rows):
    refs = list(refs)
    x_hbm = refs.pop(0)
    xt_hbm = refs.pop(0) if tail_rows else None
    gn_ref, wg_ref, wu_ref, wd_ref, gt_ref = refs[:5]
    del refs[:5]
    out_ref = refs.pop(0)
    outt_ref = refs.pop(0) if tail_rows else None
    h2_ref = refs.pop(0) if tail == "norm_out" else None
    h2t_ref = refs.pop(0) if tail == "norm_out" and tail_rows else None
    h_buf = h2_ref if h2_ref is not None else refs.pop(0)
    x_sem, xt_sem, chunk_sems = refs
    i = pl.program_id(0)
    j = pl.program_id(1)
    last = pl.num_programs(0) - 1
    head_rows = tm - tail_rows

    def copy_in(src, dst, sem):
        cp = pltpu.make_async_copy(src, dst, sem)
        cp.start()
        return cp

    @pl.when(j == 0)
    def _():
        g = gn_ref[...]
        n_chunks = tm // row_chunk

        def chunk_copy(c):
            r0 = pl.multiple_of(c * row_chunk, row_chunk)
            return pltpu.make_async_copy(x_hbm.at[pl.ds(i * tm + r0, row_chunk), :],
                                         out_ref.at[pl.ds(r0, row_chunk), :], chunk_sems.at[c])

        def norm_rows(c, chunked):
            r0 = pl.multiple_of(c * row_chunk, row_chunk)
            if chunked:
                chunk_copy(c).wait()
            xf = out_ref[pl.ds(r0, row_chunk), :]
            r = lax.rsqrt(jnp.mean(xf * xf, axis=-1, keepdims=True) + EPS)
            h_buf[pl.ds(r0, row_chunk), :] = (xf * r * g).astype(BF16)
            out_ref[pl.ds(r0, row_chunk), :] = 2.0 * xf

        @pl.when(i < last)
        def _():
            def start(c, carry):
                chunk_copy(c).start()
                return carry

            lax.fori_loop(0, n_chunks, start, 0)
            lax.fori_loop(0, n_chunks, lambda c, carry: (norm_rows(c, True), carry)[1], 0)

        @pl.when(i == last)
        def _():
            n_x = x_hbm.shape[0]
            head = copy_in(x_hbm.at[pl.ds(n_x - head_rows, head_rows), :],
                           out_ref.at[pl.ds(0, head_rows), :], x_sem)
            rest = copy_in(xt_hbm, out_ref.at[pl.ds(head_rows, tail_rows), :], xt_sem)
            head.wait()
            rest.wait()
            lax.fori_loop(0, n_chunks, lambda c, carry: (norm_rows(c, False), carry)[1], 0)

    h = h_buf[...]
    g = jnp.dot(h, _as_bf16(wg_ref, None), preferred_element_type=F32)
    u = jnp.dot(h, _as_bf16(wu_ref, None), preferred_element_type=F32)
    a = (g * _sigmoid(g) * u).astype(BF16)
    for n in range(d_model // col_chunk):
        sl = slice(n * col_chunk, (n + 1) * col_chunk)
        out_ref[:, sl] += jnp.dot(a, wd_ref[:, sl].astype(BF16), preferred_element_type=F32)

    @pl.when(j == pl.num_programs(1) - 1)
    def _():
        gt = gt_ref[...]

        def body(c, carry):
            r0 = pl.multiple_of(c * row_chunk, row_chunk)
            y = 0.5 * out_ref[pl.ds(r0, row_chunk), :]
            r = lax.rsqrt(jnp.mean(y * y, axis=-1, keepdims=True) + EPS)
            if tail == "norm_out":
                out_ref[pl.ds(r0, row_chunk), :] = y
                h2_ref[pl.ds(r0, row_chunk), :] = (y * r * gt).astype(BF16)
            else:
                out_ref[pl.ds(r0, row_chunk), :] = y * r * gt
            return carry

        lax.fori_loop(0, tm // row_chunk, body, 0)
        if tail_rows:
            @pl.when(i == last)
            def _():
                outt_ref[...] = out_ref[pl.ds(head_rows, tail_rows), :]
                if h2t_ref is not None:
                    h2t_ref[...] = h2_ref[pl.ds(head_rows, tail_rows), :]


def _ffn(x, x_tail, gn, wg, wu, wd, gtail, *, tail, row_tiles, tf):
    n_x, d = x.shape
    f = wd.shape[0]
    tail_rows = x_tail.shape[0]
    tm = (n_x + tail_rows) // row_tiles
    ni, nj = row_tiles, f // tf
    assert ni * tm == n_x + tail_rows and nj * tf == f
    assert 0 < tail_rows < tm and (tm - tail_rows) % 16 == 0
    row_chunk = max(c for c in range(16, 81, 16) if tm % c == 0)
    kern = functools.partial(_ffn_kernel, tm=tm, d_model=d, col_chunk=512, row_chunk=row_chunk,
                             tail=tail, tail_rows=tail_rows)
    once = pl.Buffered(1)
    in_specs = [
        pl.BlockSpec(memory_space=pl.ANY),
        pl.BlockSpec(memory_space=pl.ANY),
        pl.BlockSpec((1, d), lambda i, j: (0, 0)),
        pl.BlockSpec((d, tf), lambda i, j: (0, j)),
        pl.BlockSpec((d, tf), lambda i, j: (0, j)),
        pl.BlockSpec((tf, d), lambda i, j: (j, 0)),
        pl.BlockSpec((1, d), lambda i, j: (0, 0)),
    ]
    args = [x, x_tail, gn, wg, wu, wd, gtail]
    out_specs = [pl.BlockSpec((tm, d), lambda i, j: (i, 0), pipeline_mode=once),
                 pl.BlockSpec((tail_rows, d), lambda i, j: (0, 0), pipeline_mode=once)]
    out_shape = [jax.ShapeDtypeStruct((n_x, d), F32), jax.ShapeDtypeStruct((tail_rows, d), F32)]
    scratch = [pltpu.VMEM((tm, d), BF16)]
    if tail == "norm_out":
        out_specs += [pl.BlockSpec((tm, d), lambda i, j: (i, 0), pipeline_mode=once),
                      pl.BlockSpec((tail_rows, d), lambda i, j: (0, 0), pipeline_mode=once)]
        out_shape += [jax.ShapeDtypeStruct((n_x, d), BF16), jax.ShapeDtypeStruct((tail_rows, d), BF16)]
        scratch = []
    return pl.pallas_call(
        kern,
        grid=(ni, nj),
        in_specs=in_specs,
        out_specs=out_specs,
        out_shape=out_shape,
        scratch_shapes=scratch + [pltpu.SemaphoreType.DMA, pltpu.SemaphoreType.DMA,
                                  pltpu.SemaphoreType.DMA((tm // row_chunk,))],
        compiler_params=_cparams(("arbitrary", "arbitrary")),
        name="ffn_" + tail,
    )(*args)


def _mm_epilogue(kind, accs, extras):
    if kind == "identity":
        return accs[0]
    if kind == "gelu":
        return _gelu(accs[0])
    if kind == "sigmoid":
        return _sigmoid(accs[0])
    if kind == "gated_glu":
        return extras[0].astype(F32) * (accs[0] * _sigmoid(accs[1]))
    if kind == "residual":
        return extras[0].astype(F32) + accs[0]
    raise ValueError(kind)


def _mm_kernel(*refs, n_w, n_extra, kind, emit):
    lhs_ref = refs[0]
    w_refs = refs[1:1 + n_w]
    extra_refs = refs[1 + n_w:1 + n_w + n_extra]
    out_ref = refs[1 + n_w + n_extra]
    emit_refs = refs[2 + n_w + n_extra:] if emit else [None] * n_w
    lhs = lhs_ref[...]
    accs = [jnp.dot(lhs, _as_bf16(w, e), preferred_element_type=F32) for w, e in zip(w_refs, emit_refs)]
    out_ref[...] = _mm_epilogue(kind, accs, [e[...] for e in extra_refs]).astype(out_ref.dtype)


def _mm(lhs, weights, extras, *, kind, n_out, out_dtype, tm, tn, emit, name):
    m, k = lhs.shape
    assert not emit or m == tm
    in_specs = [pl.BlockSpec((tm, k), lambda i, j: (i, 0))]
    args = [lhs]
    for w, c0 in weights:
        in_specs.append(_col_tile_spec(w, k, tn, c0))
        args.append(w)
    for e, c0 in extras:
        in_specs.append(pl.BlockSpec((tm, tn), lambda i, j, o=c0 // tn: (i, j + o)))
        args.append(e)
    out_specs = [pl.BlockSpec((tm, tn), lambda i, j: (i, j))]
    out_shape = [jax.ShapeDtypeStruct((m, n_out), out_dtype)]
    if emit:
        out_specs += [pl.BlockSpec((1, k, tn), lambda i, j: (j, 0, 0)) for _ in weights]
        out_shape += [jax.ShapeDtypeStruct((n_out // tn, k, tn), BF16) for _ in weights]
    kern = functools.partial(_mm_kernel, n_w=len(weights), n_extra=len(extras), kind=kind, emit=emit)
    res = pl.pallas_call(
        kern,
        grid=(m // tm, n_out // tn),
        in_specs=in_specs,
        out_specs=out_specs,
        out_shape=out_shape,
        compiler_params=_cparams(("arbitrary", "arbitrary")),
        name=name + ("_emit" if emit else ""),
    )(*args)
    return res[0], list(res[1:])


def _zoh(lam_re, lam_im, log_dt):
    dt = jnp.exp(log_dt)
    mag = jnp.exp(dt * lam_re)
    a_re = mag * jnp.cos(dt * lam_im)
    a_im = mag * jnp.sin(dt * lam_im)
    nr, ni = a_re - 1.0, a_im
    den = lam_re * lam_re + lam_im * lam_im
    coef_re = (nr * lam_re + ni * lam_im) / den
    coef_im = (ni * lam_re - nr * lam_im) / den
    return a_re, a_im, coef_re, coef_im


def _spread_lanes(x, reps):
    w = x.shape[1]
    assert w & (w - 1) == 0
    src = lax.broadcasted_iota(jnp.int32, (w, w * reps), 0)
    dst = lax.broadcasted_iota(jnp.int32, (w, w * reps), 1)
    sel = ((dst & (w - 1)) == src).astype(BF16)
    return jnp.dot(x.astype(BF16), sel, preferred_element_type=F32)


def _disc_kernel(lre_ref, lim_ref, ldt_ref, lre_col, lim_col, ldt_col, bre_ref, bim_ref, cre_ref, cim_ref,
                 are_ref, aim_ref, bw_ref, cw_ref):
    a_re, a_im, _, _ = _zoh(lre_ref[...], lim_ref[...], ldt_ref[...])
    are_ref[...] = a_re
    aim_ref[...] = a_im

    _, _, coef_re, coef_im = _zoh(lre_col[...], lim_col[...], ldt_col[...])
    b_re = bre_ref[...]
    b_im = bim_ref[...]
    bb_re = coef_re * b_re - coef_im * b_im
    bb_im = coef_re * b_im + coef_im * b_re
    state_shift = SSM_STATE.bit_length() - 1
    chan_shift = SSM_GROUP.bit_length() - 1
    row_g = lax.broadcasted_iota(jnp.int32, (SB, UB), 0) >> state_shift
    col_g = lax.broadcasted_iota(jnp.int32, (SB, UB), 1) >> chan_shift
    diag = row_g == col_g
    for part, bb in enumerate((bb_re, bb_im)):
        wt = jnp.where(diag, _spread_lanes(bb, GROUPS_PER_BLOCK), 0.0)
        bw_ref[0, :, part * SB:(part + 1) * SB] = wt.T.astype(BF16)

    row_g = lax.broadcasted_iota(jnp.int32, (UB, SB), 0) >> chan_shift
    col_g = lax.broadcasted_iota(jnp.int32, (UB, SB), 1) >> state_shift
    diag = row_g == col_g
    for part, c in enumerate((cre_ref[...], -cim_ref[...])):
        wt = jnp.where(diag, _spread_lanes(c, GROUPS_PER_BLOCK), 0.0)
        cw_ref[0, part * SB:(part + 1) * SB, :] = wt.T.astype(BF16)


def _discretise(lam_re, lam_im, log_dt, b_re, b_im, c_re, c_im):
    g, p = lam_re.shape
    h = b_re.shape[-1]
    nb = g // GROUPS_PER_BLOCK
    col = lambda x: x.reshape(g * p, 1)
    blk = lambda rows, cols: pl.BlockSpec((rows, cols), lambda i: (i, 0))
    return pl.pallas_call(
        _disc_kernel,
        grid=(nb,),
        in_specs=[blk(GROUPS_PER_BLOCK, p), blk(GROUPS_PER_BLOCK, p), blk(GROUPS_PER_BLOCK, 1),
                  blk(SB, 1), blk(SB, 1), blk(SB, 1),
                  blk(SB, h), blk(SB, h), blk(UB, p), blk(UB, p)],
        out_specs=(blk(GROUPS_PER_BLOCK, p), blk(GROUPS_PER_BLOCK, p),
                   pl.BlockSpec((1, UB, 2 * SB), lambda i: (i, 0, 0)),
                   pl.BlockSpec((1, 2 * SB, UB), lambda i: (i, 0, 0))),
        out_shape=(jax.ShapeDtypeStruct((g, p), F32), jax.ShapeDtypeStruct((g, p), F32),
                   jax.ShapeDtypeStruct((nb, UB, 2 * SB), BF16),
                   jax.ShapeDtypeStruct((nb, 2 * SB, UB), BF16)),
        compiler_params=_cparams(("arbitrary",)),
        name="s5_discretise",
    )(lam_re, lam_im, log_dt.reshape(g, 1),
      col(lam_re), col(lam_im), col(jnp.repeat(log_dt, p)),
      b_re.reshape(g * p, h), b_im.reshape(g * p, h), c_re.reshape(g * h, p), c_im.reshape(g * h, p))


def _ssm_prompt_kernel(*refs, nbatch):
    parity = pl.program_id(1) % 2
    for fill in (0, 1):
        pl.when(parity == fill)(functools.partial(_ssm_prompt_step, *refs, nbatch=nbatch, fill=fill))


def _ssm_prompt_step(un_ref, u_ref, bw_ref, cw_ref, are_ref, aim_ref, d_ref,
                     y_ref, sre_ref, sim_ref,
                     bure2, buim2, st_re, st_im, *, nbatch, fill):
    tc = pl.program_id(1)
    scan = 1 - fill

    @pl.when(tc == 0)
    def _():
        st_re[...] = jnp.zeros_like(st_re)
        st_im[...] = jnp.zeros_like(st_im)
        bure2[1] = jnp.zeros(bure2.shape[1:], F32)
        buim2[1] = jnp.zeros(buim2.shape[1:], F32)

    bure_f, buim_f = bure2.at[fill], buim2.at[fill]
    bure, buim = bure2.at[scan], buim2.at[scan]

    pack = 8 // nbatch
    nq = SLABS // pack

    def slab_rows(c, b):
        part, c = (0, c) if c < SLABS else (1, c - SLABS)
        v = (c // nq) * nbatch + b
        r0 = v * SCAN_PITCH + (v % 2) * SCAN_SKEW
        return part, c % nq, slice(r0, r0 + SCAN_T)

    for ref in (bure_f, buim_f):
        for q in range(nq):
            for v in range(0, 8, 2):
                ref[q, v * SCAN_PITCH + SCAN_T:v * SCAN_PITCH + SCAN_T + 2 * SCAN_SKEW, :] = (
                    jnp.zeros((2 * SCAN_SKEW, LANES), F32))

    u_bf = un_ref[...].reshape(nbatch * SCAN_T, UB).astype(BF16)
    per_dot = 4
    for n in range(2 * SLABS // per_dot):
        bu = jnp.dot(u_bf, bw_ref[0, :, per_dot * n * LANES:per_dot * (n + 1) * LANES],
                     preferred_element_type=F32)
        for h in range(per_dot):
            for b in range(nbatch):
                part, q, rows = slab_rows(per_dot * n + h, b)
                (bure_f, buim_f)[part][q, rows, :] = bu[b * SCAN_T:(b + 1) * SCAN_T, h * LANES:(h + 1) * LANES]

    def packed(a_ref, q):
        return jnp.concatenate(
            [jnp.broadcast_to(a_ref[0, q + h * nq], (nbatch, LANES)) for h in range(pack)], axis=0)

    a_re = [packed(are_ref, q) for q in range(nq)]
    a_im = [packed(aim_ref, q) for q in range(nq)]
    init = tuple(st_re[q] for q in range(nq)) + tuple(st_im[q] for q in range(nq))

    def step(t, carry, active=None):
        rows = pl.ds(t, pack * nbatch, stride=SCAN_PITCH)
        new_re, new_im = [], []
        for q in range(nq):
            s_r, s_i = carry[q], carry[nq + q]
            b_r, b_i = bure[q, rows, :], buim[q, rows, :]
            n_r = a_re[q] * s_r - a_im[q] * s_i + b_r
            n_i = a_re[q] * s_i + a_im[q] * s_r + b_i
            if active is not None:
                b_r, b_i = jnp.where(active, n_r, b_r), jnp.where(active, n_i, b_i)
                n_r, n_i = jnp.where(active, n_r, s_r), jnp.where(active, n_i, s_i)
                bure[q, rows, :] = b_r
                buim[q, rows, :] = b_i
            else:
                bure[q, rows, :] = n_r
                buim[q, rows, :] = n_i
            new_re.append(n_r)
            new_im.append(n_i)
        return tuple(new_re) + tuple(new_im)

    odd = (lax.broadcasted_iota(jnp.int32, (8, LANES), 0) & 1) == 1
    carry = init
    for t in range(SCAN_SKEW):
        carry = step(t, carry, active=jnp.logical_not(odd))
    for t in range(SCAN_SKEW, SCAN_T):
        carry = step(t, carry)
    for t in range(SCAN_T, SCAN_T + SCAN_SKEW):
        carry = step(t, carry, active=odd)
    fin = carry
    for q in range(nq):
        st_re[q] = fin[q]
        st_im[q] = fin[nq + q]

    d = d_ref[0]
    for b in range(nbatch):
        y = d * u_ref[b]
        for n in range(SLABS):
            (p0, q0, rows0), (p1, q1, rows1) = slab_rows(2 * n, b), slab_rows(2 * n + 1, b)
            s_pair = jnp.concatenate([(bure, buim)[p0][q0, rows0, :], (bure, buim)[p1][q1, rows1, :]],
                                     axis=1).astype(BF16)
            y = y + jnp.dot(s_pair, cw_ref[0, 2 * n * LANES:(2 * n + 2) * LANES, :],
                            preferred_element_type=F32)
        y_ref[b] = _gelu(y).astype(y_ref.dtype)

    @pl.when(tc == pl.num_programs(1) - 1)
    def _():
        for c in range(SLABS):
            q, h = c % nq, c // nq
            sre_ref[:, c * LANES:(c + 1) * LANES] = st_re[q, h * nbatch:(h + 1) * nbatch, :]
            sim_ref[:, c * LANES:(c + 1) * LANES] = st_im[q, h * nbatch:(h + 1) * nbatch, :]


def _ssm_prompt(z3, bw, cw, a_re, a_im, d_skip):
    nbatch, seq, width = z3.shape
    nb = bw.shape[0]
    states = nb * SB
    kern = functools.partial(_ssm_prompt_kernel, nbatch=nbatch)
    nt = seq // SCAN_T
    assert 8 % nbatch == 0
    return pl.pallas_call(
        kern,
        grid=(nb, nt + 1),
        in_specs=[
            pl.BlockSpec((nbatch, SCAN_T, UB), lambda g, t: (0, jnp.minimum(t, nt - 1), g)),
            pl.BlockSpec((nbatch, SCAN_T, UB), lambda g, t: (0, jnp.maximum(t - 1, 0), g)),
            pl.BlockSpec((1, UB, 2 * SB), lambda g, t: (g, 0, 0)),
            pl.BlockSpec((1, 2 * SB, UB), lambda g, t: (g, 0, 0)),
            pl.BlockSpec((1, SLABS, 1, LANES), lambda g, t: (g, 0, 0, 0)),
            pl.BlockSpec((1, SLABS, 1, LANES), lambda g, t: (g, 0, 0, 0)),
            pl.BlockSpec((1, 1, UB), lambda g, t: (g, 0, 0)),
        ],
        out_specs=(
            pl.BlockSpec((nbatch, SCAN_T, UB), lambda g, t: (0, jnp.maximum(t - 1, 0), g)),
            pl.BlockSpec((nbatch, SB), lambda g, t: (0, g)),
            pl.BlockSpec((nbatch, SB), lambda g, t: (0, g)),
        ),
        out_shape=(
            jax.ShapeDtypeStruct((nbatch, seq, width), BF16),
            jax.ShapeDtypeStruct((nbatch, states), F32),
            jax.ShapeDtypeStruct((nbatch, states), F32),
        ),
        scratch_shapes=[
            pltpu.VMEM((2, SLABS * nbatch // 8, 8 * SCAN_PITCH, LANES), F32),
            pltpu.VMEM((2, SLABS * nbatch // 8, 8 * SCAN_PITCH, LANES), F32),
            pltpu.VMEM((SLABS * nbatch // 8, 8, LANES), F32),
            pltpu.VMEM((SLABS * nbatch // 8, 8, LANES), F32),
        ],
        compiler_params=_cparams(("arbitrary", "arbitrary")),
        name="ssm_prompt",
    )(z3, z3, bw, cw, a_re.reshape(nb, SLABS, 1, LANES), a_im.reshape(nb, SLABS, 1, LANES),
      d_skip.reshape(nb, 1, UB))


def _ssm_sample_kernel(u_ref, bw_ref, cw_ref, are_ref, aim_ref, d_ref, x0re_ref, x0im_ref,
                       y_ref, sre_ref, sim_ref):
    u = u_ref[...]
    bu = jnp.dot(u.astype(BF16), bw_ref[0], preferred_element_type=F32)
    a_re = are_ref[0]
    a_im = aim_ref[0]
    x_re = x0re_ref[...]
    x_im = x0im_ref[...]
    s_re = a_re * x_re - a_im * x_im + bu[:, :SB]
    s_im = a_re * x_im + a_im * x_re + bu[:, SB:]
    sre_ref[...] = s_re
    sim_ref[...] = s_im
    s_cat = jnp.concatenate([s_re, s_im], axis=1).astype(BF16)
    y = jnp.dot(s_cat, cw_ref[0], preferred_element_type=F32) + d_ref[0] * u
    y_ref[...] = _gelu(y).astype(y_ref.dtype)


def _ssm_sample(z, bw, cw, a_re, a_im, d_skip, x0_re, x0_im):
    nbatch = z.shape[0]
    nb = bw.shape[0]
    return pl.pallas_call(
        _ssm_sample_kernel,
        grid=(nb,),
        in_specs=[
            pl.BlockSpec((nbatch, UB), lambda g: (0, g)),
            pl.BlockSpec((1, UB, 2 * SB), lambda g: (g, 0, 0)),
            pl.BlockSpec((1, 2 * SB, UB), lambda g: (g, 0, 0)),
            pl.BlockSpec((1, 1, SB), lambda g: (g, 0, 0)),
            pl.BlockSpec((1, 1, SB), lambda g: (g, 0, 0)),
            pl.BlockSpec((1, 1, UB), lambda g: (g, 0, 0)),
            pl.BlockSpec((nbatch, SB), lambda g: (0, g)),
            pl.BlockSpec((nbatch, SB), lambda g: (0, g)),
        ],
        out_specs=(
            pl.BlockSpec((nbatch, UB), lambda g: (0, g)),
            pl.BlockSpec((nbatch, SB), lambda g: (0, g)),
            pl.BlockSpec((nbatch, SB), lambda g: (0, g)),
        ),
        out_shape=(
            jax.ShapeDtypeStruct((nbatch, nb * UB), BF16),
            jax.ShapeDtypeStruct((nbatch, nb * SB), F32),
            jax.ShapeDtypeStruct((nbatch, nb * SB), F32),
        ),
        compiler_params=_cparams(("arbitrary",)),
        name="ssm_sample",
    )(z, bw, cw, a_re.reshape(nb, 1, SB), a_im.reshape(nb, 1, SB), d_skip.reshape(nb, 1, UB),
      x0_re, x0_im)


def _gmlp_prompt_kernel(gu_ref, gv_ref, gnv_ref, ws_ref, bias_ref, wo_ref, gate_ref, p_ref,
                        out_ref, s_scr, wt_scr, *, tm, heads):
    i = pl.program_id(0)
    j = pl.program_id(1)

    @pl.when(jnp.logical_and(i == 0, j == 0))
    def _():
        row = lax.broadcasted_iota(jnp.int32, (CHUNK, CHUNK), 0)
        col = lax.broadcasted_iota(jnp.int32, (CHUNK, CHUNK), 1)
        mask = (col <= row).astype(F32)
        for g in range(heads):
            wt_scr[g] = (ws_ref[g] * mask).astype(BF16)

    @pl.when(j == 0)
    def _():
        gnv = gnv_ref[...]

        def body(c, carry):
            r0 = pl.multiple_of(c * CHUNK, CHUNK)
            gv = gv_ref[pl.ds(r0, CHUNK), :].astype(F32)
            r = lax.rsqrt(jnp.mean(gv * gv, axis=-1, keepdims=True) + EPS)
            v = (gv * r * gnv).astype(BF16)
            for g in range(heads):
                sl = slice(g * GMLP_HEAD, (g + 1) * GMLP_HEAD)
                mixed = jnp.dot(wt_scr[g], v[:, sl], preferred_element_type=F32) + bias_ref[:, sl]
                gu = gu_ref[pl.ds(r0, CHUNK), sl].astype(F32)
                s_scr[pl.ds(r0, CHUNK), sl] = (gu * mixed).astype(BF16)
            return carry

        lax.fori_loop(0, tm // CHUNK, body, 0)

    yb = jnp.dot(s_scr[...], _as_bf16(wo_ref, None), preferred_element_type=F32)
    out_ref[...] = (p_ref[...].astype(F32) + gate_ref[...].astype(F32) * yb).astype(out_ref.dtype)


def _gmlp_prompt(guv, gnv, w_s, bias_full, w_gout, gates, p, *, gate_col, tm, tn):
    m = guv.shape[0]
    width, d = w_gout.shape
    heads = w_s.shape[0]
    kern = functools.partial(_gmlp_prompt_kernel, tm=tm, heads=heads)
    return pl.pallas_call(
        kern,
        grid=(m // tm, d // tn),
        in_specs=[
            pl.BlockSpec((tm, width), lambda i, j: (i, 0)),
            pl.BlockSpec((tm, width), lambda i, j: (i, 1)),
            pl.BlockSpec((1, width), lambda i, j: (0, 0)),
            pl.BlockSpec((heads, CHUNK, CHUNK), lambda i, j: (0, 0, 0)),
            pl.BlockSpec((CHUNK, width), lambda i, j: (0, 0)),
            _col_tile_spec(w_gout, width, tn),
            pl.BlockSpec((tm, tn), lambda i, j: (i, j + gate_col // tn)),
            pl.BlockSpec((tm, tn), lambda i, j: (i, j)),
        ],
        out_specs=pl.BlockSpec((tm, tn), lambda i, j: (i, j)),
        out_shape=jax.ShapeDtypeStruct((m, d), BF16),
        scratch_shapes=[pltpu.VMEM((tm, width), BF16), pltpu.VMEM((heads, CHUNK, CHUNK), BF16)],
        compiler_params=_cparams(("arbitrary", "arbitrary")),
        name="gmlp_prompt",
    )(guv, guv, gnv, w_s, bias_full, w_gout, gates, p)


def _gmlp_sample_kernel(gu_ref, gv_ref, gnv_ref, wdiag_ref, bias_ref, wo_ref, gate_ref, p_ref,
                        out_ref, v_ref, s_scr):
    j = pl.program_id(0)

    @pl.when(j == 0)
    def _():
        gv = gv_ref[...]
        r = lax.rsqrt(jnp.mean(gv * gv, axis=-1, keepdims=True) + EPS)
        v = gv * r * gnv_ref[...]
        v_ref[...] = v
        mixed = wdiag_ref[...] * v + bias_ref[...]
        s_scr[...] = (gu_ref[...] * mixed).astype(BF16)

    yb = jnp.dot(s_scr[...], _as_bf16(wo_ref, None), preferred_element_type=F32)
    out_ref[...] = (p_ref[...] + gate_ref[...] * yb).astype(out_ref.dtype)


def _gmlp_sample(guv, gnv, wdiag, bias0, w_gout, gates, p, *, gate_col, tn):
    m = guv.shape[0]
    width, d = w_gout.shape
    return pl.pallas_call(
        _gmlp_sample_kernel,
        grid=(d // tn,),
        in_specs=[
            pl.BlockSpec((m, width), lambda j: (0, 0)),
            pl.BlockSpec((m, width), lambda j: (0, 1)),
            pl.BlockSpec((1, width), lambda j: (0, 0)),
            pl.BlockSpec((1, width), lambda j: (0, 0)),
            pl.BlockSpec((1, width), lambda j: (0, 0)),
            pl.BlockSpec((width, tn), lambda j: (0, j)),
            pl.BlockSpec((m, tn), lambda j: (0, j + gate_col // tn)),
            pl.BlockSpec((m, tn), lambda j: (0, j)),
        ],
        out_specs=(pl.BlockSpec((m, tn), lambda j: (0, j)),
                   pl.BlockSpec((m, width), lambda j: (0, 0))),
        out_shape=(jax.ShapeDtypeStruct((m, d), BF16), jax.ShapeDtypeStruct((m, width), F32)),
        scratch_shapes=[pltpu.VMEM((m, width), BF16)],
        compiler_params=_cparams(("arbitrary",)),
        name="gmlp_sample",
    )(guv, guv, gnv, wdiag, bias0, w_gout, gates, p)


def _mixer(x, h2, prm, w, *, nbatch, x0, tm_mm):
    sample = x0 is not None
    d_model = x.shape[1]
    ssm_width = prm["bw"].shape[0] * UB
    gmlp_width = prm["gnv"].shape[1]
    o1 = ssm_width
    o3 = o1 + 2 * gmlp_width
    wide, narrow = 1024, 512

    w = dict(w, win_ssm=w["w_in"], win_uv=w["w_in"], win_gate=w["w_in"], glu_a=w["w_glu"], glu_b=w["w_glu"])
    cols = dict(win_uv=o1, win_gate=o3, glu_b=d_model)

    def mm(lhs, wkeys, extras, kind, n_out, out_dtype, name, tn=wide):
        out, _ = _mm(lhs, [(w[k], cols.get(k, 0)) for k in wkeys], extras, kind=kind, n_out=n_out,
                     out_dtype=out_dtype, tm=tm_mm, tn=wide if sample else tn, emit=False, name=name)
        return out

    act = F32 if sample else BF16
    z_ssm = mm(h2, ["win_ssm"], [], "identity", o1, F32, "w_in_ssm", tn=narrow)
    guv = mm(h2, ["win_uv"], [], "gelu", 2 * gmlp_width, act, "w_in_uv")
    gates = mm(h2, ["win_gate"], [], "sigmoid", 2 * d_model, act, "w_in_gate")

    if sample:
        y, s_re, s_im = _ssm_sample(z_ssm, prm["bw"], prm["cw"], prm["a_re"], prm["a_im"],
                                    prm["d_skip"], x0[0], x0[1])
    else:
        seq = x.shape[0] // nbatch
        y, s_re, s_im = _ssm_prompt(z_ssm.reshape(nbatch, seq, ssm_width), prm["bw"], prm["cw"],
                                    prm["a_re"], prm["a_im"], prm["d_skip"])
        y = y.reshape(nbatch * seq, ssm_width)
    p = mm(y, ["glu_a", "glu_b"], [(gates, 0)], "gated_glu", d_model, act, "ssm_glu")

    if sample:
        merged, v_rows = _gmlp_sample(guv, prm["gnv"], prm["w_diag"], prm["bias0"],
                                      w["w_gout"], gates, p, gate_col=d_model, tn=wide)
    else:
        merged = _gmlp_prompt(guv, prm["gnv"], prm["w_s"], prm["bias_full"], w["w_gout"], gates, p,
                              gate_col=d_model, tm=tm_mm, tn=wide)
        v_rows = None
    x = mm(merged, ["w_out"], [(x, 0)], "residual", d_model, F32, "out_proj", tn=narrow)
    return x, s_re, s_im, v_rows


def kernel(x_prompt, x_sample, state_ssm_re, state_ssm_im, norm_ffn1, ffn1_gate, ffn1_up, ffn1_down, norm_mix, w_in, ssm_lambda_re, ssm_lambda_im, ssm_log_dt, ssm_b_re, ssm_b_im, ssm_c_re, ssm_c_im, ssm_d, ssm_w_glu, gmlp_norm_v, gmlp_w_s, gmlp_b_s, gmlp_w_out, w_out, norm_ffn2, ffn2_gate, ffn2_up, ffn2_down, norm_final):
    depth = w_in.shape[0]
    assert depth == 1, "the final RMSNorm is fused into the last FFN; one layer per step"
    batch, seq, d_model = x_prompt.shape
    dec_batch, dec_seq, _ = x_sample.shape
    assert dec_seq == 1
    groups, states = ssm_lambda_re.shape[1:]
    gmlp_width = gmlp_norm_v.shape[1]
    heads = gmlp_w_s.shape[1]
    head_dim = gmlp_width // heads
    assert seq % SCAN_T == 0 and seq % CHUNK == 0 and head_dim == GMLP_HEAD
    assert ssm_b_re.shape[-1] == SSM_GROUP and states == SSM_STATE and groups % GROUPS_PER_BLOCK == 0

    l = 0
    yp = x_prompt.reshape(batch * seq, d_model)
    ys = x_sample.reshape(dec_batch, d_model)
    gfin = norm_final.reshape(1, d_model)
    a_re, a_im, bw, cw = _discretise(ssm_lambda_re[l], ssm_lambda_im[l], ssm_log_dt[l],
                                     ssm_b_re[l], ssm_b_im[l], ssm_c_re[l], ssm_c_im[l])
    prm = dict(
        n1=norm_ffn1[l].reshape(1, d_model), nmix=norm_mix[l].reshape(1, d_model),
        n2=norm_ffn2[l].reshape(1, d_model),
        bw=bw, cw=cw, a_re=a_re, a_im=a_im, d_skip=ssm_d[l],
        gnv=gmlp_norm_v[l].reshape(1, gmlp_width), w_s=gmlp_w_s[l],
        bias_full=jnp.repeat(gmlp_b_s[l].T, head_dim, axis=1),
        w_diag=jnp.repeat(gmlp_w_s[l][:, 0, 0], head_dim).reshape(1, gmlp_width),
        bias0=jnp.repeat(gmlp_b_s[l][:, 0], head_dim).reshape(1, gmlp_width),
    )
    w_f32 = dict(w_in=w_in[l], w_glu=ssm_w_glu[l], w_gout=gmlp_w_out[l], w_out=w_out[l])
    x0 = (state_ssm_re[l].reshape(dec_batch, groups * states),
          state_ssm_im[l].reshape(dec_batch, groups * states))
    tm, tf, row_tiles = 1024, 256, 8

    yp, ys, hp, hs = _ffn(yp, ys, prm["n1"], ffn1_gate[l], ffn1_up[l], ffn1_down[l], prm["nmix"],
                          tail="norm_out", row_tiles=row_tiles, tf=tf)
    ys, sr, si, vr = _mixer(ys, hs, prm, w_f32, nbatch=dec_batch, x0=x0, tm_mm=dec_batch)
    yp, pr, pi, _ = _mixer(yp, hp, prm, w_f32, nbatch=batch, x0=None, tm_mm=tm)
    yp, ys = _ffn(yp, ys, prm["n2"], ffn2_gate[l], ffn2_up[l], ffn2_down[l], gfin,
                  tail="final", row_tiles=row_tiles, tf=tf)
    return (yp.reshape(batch, seq, d_model), ys.reshape(dec_batch, dec_seq, d_model),
            pr.reshape(1, batch, groups, states), pi.reshape(1, batch, groups, states),
            sr.reshape(1, dec_batch, groups, states), si.reshape(1, dec_batch, groups, states),
            vr.reshape(1, dec_batch, dec_seq, gmlp_width))
```

```python
import functools
import math

import jax
import jax.numpy as jnp
from jax import lax
from jax.experimental import pallas as pl
from jax.experimental.pallas import tpu as pltpu

F32 = jnp.float32
BF16 = jnp.bfloat16

EPS = 1e-6
LANES = 128
SSM_GROUP = 16
SSM_STATE = 64
GROUPS_PER_BLOCK = 16
UB = GROUPS_PER_BLOCK * SSM_GROUP
SB = GROUPS_PER_BLOCK * SSM_STATE
SLABS = SB // LANES
SCAN_T = 256
SCAN_SKEW = 4
SCAN_PITCH = SCAN_T + SCAN_SKEW
CHUNK = 128
GMLP_HEAD = 128
VMEM_LIMIT = 62 * 1024 * 1024


def _cparams(sem):
    return pltpu.CompilerParams(dimension_semantics=sem, vmem_limit_bytes=VMEM_LIMIT)


def _gelu(x):
    c = math.sqrt(2.0 / math.pi)
    return 0.5 * x * (1.0 + jnp.tanh(c * (x + 0.044715 * (x * x * x))))


def _sigmoid(x):
    return 1.0 / (1.0 + jnp.exp(-x))


def _as_bf16(w_ref, emit_ref):
    w = w_ref[0] if len(w_ref.shape) == 3 else w_ref[...]
    if w.dtype != BF16:
        w = w.astype(BF16)
    if emit_ref is not None:
        if len(emit_ref.shape) == 3:
            emit_ref[0] = w
        else:
            emit_ref[...] = w
    return w


def _col_tile_spec(w, k, tn, col0=0):
    if w.ndim == 3:
        assert w.shape[1:] == (k, tn) and col0 == 0
        return pl.BlockSpec((1, k, tn), lambda i, j: (j, 0, 0))
    return pl.BlockSpec((k, tn), lambda i, j, o=col0 // tn: (0, j + o))


def _ffn_kernel(*refs, tm, d_model, col_chunk, row_chunk, tail, tail_rows):
    refs = list(refs)
    x_hbm = refs.pop(0)
    xt_hbm = refs.pop(0) if tail_rows else None
    gn_ref, wg_ref, wu_ref, wd_ref, gt_ref = refs[:5]
    del refs[:5]
    out_hbm, outt_hbm = refs.pop(0), refs.pop(0)
    h2_hbm, h2t_hbm = (refs.pop(0), refs.pop(0)) if tail == "norm_out" else (None, None)
    out_ref, h_buf, x_sem, xt_sem, chunk_sems, out_sems, h2_sems = refs
    h2_ref = h_buf
    i = pl.program_id(0)
    j = pl.program_id(1)
    last = pl.num_programs(0) - 1
    head_rows = tm - tail_rows

    def copy_in(src, dst, sem):
        cp = pltpu.make_async_copy(src, dst, sem)
        cp.start()
        return cp

    @pl.when(j == 0)
    def _():
        g = gn_ref[...]
        n_chunks = tm // row_chunk

        def chunk_copy(c):
            r0 = pl.multiple_of(c * row_chunk, row_chunk)
            return pltpu.make_async_copy(x_hbm.at[pl.ds(i * tm + r0, row_chunk), :],
                                         out_ref.at[pl.ds(r0, row_chunk), :], chunk_sems.at[c])

        def norm_rows(c, chunked):
            r0 = pl.multiple_of(c * row_chunk, row_chunk)
            if chunked:
                chunk_copy(c).wait()
            xf = out_ref[pl.ds(r0, row_chunk), :]
            r = lax.rsqrt(jnp.mean(xf * xf, axis=-1, keepdims=True) + EPS)
            h_buf[pl.ds(r0, row_chunk), :] = (xf * r * g).astype(BF16)
            out_ref[pl.ds(r0, row_chunk), :] = 2.0 * xf

        @pl.when(i < last)
        def _():
            def start(c, carry):
                chunk_copy(c).start()
                return carry

            lax.fori_loop(0, n_chunks, start, 0)
            lax.fori_loop(0, n_chunks, lambda c, carry: (norm_rows(c, True), carry)[1], 0)

        @pl.when(i == last)
        def _():
            n_x = x_hbm.shape[0]
            head = copy_in(x_hbm.at[pl.ds(n_x - head_rows, head_rows), :],
                           out_ref.at[pl.ds(0, head_rows), :], x_sem)
            rest = copy_in(xt_hbm, out_ref.at[pl.ds(head_rows, tail_rows), :], xt_sem)
            head.wait()
            rest.wait()
            lax.fori_loop(0, n_chunks, lambda c, carry: (norm_rows(c, False), carry)[1], 0)

    h = h_buf[...]
    g = jnp.dot(h, _as_bf16(wg_ref, None), preferred_element_type=F32)
    u = jnp.dot(h, _as_bf16(wu_ref, None), preferred_element_type=F32)
    a = (g * _sigmoid(g) * u).astype(BF16)
    for n in range(d_model // col_chunk):
        sl = slice(n * col_chunk, (n + 1) * col_chunk)
        out_ref[:, sl] += jnp.dot(a, wd_ref[:, sl].astype(BF16), preferred_element_type=F32)

    @pl.when(j == pl.num_programs(1) - 1)
    def _():
        gt = gt_ref[...]
        n_chunks = tm // row_chunk
        results = [(out_ref, out_hbm, outt_hbm, out_sems)]
        if tail == "norm_out":
            results.append((h2_ref, h2_hbm, h2t_hbm, h2_sems))

        def finish_rows(c):
            r0 = pl.multiple_of(c * row_chunk, row_chunk)
            y = 0.5 * out_ref[pl.ds(r0, row_chunk), :]
            r = lax.rsqrt(jnp.mean(y * y, axis=-1, keepdims=True) + EPS)
            if tail == "norm_out":
                out_ref[pl.ds(r0, row_chunk), :] = y
                h2_ref[pl.ds(r0, row_chunk), :] = (y * r * gt).astype(BF16)
            else:
                out_ref[pl.ds(r0, row_chunk), :] = y * r * gt

        def chunk_out(c, src, dst, sems):
            r0 = pl.multiple_of(c * row_chunk, row_chunk)
            return pltpu.make_async_copy(src.at[pl.ds(r0, row_chunk), :],
                                         dst.at[pl.ds(i * tm + r0, row_chunk), :], sems.at[c])

        @pl.when(i < last)
        def _():
            def finish(c, carry):
                finish_rows(c)
                for src, dst, _, sems in results:
                    chunk_out(c, src, dst, sems).start()
                return carry

            def drain(c, carry):
                for src, dst, _, sems in results:
                    chunk_out(c, src, dst, sems).wait()
                return carry

            lax.fori_loop(0, n_chunks, finish, 0)
            lax.fori_loop(0, n_chunks, drain, 0)

        @pl.when(i == last)
        def _():
            lax.fori_loop(0, n_chunks, lambda c, carry: (finish_rows(c), carry)[1], 0)
            copies = []
            for src, dst, dst_tail, sems in results:
                n_x = dst.shape[0]
                copies.append(copy_in(src.at[pl.ds(0, head_rows), :],
                                      dst.at[pl.ds(n_x - head_rows, head_rows), :], sems.at[0]))
                copies.append(copy_in(src.at[pl.ds(head_rows, tail_rows), :], dst_tail, sems.at[1]))
            for cp in copies:
                cp.wait()


def _ffn(x, x_tail, gn, wg, wu, wd, gtail, *, tail, row_tiles, tf):
    n_x, d = x.shape
    f = wd.shape[0]
    tail_rows = x_tail.shape[0]
    tm = (n_x + tail_rows) // row_tiles
    ni, nj = row_tiles, f // tf
    assert ni * tm == n_x + tail_rows and nj * tf == f
    assert 0 < tail_rows < tm and (tm - tail_rows) % 16 == 0
    row_chunk = max(c for c in range(16, 81, 16) if tm % c == 0)
    kern = functools.partial(_ffn_kernel, tm=tm, d_model=d, col_chunk=512, row_chunk=row_chunk,
                             tail=tail, tail_rows=tail_rows)
    in_specs = [
        pl.BlockSpec(memory_space=pl.ANY),
        pl.BlockSpec(memory_space=pl.ANY),
        pl.BlockSpec((1, d), lambda i, j: (0, 0)),
        pl.BlockSpec((d, tf), lambda i, j: (0, j)),
        pl.BlockSpec((d, tf), lambda i, j: (0, j)),
        pl.BlockSpec((tf, d), lambda i, j: (j, 0)),
        pl.BlockSpec((1, d), lambda i, j: (0, 0)),
    ]
    args = [x, x_tail, gn, wg, wu, wd, gtail]
    out_shape = [jax.ShapeDtypeStruct((n_x, d), F32), jax.ShapeDtypeStruct((tail_rows, d), F32)]
    if tail == "norm_out":
        out_shape += [jax.ShapeDtypeStruct((n_x, d), BF16), jax.ShapeDtypeStruct((tail_rows, d), BF16)]
    out_specs = [pl.BlockSpec(memory_space=pl.ANY)] * len(out_shape)
    n_chunks = tm // row_chunk
    assert n_chunks >= 2
    return pl.pallas_call(
        kern,
        grid=(ni, nj),
        in_specs=in_specs,
        out_specs=out_specs,
        out_shape=out_shape,
        scratch_shapes=[pltpu.VMEM((tm, d), F32), pltpu.VMEM((tm, d), BF16),
                        pltpu.SemaphoreType.DMA, pltpu.SemaphoreType.DMA,
                        pltpu.SemaphoreType.DMA((n_chunks,)), pltpu.SemaphoreType.DMA((n_chunks,)),
                        pltpu.SemaphoreType.DMA((n_chunks,))],
        compiler_params=_cparams(("arbitrary", "arbitrary")),
        name="ffn_" + tail,
    )(*args)


def _mm_epilogue(kind, accs, extras):
    if kind == "identity":
        return accs[0]
    if kind == "gelu":
        return _gelu(accs[0])
    if kind == "sigmoid":
        return _sigmoid(accs[0])
    if kind == "gated_glu":
        return extras[0].astype(F32) * (accs[0] * _sigmoid(accs[1]))
    if kind == "residual":
        return extras[0].astype(F32) + accs[0]
    raise ValueError(kind)


def _mm_kernel(*refs, n_w, n_extra, kind, emit):
    lhs_ref = refs[0]
    w_refs = refs[1:1 + n_w]
    extra_refs = refs[1 + n_w:1 + n_w + n_extra]
    out_ref = refs[1 + n_w + n_extra]
    emit_refs = refs[2 + n_w + n_extra:] if emit else [None] * n_w
    lhs = lhs_ref[...]
    accs = [jnp.dot(lhs, _as_bf16(w, e), preferred_element_type=F32) for w, e in zip(w_refs, emit_refs)]
    out_ref[...] = _mm_epilogue(kind, accs, [e[...] for e in extra_refs]).astype(out_ref.dtype)


def _mm(lhs, weights, extras, *, kind, n_out, out_dtype, tm, tn, emit, name):
    m, k = lhs.shape
    assert not emit or m == tm
    in_specs = [pl.BlockSpec((tm, k), lambda i, j: (i, 0))]
    args = [lhs]
    for w, c0 in weights:
        in_specs.append(_col_tile_spec(w, k, tn, c0))
        args.append(w)
    for e, c0 in extras:
        in_specs.append(pl.BlockSpec((tm, tn), lambda i, j, o=c0 // tn: (i, j + o)))
        args.append(e)
    out_specs = [pl.BlockSpec((tm, tn), lambda i, j: (i, j))]
    out_shape = [jax.ShapeDtypeStruct((m, n_out), out_dtype)]
    if emit:
        out_specs += [pl.BlockSpec((1, k, tn), lambda i, j: (j, 0, 0)) for _ in weights]
        out_shape += [jax.ShapeDtypeStruct((n_out // tn, k, tn), BF16) for _ in weights]
    kern = functools.partial(_mm_kernel, n_w=len(weights), n_extra=len(extras), kind=kind, emit=emit)
    res = pl.pallas_call(
        kern,
        grid=(m // tm, n_out // tn),
        in_specs=in_specs,
        out_specs=out_specs,
        out_shape=out_shape,
        compiler_params=_cparams(("arbitrary", "arbitrary")),
        name=name + ("_emit" if emit else ""),
    )(*args)
    return res[0], list(res[1:])


def _zoh(lam_re, lam_im, log_dt):
    dt = jnp.exp(log_dt)
    mag = jnp.exp(dt * lam_re)
    a_re = mag * jnp.cos(dt * lam_im)
    a_im = mag * jnp.sin(dt * lam_im)
    nr, ni = a_re - 1.0, a_im
    den = lam_re * lam_re + lam_im * lam_im
    coef_re = (nr * lam_re + ni * lam_im) / den
    coef_im = (ni * lam_re - nr * lam_im) / den
    return a_re, a_im, coef_re, coef_im


def _spread_lanes(x, reps):
    w = x.shape[1]
    assert w & (w - 1) == 0
    src = lax.broadcasted_iota(jnp.int32, (w, w * reps), 0)
    dst = lax.broadcasted_iota(jnp.int32, (w, w * reps), 1)
    sel = ((dst & (w - 1)) == src).astype(BF16)
    return jnp.dot(x.astype(BF16), sel, preferred_element_type=F32)


def _disc_kernel(lre_ref, lim_ref, ldt_ref, lre_col, lim_col, ldt_col, bre_ref, bim_ref, cre_ref, cim_ref,
                 are_ref, aim_ref, bw_ref, cw_ref):
    a_re, a_im, _, _ = _zoh(lre_ref[...], lim_ref[...], ldt_ref[...])
    are_ref[...] = a_re
    aim_ref[...] = a_im

    _, _, coef_re, coef_im = _zoh(lre_col[...], lim_col[...], ldt_col[...])
    b_re = bre_ref[...]
    b_im = bim_ref[...]
    bb_re = coef_re * b_re - coef_im * b_im
    bb_im = coef_re * b_im + coef_im * b_re
    state_shift = SSM_STATE.bit_length() - 1
    chan_shift = SSM_GROUP.bit_length() - 1
    row_g = lax.broadcasted_iota(jnp.int32, (SB, UB), 0) >> state_shift
    col_g = lax.broadcasted_iota(jnp.int32, (SB, UB), 1) >> chan_shift
    diag = row_g == col_g
    for part, bb in enumerate((bb_re, bb_im)):
        wt = jnp.where(diag, _spread_lanes(bb, GROUPS_PER_BLOCK), 0.0)
        bw_ref[0, :, part * SB:(part + 1) * SB] = wt.T.astype(BF16)

    row_g = lax.broadcasted_iota(jnp.int32, (UB, SB), 0) >> chan_shift
    col_g = lax.broadcasted_iota(jnp.int32, (UB, SB), 1) >> state_shift
    diag = row_g == col_g
    for part, c in enumerate((cre_ref[...], -cim_ref[...])):
        wt = jnp.where(diag, _spread_lanes(c, GROUPS_PER_BLOCK), 0.0)
        cw_ref[0, part * SB:(part + 1) * SB, :] = wt.T.astype(BF16)


def _discretise(lam_re, lam_im, log_dt, b_re, b_im, c_re, c_im):
    g, p = lam_re.shape
    h = b_re.shape[-1]
    nb = g // GROUPS_PER_BLOCK
    col = lambda x: x.reshape(g * p, 1)
    blk = lambda rows, cols: pl.BlockSpec((rows, cols), lambda i: (i, 0))
    return pl.pallas_call(
        _disc_kernel,
        grid=(nb,),
        in_specs=[blk(GROUPS_PER_BLOCK, p), blk(GROUPS_PER_BLOCK, p), blk(GROUPS_PER_BLOCK, 1),
                  blk(SB, 1), blk(SB, 1), blk(SB, 1),
                  blk(SB, h), blk(SB, h), blk(UB, p), blk(UB, p)],
        out_specs=(blk(GROUPS_PER_BLOCK, p), blk(GROUPS_PER_BLOCK, p),
                   pl.BlockSpec((1, UB, 2 * SB), lambda i: (i, 0, 0)),
                   pl.BlockSpec((1, 2 * SB, UB), lambda i: (i, 0, 0))),
        out_shape=(jax.ShapeDtypeStruct((g, p), F32), jax.ShapeDtypeStruct((g, p), F32),
                   jax.ShapeDtypeStruct((nb, UB, 2 * SB), BF16),
                   jax.ShapeDtypeStruct((nb, 2 * SB, UB), BF16)),
        compiler_params=_cparams(("arbitrary",)),
        name="s5_discretise",
    )(lam_re, lam_im, log_dt.reshape(g, 1),
      col(lam_re), col(lam_im), col(jnp.repeat(log_dt, p)),
      b_re.reshape(g * p, h), b_im.reshape(g * p, h), c_re.reshape(g * h, p), c_im.reshape(g * h, p))


def _ssm_prompt_kernel(*refs, nbatch):
    parity = pl.program_id(1) % 2
    for fill in (0, 1):
        pl.when(parity == fill)(functools.partial(_ssm_prompt_step, *refs, nbatch=nbatch, fill=fill))


def _ssm_prompt_step(un_ref, u_ref, bw_ref, cw_ref, are_ref, aim_ref, d_ref,
                     y_ref, sre_ref, sim_ref,
                     bure2, buim2, st_re, st_im, *, nbatch, fill):
    tc = pl.program_id(1)
    scan = 1 - fill

    @pl.when(tc == 0)
    def _():
        st_re[...] = jnp.zeros_like(st_re)
        st_im[...] = jnp.zeros_like(st_im)
        bure2[1] = jnp.zeros(bure2.shape[1:], F32)
        buim2[1] = jnp.zeros(buim2.shape[1:], F32)

    bure_f, buim_f = bure2.at[fill], buim2.at[fill]
    bure, buim = bure2.at[scan], buim2.at[scan]

    pack = 8 // nbatch
    nq = SLABS // pack

    def slab_rows(c, b):
        part, c = (0, c) if c < SLABS else (1, c - SLABS)
        v = (c // nq) * nbatch + b
        r0 = v * SCAN_PITCH + (v % 2) * SCAN_SKEW
        return part, c % nq, slice(r0, r0 + SCAN_T)

    for ref in (bure_f, buim_f):
        for q in range(nq):
            for v in range(0, 8, 2):
                ref[q, v * SCAN_PITCH + SCAN_T:v * SCAN_PITCH + SCAN_T + 2 * SCAN_SKEW, :] = (
                    jnp.zeros((2 * SCAN_SKEW, LANES), F32))

    u_bf = un_ref[...].reshape(nbatch * SCAN_T, UB).astype(BF16)
    per_dot = 4
    for n in range(2 * SLABS // per_dot):
        bu = jnp.dot(u_bf, bw_ref[0, :, per_dot * n * LANES:per_dot * (n + 1) * LANES],
                     preferred_element_type=F32)
        for h in range(per_dot):
            for b in range(nbatch):
                part, q, rows = slab_rows(per_dot * n + h, b)
                (bure_f, buim_f)[part][q, rows, :] = bu[b * SCAN_T:(b + 1) * SCAN_T, h * LANES:(h + 1) * LANES]

    def packed(a_ref, q):
        return jnp.concatenate(
            [jnp.broadcast_to(a_ref[0, q + h * nq], (nbatch, LANES)) for h in range(pack)], axis=0)

    a_re = [packed(are_ref, q) for q in range(nq)]
    a_im = [packed(aim_ref, q) for q in range(nq)]
    init = tuple(st_re[q] for q in range(nq)) + tuple(st_im[q] for q in range(nq))

    def step(t, carry, active=None):
        rows = pl.ds(t, pack * nbatch, stride=SCAN_PITCH)
        new_re, new_im = [], []
        for q in range(nq):
            s_r, s_i = carry[q], carry[nq + q]
            b_r, b_i = bure[q, rows, :], buim[q, rows, :]
            n_r = a_re[q] * s_r - a_im[q] * s_i + b_r
            n_i = a_re[q] * s_i + a_im[q] * s_r + b_i
            if active is not None:
                b_r, b_i = jnp.where(active, n_r, b_r), jnp.where(active, n_i, b_i)
                n_r, n_i = jnp.where(active, n_r, s_r), jnp.where(active, n_i, s_i)
                bure[q, rows, :] = b_r
                buim[q, rows, :] = b_i
            else:
                bure[q, rows, :] = n_r
                buim[q, rows, :] = n_i
            new_re.append(n_r)
            new_im.append(n_i)
        return tuple(new_re) + tuple(new_im)

    odd = (lax.broadcasted_iota(jnp.int32, (8, LANES), 0) & 1) == 1
    carry = init
    for t in range(SCAN_SKEW):
        carry = step(t, carry, active=jnp.logical_not(odd))
    for t in range(SCAN_SKEW, SCAN_T):
        carry = step(t, carry)
    for t in range(SCAN_T, SCAN_T + SCAN_SKEW):
        carry = step(t, carry, active=odd)
    fin = carry
    for q in range(nq):
        st_re[q] = fin[q]
        st_im[q] = fin[nq + q]

    d = d_ref[0]
    for b in range(nbatch):
        y = d * u_ref[b]
        for n in range(SLABS):
            (p0, q0, rows0), (p1, q1, rows1) = slab_rows(2 * n, b), slab_rows(2 * n + 1, b)
            s_pair = jnp.concatenate([(bure, buim)[p0][q0, rows0, :], (bure, buim)[p1][q1, rows1, :]],
                                     axis=1).astype(BF16)
            y = y + jnp.dot(s_pair, cw_ref[0, 2 * n * LANES:(2 * n + 2) * LANES, :],
                            preferred_element_type=F32)
        y_ref[b] = _gelu(y).astype(y_ref.dtype)

    @pl.when(tc == pl.num_programs(1) - 1)
    def _():
        for c in range(SLABS):
            q, h = c % nq, c // nq
            sre_ref[:, c * LANES:(c + 1) * LANES] = st_re[q, h * nbatch:(h + 1) * nbatch, :]
            sim_ref[:, c * LANES:(c + 1) * LANES] = st_im[q, h * nbatch:(h + 1) * nbatch, :]


def _ssm_prompt(z3, bw, cw, a_re, a_im, d_skip):
    nbatch, seq, width = z3.shape
    nb = bw.shape[0]
    states = nb * SB
    kern = functools.partial(_ssm_prompt_kernel, nbatch=nbatch)
    nt = seq // SCAN_T
    assert 8 % nbatch == 0
    return pl.pallas_call(
        kern,
        grid=(nb, nt + 1),
        in_specs=[
            pl.BlockSpec((nbatch, SCAN_T, UB), lambda g, t: (0, jnp.minimum(t, nt - 1), g)),
            pl.BlockSpec((nbatch, SCAN_T, UB), lambda g, t: (0, jnp.maximum(t - 1, 0), g)),
            pl.BlockSpec((1, UB, 2 * SB), lambda g, t: (g, 0, 0)),
            pl.BlockSpec((1, 2 * SB, UB), lambda g, t: (g, 0, 0)),
            pl.BlockSpec((1, SLABS, 1, LANES), lambda g, t: (g, 0, 0, 0)),
            pl.BlockSpec((1, SLABS, 1, LANES), lambda g, t: (g, 0, 0, 0)),
            pl.BlockSpec((1, 1, UB), lambda g, t: (g, 0, 0)),
        ],
        out_specs=(
            pl.BlockSpec((nbatch, SCAN_T, UB), lambda g, t: (0, jnp.maximum(t - 1, 0), g)),
            pl.BlockSpec((nbatch, SB), lambda g, t: (0, g)),
            pl.BlockSpec((nbatch, SB), lambda g, t: (0, g)),
        ),
        out_shape=(
            jax.ShapeDtypeStruct((nbatch, seq, width), BF16),
            jax.ShapeDtypeStruct((nbatch, states), F32),
            jax.ShapeDtypeStruct((nbatch, states), F32),
        ),
        scratch_shapes=[
            pltpu.VMEM((2, SLABS * nbatch // 8, 8 * SCAN_PITCH, LANES), F32),
            pltpu.VMEM((2, SLABS * nbatch // 8, 8 * SCAN_PITCH, LANES), F32),
            pltpu.VMEM((SLABS * nbatch // 8, 8, LANES), F32),
            pltpu.VMEM((SLABS * nbatch // 8, 8, LANES), F32),
        ],
        compiler_params=_cparams(("arbitrary", "arbitrary")),
        name="ssm_prompt",
    )(z3, z3, bw, cw, a_re.reshape(nb, SLABS, 1, LANES), a_im.reshape(nb, SLABS, 1, LANES),
      d_skip.reshape(nb, 1, UB))


def _ssm_sample_kernel(u_ref, bw_ref, cw_ref, are_ref, aim_ref, d_ref, x0re_ref, x0im_ref,
                       y_ref, sre_ref, sim_ref):
    u = u_ref[...]
    bu = jnp.dot(u.astype(BF16), bw_ref[0], preferred_element_type=F32)
    a_re = are_ref[0]
    a_im = aim_ref[0]
    x_re = x0re_ref[...]
    x_im = x0im_ref[...]
    s_re = a_re * x_re - a_im * x_im + bu[:, :SB]
    s_im = a_re * x_im + a_im * x_re + bu[:, SB:]
    sre_ref[...] = s_re
    sim_ref[...] = s_im
    s_cat = jnp.concatenate([s_re, s_im], axis=1).astype(BF16)
    y = jnp.dot(s_cat, cw_ref[0], preferred_element_type=F32) + d_ref[0] * u
    y_ref[...] = _gelu(y).astype(y_ref.dtype)


def _ssm_sample(z, bw, cw, a_re, a_im, d_skip, x0_re, x0_im):
    nbatch = z.shape[0]
    nb = bw.shape[0]
    return pl.pallas_call(
        _ssm_sample_kernel,
        grid=(nb,),
        in_specs=[
            pl.BlockSpec((nbatch, UB), lambda g: (0, g)),
            pl.BlockSpec((1, UB, 2 * SB), lambda g: (g, 0, 0)),
            pl.BlockSpec((1, 2 * SB, UB), lambda g: (g, 0, 0)),
            pl.BlockSpec((1, 1, SB), lambda g: (g, 0, 0)),
            pl.BlockSpec((1, 1, SB), lambda g: (g, 0, 0)),
            pl.BlockSpec((1, 1, UB), lambda g: (g, 0, 0)),
            pl.BlockSpec((nbatch, SB), lambda g: (0, g)),
            pl.BlockSpec((nbatch, SB), lambda g: (0, g)),
        ],
        out_specs=(
            pl.BlockSpec((nbatch, UB), lambda g: (0, g)),
            pl.BlockSpec((nbatch, SB), lambda g: (0, g)),
            pl.BlockSpec((nbatch, SB), lambda g: (0, g)),
        ),
        out_shape=(
            jax.ShapeDtypeStruct((nbatch, nb * UB), BF16),
            jax.ShapeDtypeStruct((nbatch, nb * SB), F32),
            jax.ShapeDtypeStruct((nbatch, nb * SB), F32),
        ),
        compiler_params=_cparams(("arbitrary",)),
        name="ssm_sample",
    )(z, bw, cw, a_re.reshape(nb, 1, SB), a_im.reshape(nb, 1, SB), d_skip.reshape(nb, 1, UB),
      x0_re, x0_im)


def _gmlp_prompt_kernel(gu_ref, gv_ref, gnv_ref, ws_ref, bias_ref, wo_ref, gate_ref, p_ref,
                        out_ref, s_scr, wt_scr, *, tm, heads):
    i = pl.program_id(0)
    j = pl.program_id(1)

    @pl.when(jnp.logical_and(i == 0, j == 0))
    def _():
        row = lax.broadcasted_iota(jnp.int32, (CHUNK, CHUNK), 0)
        col = lax.broadcasted_iota(jnp.int32, (CHUNK, CHUNK), 1)
        mask = (col <= row).astype(F32)
        for g in range(heads):
            wt_scr[g] = (ws_ref[g] * mask).astype(BF16)

    @pl.when(j == 0)
    def _():
        gnv = gnv_ref[...]

        def body(c, carry):
            r0 = pl.multiple_of(c * CHUNK, CHUNK)
            gv = gv_ref[pl.ds(r0, CHUNK), :].astype(F32)
            r = lax.rsqrt(jnp.mean(gv * gv, axis=-1, keepdims=True) + EPS)
            v = (gv * r * gnv).astype(BF16)
            for g in range(heads):
                sl = slice(g * GMLP_HEAD, (g + 1) * GMLP_HEAD)
                mixed = jnp.dot(wt_scr[g], v[:, sl], preferred_element_type=F32) + bias_ref[:, sl]
                gu = gu_ref[pl.ds(r0, CHUNK), sl].astype(F32)
                s_scr[pl.ds(r0, CHUNK), sl] = (gu * mixed).astype(BF16)
            return carry

        lax.fori_loop(0, tm // CHUNK, body, 0)

    yb = jnp.dot(s_scr[...], _as_bf16(wo_ref, None), preferred_element_type=F32)
    out_ref[...] = (p_ref[...].astype(F32) + gate_ref[...].astype(F32) * yb).astype(out_ref.dtype)


def _gmlp_prompt(guv, gnv, w_s, bias_full, w_gout, gates, p, *, gate_col, tm, tn):
    m = guv.shape[0]
    width, d = w_gout.shape
    heads = w_s.shape[0]
    kern = functools.partial(_gmlp_prompt_kernel, tm=tm, heads=heads)
    return pl.pallas_call(
        kern,
        grid=(m // tm, d // tn),
        in_specs=[
            pl.BlockSpec((tm, width), lambda i, j: (i, 0)),
            pl.BlockSpec((tm, width), lambda i, j: (i, 1)),
            pl.BlockSpec((1, width), lambda i, j: (0, 0)),
            pl.BlockSpec((heads, CHUNK, CHUNK), lambda i, j: (0, 0, 0)),
            pl.BlockSpec((CHUNK, width), lambda i, j: (0, 0)),
            _col_tile_spec(w_gout, width, tn),
            pl.BlockSpec((tm, tn), lambda i, j: (i, j + gate_col // tn)),
            pl.BlockSpec((tm, tn), lambda i, j: (i, j)),
        ],
        out_specs=pl.BlockSpec((tm, tn), lambda i, j: (i, j)),
        out_shape=jax.ShapeDtypeStruct((m, d), BF16),
        scratch_shapes=[pltpu.VMEM((tm, width), BF16), pltpu.VMEM((heads, CHUNK, CHUNK), BF16)],
        compiler_params=_cparams(("arbitrary", "arbitrary")),
        name="gmlp_prompt",
    )(guv, guv, gnv, w_s, bias_full, w_gout, gates, p)


def _gmlp_sample_kernel(gu_ref, gv_ref, gnv_ref, wdiag_ref, bias_ref, wo_ref, gate_ref, p_ref,
                        out_ref, v_ref, s_scr):
    j = pl.program_id(0)

    @pl.when(j == 0)
    def _():
        gv = gv_ref[...]
        r = lax.rsqrt(jnp.mean(gv * gv, axis=-1, keepdims=True) + EPS)
        v = gv * r * gnv_ref[...]
        v_ref[...] = v
        mixed = wdiag_ref[...] * v + bias_ref[...]
        s_scr[...] = (gu_ref[...] * mixed).astype(BF16)

    yb = jnp.dot(s_scr[...], _as_bf16(wo_ref, None), preferred_element_type=F32)
    out_ref[...] = (p_ref[...] + gate_ref[...] * yb).astype(out_ref.dtype)


def _gmlp_sample(guv, gnv, wdiag, bias0, w_gout, gates, p, *, gate_col, tn):
    m = guv.shape[0]
    width, d = w_gout.shape
    return pl.pallas_call(
        _gmlp_sample_kernel,
        grid=(d // tn,),
        in_specs=[
            pl.BlockSpec((m, width), lambda j: (0, 0)),
            pl.BlockSpec((m, width), lambda j: (0, 1)),
            pl.BlockSpec((1, width), lambda j: (0, 0)),
            pl.BlockSpec((1, width), lambda j: (0, 0)),
            pl.BlockSpec((1, width), lambda j: (0, 0)),
            pl.BlockSpec((width, tn), lambda j: (0, j)),
            pl.BlockSpec((m, tn), lambda j: (0, j + gate_col // tn)),
            pl.BlockSpec((m, tn), lambda j: (0, j)),
        ],
        out_specs=(pl.BlockSpec((m, tn), lambda j: (0, j)),
                   pl.BlockSpec((m, width), lambda j: (0, 0))),
        out_shape=(jax.ShapeDtypeStruct((m, d), BF16), jax.ShapeDtypeStruct((m, width), F32)),
        scratch_shapes=[pltpu.VMEM((m, width), BF16)],
        compiler_params=_cparams(("arbitrary",)),
        name="gmlp_sample",
    )(guv, guv, gnv, wdiag, bias0, w_gout, gates, p)


def _mixer(x, h2, prm, w, *, nbatch, x0, tm_mm):
    sample = x0 is not None
    d_model = x.shape[1]
    ssm_width = prm["bw"].shape[0] * UB
    gmlp_width = prm["gnv"].shape[1]
    o1 = ssm_width
    o3 = o1 + 2 * gmlp_width
    wide, narrow = 1024, 512

    w = dict(w, win_ssm=w["w_in"], win_uv=w["w_in"], win_gate=w["w_in"], glu_a=w["w_glu"], glu_b=w["w_glu"])
    cols = dict(win_uv=o1, win_gate=o3, glu_b=d_model)

    def mm(lhs, wkeys, extras, kind, n_out, out_dtype, name, tn=wide):
        out, _ = _mm(lhs, [(w[k], cols.get(k, 0)) for k in wkeys], extras, kind=kind, n_out=n_out,
                     out_dtype=out_dtype, tm=tm_mm, tn=wide if sample else tn, emit=False, name=name)
        return out

    act = F32 if sample else BF16
    z_ssm = mm(h2, ["win_ssm"], [], "identity", o1, F32, "w_in_ssm", tn=narrow)
    guv = mm(h2, ["win_uv"], [], "gelu", 2 * gmlp_width, act, "w_in_uv")
    gates = mm(h2, ["win_gate"], [], "sigmoid", 2 * d_model, act, "w_in_gate")

    if sample:
        y, s_re, s_im = _ssm_sample(z_ssm, prm["bw"], prm["cw"], prm["a_re"], prm["a_im"],
                                    prm["d_skip"], x0[0], x0[1])
    else:
        seq = x.shape[0] // nbatch
        y, s_re, s_im = _ssm_prompt(z_ssm.reshape(nbatch, seq, ssm_width), prm["bw"], prm["cw"],
                                    prm["a_re"], prm["a_im"], prm["d_skip"])
        y = y.reshape(nbatch * seq, ssm_width)
    p = mm(y, ["glu_a", "glu_b"], [(gates, 0)], "gated_glu", d_model, act, "ssm_glu")

    if sample:
        merged, v_rows = _gmlp_sample(guv, prm["gnv"], prm["w_diag"], prm["bias0"],
                                      w["w_gout"], gates, p, gate_col=d_model, tn=wide)
    else:
        merged = _gmlp_prompt(guv, prm["gnv"], prm["w_s"], prm["bias_full"], w["w_gout"], gates, p,
                              gate_col=d_model, tm=tm_mm, tn=wide)
        v_rows = None
    x = mm(merged, ["w_out"], [(x, 0)], "residual", d_model, F32, "out_proj", tn=narrow)
    return x, s_re, s_im, v_rows


def kernel(x_prompt, x_sample, state_ssm_re, state_ssm_im, norm_ffn1, ffn1_gate, ffn1_up, ffn1_down, norm_mix, w_in, ssm_lambda_re, ssm_lambda_im, ssm_log_dt, ssm_b_re, ssm_b_im, ssm_c_re, ssm_c_im, ssm_d, ssm_w_glu, gmlp_norm_v, gmlp_w_s, gmlp_b_s, gmlp_w_out, w_out, norm_ffn2, ffn2_gate, ffn2_up, ffn2_down, norm_final):
    depth = w_in.shape[0]
    assert depth == 1, "the final RMSNorm is fused into the last FFN; one layer per step"
    batch, seq, d_model = x_prompt.shape
    dec_batch, dec_seq, _ = x_sample.shape
    assert dec_seq == 1
    groups, states = ssm_lambda_re.shape[1:]
    gmlp_width = gmlp_norm_v.shape[1]
    heads = gmlp_w_s.shape[1]
    head_dim = gmlp_width // heads
    assert seq % SCAN_T == 0 and seq % CHUNK == 0 and head_dim == GMLP_HEAD
    assert ssm_b_re.shape[-1] == SSM_GROUP and states == SSM_STATE and groups % GROUPS_PER_BLOCK == 0

    l = 0
    yp = x_prompt.reshape(batch * seq, d_model)
    ys = x_sample.reshape(dec_batch, d_model)
    gfin = norm_final.reshape(1, d_model)
    a_re, a_im, bw, cw = _discretise(ssm_lambda_re[l], ssm_lambda_im[l], ssm_log_dt[l],
                                     ssm_b_re[l], ssm_b_im[l], ssm_c_re[l], ssm_c_im[l])
    prm = dict(
        n1=norm_ffn1[l].reshape(1, d_model), nmix=norm_mix[l].reshape(1, d_model),
        n2=norm_ffn2[l].reshape(1, d_model),
        bw=bw, cw=cw, a_re=a_re, a_im=a_im, d_skip=ssm_d[l],
        gnv=gmlp_norm_v[l].reshape(1, gmlp_width), w_s=gmlp_w_s[l],
        bias_full=jnp.repeat(gmlp_b_s[l].T, head_dim, axis=1),
        w_diag=jnp.repeat(gmlp_w_s[l][:, 0, 0], head_dim).reshape(1, gmlp_width),
        bias0=jnp.repeat(gmlp_b_s[l][:, 0], head_dim).reshape(1, gmlp_width),
    )
    w_f32 = dict(w_in=w_in[l], w_glu=ssm_w_glu[l], w_gout=gmlp_w_out[l], w_out=w_out[l])
    x0 = (state_ssm_re[l].reshape(dec_batch, groups * states),
          state_ssm_im[l].reshape(dec_batch, groups * states))
    tm, tf, row_tiles = 1024, 256, 8

    yp, ys, hp, hs = _ffn(yp, ys, prm["n1"], ffn1_gate[l], ffn1_up[l], ffn1_down[l], prm["nmix"],
                          tail="norm_out", row_tiles=row_tiles, tf=tf)
    ys, sr, si, vr = _mixer(ys, hs, prm, w_f32, nbatch=dec_batch, x0=x0, tm_mm=dec_batch)
    yp, pr, pi, _ = _mixer(yp, hp, prm, w_f32, nbatch=batch, x0=None, tm_mm=tm)
    yp, ys = _ffn(yp, ys, prm["n2"], ffn2_gate[l], ffn2_up[l], ffn2_down[l], gfin,
                  tail="final", row_tiles=row_tiles, tf=tf)
    return (yp.reshape(batch, seq, d_model), ys.reshape(dec_batch, dec_seq, d_model),
            pr.reshape(1, batch, groups, states), pi.reshape(1, batch, groups, states),
            sr.reshape(1, dec_batch, groups, states), si.reshape(1, dec_batch, groups, states),
            vr.reshape(1, dec_batch, dec_seq, gmlp_width))
```

```python
import functools
import math

import jax
import jax.numpy as jnp
from jax import lax
from jax.experimental import pallas as pl
from jax.experimental.pallas import tpu as pltpu

F32 = jnp.float32
BF16 = jnp.bfloat16

EPS = 1e-6
LANES = 128
SSM_GROUP = 16
SSM_STATE = 64
GROUPS_PER_BLOCK = 16
UB = GROUPS_PER_BLOCK * SSM_GROUP
SB = GROUPS_PER_BLOCK * SSM_STATE
SLABS = SB // LANES
SCAN_T = 256
SCAN_SKEW = 4
SCAN_PITCH = SCAN_T + SCAN_SKEW
CHUNK = 128
GMLP_HEAD = 128
VMEM_LIMIT = 62 * 1024 * 1024


def _cparams(sem):
    return pltpu.CompilerParams(dimension_semantics=sem, vmem_limit_bytes=VMEM_LIMIT)


def _gelu(x):
    c = math.sqrt(2.0 / math.pi)
    return 0.5 * x * (1.0 + jnp.tanh(c * (x + 0.044715 * (x * x * x))))


def _sigmoid(x):
    return 1.0 / (1.0 + jnp.exp(-x))


def _as_bf16(w_ref, emit_ref):
    w = w_ref[0] if len(w_ref.shape) == 3 else w_ref[...]
    if w.dtype != BF16:
        w = w.astype(BF16)
    if emit_ref is not None:
        if len(emit_ref.shape) == 3:
            emit_ref[0] = w
        else:
            emit_ref[...] = w
    return w


def _col_tile_spec(w, k, tn, col0=0):
    if w.ndim == 3:
        assert w.shape[1:] == (k, tn) and col0 == 0
        return pl.BlockSpec((1, k, tn), lambda i, j: (j, 0, 0))
    return pl.BlockSpec((k, tn), lambda i, j, o=col0 // tn: (0, j + o))


def _ffn_kernel(*refs, tm, d_model, col_chunk, row_chunk, tail, tail_rows):
    refs = list(refs)
    x_hbm = refs.pop(0)
    xt_hbm = refs.pop(0) if tail_rows else None
    gn_ref, wg_ref, wu_ref, wd_ref, gt_ref = refs[:5]
    del refs[:5]
    out_hbm, outt_hbm = refs.pop(0), refs.pop(0)
    h2_hbm, h2t_hbm = (refs.pop(0), refs.pop(0)) if tail == "norm_out" else (None, None)
    out_ref, h_buf, x_sem, xt_sem, chunk_sems, out_sems, h2_sems = refs
    h2_ref = h_buf
    i = pl.program_id(0)
    j = pl.program_id(1)
    last = pl.num_programs(0) - 1
    head_rows = tm - tail_rows

    def copy_in(src, dst, sem):
        cp = pltpu.make_async_copy(src, dst, sem)
        cp.start()
        return cp

    @pl.when(j == 0)
    def _():
        g = gn_ref[...]
        n_chunks = tm // row_chunk

        def chunk_copy(c):
            r0 = pl.multiple_of(c * row_chunk, row_chunk)
            return pltpu.make_async_copy(x_hbm.at[pl.ds(i * tm + r0, row_chunk), :],
                                         out_ref.at[pl.ds(r0, row_chunk), :], chunk_sems.at[c])

        def norm_rows(c, chunked):
            r0 = pl.multiple_of(c * row_chunk, row_chunk)
            if chunked:
                chunk_copy(c).wait()
            xf = out_ref[pl.ds(r0, row_chunk), :]
            r = lax.rsqrt(jnp.mean(xf * xf, axis=-1, keepdims=True) + EPS)
            h_buf[pl.ds(r0, row_chunk), :] = (xf * r * g).astype(BF16)
            out_ref[pl.ds(r0, row_chunk), :] = 2.0 * xf

        @pl.when(i < last)
        def _():
            def start(c, carry):
                chunk_copy(c).start()
                return carry

            lax.fori_loop(0, n_chunks, start, 0)
            lax.fori_loop(0, n_chunks, lambda c, carry: (norm_rows(c, True), carry)[1], 0)

        @pl.when(i == last)
        def _():
            n_x = x_hbm.shape[0]
            head = copy_in(x_hbm.at[pl.ds(n_x - head_rows, head_rows), :],
                           out_ref.at[pl.ds(0, head_rows), :], x_sem)
            rest = copy_in(xt_hbm, out_ref.at[pl.ds(head_rows, tail_rows), :], xt_sem)
            head.wait()
            rest.wait()
            lax.fori_loop(0, n_chunks, lambda c, carry: (norm_rows(c, False), carry)[1], 0)

    h = h_buf[...]
    g = jnp.dot(h, _as_bf16(wg_ref, None), preferred_element_type=F32)
    u = jnp.dot(h, _as_bf16(wu_ref, None), preferred_element_type=F32)
    a = (g * _sigmoid(g) * u).astype(BF16)
    for n in range(d_model // col_chunk):
        sl = slice(n * col_chunk, (n + 1) * col_chunk)
        out_ref[:, sl] += jnp.dot(a, wd_ref[:, sl].astype(BF16), preferred_element_type=F32)

    @pl.when(j == pl.num_programs(1) - 1)
    def _():
        gt = gt_ref[...]
        n_chunks = tm // row_chunk
        results = [(out_ref, out_hbm, outt_hbm, out_sems)]
        if tail == "norm_out":
            results.append((h2_ref, h2_hbm, h2t_hbm, h2_sems))

        def finish_rows(c):
            r0 = pl.multiple_of(c * row_chunk, row_chunk)
            y = 0.5 * out_ref[pl.ds(r0, row_chunk), :]
            r = lax.rsqrt(jnp.mean(y * y, axis=-1, keepdims=True) + EPS)
            if tail == "norm_out":
                out_ref[pl.ds(r0, row_chunk), :] = y
                h2_ref[pl.ds(r0, row_chunk), :] = (y * r * gt).astype(BF16)
            else:
                out_ref[pl.ds(r0, row_chunk), :] = y * r * gt

        def chunk_out(c, src, dst, sems):
            r0 = pl.multiple_of(c * row_chunk, row_chunk)
            return pltpu.make_async_copy(src.at[pl.ds(r0, row_chunk), :],
                                         dst.at[pl.ds(i * tm + r0, row_chunk), :], sems.at[c])

        @pl.when(i < last)
        def _():
            def finish(c, carry):
                finish_rows(c)
                for src, dst, _, sems in results:
                    chunk_out(c, src, dst, sems).start()
                return carry

            def drain(c, carry):
                for src, dst, _, sems in results:
                    chunk_out(c, src, dst, sems).wait()
                return carry

            lax.fori_loop(0, n_chunks, finish, 0)
            lax.fori_loop(0, n_chunks, drain, 0)

        @pl.when(i == last)
        def _():
            lax.fori_loop(0, n_chunks, lambda c, carry: (finish_rows(c), carry)[1], 0)
            copies = []
            for src, dst, dst_tail, sems in results:
                n_x = dst.shape[0]
                copies.append(copy_in(src.at[pl.ds(0, head_rows), :],
                                      dst.at[pl.ds(n_x - head_rows, head_rows), :], sems.at[0]))
                copies.append(copy_in(src.at[pl.ds(head_rows, tail_rows), :], dst_tail, sems.at[1]))
            for cp in copies:
                cp.wait()


def _ffn(x, x_tail, gn, wg, wu, wd, gtail, *, tail, row_tiles, tf):
    n_x, d = x.shape
    f = wd.shape[0]
    tail_rows = x_tail.shape[0]
    tm = (n_x + tail_rows) // row_tiles
    ni, nj = row_tiles, f // tf
    assert ni * tm == n_x + tail_rows and nj * tf == f
    assert 0 < tail_rows < tm and (tm - tail_rows) % 16 == 0
    row_chunk = max(c for c in range(16, 81, 16) if tm % c == 0)
    kern = functools.partial(_ffn_kernel, tm=tm, d_model=d, col_chunk=512, row_chunk=row_chunk,
                             tail=tail, tail_rows=tail_rows)
    in_specs = [
        pl.BlockSpec(memory_space=pl.ANY),
        pl.BlockSpec(memory_space=pl.ANY),
        pl.BlockSpec((1, d), lambda i, j: (0, 0)),
        pl.BlockSpec((d, tf), lambda i, j: (0, j)),
        pl.BlockSpec((d, tf), lambda i, j: (0, j)),
        pl.BlockSpec((tf, d), lambda i, j: (j, 0)),
        pl.BlockSpec((1, d), lambda i, j: (0, 0)),
    ]
    args = [x, x_tail, gn, wg, wu, wd, gtail]
    out_shape = [jax.ShapeDtypeStruct((n_x, d), F32), jax.ShapeDtypeStruct((tail_rows, d), F32)]
    if tail == "norm_out":
        out_shape += [jax.ShapeDtypeStruct((n_x, d), BF16), jax.ShapeDtypeStruct((tail_rows, d), BF16)]
    out_specs = [pl.BlockSpec(memory_space=pl.ANY)] * len(out_shape)
    n_chunks = tm // row_chunk
    assert n_chunks >= 2
    return pl.pallas_call(
        kern,
        grid=(ni, nj),
        in_specs=in_specs,
        out_specs=out_specs,
        out_shape=out_shape,
        scratch_shapes=[pltpu.VMEM((tm, d), F32), pltpu.VMEM((tm, d), BF16),
                        pltpu.SemaphoreType.DMA, pltpu.SemaphoreType.DMA,
                        pltpu.SemaphoreType.DMA((n_chunks,)), pltpu.SemaphoreType.DMA((n_chunks,)),
                        pltpu.SemaphoreType.DMA((n_chunks,))],
        compiler_params=_cparams(("arbitrary", "arbitrary")),
        name="ffn_" + tail,
    )(*args)


def _mm_epilogue(kind, accs, extras):
    if kind == "identity":
        return accs[0]
    if kind == "gelu":
        return _gelu(accs[0])
    if kind == "sigmoid":
        return _sigmoid(accs[0])
    if kind == "gated_glu":
        return extras[0].astype(F32) * (accs[0] * _sigmoid(accs[1]))
    if kind == "residual":
        return extras[0].astype(F32) + accs[0]
    raise ValueError(kind)


def _mm_kernel(*refs, n_w, n_extra, kind, emit):
    lhs_ref = refs[0]
    w_refs = refs[1:1 + n_w]
    extra_refs = refs[1 + n_w:1 + n_w + n_extra]
    out_ref = refs[1 + n_w + n_extra]
    emit_refs = refs[2 + n_w + n_extra:] if emit else [None] * n_w
    lhs = lhs_ref[...]
    accs = [jnp.dot(lhs, _as_bf16(w, e), preferred_element_type=F32) for w, e in zip(w_refs, emit_refs)]
    out_ref[...] = _mm_epilogue(kind, accs, [e[...] for e in extra_refs]).astype(out_ref.dtype)


def _mm(lhs, weights, extras, *, kind, n_out, out_dtype, tm, tn, emit, name):
    m, k = lhs.shape
    assert not emit or m == tm
    in_specs = [pl.BlockSpec((tm, k), lambda i, j: (i, 0))]
    args = [lhs]
    for w, c0 in weights:
        in_specs.append(_col_tile_spec(w, k, tn, c0))
        args.append(w)
    for e, c0 in extras:
        in_specs.append(pl.BlockSpec((tm, tn), lambda i, j, o=c0 // tn: (i, j + o)))
        args.append(e)
    out_specs = [pl.BlockSpec((tm, tn), lambda i, j: (i, j))]
    out_shape = [jax.ShapeDtypeStruct((m, n_out), out_dtype)]
    if emit:
        out_specs += [pl.BlockSpec((1, k, tn), lambda i, j: (j, 0, 0)) for _ in weights]
        out_shape += [jax.ShapeDtypeStruct((n_out // tn, k, tn), BF16) for _ in weights]
    kern = functools.partial(_mm_kernel, n_w=len(weights), n_extra=len(extras), kind=kind, emit=emit)
    res = pl.pallas_call(
        kern,
        grid=(m // tm, n_out // tn),
        in_specs=in_specs,
        out_specs=out_specs,
        out_shape=out_shape,
        compiler_params=_cparams(("arbitrary", "arbitrary")),
        name=name + ("_emit" if emit else ""),
    )(*args)
    return res[0], list(res[1:])


def _zoh(lam_re, lam_im, log_dt):
    dt = jnp.exp(log_dt)
    mag = jnp.exp(dt * lam_re)
    a_re = mag * jnp.cos(dt * lam_im)
    a_im = mag * jnp.sin(dt * lam_im)
    nr, ni = a_re - 1.0, a_im
    den = lam_re * lam_re + lam_im * lam_im
    coef_re = (nr * lam_re + ni * lam_im) / den
    coef_im = (ni * lam_re - nr * lam_im) / den
    return a_re, a_im, coef_re, coef_im


def _spread_lanes(x, reps):
    w = x.shape[1]
    assert w & (w - 1) == 0
    src = lax.broadcasted_iota(jnp.int32, (w, w * reps), 0)
    dst = lax.broadcasted_iota(jnp.int32, (w, w * reps), 1)
    sel = ((dst & (w - 1)) == src).astype(BF16)
    return jnp.dot(x.astype(BF16), sel, preferred_element_type=F32)


def _disc_kernel(lre_ref, lim_ref, ldt_ref, lre_col, lim_col, ldt_col, bre_ref, bim_ref, cre_ref, cim_ref,
                 are_ref, aim_ref, bw_ref, cw_ref):
    a_re, a_im, _, _ = _zoh(lre_ref[...], lim_ref[...], ldt_ref[...])
    are_ref[...] = a_re
    aim_ref[...] = a_im

    _, _, coef_re, coef_im = _zoh(lre_col[...], lim_col[...], ldt_col[...])
    b_re = bre_ref[...]
    b_im = bim_ref[...]
    bb_re = coef_re * b_re - coef_im * b_im
    bb_im = coef_re * b_im + coef_im * b_re
    state_shift = SSM_STATE.bit_length() - 1
    chan_shift = SSM_GROUP.bit_length() - 1
    row_g = lax.broadcasted_iota(jnp.int32, (SB, UB), 0) >> state_shift
    col_g = lax.broadcasted_iota(jnp.int32, (SB, UB), 1) >> chan_shift
    diag = row_g == col_g
    for part, bb in enumerate((bb_re, bb_im)):
        wt = jnp.where(diag, _spread_lanes(bb, GROUPS_PER_BLOCK), 0.0)
        bw_ref[0, :, part * SB:(part + 1) * SB] = wt.T.astype(BF16)

    row_g = lax.broadcasted_iota(jnp.int32, (UB, SB), 0) >> chan_shift
    col_g = lax.broadcasted_iota(jnp.int32, (UB, SB), 1) >> state_shift
    diag = row_g == col_g
    for part, c in enumerate((cre_ref[...], -cim_ref[...])):
        wt = jnp.where(diag, _spread_lanes(c, GROUPS_PER_BLOCK), 0.0)
        cw_ref[0, part * SB:(part + 1) * SB, :] = wt.T.astype(BF16)


def _discretise(lam_re, lam_im, log_dt, b_re, b_im, c_re, c_im):
    g, p = lam_re.shape
    h = b_re.shape[-1]
    nb = g // GROUPS_PER_BLOCK
    col = lambda x: x.reshape(g * p, 1)
    blk = lambda rows, cols: pl.BlockSpec((rows, cols), lambda i: (i, 0))
    return pl.pallas_call(
        _disc_kernel,
        grid=(nb,),
        in_specs=[blk(GROUPS_PER_BLOCK, p), blk(GROUPS_PER_BLOCK, p), blk(GROUPS_PER_BLOCK, 1),
                  blk(SB, 1), blk(SB, 1), blk(SB, 1),
                  blk(SB, h), blk(SB, h), blk(UB, p), blk(UB, p)],
        out_specs=(blk(GROUPS_PER_BLOCK, p), blk(GROUPS_PER_BLOCK, p),
                   pl.BlockSpec((1, UB, 2 * SB), lambda i: (i, 0, 0)),
                   pl.BlockSpec((1, 2 * SB, UB), lambda i: (i, 0, 0))),
        out_shape=(jax.ShapeDtypeStruct((g, p), F32), jax.ShapeDtypeStruct((g, p), F32),
                   jax.ShapeDtypeStruct((nb, UB, 2 * SB), BF16),
                   jax.ShapeDtypeStruct((nb, 2 * SB, UB), BF16)),
        compiler_params=_cparams(("arbitrary",)),
        name="s5_discretise",
    )(lam_re, lam_im, log_dt.reshape(g, 1),
      col(lam_re), col(lam_im), col(jnp.repeat(log_dt, p)),
      b_re.reshape(g * p, h), b_im.reshape(g * p, h), c_re.reshape(g * h, p), c_im.reshape(g * h, p))


def _ssm_prompt_kernel(*refs, nbatch):
    parity = pl.program_id(1) % 2
    for fill in (0, 1):
        pl.when(parity == fill)(functools.partial(_ssm_prompt_step, *refs, nbatch=nbatch, fill=fill))


def _ssm_prompt_step(un_ref, u_ref, bw_ref, cw_ref, are_ref, aim_ref, d_ref,
                     y_ref, sre_ref, sim_ref,
                     bure2, buim2, st_re, st_im, *, nbatch, fill):
    tc = pl.program_id(1)
    scan = 1 - fill

    @pl.when(tc == 0)
    def _():
        st_re[...] = jnp.zeros_like(st_re)
        st_im[...] = jnp.zeros_like(st_im)
        bure2[1] = jnp.zeros(bure2.shape[1:], F32)
        buim2[1] = jnp.zeros(buim2.shape[1:], F32)

    bure_f, buim_f = bure2.at[fill], buim2.at[fill]
    bure, buim = bure2.at[scan], buim2.at[scan]

    pack = 8 // nbatch
    nq = SLABS // pack

    def slab_rows(c, b):
        part, c = (0, c) if c < SLABS else (1, c - SLABS)
        v = (c // nq) * nbatch + b
        r0 = v * SCAN_PITCH + (v % 2) * SCAN_SKEW
        return part, c % nq, slice(r0, r0 + SCAN_T)

    for ref in (bure_f, buim_f):
        for q in range(nq):
            for v in range(0, 8, 2):
                ref[q, v * SCAN_PITCH + SCAN_T:v * SCAN_PITCH + SCAN_T + 2 * SCAN_SKEW, :] = (
                    jnp.zeros((2 * SCAN_SKEW, LANES), F32))

    u_bf = un_ref[...].reshape(nbatch * SCAN_T, UB).astype(BF16)
    per_dot = 4
    for n in range(2 * SLABS // per_dot):
        bu = jnp.dot(u_bf, bw_ref[0, :, per_dot * n * LANES:per_dot * (n + 1) * LANES],
                     preferred_element_type=F32)
        for h in range(per_dot):
            for b in range(nbatch):
                part, q, rows = slab_rows(per_dot * n + h, b)
                (bure_f, buim_f)[part][q, rows, :] = bu[b * SCAN_T:(b + 1) * SCAN_T, h * LANES:(h + 1) * LANES]

    def packed(a_ref, q):
        return jnp.concatenate(
            [jnp.broadcast_to(a_ref[0, q + h * nq], (nbatch, LANES)) for h in range(pack)], axis=0)

    a_re = [packed(are_ref, q) for q in range(nq)]
    a_im = [packed(aim_ref, q) for q in range(nq)]
    init = tuple(st_re[q] for q in range(nq)) + tuple(st_im[q] for q in range(nq))

    def step(t, carry, active=None):
        rows = pl.ds(t, pack * nbatch, stride=SCAN_PITCH)
        new_re, new_im = [], []
        for q in range(nq):
            s_r, s_i = carry[q], carry[nq + q]
            b_r, b_i = bure[q, rows, :], buim[q, rows, :]
            n_r = a_re[q] * s_r - a_im[q] * s_i + b_r
            n_i = a_re[q] * s_i + a_im[q] * s_r + b_i
            if active is not None:
                b_r, b_i = jnp.where(active, n_r, b_r), jnp.where(active, n_i, b_i)
                n_r, n_i = jnp.where(active, n_r, s_r), jnp.where(active, n_i, s_i)
                bure[q, rows, :] = b_r
                buim[q, rows, :] = b_i
            else:
                bure[q, rows, :] = n_r
                buim[q, rows, :] = n_i
            new_re.append(n_r)
            new_im.append(n_i)
        return tuple(new_re) + tuple(new_im)

    odd = (lax.broadcasted_iota(jnp.int32, (8, LANES), 0) & 1) == 1
    carry = init
    for t in range(SCAN_SKEW):
        carry = step(t, carry, active=jnp.logical_not(odd))
    for t in range(SCAN_SKEW, SCAN_T):
        carry = step(t, carry)
    for t in range(SCAN_T, SCAN_T + SCAN_SKEW):
        carry = step(t, carry, active=odd)
    fin = carry
    for q in range(nq):
        st_re[q] = fin[q]
        st_im[q] = fin[nq + q]

    d = d_ref[0]
    for b in range(nbatch):
        y = d * u_ref[b]
        for n in range(SLABS):
            (p0, q0, rows0), (p1, q1, rows1) = slab_rows(2 * n, b), slab_rows(2 * n + 1, b)
            s_pair = jnp.concatenate([(bure, buim)[p0][q0, rows0, :], (bure, buim)[p1][q1, rows1, :]],
                                     axis=1).astype(BF16)
            y = y + jnp.dot(s_pair, cw_ref[0, 2 * n * LANES:(2 * n + 2) * LANES, :],
                            preferred_element_type=F32)
        y_ref[b] = _gelu(y).astype(y_ref.dtype)

    @pl.when(tc == pl.num_programs(1) - 1)
    def _():
        for c in range(SLABS):
            q, h = c % nq, c // nq
            sre_ref[:, c * LANES:(c + 1) * LANES] = st_re[q, h * nbatch:(h + 1) * nbatch, :]
            sim_ref[:, c * LANES:(c + 1) * LANES] = st_im[q, h * nbatch:(h + 1) * nbatch, :]


def _ssm_prompt(z3, bw, cw, a_re, a_im, d_skip):
    nbatch, seq, width = z3.shape
    nb = bw.shape[0]
    states = nb * SB
    kern = functools.partial(_ssm_prompt_kernel, nbatch=nbatch)
    nt = seq // SCAN_T
    assert 8 % nbatch == 0
    return pl.pallas_call(
        kern,
        grid=(nb, nt + 1),
        in_specs=[
            pl.BlockSpec((nbatch, SCAN_T, UB), lambda g, t: (0, jnp.minimum(t, nt - 1), g)),
            pl.BlockSpec((nbatch, SCAN_T, UB), lambda g, t: (0, jnp.maximum(t - 1, 0), g)),
            pl.BlockSpec((1, UB, 2 * SB), lambda g, t: (g, 0, 0)),
            pl.BlockSpec((1, 2 * SB, UB), lambda g, t: (g, 0, 0)),
            pl.BlockSpec((1, SLABS, 1, LANES), lambda g, t: (g, 0, 0, 0)),
            pl.BlockSpec((1, SLABS, 1, LANES), lambda g, t: (g, 0, 0, 0)),
            pl.BlockSpec((1, 1, UB), lambda g, t: (g, 0, 0)),
        ],
        out_specs=(
            pl.BlockSpec((nbatch, SCAN_T, UB), lambda g, t: (0, jnp.maximum(t - 1, 0), g)),
            pl.BlockSpec((nbatch, SB), lambda g, t: (0, g)),
            pl.BlockSpec((nbatch, SB), lambda g, t: (0, g)),
        ),
        out_shape=(
            jax.ShapeDtypeStruct((nbatch, seq, width), BF16),
            jax.ShapeDtypeStruct((nbatch, states), F32),
            jax.ShapeDtypeStruct((nbatch, states), F32),
        ),
        scratch_shapes=[
            pltpu.VMEM((2, SLABS * nbatch // 8, 8 * SCAN_PITCH, LANES), F32),
            pltpu.VMEM((2, SLABS * nbatch // 8, 8 * SCAN_PITCH, LANES), F32),
            pltpu.VMEM((SLABS * nbatch // 8, 8, LANES), F32),
            pltpu.VMEM((SLABS * nbatch // 8, 8, LANES), F32),
        ],
        compiler_params=_cparams(("arbitrary", "arbitrary")),
        name="ssm_prompt",
    )(z3, z3, bw, cw, a_re.reshape(nb, SLABS, 1, LANES), a_im.reshape(nb, SLABS, 1, LANES),
      d_skip.reshape(nb, 1, UB))


def _ssm_sample_kernel(u_ref, bw_ref, cw_ref, are_ref, aim_ref, d_ref, x0re_ref, x0im_ref,
                       y_ref, sre_ref, sim_ref):
    u = u_ref[...]
    bu = jnp.dot(u.astype(BF16), bw_ref[0], preferred_element_type=F32)
    a_re = are_ref[0]
    a_im = aim_ref[0]
    x_re = x0re_ref[...]
    x_im = x0im_ref[...]
    s_re = a_re * x_re - a_im * x_im + bu[:, :SB]
    s_im = a_re * x_im + a_im * x_re + bu[:, SB:]
    sre_ref[...] = s_re
    sim_ref[...] = s_im
    s_cat = jnp.concatenate([s_re, s_im], axis=1).astype(BF16)
    y = jnp.dot(s_cat, cw_ref[0], preferred_element_type=F32) + d_ref[0] * u
    y_ref[...] = _gelu(y).astype(y_ref.dtype)


def _ssm_sample(z, bw, cw, a_re, a_im, d_skip, x0_re, x0_im):
    nbatch = z.shape[0]
    nb = bw.shape[0]
    return pl.pallas_call(
        _ssm_sample_kernel,
        grid=(nb,),
        in_specs=[
            pl.BlockSpec((nbatch, UB), lambda g: (0, g)),
            pl.BlockSpec((1, UB, 2 * SB), lambda g: (g, 0, 0)),
            pl.BlockSpec((1, 2 * SB, UB), lambda g: (g, 0, 0)),
            pl.BlockSpec((1, 1, SB), lambda g: (g, 0, 0)),
            pl.BlockSpec((1, 1, SB), lambda g: (g, 0, 0)),
            pl.BlockSpec((1, 1, UB), lambda g: (g, 0, 0)),
            pl.BlockSpec((nbatch, SB), lambda g: (0, g)),
            pl.BlockSpec((nbatch, SB), lambda g: (0, g)),
        ],
        out_specs=(
            pl.BlockSpec((nbatch, UB), lambda g: (0, g)),
            pl.BlockSpec((nbatch, SB), lambda g: (0, g)),
            pl.BlockSpec((nbatch, SB), lambda g: (0, g)),
        ),
        out_shape=(
            jax.ShapeDtypeStruct((nbatch, nb * UB), BF16),
            jax.ShapeDtypeStruct((nbatch, nb * SB), F32),
            jax.ShapeDtypeStruct((nbatch, nb * SB), F32),
        ),
        compiler_params=_cparams(("arbitrary",)),
        name="ssm_sample",
    )(z, bw, cw, a_re.reshape(nb, 1, SB), a_im.reshape(nb, 1, SB), d_skip.reshape(nb, 1, UB),
      x0_re, x0_im)


def _gmlp_prompt_kernel(gu_ref, gv_ref, gnv_ref, ws_ref, bias_ref, wo_ref, gate_ref, p_ref,
                        out_ref, s_scr, wt_scr, *, tm, heads):
    i = pl.program_id(0)
    j = pl.program_id(1)

    @pl.when(jnp.logical_and(i == 0, j == 0))
    def _():
        row = lax.broadcasted_iota(jnp.int32, (CHUNK, CHUNK), 0)
        col = lax.broadcasted_iota(jnp.int32, (CHUNK, CHUNK), 1)
        mask = (col <= row).astype(F32)
        for g in range(heads):
            wt_scr[g] = (ws_ref[g] * mask).astype(BF16)

    @pl.when(j == 0)
    def _():
        gnv = gnv_ref[...]

        def body(c, carry):
            r0 = pl.multiple_of(c * CHUNK, CHUNK)
            gv = gv_ref[pl.ds(r0, CHUNK), :].astype(F32)
            r = lax.rsqrt(jnp.mean(gv * gv, axis=-1, keepdims=True) + EPS)
            v = (gv * r * gnv).astype(BF16)
            for g in range(heads):
                sl = slice(g * GMLP_HEAD, (g + 1) * GMLP_HEAD)
                mixed = jnp.dot(wt_scr[g], v[:, sl], preferred_element_type=F32) + bias_ref[:, sl]
                gu = gu_ref[pl.ds(r0, CHUNK), sl].astype(F32)
                s_scr[pl.ds(r0, CHUNK), sl] = (gu * mixed).astype(BF16)
            return carry

        lax.fori_loop(0, tm // CHUNK, body, 0)

    yb = jnp.dot(s_scr[...], _as_bf16(wo_ref, None), preferred_element_type=F32)
    out_ref[...] = (p_ref[...].astype(F32) + gate_ref[...].astype(F32) * yb).astype(out_ref.dtype)


def _gmlp_prompt(guv, gnv, w_s, bias_full, w_gout, gates, p, *, gate_col, tm, tn):
    m = guv.shape[0]
    width, d = w_gout.shape
    heads = w_s.shape[0]
    kern = functools.partial(_gmlp_prompt_kernel, tm=tm, heads=heads)
    return pl.pallas_call(
        kern,
        grid=(m // tm, d // tn),
        in_specs=[
            pl.BlockSpec((tm, width), lambda i, j: (i, 0)),
            pl.BlockSpec((tm, width), lambda i, j: (i, 1)),
            pl.BlockSpec((1, width), lambda i, j: (0, 0)),
            pl.BlockSpec((heads, CHUNK, CHUNK), lambda i, j: (0, 0, 0)),
            pl.BlockSpec((CHUNK, width), lambda i, j: (0, 0)),
            _col_tile_spec(w_gout, width, tn),
            pl.BlockSpec((tm, tn), lambda i, j: (i, j + gate_col // tn)),
            pl.BlockSpec((tm, tn), lambda i, j: (i, j)),
        ],
        out_specs=pl.BlockSpec((tm, tn), lambda i, j: (i, j)),
        out_shape=jax.ShapeDtypeStruct((m, d), BF16),
        scratch_shapes=[pltpu.VMEM((tm, width), BF16), pltpu.VMEM((heads, CHUNK, CHUNK), BF16)],
        compiler_params=_cparams(("arbitrary", "arbitrary")),
        name="gmlp_prompt",
    )(guv, guv, gnv, w_s, bias_full, w_gout, gates, p)


def _gmlp_sample_kernel(gu_ref, gv_ref, gnv_ref, wdiag_ref, bias_ref, wo_ref, gate_ref, p_ref,
                        out_ref, v_ref, s_scr):
    j = pl.program_id(0)

    @pl.when(j == 0)
    def _():
        gv = gv_ref[...]
        r = lax.rsqrt(jnp.mean(gv * gv, axis=-1, keepdims=True) + EPS)
        v = gv * r * gnv_ref[...]
        v_ref[...] = v
        mixed = wdiag_ref[...] * v + bias_ref[...]
        s_scr[...] = (gu_ref[...] * mixed).astype(BF16)

    yb = jnp.dot(s_scr[...], _as_bf16(wo_ref, None), preferred_element_type=F32)
    out_ref[...] = (p_ref[...] + gate_ref[...] * yb).astype(out_ref.dtype)


def _gmlp_sample(guv, gnv, wdiag, bias0, w_gout, gates, p, *, gate_col, tn):
    m = guv.shape[0]
    width, d = w_gout.shape
    return pl.pallas_call(
        _gmlp_sample_kernel,
        grid=(d // tn,),
        in_specs=[
            pl.BlockSpec((m, width), lambda j: (0, 0)),
            pl.BlockSpec((m, width), lambda j: (0, 1)),
            pl.BlockSpec((1, width), lambda j: (0, 0)),
            pl.BlockSpec((1, width), lambda j: (0, 0)),
            pl.BlockSpec((1, width), lambda j: (0, 0)),
            pl.BlockSpec((width, tn), lambda j: (0, j)),
            pl.BlockSpec((m, tn), lambda j: (0, j + gate_col // tn)),
            pl.BlockSpec((m, tn), lambda j: (0, j)),
        ],
        out_specs=(pl.BlockSpec((m, tn), lambda j: (0, j)),
                   pl.BlockSpec((m, width), lambda j: (0, 0))),
        out_shape=(jax.ShapeDtypeStruct((m, d), BF16), jax.ShapeDtypeStruct((m, width), F32)),
        scratch_shapes=[pltpu.VMEM((m, width), BF16)],
        compiler_params=_cparams(("arbitrary",)),
        name="gmlp_sample",
    )(guv, guv, gnv, wdiag, bias0, w_gout, gates, p)


def _mixer(x, h2, prm, w, *, nbatch, x0, tm_mm, copies=None):
    sample = x0 is not None
    copies = {} if copies is None else copies
    d_model = x.shape[1]
    ssm_width = prm["bw"].shape[0] * UB
    gmlp_width = prm["gnv"].shape[1]
    o1 = ssm_width
    o3 = o1 + 2 * gmlp_width
    tn = 1024

    w = dict(w, win_ssm=w["w_in"], win_uv=w["w_in"], win_gate=w["w_in"], glu_a=w["w_glu"], glu_b=w["w_glu"])
    cols = dict(win_uv=o1, win_gate=o3, glu_b=d_model)

    def mm(lhs, wkeys, extras, kind, n_out, out_dtype, name, copied=False):
        if copied and not sample:
            weights = [(copies[k], 0) for k in wkeys]
        else:
            weights = [(w[k], cols.get(k, 0)) for k in wkeys]
        out, emitted = _mm(lhs, weights, extras, kind=kind, n_out=n_out, out_dtype=out_dtype,
                           tm=tm_mm, tn=tn, emit=copied and sample, name=name)
        copies.update(zip(wkeys, emitted))
        return out

    act = F32 if sample else BF16
    z_ssm = mm(h2, ["win_ssm"], [], "identity", o1, F32, "w_in_ssm", copied=True)
    guv = mm(h2, ["win_uv"], [], "gelu", 2 * gmlp_width, act, "w_in_uv")
    gates = mm(h2, ["win_gate"], [], "sigmoid", 2 * d_model, act, "w_in_gate")

    if sample:
        y, s_re, s_im = _ssm_sample(z_ssm, prm["bw"], prm["cw"], prm["a_re"], prm["a_im"],
                                    prm["d_skip"], x0[0], x0[1])
    else:
        seq = x.shape[0] // nbatch
        y, s_re, s_im = _ssm_prompt(z_ssm.reshape(nbatch, seq, ssm_width), prm["bw"], prm["cw"],
                                    prm["a_re"], prm["a_im"], prm["d_skip"])
        y = y.reshape(nbatch * seq, ssm_width)
    p = mm(y, ["glu_a", "glu_b"], [(gates, 0)], "gated_glu", d_model, act, "ssm_glu")

    if sample:
        merged, v_rows = _gmlp_sample(guv, prm["gnv"], prm["w_diag"], prm["bias0"],
                                      w["w_gout"], gates, p, gate_col=d_model, tn=tn)
    else:
        merged = _gmlp_prompt(guv, prm["gnv"], prm["w_s"], prm["bias_full"], w["w_gout"], gates, p,
                              gate_col=d_model, tm=tm_mm, tn=tn)
        v_rows = None
    x = mm(merged, ["w_out"], [(x, 0)], "residual", d_model, F32, "out_proj", copied=True)
    return x, s_re, s_im, v_rows, copies


def kernel(x_prompt, x_sample, state_ssm_re, state_ssm_im, norm_ffn1, ffn1_gate, ffn1_up, ffn1_down, norm_mix, w_in, ssm_lambda_re, ssm_lambda_im, ssm_log_dt, ssm_b_re, ssm_b_im, ssm_c_re, ssm_c_im, ssm_d, ssm_w_glu, gmlp_norm_v, gmlp_w_s, gmlp_b_s, gmlp_w_out, w_out, norm_ffn2, ffn2_gate, ffn2_up, ffn2_down, norm_final):
    depth = w_in.shape[0]
    assert depth == 1, "the final RMSNorm is fused into the last FFN; one layer per step"
    batch, seq, d_model = x_prompt.shape
    dec_batch, dec_seq, _ = x_sample.shape
    assert dec_seq == 1
    groups, states = ssm_lambda_re.shape[1:]
    gmlp_width = gmlp_norm_v.shape[1]
    heads = gmlp_w_s.shape[1]
    head_dim = gmlp_width // heads
    assert seq % SCAN_T == 0 and seq % CHUNK == 0 and head_dim == GMLP_HEAD
    assert ssm_b_re.shape[-1] == SSM_GROUP and states == SSM_STATE and groups % GROUPS_PER_BLOCK == 0

    l = 0
    yp = x_prompt.reshape(batch * seq, d_model)
    ys = x_sample.reshape(dec_batch, d_model)
    gfin = norm_final.reshape(1, d_model)
    a_re, a_im, bw, cw = _discretise(ssm_lambda_re[l], ssm_lambda_im[l], ssm_log_dt[l],
                                     ssm_b_re[l], ssm_b_im[l], ssm_c_re[l], ssm_c_im[l])
    prm = dict(
        n1=norm_ffn1[l].reshape(1, d_model), nmix=norm_mix[l].reshape(1, d_model),
        n2=norm_ffn2[l].reshape(1, d_model),
        bw=bw, cw=cw, a_re=a_re, a_im=a_im, d_skip=ssm_d[l],
        gnv=gmlp_norm_v[l].reshape(1, gmlp_width), w_s=gmlp_w_s[l],
        bias_full=jnp.repeat(gmlp_b_s[l].T, head_dim, axis=1),
        w_diag=jnp.repeat(gmlp_w_s[l][:, 0, 0], head_dim).reshape(1, gmlp_width),
        bias0=jnp.repeat(gmlp_b_s[l][:, 0], head_dim).reshape(1, gmlp_width),
    )
    w_f32 = dict(w_in=w_in[l], w_glu=ssm_w_glu[l], w_gout=gmlp_w_out[l], w_out=w_out[l])
    x0 = (state_ssm_re[l].reshape(dec_batch, groups * states),
          state_ssm_im[l].reshape(dec_batch, groups * states))
    tm, tf, row_tiles = 1024, 256, 8

    yp, ys, hp, hs = _ffn(yp, ys, prm["n1"], ffn1_gate[l], ffn1_up[l], ffn1_down[l], prm["nmix"],
                          tail="norm_out", row_tiles=row_tiles, tf=tf)
    ys, sr, si, vr, copies = _mixer(ys, hs, prm, w_f32, nbatch=dec_batch, x0=x0, tm_mm=dec_batch)
    yp, pr, pi, _, _ = _mixer(yp, hp, prm, w_f32, nbatch=batch, x0=None, tm_mm=tm, copies=copies)
    yp, ys = _ffn(yp, ys, prm["n2"], ffn2_gate[l], ffn2_up[l], ffn2_down[l], gfin,
                  tail="final", row_tiles=row_tiles, tf=tf)
    return (yp.reshape(batch, seq, d_model), ys.reshape(dec_batch, dec_seq, d_model),
            pr.reshape(1, batch, groups, states), pi.reshape(1, batch, groups, states),
            sr.reshape(1, dec_batch, groups, states), si.reshape(1, dec_batch, groups, states),
            vr.reshape(1, dec_batch, dec_seq, gmlp_width))
```

```python
import functools
import math

import jax
import jax.numpy as jnp
from jax import lax
from jax.experimental import pallas as pl
from jax.experimental.pallas import tpu as pltpu

F32 = jnp.float32
BF16 = jnp.bfloat16

EPS = 1e-6
LANES = 128
SSM_GROUP = 16
SSM_STATE = 64
GROUPS_PER_BLOCK = 16
UB = GROUPS_PER_BLOCK * SSM_GROUP
SB = GROUPS_PER_BLOCK * SSM_STATE
SLABS = SB // LANES
SCAN_T = 256
SCAN_SKEW = 4
SCAN_PITCH = SCAN_T + SCAN_SKEW
CHUNK = 128
GMLP_HEAD = 128
VMEM_LIMIT = 62 * 1024 * 1024


def _cparams(sem):
    return pltpu.CompilerParams(dimension_semantics=sem, vmem_limit_bytes=VMEM_LIMIT)


def _gelu(x):
    c = math.sqrt(2.0 / math.pi)
    return 0.5 * x * (1.0 + jnp.tanh(c * (x + 0.044715 * (x * x * x))))


def _sigmoid(x):
    return 1.0 / (1.0 + jnp.exp(-x))


def _as_bf16(w_ref, emit_ref):
    w = w_ref[0] if len(w_ref.shape) == 3 else w_ref[...]
    if w.dtype != BF16:
        w = w.astype(BF16)
    if emit_ref is not None:
        if len(emit_ref.shape) == 3:
            emit_ref[0] = w
        else:
            emit_ref[...] = w
    return w


def _col_tile_spec(w, k, tn, col0=0):
    if w.ndim == 3:
        assert w.shape[1:] == (k, tn) and col0 == 0
        return pl.BlockSpec((1, k, tn), lambda i, j: (j, 0, 0))
    return pl.BlockSpec((k, tn), lambda i, j, o=col0 // tn: (0, j + o))


def _ffn_kernel(*refs, tm, d_model, col_chunk, row_chunk, tail, tail_rows):
    refs = list(refs)
    x_hbm = refs.pop(0)
    xt_hbm = refs.pop(0) if tail_rows else None
    gn_ref, wg_ref, wu_ref, wd_ref, gt_ref = refs[:5]
    del refs[:5]
    out_hbm, outt_hbm = refs.pop(0), refs.pop(0)
    h2_hbm, h2t_hbm = (refs.pop(0), refs.pop(0)) if tail == "norm_out" else (None, None)
    out_ref, h_buf, x_sem, xt_sem, chunk_sems, out_sems, h2_sems = refs
    h2_ref = h_buf
    i = pl.program_id(0)
    j = pl.program_id(1)
    last = pl.num_programs(0) - 1
    head_rows = tm - tail_rows

    def copy_in(src, dst, sem):
        cp = pltpu.make_async_copy(src, dst, sem)
        cp.start()
        return cp

    @pl.when(j == 0)
    def _():
        g = gn_ref[...]
        n_chunks = tm // row_chunk

        def chunk_copy(c):
            r0 = pl.multiple_of(c * row_chunk, row_chunk)
            return pltpu.make_async_copy(x_hbm.at[pl.ds(i * tm + r0, row_chunk), :],
                                         out_ref.at[pl.ds(r0, row_chunk), :], chunk_sems.at[c])

        def norm_rows(c, chunked):
            r0 = pl.multiple_of(c * row_chunk, row_chunk)
            if chunked:
                chunk_copy(c).wait()
            xf = out_ref[pl.ds(r0, row_chunk), :]
            r = lax.rsqrt(jnp.mean(xf * xf, axis=-1, keepdims=True) + EPS)
            h_buf[pl.ds(r0, row_chunk), :] = (xf * r * g).astype(BF16)
            out_ref[pl.ds(r0, row_chunk), :] = 2.0 * xf

        @pl.when(i < last)
        def _():
            def start(c, carry):
                chunk_copy(c).start()
                return carry

            lax.fori_loop(0, n_chunks, start, 0)
            lax.fori_loop(0, n_chunks, lambda c, carry: (norm_rows(c, True), carry)[1], 0)

        @pl.when(i == last)
        def _():
            n_x = x_hbm.shape[0]
            head = copy_in(x_hbm.at[pl.ds(n_x - head_rows, head_rows), :],
                           out_ref.at[pl.ds(0, head_rows), :], x_sem)
            rest = copy_in(xt_hbm, out_ref.at[pl.ds(head_rows, tail_rows), :], xt_sem)
            head.wait()
            rest.wait()
            lax.fori_loop(0, n_chunks, lambda c, carry: (norm_rows(c, False), carry)[1], 0)

    h = h_buf[...]
    g = jnp.dot(h, _as_bf16(wg_ref, None), preferred_element_type=F32)
    u = jnp.dot(h, _as_bf16(wu_ref, None), preferred_element_type=F32)
    a = (g * _sigmoid(g) * u).astype(BF16)
    for n in range(d_model // col_chunk):
        sl = slice(n * col_chunk, (n + 1) * col_chunk)
        out_ref[:, sl] += jnp.dot(a, wd_ref[:, sl].astype(BF16), preferred_element_type=F32)

    @pl.when(j == pl.num_programs(1) - 1)
    def _():
        gt = gt_ref[...]
        n_chunks = tm // row_chunk
        results = [(out_ref, out_hbm, outt_hbm, out_sems)]
        if tail == "norm_out":
            results.append((h2_ref, h2_hbm, h2t_hbm, h2_sems))

        def finish_rows(c):
            r0 = pl.multiple_of(c * row_chunk, row_chunk)
            y = 0.5 * out_ref[pl.ds(r0, row_chunk), :]
            r = lax.rsqrt(jnp.mean(y * y, axis=-1, keepdims=True) + EPS)
            if tail == "norm_out":
                out_ref[pl.ds(r0, row_chunk), :] = y
                h2_ref[pl.ds(r0, row_chunk), :] = (y * r * gt).astype(BF16)
            else:
                out_ref[pl.ds(r0, row_chunk), :] = y * r * gt

        def chunk_out(c, src, dst, sems):
            r0 = pl.multiple_of(c * row_chunk, row_chunk)
            return pltpu.make_async_copy(src.at[pl.ds(r0, row_chunk), :],
                                         dst.at[pl.ds(i * tm + r0, row_chunk), :], sems.at[c])

        @pl.when(i < last)
        def _():
            def finish(c, carry):
                finish_rows(c)
                for src, dst, _, sems in results:
                    chunk_out(c, src, dst, sems).start()
                return carry

            def drain(c, carry):
                for src, dst, _, sems in results:
                    chunk_out(c, src, dst, sems).wait()
                return carry

            lax.fori_loop(0, n_chunks, finish, 0)
            lax.fori_loop(0, n_chunks, drain, 0)

        @pl.when(i == last)
        def _():
            lax.fori_loop(0, n_chunks, lambda c, carry: (finish_rows(c), carry)[1], 0)
            copies = []
            for src, dst, dst_tail, sems in results:
                n_x = dst.shape[0]
                copies.append(copy_in(src.at[pl.ds(0, head_rows), :],
                                      dst.at[pl.ds(n_x - head_rows, head_rows), :], sems.at[0]))
                copies.append(copy_in(src.at[pl.ds(head_rows, tail_rows), :], dst_tail, sems.at[1]))
            for cp in copies:
                cp.wait()


def _ffn(x, x_tail, gn, wg, wu, wd, gtail, *, tail, row_tiles, tf):
    n_x, d = x.shape
    f = wd.shape[0]
    tail_rows = x_tail.shape[0]
    tm = (n_x + tail_rows) // row_tiles
    ni, nj = row_tiles, f // tf
    assert ni * tm == n_x + tail_rows and nj * tf == f
    assert 0 < tail_rows < tm and (tm - tail_rows) % 16 == 0
    row_chunk = max(c for c in range(16, 81, 16) if tm % c == 0)
    kern = functools.partial(_ffn_kernel, tm=tm, d_model=d, col_chunk=512, row_chunk=row_chunk,
                             tail=tail, tail_rows=tail_rows)
    in_specs = [
        pl.BlockSpec(memory_space=pl.ANY),
        pl.BlockSpec(memory_space=pl.ANY),
        pl.BlockSpec((1, d), lambda i, j: (0, 0)),
        pl.BlockSpec((d, tf), lambda i, j: (0, j)),
        pl.BlockSpec((d, tf), lambda i, j: (0, j)),
        pl.BlockSpec((tf, d), lambda i, j: (j, 0)),
        pl.BlockSpec((1, d), lambda i, j: (0, 0)),
    ]
    args = [x, x_tail, gn, wg, wu, wd, gtail]
    out_shape = [jax.ShapeDtypeStruct((n_x, d), F32), jax.ShapeDtypeStruct((tail_rows, d), F32)]
    if tail == "norm_out":
        out_shape += [jax.ShapeDtypeStruct((n_x, d), BF16), jax.ShapeDtypeStruct((tail_rows, d), BF16)]
    out_specs = [pl.BlockSpec(memory_space=pl.ANY)] * len(out_shape)
    n_chunks = tm // row_chunk
    assert n_chunks >= 2
    return pl.pallas_call(
        kern,
        grid=(ni, nj),
        in_specs=in_specs,
        out_specs=out_specs,
        out_shape=out_shape,
        scratch_shapes=[pltpu.VMEM((tm, d), F32), pltpu.VMEM((tm, d), BF16),
                        pltpu.SemaphoreType.DMA, pltpu.SemaphoreType.DMA,
                        pltpu.SemaphoreType.DMA((n_chunks,)), pltpu.SemaphoreType.DMA((n_chunks,)),
                        pltpu.SemaphoreType.DMA((n_chunks,))],
        compiler_params=_cparams(("arbitrary", "arbitrary")),
        name="ffn_" + tail,
    )(*args)


def _mm_epilogue(kind, accs, extras):
    if kind == "identity":
        return accs[0]
    if kind == "gelu":
        return _gelu(accs[0])
    if kind == "sigmoid":
        return _sigmoid(accs[0])
    if kind == "gated_glu":
        return extras[0].astype(F32) * (accs[0] * _sigmoid(accs[1]))
    if kind == "residual":
        return extras[0].astype(F32) + accs[0]
    raise ValueError(kind)


def _mm_kernel(*refs, n_w, n_extra, kind, emit):
    lhs_ref = refs[0]
    w_refs = refs[1:1 + n_w]
    extra_refs = refs[1 + n_w:1 + n_w + n_extra]
    out_ref = refs[1 + n_w + n_extra]
    emit_refs = refs[2 + n_w + n_extra:] if emit else [None] * n_w
    lhs = lhs_ref[...]
    accs = [jnp.dot(lhs, _as_bf16(w, e), preferred_element_type=F32) for w, e in zip(w_refs, emit_refs)]
    out_ref[...] = _mm_epilogue(kind, accs, [e[...] for e in extra_refs]).astype(out_ref.dtype)


def _mm(lhs, weights, extras, *, kind, n_out, out_dtype, tm, tn, emit, name):
    m, k = lhs.shape
    assert not emit or m == tm
    in_specs = [pl.BlockSpec((tm, k), lambda i, j: (i, 0))]
    args = [lhs]
    for w, c0 in weights:
        in_specs.append(_col_tile_spec(w, k, tn, c0))
        args.append(w)
    for e, c0 in extras:
        in_specs.append(pl.BlockSpec((tm, tn), lambda i, j, o=c0 // tn: (i, j + o)))
        args.append(e)
    out_specs = [pl.BlockSpec((tm, tn), lambda i, j: (i, j))]
    out_shape = [jax.ShapeDtypeStruct((m, n_out), out_dtype)]
    if emit:
        out_specs += [pl.BlockSpec((1, k, tn), lambda i, j: (j, 0, 0)) for _ in weights]
        out_shape += [jax.ShapeDtypeStruct((n_out // tn, k, tn), BF16) for _ in weights]
    kern = functools.partial(_mm_kernel, n_w=len(weights), n_extra=len(extras), kind=kind, emit=emit)
    res = pl.pallas_call(
        kern,
        grid=(m // tm, n_out // tn),
        in_specs=in_specs,
        out_specs=out_specs,
        out_shape=out_shape,
        compiler_params=_cparams(("arbitrary", "arbitrary")),
        name=name + ("_emit" if emit else ""),
    )(*args)
    return res[0], list(res[1:])


def _zoh(lam_re, lam_im, log_dt):
    dt = jnp.exp(log_dt)
    mag = jnp.exp(dt * lam_re)
    a_re = mag * jnp.cos(dt * lam_im)
    a_im = mag * jnp.sin(dt * lam_im)
    nr, ni = a_re - 1.0, a_im
    den = lam_re * lam_re + lam_im * lam_im
    coef_re = (nr * lam_re + ni * lam_im) / den
    coef_im = (ni * lam_re - nr * lam_im) / den
    return a_re, a_im, coef_re, coef_im


def _spread_lanes(x, reps):
    w = x.shape[1]
    assert w & (w - 1) == 0
    src = lax.broadcasted_iota(jnp.int32, (w, w * reps), 0)
    dst = lax.broadcasted_iota(jnp.int32, (w, w * reps), 1)
    sel = ((dst & (w - 1)) == src).astype(BF16)
    return jnp.dot(x.astype(BF16), sel, preferred_element_type=F32)


def _disc_kernel(lre_ref, lim_ref, ldt_ref, lre_col, lim_col, ldt_col, bre_ref, bim_ref, cre_ref, cim_ref,
                 are_ref, aim_ref, bw_ref, cw_ref):
    a_re, a_im, _, _ = _zoh(lre_ref[...], lim_ref[...], ldt_ref[...])
    are_ref[...] = a_re
    aim_ref[...] = a_im

    _, _, coef_re, coef_im = _zoh(lre_col[...], lim_col[...], ldt_col[...])
    b_re = bre_ref[...]
    b_im = bim_ref[...]
    bb_re = coef_re * b_re - coef_im * b_im
    bb_im = coef_re * b_im + coef_im * b_re
    state_shift = SSM_STATE.bit_length() - 1
    chan_shift = SSM_GROUP.bit_length() - 1
    row_g = lax.broadcasted_iota(jnp.int32, (SB, UB), 0) >> state_shift
    col_g = lax.broadcasted_iota(jnp.int32, (SB, UB), 1) >> chan_shift
    diag = row_g == col_g
    for part, bb in enumerate((bb_re, bb_im)):
        wt = jnp.where(diag, _spread_lanes(bb, GROUPS_PER_BLOCK), 0.0)
        bw_ref[0, :, part * SB:(part + 1) * SB] = wt.T.astype(BF16)

    row_g = lax.broadcasted_iota(jnp.int32, (UB, SB), 0) >> chan_shift
    col_g = lax.broadcasted_iota(jnp.int32, (UB, SB), 1) >> state_shift
    diag = row_g == col_g
    for part, c in enumerate((cre_ref[...], -cim_ref[...])):
        wt = jnp.where(diag, _spread_lanes(c, GROUPS_PER_BLOCK), 0.0)
        cw_ref[0, part * SB:(part + 1) * SB, :] = wt.T.astype(BF16)


def _discretise(lam_re, lam_im, log_dt, b_re, b_im, c_re, c_im):
    g, p = lam_re.shape
    h = b_re.shape[-1]
    nb = g // GROUPS_PER_BLOCK
    col = lambda x: x.reshape(g * p, 1)
    blk = lambda rows, cols: pl.BlockSpec((rows, cols), lambda i: (i, 0))
    return pl.pallas_call(
        _disc_kernel,
        grid=(nb,),
        in_specs=[blk(GROUPS_PER_BLOCK, p), blk(GROUPS_PER_BLOCK, p), blk(GROUPS_PER_BLOCK, 1),
                  blk(SB, 1), blk(SB, 1), blk(SB, 1),
                  blk(SB, h), blk(SB, h), blk(UB, p), blk(UB, p)],
        out_specs=(blk(GROUPS_PER_BLOCK, p), blk(GROUPS_PER_BLOCK, p),
                   pl.BlockSpec((1, UB, 2 * SB), lambda i: (i, 0, 0)),
                   pl.BlockSpec((1, 2 * SB, UB), lambda i: (i, 0, 0))),
        out_shape=(jax.ShapeDtypeStruct((g, p), F32), jax.ShapeDtypeStruct((g, p), F32),
                   jax.ShapeDtypeStruct((nb, UB, 2 * SB), BF16),
                   jax.ShapeDtypeStruct((nb, 2 * SB, UB), BF16)),
        compiler_params=_cparams(("arbitrary",)),
        name="s5_discretise",
    )(lam_re, lam_im, log_dt.reshape(g, 1),
      col(lam_re), col(lam_im), col(jnp.repeat(log_dt, p)),
      b_re.reshape(g * p, h), b_im.reshape(g * p, h), c_re.reshape(g * h, p), c_im.reshape(g * h, p))


def _ssm_prompt_kernel(*refs, nbatch):
    tc = pl.program_id(1)
    nt = pl.num_programs(1) - 1
    step = functools.partial(_ssm_prompt_step, *refs, nbatch=nbatch)
    pl.when(tc == 0)(functools.partial(step, fill=0, project=True, scan_prev=False))
    for fill in (0, 1):
        middle = jnp.logical_and(jnp.logical_and(tc > 0, tc < nt), tc % 2 == fill)
        pl.when(middle)(functools.partial(step, fill=fill, project=True, scan_prev=True))
        pl.when(jnp.logical_and(tc == nt, tc % 2 == fill))(
            functools.partial(step, fill=fill, project=False, scan_prev=True))


def _ssm_prompt_step(un_ref, u_ref, bw_ref, cw_ref, are_ref, aim_ref, d_ref,
                     y_ref, sre_ref, sim_ref,
                     bure2, buim2, st_re, st_im, *, nbatch, fill, project, scan_prev):
    tc = pl.program_id(1)
    scan = 1 - fill

    if not scan_prev:
        st_re[...] = jnp.zeros_like(st_re)
        st_im[...] = jnp.zeros_like(st_im)

    bure_f, buim_f = bure2.at[fill], buim2.at[fill]
    bure, buim = bure2.at[scan], buim2.at[scan]

    pack = 8 // nbatch
    nq = SLABS // pack

    def slab_rows(c, b):
        part, c = (0, c) if c < SLABS else (1, c - SLABS)
        v = (c // nq) * nbatch + b
        r0 = v * SCAN_PITCH + (v % 2) * SCAN_SKEW
        return part, c % nq, slice(r0, r0 + SCAN_T)

    if project:
        for ref in (bure_f, buim_f):
            for q in range(nq):
                for v in range(0, 8, 2):
                    ref[q, v * SCAN_PITCH + SCAN_T:v * SCAN_PITCH + SCAN_T + 2 * SCAN_SKEW, :] = (
                        jnp.zeros((2 * SCAN_SKEW, LANES), F32))

        u_bf = un_ref[...].reshape(nbatch * SCAN_T, UB).astype(BF16)
        per_dot = 4
        for n in range(2 * SLABS // per_dot):
            bu = jnp.dot(u_bf, bw_ref[0, :, per_dot * n * LANES:per_dot * (n + 1) * LANES],
                         preferred_element_type=F32)
            for h in range(per_dot):
                for b in range(nbatch):
                    part, q, rows = slab_rows(per_dot * n + h, b)
                    (bure_f, buim_f)[part][q, rows, :] = (
                        bu[b * SCAN_T:(b + 1) * SCAN_T, h * LANES:(h + 1) * LANES])
    if not scan_prev:
        return

    def packed(a_ref, q):
        return jnp.concatenate(
            [jnp.broadcast_to(a_ref[0, q + h * nq], (nbatch, LANES)) for h in range(pack)], axis=0)

    a_re = [packed(are_ref, q) for q in range(nq)]
    a_im = [packed(aim_ref, q) for q in range(nq)]
    init = tuple(st_re[q] for q in range(nq)) + tuple(st_im[q] for q in range(nq))

    def step(t, carry, active=None):
        rows = pl.ds(t, pack * nbatch, stride=SCAN_PITCH)
        new_re, new_im = [], []
        for q in range(nq):
            s_r, s_i = carry[q], carry[nq + q]
            b_r, b_i = bure[q, rows, :], buim[q, rows, :]
            n_r = a_re[q] * s_r - a_im[q] * s_i + b_r
            n_i = a_re[q] * s_i + a_im[q] * s_r + b_i
            if active is not None:
                b_r, b_i = jnp.where(active, n_r, b_r), jnp.where(active, n_i, b_i)
                n_r, n_i = jnp.where(active, n_r, s_r), jnp.where(active, n_i, s_i)
                bure[q, rows, :] = b_r
                buim[q, rows, :] = b_i
            else:
                bure[q, rows, :] = n_r
                buim[q, rows, :] = n_i
            new_re.append(n_r)
            new_im.append(n_i)
        return tuple(new_re) + tuple(new_im)

    odd = (lax.broadcasted_iota(jnp.int32, (8, LANES), 0) & 1) == 1
    carry = init
    for t in range(SCAN_SKEW):
        carry = step(t, carry, active=jnp.logical_not(odd))
    for t in range(SCAN_SKEW, SCAN_T):
        carry = step(t, carry)
    for t in range(SCAN_T, SCAN_T + SCAN_SKEW):
        carry = step(t, carry, active=odd)
    fin = carry
    for q in range(nq):
        st_re[q] = fin[q]
        st_im[q] = fin[nq + q]

    d = d_ref[0]
    for b in range(nbatch):
        y = d * u_ref[b]
        for n in range(SLABS):
            (p0, q0, rows0), (p1, q1, rows1) = slab_rows(2 * n, b), slab_rows(2 * n + 1, b)
            s_pair = jnp.concatenate([(bure, buim)[p0][q0, rows0, :], (bure, buim)[p1][q1, rows1, :]],
                                     axis=1).astype(BF16)
            y = y + jnp.dot(s_pair, cw_ref[0, 2 * n * LANES:(2 * n + 2) * LANES, :],
                            preferred_element_type=F32)
        y_ref[b] = _gelu(y).astype(y_ref.dtype)

    @pl.when(tc == pl.num_programs(1) - 1)
    def _():
        for c in range(SLABS):
            q, h = c % nq, c // nq
            sre_ref[:, c * LANES:(c + 1) * LANES] = st_re[q, h * nbatch:(h + 1) * nbatch, :]
            sim_ref[:, c * LANES:(c + 1) * LANES] = st_im[q, h * nbatch:(h + 1) * nbatch, :]


def _ssm_prompt(z3, bw, cw, a_re, a_im, d_skip):
    nbatch, seq, width = z3.shape
    nb = bw.shape[0]
    states = nb * SB
    kern = functools.partial(_ssm_prompt_kernel, nbatch=nbatch)
    nt = seq // SCAN_T
    assert 8 % nbatch == 0
    return pl.pallas_call(
        kern,
        grid=(nb, nt + 1),
        in_specs=[
            pl.BlockSpec((nbatch, SCAN_T, UB), lambda g, t: (0, jnp.minimum(t, nt - 1), g)),
            pl.BlockSpec((nbatch, SCAN_T, UB), lambda g, t: (0, jnp.maximum(t - 1, 0), g)),
            pl.BlockSpec((1, UB, 2 * SB), lambda g, t: (g, 0, 0)),
            pl.BlockSpec((1, 2 * SB, UB), lambda g, t: (g, 0, 0)),
            pl.BlockSpec((1, SLABS, 1, LANES), lambda g, t: (g, 0, 0, 0)),
            pl.BlockSpec((1, SLABS, 1, LANES), lambda g, t: (g, 0, 0, 0)),
            pl.BlockSpec((1, 1, UB), lambda g, t: (g, 0, 0)),
        ],
        out_specs=(
            pl.BlockSpec((nbatch, SCAN_T, UB), lambda g, t: (0, jnp.maximum(t - 1, 0), g)),
            pl.BlockSpec((nbatch, SB), lambda g, t: (0, g)),
            pl.BlockSpec((nbatch, SB), lambda g, t: (0, g)),
        ),
        out_shape=(
            jax.ShapeDtypeStruct((nbatch, seq, width), BF16),
            jax.ShapeDtypeStruct((nbatch, states), F32),
            jax.ShapeDtypeStruct((nbatch, states), F32),
        ),
        scratch_shapes=[
            pltpu.VMEM((2, SLABS * nbatch // 8, 8 * SCAN_PITCH, LANES), F32),
            pltpu.VMEM((2, SLABS * nbatch // 8, 8 * SCAN_PITCH, LANES), F32),
            pltpu.VMEM((SLABS * nbatch // 8, 8, LANES), F32),
            pltpu.VMEM((SLABS * nbatch // 8, 8, LANES), F32),
        ],
        compiler_params=_cparams(("arbitrary", "arbitrary")),
        name="ssm_prompt",
    )(z3, z3, bw, cw, a_re.reshape(nb, SLABS, 1, LANES), a_im.reshape(nb, SLABS, 1, LANES),
      d_skip.reshape(nb, 1, UB))


def _ssm_sample_kernel(u_ref, bw_ref, cw_ref, are_ref, aim_ref, d_ref, x0re_ref, x0im_ref,
                       y_ref, sre_ref, sim_ref):
    u = u_ref[...]
    bu = jnp.dot(u.astype(BF16), bw_ref[0], preferred_element_type=F32)
    a_re = are_ref[0]
    a_im = aim_ref[0]
    x_re = x0re_ref[...]
    x_im = x0im_ref[...]
    s_re = a_re * x_re - a_im * x_im + bu[:, :SB]
    s_im = a_re * x_im + a_im * x_re + bu[:, SB:]
    sre_ref[...] = s_re
    sim_ref[...] = s_im
    s_cat = jnp.concatenate([s_re, s_im], axis=1).astype(BF16)
    y = jnp.dot(s_cat, cw_ref[0], preferred_element_type=F32) + d_ref[0] * u
    y_ref[...] = _gelu(y).astype(y_ref.dtype)


def _ssm_sample(z, bw, cw, a_re, a_im, d_skip, x0_re, x0_im):
    nbatch = z.shape[0]
    nb = bw.shape[0]
    return pl.pallas_call(
        _ssm_sample_kernel,
        grid=(nb,),
        in_specs=[
            pl.BlockSpec((nbatch, UB), lambda g: (0, g)),
            pl.BlockSpec((1, UB, 2 * SB), lambda g: (g, 0, 0)),
            pl.BlockSpec((1, 2 * SB, UB), lambda g: (g, 0, 0)),
            pl.BlockSpec((1, 1, SB), lambda g: (g, 0, 0)),
            pl.BlockSpec((1, 1, SB), lambda g: (g, 0, 0)),
            pl.BlockSpec((1, 1, UB), lambda g: (g, 0, 0)),
            pl.BlockSpec((nbatch, SB), lambda g: (0, g)),
            pl.BlockSpec((nbatch, SB), lambda g: (0, g)),
        ],
        out_specs=(
            pl.BlockSpec((nbatch, UB), lambda g: (0, g)),
            pl.BlockSpec((nbatch, SB), lambda g: (0, g)),
            pl.BlockSpec((nbatch, SB), lambda g: (0, g)),
        ),
        out_shape=(
            jax.ShapeDtypeStruct((nbatch, nb * UB), BF16),
            jax.ShapeDtypeStruct((nbatch, nb * SB), F32),
            jax.ShapeDtypeStruct((nbatch, nb * SB), F32),
        ),
        compiler_params=_cparams(("arbitrary",)),
        name="ssm_sample",
    )(z, bw, cw, a_re.reshape(nb, 1, SB), a_im.reshape(nb, 1, SB), d_skip.reshape(nb, 1, UB),
      x0_re, x0_im)


def _gmlp_prompt_kernel(gu_ref, gv_ref, gnv_ref, ws_ref, bias_ref, wo_ref, gate_ref, p_ref,
                        out_ref, s_scr, wt_scr, *, tm, heads):
    i = pl.program_id(0)
    j = pl.program_id(1)

    @pl.when(jnp.logical_and(i == 0, j == 0))
    def _():
        row = lax.broadcasted_iota(jnp.int32, (CHUNK, CHUNK), 0)
        col = lax.broadcasted_iota(jnp.int32, (CHUNK, CHUNK), 1)
        mask = (col <= row).astype(F32)
        for g in range(heads):
            wt_scr[g] = (ws_ref[g] * mask).astype(BF16)

    @pl.when(j == 0)
    def _():
        gnv = gnv_ref[...]

        def body(c, carry):
            r0 = pl.multiple_of(c * CHUNK, CHUNK)
            gv = gv_ref[pl.ds(r0, CHUNK), :].astype(F32)
            r = lax.rsqrt(jnp.mean(gv * gv, axis=-1, keepdims=True) + EPS)
            v = (gv * r * gnv).astype(BF16)
            for g in range(heads):
                sl = slice(g * GMLP_HEAD, (g + 1) * GMLP_HEAD)
                mixed = jnp.dot(wt_scr[g], v[:, sl], preferred_element_type=F32) + bias_ref[:, sl]
                gu = gu_ref[pl.ds(r0, CHUNK), sl].astype(F32)
                s_scr[pl.ds(r0, CHUNK), sl] = (gu * mixed).astype(BF16)
            return carry

        lax.fori_loop(0, tm // CHUNK, body, 0)

    yb = jnp.dot(s_scr[...], _as_bf16(wo_ref, None), preferred_element_type=F32)
    out_ref[...] = (p_ref[...].astype(F32) + gate_ref[...].astype(F32) * yb).astype(out_ref.dtype)


def _gmlp_prompt(guv, gnv, w_s, bias_full, w_gout, gates, p, *, gate_col, tm, tn):
    m = guv.shape[0]
    ntiles, width, _ = w_gout.shape
    d = ntiles * tn
    heads = w_s.shape[0]
    kern = functools.partial(_gmlp_prompt_kernel, tm=tm, heads=heads)
    return pl.pallas_call(
        kern,
        grid=(m // tm, d // tn),
        in_specs=[
            pl.BlockSpec((tm, width), lambda i, j: (i, 0)),
            pl.BlockSpec((tm, width), lambda i, j: (i, 1)),
            pl.BlockSpec((1, width), lambda i, j: (0, 0)),
            pl.BlockSpec((heads, CHUNK, CHUNK), lambda i, j: (0, 0, 0)),
            pl.BlockSpec((CHUNK, width), lambda i, j: (0, 0)),
            _col_tile_spec(w_gout, width, tn),
            pl.BlockSpec((tm, tn), lambda i, j: (i, j + gate_col // tn)),
            pl.BlockSpec((tm, tn), lambda i, j: (i, j)),
        ],
        out_specs=pl.BlockSpec((tm, tn), lambda i, j: (i, j)),
        out_shape=jax.ShapeDtypeStruct((m, d), BF16),
        scratch_shapes=[pltpu.VMEM((tm, width), BF16), pltpu.VMEM((heads, CHUNK, CHUNK), BF16)],
        compiler_params=_cparams(("arbitrary", "arbitrary")),
        name="gmlp_prompt",
    )(guv, guv, gnv, w_s, bias_full, w_gout, gates, p)


def _gmlp_sample_kernel(gu_ref, gv_ref, gnv_ref, wdiag_ref, bias_ref, wo_ref, gate_ref, p_ref,
                        out_ref, v_ref, wcopy_ref, s_scr):
    j = pl.program_id(0)

    @pl.when(j == 0)
    def _():
        gv = gv_ref[...]
        r = lax.rsqrt(jnp.mean(gv * gv, axis=-1, keepdims=True) + EPS)
        v = gv * r * gnv_ref[...]
        v_ref[...] = v
        mixed = wdiag_ref[...] * v + bias_ref[...]
        s_scr[...] = (gu_ref[...] * mixed).astype(BF16)

    yb = jnp.dot(s_scr[...], _as_bf16(wo_ref, wcopy_ref), preferred_element_type=F32)
    out_ref[...] = (p_ref[...] + gate_ref[...] * yb).astype(out_ref.dtype)


def _gmlp_sample(guv, gnv, wdiag, bias0, w_gout, gates, p, *, gate_col, tn):
    m = guv.shape[0]
    width, d = w_gout.shape
    return pl.pallas_call(
        _gmlp_sample_kernel,
        grid=(d // tn,),
        in_specs=[
            pl.BlockSpec((m, width), lambda j: (0, 0)),
            pl.BlockSpec((m, width), lambda j: (0, 1)),
            pl.BlockSpec((1, width), lambda j: (0, 0)),
            pl.BlockSpec((1, width), lambda j: (0, 0)),
            pl.BlockSpec((1, width), lambda j: (0, 0)),
            pl.BlockSpec((width, tn), lambda j: (0, j)),
            pl.BlockSpec((m, tn), lambda j: (0, j + gate_col // tn)),
            pl.BlockSpec((m, tn), lambda j: (0, j)),
        ],
        out_specs=(pl.BlockSpec((m, tn), lambda j: (0, j)),
                   pl.BlockSpec((m, width), lambda j: (0, 0)),
                   pl.BlockSpec((1, width, tn), lambda j: (j, 0, 0))),
        out_shape=(jax.ShapeDtypeStruct((m, d), BF16), jax.ShapeDtypeStruct((m, width), F32),
                   jax.ShapeDtypeStruct((d // tn, width, tn), BF16)),
        scratch_shapes=[pltpu.VMEM((m, width), BF16)],
        compiler_params=_cparams(("arbitrary",)),
        name="gmlp_sample",
    )(guv, guv, gnv, wdiag, bias0, w_gout, gates, p)


def _mixer(x, h2, prm, w, *, nbatch, x0, tm_mm, copies=None):
    sample = x0 is not None
    copies = {} if copies is None else copies
    d_model = x.shape[1]
    ssm_width = prm["bw"].shape[0] * UB
    gmlp_width = prm["gnv"].shape[1]
    o1 = ssm_width
    o3 = o1 + 2 * gmlp_width
    tn = 1024

    w = dict(w, win_ssm=w["w_in"], win_uv=w["w_in"], win_gate=w["w_in"], glu_a=w["w_glu"], glu_b=w["w_glu"])
    cols = dict(win_uv=o1, win_gate=o3, glu_b=d_model)

    def mm(lhs, wkeys, extras, kind, n_out, out_dtype, name, copied=False):
        if copied and not sample:
            weights = [(copies[k], 0) for k in wkeys]
        else:
            weights = [(w[k], cols.get(k, 0)) for k in wkeys]
        out, emitted = _mm(lhs, weights, extras, kind=kind, n_out=n_out, out_dtype=out_dtype,
                           tm=tm_mm, tn=tn, emit=copied and sample, name=name)
        copies.update(zip(wkeys, emitted))
        return out

    act = F32 if sample else BF16
    z_ssm = mm(h2, ["win_ssm"], [], "identity", o1, F32, "w_in_ssm", copied=True)
    guv = mm(h2, ["win_uv"], [], "gelu", 2 * gmlp_width, act, "w_in_uv")
    gates = mm(h2, ["win_gate"], [], "sigmoid", 2 * d_model, act, "w_in_gate")

    if sample:
        y, s_re, s_im = _ssm_sample(z_ssm, prm["bw"], prm["cw"], prm["a_re"], prm["a_im"],
                                    prm["d_skip"], x0[0], x0[1])
    else:
        seq = x.shape[0] // nbatch
        y, s_re, s_im = _ssm_prompt(z_ssm.reshape(nbatch, seq, ssm_width), prm["bw"], prm["cw"],
                                    prm["a_re"], prm["a_im"], prm["d_skip"])
        y = y.reshape(nbatch * seq, ssm_width)
    p = mm(y, ["glu_a", "glu_b"], [(gates, 0)], "gated_glu", d_model, act, "ssm_glu")

    if sample:
        merged, v_rows, copies["w_gout"] = _gmlp_sample(guv, prm["gnv"], prm["w_diag"], prm["bias0"],
                                                        w["w_gout"], gates, p, gate_col=d_model, tn=tn)
    else:
        merged = _gmlp_prompt(guv, prm["gnv"], prm["w_s"], prm["bias_full"], copies["w_gout"], gates, p,
                              gate_col=d_model, tm=tm_mm, tn=tn)
        v_rows = None
    x = mm(merged, ["w_out"], [(x, 0)], "residual", d_model, F32, "out_proj", copied=True)
    return x, s_re, s_im, v_rows, copies


def kernel(x_prompt, x_sample, state_ssm_re, state_ssm_im, norm_ffn1, ffn1_gate, ffn1_up, ffn1_down, norm_mix, w_in, ssm_lambda_re, ssm_lambda_im, ssm_log_dt, ssm_b_re, ssm_b_im, ssm_c_re, ssm_c_im, ssm_d, ssm_w_glu, gmlp_norm_v, gmlp_w_s, gmlp_b_s, gmlp_w_out, w_out, norm_ffn2, ffn2_gate, ffn2_up, ffn2_down, norm_final):
    depth = w_in.shape[0]
    assert depth == 1, "the final RMSNorm is fused into the last FFN; one layer per step"
    batch, seq, d_model = x_prompt.shape
    dec_batch, dec_seq, _ = x_sample.shape
    assert dec_seq == 1
    groups, states = ssm_lambda_re.shape[1:]
    gmlp_width = gmlp_norm_v.shape[1]
    heads = gmlp_w_s.shape[1]
    head_dim = gmlp_width // heads
    assert seq % SCAN_T == 0 and seq % CHUNK == 0 and head_dim == GMLP_HEAD
    assert ssm_b_re.shape[-1] == SSM_GROUP and states == SSM_STATE and groups % GROUPS_PER_BLOCK == 0

    l = 0
    yp = x_prompt.reshape(batch * seq, d_model)
    ys = x_sample.reshape(dec_batch, d_model)
    gfin = norm_final.reshape(1, d_model)
    a_re, a_im, bw, cw = _discretise(ssm_lambda_re[l], ssm_lambda_im[l], ssm_log_dt[l],
                                     ssm_b_re[l], ssm_b_im[l], ssm_c_re[l], ssm_c_im[l])
    prm = dict(
        n1=norm_ffn1[l].reshape(1, d_model), nmix=norm_mix[l].reshape(1, d_model),
        n2=norm_ffn2[l].reshape(1, d_model),
        bw=bw, cw=cw, a_re=a_re, a_im=a_im, d_skip=ssm_d[l],
        gnv=gmlp_norm_v[l].reshape(1, gmlp_width), w_s=gmlp_w_s[l],
        bias_full=jnp.repeat(gmlp_b_s[l].T, head_dim, axis=1),
        w_diag=jnp.repeat(gmlp_w_s[l][:, 0, 0], head_dim).reshape(1, gmlp_width),
        bias0=jnp.repeat(gmlp_b_s[l][:, 0], head_dim).reshape(1, gmlp_width),
    )
    w_f32 = dict(w_in=w_in[l], w_glu=ssm_w_glu[l], w_gout=gmlp_w_out[l], w_out=w_out[l])
    x0 = (state_ssm_re[l].reshape(dec_batch, groups * states),
          state_ssm_im[l].reshape(dec_batch, groups * states))
    tm, tf, row_tiles = 1024, 256, 8

    yp, ys, hp, hs = _ffn(yp, ys, prm["n1"], ffn1_gate[l], ffn1_up[l], ffn1_down[l], prm["nmix"],
                          tail="norm_out", row_tiles=row_tiles, tf=tf)
    ys, sr, si, vr, copies = _mixer(ys, hs, prm, w_f32, nbatch=dec_batch, x0=x0, tm_mm=dec_batch)
    yp, pr, pi, _, _ = _mixer(yp, hp, prm, w_f32, nbatch=batch, x0=None, tm_mm=tm, copies=copies)
    yp, ys = _ffn(yp, ys, prm["n2"], ffn2_gate[l], ffn2_up[l], ffn2_down[l], gfin,
                  tail="final", row_tiles=row_tiles, tf=tf)
    return (yp.reshape(batch, seq, d_model), ys.reshape(dec_batch, dec_seq, d_model),
            pr.reshape(1, batch, groups, states), pi.reshape(1, batch, groups, states),
            sr.reshape(1, dec_batch, groups, states), si.reshape(1, dec_batch, groups, states),
            vr.reshape(1, dec_batch, dec_seq, gmlp_width))
```

```python
import functools
import math

import jax
import jax.numpy as jnp
from jax import lax
from jax.experimental import pallas as pl
from jax.experimental.pallas import tpu as pltpu

F32 = jnp.float32
BF16 = jnp.bfloat16

EPS = 1e-6
LANES = 128
SSM_GROUP = 16
SSM_STATE = 64
GROUPS_PER_BLOCK = 16
UB = GROUPS_PER_BLOCK * SSM_GROUP
SB = GROUPS_PER_BLOCK * SSM_STATE
SLABS = SB // LANES
SCAN_T = 256
SCAN_SKEW = 4
SCAN_PITCH = SCAN_T + SCAN_SKEW
CHUNK = 128
GMLP_HEAD = 128
VMEM_LIMIT = 62 * 1024 * 1024


def _cparams(sem):
    return pltpu.CompilerParams(dimension_semantics=sem, vmem_limit_bytes=VMEM_LIMIT)


def _gelu(x):
    c = math.sqrt(2.0 / math.pi)
    return 0.5 * x * (1.0 + jnp.tanh(c * (x + 0.044715 * (x * x * x))))


def _sigmoid(x):
    return 1.0 / (1.0 + jnp.exp(-x))


def _as_bf16(w_ref, emit_ref):
    w = w_ref[0] if len(w_ref.shape) == 3 else w_ref[...]
    if w.dtype != BF16:
        w = w.astype(BF16)
    if emit_ref is not None:
        if len(emit_ref.shape) == 3:
            emit_ref[0] = w
        else:
            emit_ref[...] = w
    return w


def _col_tile_spec(w, k, tn, col0=0):
    if w.ndim == 3:
        assert w.shape[1:] == (k, tn) and col0 == 0
        return pl.BlockSpec((1, k, tn), lambda i, j: (j, 0, 0))
    return pl.BlockSpec((k, tn), lambda i, j, o=col0 // tn: (0, j + o))


def _ffn_kernel(*refs, tm, d_model, col_chunk, row_chunk, tail, tail_rows):
    refs = list(refs)
    x_hbm = refs.pop(0)
    xt_hbm = refs.pop(0) if tail_rows else None
    gn_ref, wg_ref, wu_ref, wd_ref, gt_ref = refs[:5]
    del refs[:5]
    out_hbm, outt_hbm = refs.pop(0), refs.pop(0)
    h2_hbm, h2t_hbm = (refs.pop(0), refs.pop(0)) if tail == "norm_out" else (None, None)
    out_ref, h_buf, x_sem, xt_sem, chunk_sems, out_sems, h2_sems = refs
    h2_ref = h_buf
    i = pl.program_id(0)
    j = pl.program_id(1)
    last = pl.num_programs(0) - 1
    head_rows = tm - tail_rows

    def start_copy(src, dst, sem):
        cp = pltpu.make_async_copy(src, dst, sem)
        cp.start()
        return cp

    @pl.when(j == 0)
    def _():
        g = gn_ref[...]
        n_chunks = tm // row_chunk

        def chunk_copy(c):
            r0 = pl.multiple_of(c * row_chunk, row_chunk)
            return pltpu.make_async_copy(x_hbm.at[pl.ds(i * tm + r0, row_chunk), :],
                                         out_ref.at[pl.ds(r0, row_chunk), :], chunk_sems.at[c])

        def norm_rows(c, chunked):
            r0 = pl.multiple_of(c * row_chunk, row_chunk)
            if chunked:
                chunk_copy(c).wait()
            xf = out_ref[pl.ds(r0, row_chunk), :]
            r = lax.rsqrt(jnp.mean(xf * xf, axis=-1, keepdims=True) + EPS)
            h_buf[pl.ds(r0, row_chunk), :] = (xf * r * g).astype(BF16)
            out_ref[pl.ds(r0, row_chunk), :] = 2.0 * xf

        @pl.when(i < last)
        def _():
            def start(c, carry):
                chunk_copy(c).start()
                return carry

            lax.fori_loop(0, n_chunks, start, 0)
            lax.fori_loop(0, n_chunks, lambda c, carry: (norm_rows(c, True), carry)[1], 0)

        @pl.when(i == last)
        def _():
            n_x = x_hbm.shape[0]
            head = start_copy(x_hbm.at[pl.ds(n_x - head_rows, head_rows), :],
                           out_ref.at[pl.ds(0, head_rows), :], x_sem)
            rest = start_copy(xt_hbm, out_ref.at[pl.ds(head_rows, tail_rows), :], xt_sem)
            head.wait()
            rest.wait()
            lax.fori_loop(0, n_chunks, lambda c, carry: (norm_rows(c, False), carry)[1], 0)

    h = h_buf[...]
    g = jnp.dot(h, _as_bf16(wg_ref, None), preferred_element_type=F32)
    u = jnp.dot(h, _as_bf16(wu_ref, None), preferred_element_type=F32)
    a = (g * _sigmoid(g) * u).astype(BF16)
    for n in range(d_model // col_chunk):
        sl = slice(n * col_chunk, (n + 1) * col_chunk)
        out_ref[:, sl] += jnp.dot(a, wd_ref[:, sl].astype(BF16), preferred_element_type=F32)

    @pl.when(j == pl.num_programs(1) - 1)
    def _():
        gt = gt_ref[...]
        n_chunks = tm // row_chunk
        results = [(out_ref, out_hbm, outt_hbm, out_sems)]
        if tail == "norm_out":
            results.append((h2_ref, h2_hbm, h2t_hbm, h2_sems))

        def finish_rows(c):
            r0 = pl.multiple_of(c * row_chunk, row_chunk)
            y = 0.5 * out_ref[pl.ds(r0, row_chunk), :]
            r = lax.rsqrt(jnp.mean(y * y, axis=-1, keepdims=True) + EPS)
            if tail == "norm_out":
                out_ref[pl.ds(r0, row_chunk), :] = y
                h2_ref[pl.ds(r0, row_chunk), :] = (y * r * gt).astype(BF16)
            else:
                out_ref[pl.ds(r0, row_chunk), :] = y * r * gt

        def chunk_out(c, src, dst, sems):
            r0 = pl.multiple_of(c * row_chunk, row_chunk)
            return pltpu.make_async_copy(src.at[pl.ds(r0, row_chunk), :],
                                         dst.at[pl.ds(i * tm + r0, row_chunk), :], sems.at[c])

        @pl.when(i < last)
        def _():
            def finish(c, carry):
                finish_rows(c)
                for src, dst, _, sems in results:
                    chunk_out(c, src, dst, sems).start()
                return carry

            def drain(c, carry):
                for src, dst, _, sems in results:
                    chunk_out(c, src, dst, sems).wait()
                return carry

            lax.fori_loop(0, n_chunks, finish, 0)
            lax.fori_loop(0, n_chunks, drain, 0)

        @pl.when(i == last)
        def _():
            lax.fori_loop(0, n_chunks, lambda c, carry: (finish_rows(c), carry)[1], 0)
            copies = []
            for src, dst, dst_tail, sems in results:
                n_x = dst.shape[0]
                copies.append(start_copy(src.at[pl.ds(0, head_rows), :],
                                      dst.at[pl.ds(n_x - head_rows, head_rows), :], sems.at[0]))
                copies.append(start_copy(src.at[pl.ds(head_rows, tail_rows), :], dst_tail, sems.at[1]))
            for cp in copies:
                cp.wait()


def _ffn(x, x_tail, gn, wg, wu, wd, gtail, *, tail, row_tiles, tf):
    n_x, d = x.shape
    f = wd.shape[0]
    tail_rows = x_tail.shape[0]
    tm = (n_x + tail_rows) // row_tiles
    ni, nj = row_tiles, f // tf
    assert ni * tm == n_x + tail_rows and nj * tf == f
    assert 0 < tail_rows < tm and (tm - tail_rows) % 16 == 0
    row_chunk = max(c for c in range(16, 81, 16) if tm % c == 0)
    kern = functools.partial(_ffn_kernel, tm=tm, d_model=d, col_chunk=512, row_chunk=row_chunk,
                             tail=tail, tail_rows=tail_rows)
    in_specs = [
        pl.BlockSpec(memory_space=pl.ANY),
        pl.BlockSpec(memory_space=pl.ANY),
        pl.BlockSpec((1, d), lambda i, j: (0, 0)),
        pl.BlockSpec((d, tf), lambda i, j: (0, j)),
        pl.BlockSpec((d, tf), lambda i, j: (0, j)),
        pl.BlockSpec((tf, d), lambda i, j: (j, 0)),
        pl.BlockSpec((1, d), lambda i, j: (0, 0)),
    ]
    args = [x, x_tail, gn, wg, wu, wd, gtail]
    out_shape = [jax.ShapeDtypeStruct((n_x, d), F32), jax.ShapeDtypeStruct((tail_rows, d), F32)]
    if tail == "norm_out":
        out_shape += [jax.ShapeDtypeStruct((n_x, d), BF16), jax.ShapeDtypeStruct((tail_rows, d), BF16)]
    out_specs = [pl.BlockSpec(memory_space=pl.ANY)] * len(out_shape)
    n_chunks = tm // row_chunk
    assert n_chunks >= 2
    return pl.pallas_call(
        kern,
        grid=(ni, nj),
        in_specs=in_specs,
        out_specs=out_specs,
        out_shape=out_shape,
        scratch_shapes=[pltpu.VMEM((tm, d), F32), pltpu.VMEM((tm, d), BF16),
                        pltpu.SemaphoreType.DMA, pltpu.SemaphoreType.DMA,
                        pltpu.SemaphoreType.DMA((n_chunks,)), pltpu.SemaphoreType.DMA((n_chunks,)),
                        pltpu.SemaphoreType.DMA((n_chunks,))],
        compiler_params=_cparams(("arbitrary", "arbitrary")),
        name="ffn_" + tail,
    )(*args)


def _mm_epilogue(kind, accs, extras):
    if kind == "identity":
        return accs[0]
    if kind == "gelu":
        return _gelu(accs[0])
    if kind == "sigmoid":
        return _sigmoid(accs[0])
    if kind == "gated_glu":
        return extras[0].astype(F32) * (accs[0] * _sigmoid(accs[1]))
    if kind == "residual":
        return extras[0].astype(F32) + accs[0]
    raise ValueError(kind)


def _mm_kernel(*refs, n_w, n_extra, kind, emit):
    lhs_ref = refs[0]
    w_refs = refs[1:1 + n_w]
    extra_refs = refs[1 + n_w:1 + n_w + n_extra]
    out_ref = refs[1 + n_w + n_extra]
    emit_refs = refs[2 + n_w + n_extra:] if emit else [None] * n_w
    lhs = lhs_ref[...]
    accs = [jnp.dot(lhs, _as_bf16(w, e), preferred_element_type=F32) for w, e in zip(w_refs, emit_refs)]
    out_ref[...] = _mm_epilogue(kind, accs, [e[...] for e in extra_refs]).astype(out_ref.dtype)


def _mm(lhs, weights, extras, *, kind, n_out, out_dtype, tm, tn, emit, name):
    m, k = lhs.shape
    assert not emit or m == tm
    in_specs = [pl.BlockSpec((tm, k), lambda i, j: (i, 0))]
    args = [lhs]
    for w, c0 in weights:
        in_specs.append(_col_tile_spec(w, k, tn, c0))
        args.append(w)
    for e, c0 in extras:
        in_specs.append(pl.BlockSpec((tm, tn), lambda i, j, o=c0 // tn: (i, j + o)))
        args.append(e)
    out_specs = [pl.BlockSpec((tm, tn), lambda i, j: (i, j))]
    out_shape = [jax.ShapeDtypeStruct((m, n_out), out_dtype)]
    if emit:
        out_specs += [pl.BlockSpec((1, k, tn), lambda i, j: (j, 0, 0)) for _ in weights]
        out_shape += [jax.ShapeDtypeStruct((n_out // tn, k, tn), BF16) for _ in weights]
    kern = functools.partial(_mm_kernel, n_w=len(weights), n_extra=len(extras), kind=kind, emit=emit)
    res = pl.pallas_call(
        kern,
        grid=(m // tm, n_out // tn),
        in_specs=in_specs,
        out_specs=out_specs,
        out_shape=out_shape,
        compiler_params=_cparams(("arbitrary", "arbitrary")),
        name=name + ("_emit" if emit else ""),
    )(*args)
    return res[0], list(res[1:])


def _zoh(lam_re, lam_im, log_dt):
    dt = jnp.exp(log_dt)
    mag = jnp.exp(dt * lam_re)
    a_re = mag * jnp.cos(dt * lam_im)
    a_im = mag * jnp.sin(dt * lam_im)
    nr, ni = a_re - 1.0, a_im
    den = lam_re * lam_re + lam_im * lam_im
    coef_re = (nr * lam_re + ni * lam_im) / den
    coef_im = (ni * lam_re - nr * lam_im) / den
    return a_re, a_im, coef_re, coef_im


def _spread_lanes(x, reps):
    w = x.shape[1]
    assert w & (w - 1) == 0
    src = lax.broadcasted_iota(jnp.int32, (w, w * reps), 0)
    dst = lax.broadcasted_iota(jnp.int32, (w, w * reps), 1)
    sel = ((dst & (w - 1)) == src).astype(BF16)
    return jnp.dot(x.astype(BF16), sel, preferred_element_type=F32)


def _disc_kernel(lre_ref, lim_ref, ldt_ref, lre_col, lim_col, ldt_col, bre_ref, bim_ref, cre_ref, cim_ref,
                 are_ref, aim_ref, bw_ref, cw_ref):
    a_re, a_im, _, _ = _zoh(lre_ref[...], lim_ref[...], ldt_ref[...])
    are_ref[...] = a_re
    aim_ref[...] = a_im

    _, _, coef_re, coef_im = _zoh(lre_col[...], lim_col[...], ldt_col[...])
    b_re = bre_ref[...]
    b_im = bim_ref[...]
    bb_re = coef_re * b_re - coef_im * b_im
    bb_im = coef_re * b_im + coef_im * b_re
    state_shift = SSM_STATE.bit_length() - 1
    chan_shift = SSM_GROUP.bit_length() - 1
    row_g = lax.broadcasted_iota(jnp.int32, (SB, UB), 0) >> state_shift
    col_g = lax.broadcasted_iota(jnp.int32, (SB, UB), 1) >> chan_shift
    diag = row_g == col_g
    for part, bb in enumerate((bb_re, bb_im)):
        wt = jnp.where(diag, _spread_lanes(bb, GROUPS_PER_BLOCK), 0.0)
        bw_ref[0, :, part * SB:(part + 1) * SB] = wt.T.astype(BF16)

    row_g = lax.broadcasted_iota(jnp.int32, (UB, SB), 0) >> chan_shift
    col_g = lax.broadcasted_iota(jnp.int32, (UB, SB), 1) >> state_shift
    diag = row_g == col_g
    for part, c in enumerate((cre_ref[...], -cim_ref[...])):
        wt = jnp.where(diag, _spread_lanes(c, GROUPS_PER_BLOCK), 0.0)
        cw_ref[0, part * SB:(part + 1) * SB, :] = wt.T.astype(BF16)


def _discretise(lam_re, lam_im, log_dt, b_re, b_im, c_re, c_im):
    g, p = lam_re.shape
    h = b_re.shape[-1]
    nb = g // GROUPS_PER_BLOCK
    col = lambda x: x.reshape(g * p, 1)
    blk = lambda rows, cols: pl.BlockSpec((rows, cols), lambda i: (i, 0))
    return pl.pallas_call(
        _disc_kernel,
        grid=(nb,),
        in_specs=[blk(GROUPS_PER_BLOCK, p), blk(GROUPS_PER_BLOCK, p), blk(GROUPS_PER_BLOCK, 1),
                  blk(SB, 1), blk(SB, 1), blk(SB, 1),
                  blk(SB, h), blk(SB, h), blk(UB, p), blk(UB, p)],
        out_specs=(blk(GROUPS_PER_BLOCK, p), blk(GROUPS_PER_BLOCK, p),
                   pl.BlockSpec((1, UB, 2 * SB), lambda i: (i, 0, 0)),
                   pl.BlockSpec((1, 2 * SB, UB), lambda i: (i, 0, 0))),
        out_shape=(jax.ShapeDtypeStruct((g, p), F32), jax.ShapeDtypeStruct((g, p), F32),
                   jax.ShapeDtypeStruct((nb, UB, 2 * SB), BF16),
                   jax.ShapeDtypeStruct((nb, 2 * SB, UB), BF16)),
        compiler_params=_cparams(("arbitrary",)),
        name="s5_discretise",
    )(lam_re, lam_im, log_dt.reshape(g, 1),
      col(lam_re), col(lam_im), col(jnp.repeat(log_dt, p)),
      b_re.reshape(g * p, h), b_im.reshape(g * p, h), c_re.reshape(g * h, p), c_im.reshape(g * h, p))


def _ssm_prompt_kernel(*refs, nbatch):
    tc = pl.program_id(1)
    nt = pl.num_programs(1) - 1
    step = functools.partial(_ssm_prompt_step, *refs, nbatch=nbatch)
    pl.when(tc == 0)(functools.partial(step, fill=0, project=True, scan_prev=False))
    for fill in (0, 1):
        middle = jnp.logical_and(jnp.logical_and(tc > 0, tc < nt), tc % 2 == fill)
        pl.when(middle)(functools.partial(step, fill=fill, project=True, scan_prev=True))
        pl.when(jnp.logical_and(tc == nt, tc % 2 == fill))(
            functools.partial(step, fill=fill, project=False, scan_prev=True))


def _ssm_prompt_step(un_ref, u_ref, bw_ref, cw_ref, are_ref, aim_ref, d_ref,
                     y_ref, sre_ref, sim_ref,
                     bure2, buim2, st_re, st_im, *, nbatch, fill, project, scan_prev):
    tc = pl.program_id(1)
    scan = 1 - fill

    if not scan_prev:
        st_re[...] = jnp.zeros_like(st_re)
        st_im[...] = jnp.zeros_like(st_im)

    bure_f, buim_f = bure2.at[fill], buim2.at[fill]
    bure, buim = bure2.at[scan], buim2.at[scan]

    pack = 8 // nbatch
    nq = SLABS // pack

    def slab_rows(c, b):
        part, c = (0, c) if c < SLABS else (1, c - SLABS)
        v = (c // nq) * nbatch + b
        r0 = v * SCAN_PITCH + (v % 2) * SCAN_SKEW
        return part, c % nq, slice(r0, r0 + SCAN_T)

    if project:
        for ref in (bure_f, buim_f):
            for q in range(nq):
                for v in range(0, 8, 2):
                    ref[q, v * SCAN_PITCH + SCAN_T:v * SCAN_PITCH + SCAN_T + 2 * SCAN_SKEW, :] = (
                        jnp.zeros((2 * SCAN_SKEW, LANES), F32))

        u_bf = un_ref[...].reshape(nbatch * SCAN_T, UB).astype(BF16)
        per_dot = 4
        for n in range(2 * SLABS // per_dot):
            bu = jnp.dot(u_bf, bw_ref[0, :, per_dot * n * LANES:per_dot * (n + 1) * LANES],
                         preferred_element_type=F32)
            for h in range(per_dot):
                for b in range(nbatch):
                    part, q, rows = slab_rows(per_dot * n + h, b)
                    (bure_f, buim_f)[part][q, rows, :] = (
                        bu[b * SCAN_T:(b + 1) * SCAN_T, h * LANES:(h + 1) * LANES])
    if not scan_prev:
        return

    def packed(a_ref, q):
        return jnp.concatenate(
            [jnp.broadcast_to(a_ref[0, q + h * nq], (nbatch, LANES)) for h in range(pack)], axis=0)

    a_re = [packed(are_ref, q) for q in range(nq)]
    a_im = [packed(aim_ref, q) for q in range(nq)]
    init = tuple(st_re[q] for q in range(nq)) + tuple(st_im[q] for q in range(nq))

    def step(t, carry, active=None):
        rows = pl.ds(t, pack * nbatch, stride=SCAN_PITCH)
        new_re, new_im = [], []
        for q in range(nq):
            s_r, s_i = carry[q], carry[nq + q]
            b_r, b_i = bure[q, rows, :], buim[q, rows, :]
            n_r = a_re[q] * s_r - a_im[q] * s_i + b_r
            n_i = a_re[q] * s_i + a_im[q] * s_r + b_i
            if active is not None:
                b_r, b_i = jnp.where(active, n_r, b_r), jnp.where(active, n_i, b_i)
                n_r, n_i = jnp.where(active, n_r, s_r), jnp.where(active, n_i, s_i)
                bure[q, rows, :] = b_r
                buim[q, rows, :] = b_i
            else:
                bure[q, rows, :] = n_r
                buim[q, rows, :] = n_i
            new_re.append(n_r)
            new_im.append(n_i)
        return tuple(new_re) + tuple(new_im)

    odd = (lax.broadcasted_iota(jnp.int32, (8, LANES), 0) & 1) == 1
    carry = init
    for t in range(SCAN_SKEW):
        carry = step(t, carry, active=jnp.logical_not(odd))
    for t in range(SCAN_SKEW, SCAN_T):
        carry = step(t, carry)
    for t in range(SCAN_T, SCAN_T + SCAN_SKEW):
        carry = step(t, carry, active=odd)
    fin = carry
    for q in range(nq):
        st_re[q] = fin[q]
        st_im[q] = fin[nq + q]

    d = d_ref[0]
    for b in range(nbatch):
        y = d * u_ref[b]
        for n in range(SLABS):
            (p0, q0, rows0), (p1, q1, rows1) = slab_rows(2 * n, b), slab_rows(2 * n + 1, b)
            s_pair = jnp.concatenate([(bure, buim)[p0][q0, rows0, :], (bure, buim)[p1][q1, rows1, :]],
                                     axis=1).astype(BF16)
            y = y + jnp.dot(s_pair, cw_ref[0, 2 * n * LANES:(2 * n + 2) * LANES, :],
                            preferred_element_type=F32)
        y_ref[b] = _gelu(y).astype(y_ref.dtype)

    @pl.when(tc == pl.num_programs(1) - 1)
    def _():
        for c in range(SLABS):
            q, h = c % nq, c // nq
            sre_ref[:, c * LANES:(c + 1) * LANES] = st_re[q, h * nbatch:(h + 1) * nbatch, :]
            sim_ref[:, c * LANES:(c + 1) * LANES] = st_im[q, h * nbatch:(h + 1) * nbatch, :]


def _ssm_prompt(z3, bw, cw, a_re, a_im, d_skip):
    nbatch, seq, width = z3.shape
    nb = bw.shape[0]
    states = nb * SB
    kern = functools.partial(_ssm_prompt_kernel, nbatch=nbatch)
    nt = seq // SCAN_T
    assert 8 % nbatch == 0
    return pl.pallas_call(
        kern,
        grid=(nb, nt + 1),
        in_specs=[
            pl.BlockSpec((nbatch, SCAN_T, UB), lambda g, t: (0, jnp.minimum(t, nt - 1), g)),
            pl.BlockSpec((nbatch, SCAN_T, UB), lambda g, t: (0, jnp.maximum(t - 1, 0), g)),
            pl.BlockSpec((1, UB, 2 * SB), lambda g, t: (g, 0, 0)),
            pl.BlockSpec((1, 2 * SB, UB), lambda g, t: (g, 0, 0)),
            pl.BlockSpec((1, SLABS, 1, LANES), lambda g, t: (g, 0, 0, 0)),
            pl.BlockSpec((1, SLABS, 1, LANES), lambda g, t: (g, 0, 0, 0)),
            pl.BlockSpec((1, 1, UB), lambda g, t: (g, 0, 0)),
        ],
        out_specs=(
            pl.BlockSpec((nbatch, SCAN_T, UB), lambda g, t: (0, jnp.maximum(t - 1, 0), g)),
            pl.BlockSpec((nbatch, SB), lambda g, t: (0, g)),
            pl.BlockSpec((nbatch, SB), lambda g, t: (0, g)),
        ),
        out_shape=(
            jax.ShapeDtypeStruct((nbatch, seq, width), BF16),
            jax.ShapeDtypeStruct((nbatch, states), F32),
            jax.ShapeDtypeStruct((nbatch, states), F32),
        ),
        scratch_shapes=[
            pltpu.VMEM((2, SLABS * nbatch // 8, 8 * SCAN_PITCH, LANES), F32),
            pltpu.VMEM((2, SLABS * nbatch // 8, 8 * SCAN_PITCH, LANES), F32),
            pltpu.VMEM((SLABS * nbatch // 8, 8, LANES), F32),
            pltpu.VMEM((SLABS * nbatch // 8, 8, LANES), F32),
        ],
        compiler_params=_cparams(("arbitrary", "arbitrary")),
        name="ssm_prompt",
    )(z3, z3, bw, cw, a_re.reshape(nb, SLABS, 1, LANES), a_im.reshape(nb, SLABS, 1, LANES),
      d_skip.reshape(nb, 1, UB))


def _ssm_sample_kernel(u_ref, bw_ref, cw_ref, are_ref, aim_ref, d_ref, x0re_ref, x0im_ref,
                       y_ref, sre_ref, sim_ref):
    u = u_ref[...]
    bu = jnp.dot(u.astype(BF16), bw_ref[0], preferred_element_type=F32)
    a_re = are_ref[0]
    a_im = aim_ref[0]
    x_re = x0re_ref[...]
    x_im = x0im_ref[...]
    s_re = a_re * x_re - a_im * x_im + bu[:, :SB]
    s_im = a_re * x_im + a_im * x_re + bu[:, SB:]
    sre_ref[...] = s_re
    sim_ref[...] = s_im
    s_cat = jnp.concatenate([s_re, s_im], axis=1).astype(BF16)
    y = jnp.dot(s_cat, cw_ref[0], preferred_element_type=F32) + d_ref[0] * u
    y_ref[...] = _gelu(y).astype(y_ref.dtype)


def _ssm_sample(z, bw, cw, a_re, a_im, d_skip, x0_re, x0_im):
    nbatch = z.shape[0]
    nb = bw.shape[0]
    return pl.pallas_call(
        _ssm_sample_kernel,
        grid=(nb,),
        in_specs=[
            pl.BlockSpec((nbatch, UB), lambda g: (0, g)),
            pl.BlockSpec((1, UB, 2 * SB), lambda g: (g, 0, 0)),
            pl.BlockSpec((1, 2 * SB, UB), lambda g: (g, 0, 0)),
            pl.BlockSpec((1, 1, SB), lambda g: (g, 0, 0)),
            pl.BlockSpec((1, 1, SB), lambda g: (g, 0, 0)),
            pl.BlockSpec((1, 1, UB), lambda g: (g, 0, 0)),
            pl.BlockSpec((nbatch, SB), lambda g: (0, g)),
            pl.BlockSpec((nbatch, SB), lambda g: (0, g)),
        ],
        out_specs=(
            pl.BlockSpec((nbatch, UB), lambda g: (0, g)),
            pl.BlockSpec((nbatch, SB), lambda g: (0, g)),
            pl.BlockSpec((nbatch, SB), lambda g: (0, g)),
        ),
        out_shape=(
            jax.ShapeDtypeStruct((nbatch, nb * UB), BF16),
            jax.ShapeDtypeStruct((nbatch, nb * SB), F32),
            jax.ShapeDtypeStruct((nbatch, nb * SB), F32),
        ),
        compiler_params=_cparams(("arbitrary",)),
        name="ssm_sample",
    )(z, bw, cw, a_re.reshape(nb, 1, SB), a_im.reshape(nb, 1, SB), d_skip.reshape(nb, 1, UB),
      x0_re, x0_im)


def _gmlp_prompt_kernel(gu_ref, gv_ref, gnv_ref, ws_ref, bias_ref, wo_ref, gate_ref, p_ref,
                        out_ref, s_scr, wt_scr, *, tm, heads):
    i = pl.program_id(0)
    j = pl.program_id(1)

    @pl.when(jnp.logical_and(i == 0, j == 0))
    def _():
        row = lax.broadcasted_iota(jnp.int32, (CHUNK, CHUNK), 0)
        col = lax.broadcasted_iota(jnp.int32, (CHUNK, CHUNK), 1)
        mask = (col <= row).astype(F32)
        for g in range(heads):
            wt_scr[g] = (ws_ref[g] * mask).astype(BF16)

    @pl.when(j == 0)
    def _():
        gnv = gnv_ref[...]

        def body(c, carry):
            r0 = pl.multiple_of(c * CHUNK, CHUNK)
            gv = gv_ref[pl.ds(r0, CHUNK), :].astype(F32)
            r = lax.rsqrt(jnp.mean(gv * gv, axis=-1, keepdims=True) + EPS)
            v = (gv * r * gnv).astype(BF16)
            for g in range(heads):
                sl = slice(g * GMLP_HEAD, (g + 1) * GMLP_HEAD)
                mixed = jnp.dot(wt_scr[g], v[:, sl], preferred_element_type=F32) + bias_ref[:, sl]
                gu = gu_ref[pl.ds(r0, CHUNK), sl].astype(F32)
                s_scr[pl.ds(r0, CHUNK), sl] = (gu * mixed).astype(BF16)
            return carry

        lax.fori_loop(0, tm // CHUNK, body, 0)

    yb = jnp.dot(s_scr[...], _as_bf16(wo_ref, None), preferred_element_type=F32)
    out_ref[...] = (p_ref[...].astype(F32) + gate_ref[...].astype(F32) * yb).astype(out_ref.dtype)


def _gmlp_prompt(guv, gnv, w_s, bias_full, w_gout, gates, p, *, gate_col, tm, tn):
    m = guv.shape[0]
    ntiles, width, _ = w_gout.shape
    d = ntiles * tn
    heads = w_s.shape[0]
    kern = functools.partial(_gmlp_prompt_kernel, tm=tm, heads=heads)
    return pl.pallas_call(
        kern,
        grid=(m // tm, d // tn),
        in_specs=[
            pl.BlockSpec((tm, width), lambda i, j: (i, 0)),
            pl.BlockSpec((tm, width), lambda i, j: (i, 1)),
            pl.BlockSpec((1, width), lambda i, j: (0, 0)),
            pl.BlockSpec((heads, CHUNK, CHUNK), lambda i, j: (0, 0, 0)),
            pl.BlockSpec((CHUNK, width), lambda i, j: (0, 0)),
            _col_tile_spec(w_gout, width, tn),
            pl.BlockSpec((tm, tn), lambda i, j: (i, j + gate_col // tn)),
            pl.BlockSpec((tm, tn), lambda i, j: (i, j)),
        ],
        out_specs=pl.BlockSpec((tm, tn), lambda i, j: (i, j)),
        out_shape=jax.ShapeDtypeStruct((m, d), BF16),
        scratch_shapes=[pltpu.VMEM((tm, width), BF16), pltpu.VMEM((heads, CHUNK, CHUNK), BF16)],
        compiler_params=_cparams(("arbitrary", "arbitrary")),
        name="gmlp_prompt",
    )(guv, guv, gnv, w_s, bias_full, w_gout, gates, p)


def _gmlp_sample_kernel(gu_ref, gv_ref, gnv_ref, wdiag_ref, bias_ref, wo_ref, gate_ref, p_ref,
                        out_ref, v_ref, wcopy_ref, s_scr):
    j = pl.program_id(0)

    @pl.when(j == 0)
    def _():
        gv = gv_ref[...]
        r = lax.rsqrt(jnp.mean(gv * gv, axis=-1, keepdims=True) + EPS)
        v = gv * r * gnv_ref[...]
        v_ref[...] = v
        mixed = wdiag_ref[...] * v + bias_ref[...]
        s_scr[...] = (gu_ref[...] * mixed).astype(BF16)

    yb = jnp.dot(s_scr[...], _as_bf16(wo_ref, wcopy_ref), preferred_element_type=F32)
    out_ref[...] = (p_ref[...] + gate_ref[...] * yb).astype(out_ref.dtype)


def _gmlp_sample(guv, gnv, wdiag, bias0, w_gout, gates, p, *, gate_col, tn):
    m = guv.shape[0]
    width, d = w_gout.shape
    return pl.pallas_call(
        _gmlp_sample_kernel,
        grid=(d // tn,),
        in_specs=[
            pl.BlockSpec((m, width), lambda j: (0, 0)),
            pl.BlockSpec((m, width), lambda j: (0, 1)),
            pl.BlockSpec((1, width), lambda j: (0, 0)),
            pl.BlockSpec((1, width), lambda j: (0, 0)),
            pl.BlockSpec((1, width), lambda j: (0, 0)),
            pl.BlockSpec((width, tn), lambda j: (0, j)),
            pl.BlockSpec((m, tn), lambda j: (0, j + gate_col // tn)),
            pl.BlockSpec((m, tn), lambda j: (0, j)),
        ],
        out_specs=(pl.BlockSpec((m, tn), lambda j: (0, j)),
                   pl.BlockSpec((m, width), lambda j: (0, 0)),
                   pl.BlockSpec((1, width, tn), lambda j: (j, 0, 0))),
        out_shape=(jax.ShapeDtypeStruct((m, d), BF16), jax.ShapeDtypeStruct((m, width), F32),
                   jax.ShapeDtypeStruct((d // tn, width, tn), BF16)),
        scratch_shapes=[pltpu.VMEM((m, width), BF16)],
        compiler_params=_cparams(("arbitrary",)),
        name="gmlp_sample",
    )(guv, guv, gnv, wdiag, bias0, w_gout, gates, p)


def _mixer(x, h2, prm, w, *, nbatch, x0, tm_mm, copies=None):
    sample = x0 is not None
    copies = {} if copies is None else copies
    d_model = x.shape[1]
    ssm_width = prm["bw"].shape[0] * UB
    gmlp_width = prm["gnv"].shape[1]
    o1 = ssm_width
    o3 = o1 + 2 * gmlp_width
    tn = 1024

    w = dict(w, win_ssm=w["w_in"], win_uv=w["w_in"], win_gate=w["w_in"], glu_a=w["w_glu"], glu_b=w["w_glu"])
    cols = dict(win_uv=o1, win_gate=o3, glu_b=d_model)

    def mm(lhs, wkeys, extras, kind, n_out, out_dtype, name, copied=False):
        if copied and not sample:
            weights = [(copies[k], 0) for k in wkeys]
        else:
            weights = [(w[k], cols.get(k, 0)) for k in wkeys]
        out, emitted = _mm(lhs, weights, extras, kind=kind, n_out=n_out, out_dtype=out_dtype,
                           tm=tm_mm, tn=tn, emit=copied and sample, name=name)
        copies.update(zip(wkeys, emitted))
        return out

    act = F32 if sample else BF16
    z_ssm = mm(h2, ["win_ssm"], [], "identity", o1, F32, "w_in_ssm", copied=True)
    guv = mm(h2, ["win_uv"], [], "gelu", 2 * gmlp_width, act, "w_in_uv")
    gates = mm(h2, ["win_gate"], [], "sigmoid", 2 * d_model, act, "w_in_gate")

    if sample:
        y, s_re, s_im = _ssm_sample(z_ssm, prm["bw"], prm["cw"], prm["a_re"], prm["a_im"],
                                    prm["d_skip"], x0[0], x0[1])
    else:
        seq = x.shape[0] // nbatch
        y, s_re, s_im = _ssm_prompt(z_ssm.reshape(nbatch, seq, ssm_width), prm["bw"], prm["cw"],
                                    prm["a_re"], prm["a_im"], prm["d_skip"])
        y = y.reshape(nbatch * seq, ssm_width)
    p = mm(y, ["glu_a", "glu_b"], [(gates, 0)], "gated_glu", d_model, act, "ssm_glu")

    if sample:
        merged, v_rows, copies["w_gout"] = _gmlp_sample(guv, prm["gnv"], prm["w_diag"], prm["bias0"],
                                                        w["w_gout"], gates, p, gate_col=d_model, tn=tn)
    else:
        merged = _gmlp_prompt(guv, prm["gnv"], prm["w_s"], prm["bias_full"], copies["w_gout"], gates, p,
                              gate_col=d_model, tm=tm_mm, tn=tn)
        v_rows = None
    x = mm(merged, ["w_out"], [(x, 0)], "residual", d_model, F32, "out_proj", copied=True)
    return x, s_re, s_im, v_rows, copies


def kernel(x_prompt, x_sample, state_ssm_re, state_ssm_im, norm_ffn1, ffn1_gate, ffn1_up, ffn1_down, norm_mix, w_in, ssm_lambda_re, ssm_lambda_im, ssm_log_dt, ssm_b_re, ssm_b_im, ssm_c_re, ssm_c_im, ssm_d, ssm_w_glu, gmlp_norm_v, gmlp_w_s, gmlp_b_s, gmlp_w_out, w_out, norm_ffn2, ffn2_gate, ffn2_up, ffn2_down, norm_final):
    depth = w_in.shape[0]
    assert depth == 1, "the final RMSNorm is fused into the last FFN; one layer per step"
    batch, seq, d_model = x_prompt.shape
    dec_batch, dec_seq, _ = x_sample.shape
    assert dec_seq == 1
    groups, states = ssm_lambda_re.shape[1:]
    gmlp_width = gmlp_norm_v.shape[1]
    heads = gmlp_w_s.shape[1]
    head_dim = gmlp_width // heads
    assert seq % SCAN_T == 0 and seq % CHUNK == 0 and head_dim == GMLP_HEAD
    assert ssm_b_re.shape[-1] == SSM_GROUP and states == SSM_STATE and groups % GROUPS_PER_BLOCK == 0

    l = 0
    yp = x_prompt.reshape(batch * seq, d_model)
    ys = x_sample.reshape(dec_batch, d_model)
    gfin = norm_final.reshape(1, d_model)
    a_re, a_im, bw, cw = _discretise(ssm_lambda_re[l], ssm_lambda_im[l], ssm_log_dt[l],
                                     ssm_b_re[l], ssm_b_im[l], ssm_c_re[l], ssm_c_im[l])
    prm = dict(
        n1=norm_ffn1[l].reshape(1, d_model), nmix=norm_mix[l].reshape(1, d_model),
        n2=norm_ffn2[l].reshape(1, d_model),
        bw=bw, cw=cw, a_re=a_re, a_im=a_im, d_skip=ssm_d[l],
        gnv=gmlp_norm_v[l].reshape(1, gmlp_width), w_s=gmlp_w_s[l],
        bias_full=jnp.repeat(gmlp_b_s[l].T, head_dim, axis=1),
        w_diag=jnp.repeat(gmlp_w_s[l][:, 0, 0], head_dim).reshape(1, gmlp_width),
        bias0=jnp.repeat(gmlp_b_s[l][:, 0], head_dim).reshape(1, gmlp_width),
    )
    w_f32 = dict(w_in=w_in[l], w_glu=ssm_w_glu[l], w_gout=gmlp_w_out[l], w_out=w_out[l])
    x0 = (state_ssm_re[l].reshape(dec_batch, groups * states),
          state_ssm_im[l].reshape(dec_batch, groups * states))
    tm, tf, row_tiles = 1024, 256, 8

    yp, ys, hp, hs = _ffn(yp, ys, prm["n1"], ffn1_gate[l], ffn1_up[l], ffn1_down[l], prm["nmix"],
                          tail="norm_out", row_tiles=row_tiles, tf=tf)
    ys, sr, si, vr, copies = _mixer(ys, hs, prm, w_f32, nbatch=dec_batch, x0=x0, tm_mm=dec_batch)
    yp, pr, pi, _, _ = _mixer(yp, hp, prm, w_f32, nbatch=batch, x0=None, tm_mm=tm, copies=copies)
    yp, ys = _ffn(yp, ys, prm["n2"], ffn2_gate[l], ffn2_up[l], ffn2_down[l], gfin,
                  tail="final", row_tiles=row_tiles, tf=tf)
    return (yp.reshape(batch, seq, d_model), ys.reshape(dec_batch, dec_seq, d_model),
            pr.reshape(1, batch, groups, states), pi.reshape(1, batch, groups, states),
            sr.reshape(1, dec_batch, groups, states), si.reshape(1, dec_batch, groups, states),
            vr.reshape(1, dec_batch, dec_seq, gmlp_width))
```
